```python
import math
import jax, jax.numpy as jnp
from jax import lax
import numpy as np

D_MODEL = 1024
BATCH = 8
SEQ = 2048
DEPTH = 1
DEC_BATCH = 128
DEC_SEQ = 8
PAST_LEN = 16384
PAGE_SIZE = 128

D_INNER = 2 * D_MODEL
HEAD_DIM = 64
N_HEADS = D_INNER // HEAD_DIM
N_GROUPS = 8
HEADS_PER_GROUP = N_HEADS // N_GROUPS
D_STATE = 128
CONV_W = 4
CONV_DIM = D_INNER + 2 * N_GROUPS * D_STATE
CHUNK = 128
POOL_DIM = D_MODEL
POOL_WINDOWS = (2, 4, 8, 16)
N_POOL_GROUPS = len(POOL_WINDOWS)
POOL_GC = POOL_DIM // N_POOL_GROUPS
POOL_BUF = max(POOL_WINDOWS) - 1
N_BRANCH = 2
IN_DIM = D_INNER + CONV_DIM + N_HEADS + POOL_DIM + N_BRANCH * D_MODEL
D_FF = 2816
PLE_DIM = 256
EPS = 1e-6

kernel_name = 'hybrid_ssd_pool_macaron_step'


def _rmsnorm(x, g):
    xf = x.astype(jnp.float32)
    y = xf * lax.rsqrt(jnp.mean(xf * xf, axis=-1, keepdims=True) + EPS)
    return (y * g.astype(jnp.float32)).astype(x.dtype)


def _swiglu(u, w_gu, w_down):
    gate, up = jnp.split(u @ w_gu, 2, axis=-1)
    return (jax.nn.silu(gate) * up) @ w_down


def _ssd(x, dt, A, Bm, Cm, h0):
    b, l = x.shape[:2]
    q = min(CHUNK, l)
    nc = -(-l // q)
    pad = nc * q - l
    if pad:
        padw = lambda a: jnp.pad(a, [(0, 0), (0, pad)] + [(0, 0)] * (a.ndim - 2))
        x, dt, Bm, Cm = padw(x), padw(dt), padw(Bm), padw(Cm)
    rs = lambda a: a.reshape((b, nc, q) + a.shape[2:])
    x, dt, Bm, Cm = rs(x), rs(dt), rs(Bm), rs(Cm)
    a_cs = jnp.cumsum(dt * A, axis=2)
    causal = jnp.tril(jnp.ones((q, q), dtype=bool))
    diff = a_cs[:, :, :, None] - a_cs[:, :, None, :]
    decay = jnp.exp(jnp.where(causal[:, :, None, None], diff, -jnp.inf))
    cb = jnp.einsum('bcqgn,bcsgn->bcqsg', Cm, Bm).astype(jnp.float32)
    w_in = cb[..., None] * decay * dt[:, :, None]
    y_diag = jnp.einsum('bcqsgr,bcsgrp->bcqgrp', w_in, x)
    decay_s = jnp.exp(a_cs[:, :, -1:] - a_cs)
    xw = (decay_s * dt)[..., None] * x
    st = jnp.einsum('bcsgn,bcsgrp->bcgrpn', Bm, xw).astype(jnp.float32)
    chunk_decay = jnp.exp(a_cs[:, :, -1])

    def step(h, inp):
        d, s = inp
        return d[..., None, None] * h + s, h

    h_last, h_prev = lax.scan(step, h0.astype(jnp.float32),
                              (jnp.moveaxis(chunk_decay, 1, 0), jnp.moveaxis(st, 1, 0)))
    y_off = jnp.einsum('bcqgn,cbgrpn->bcqgrp', Cm, h_prev) * jnp.exp(a_cs)[..., None]
    y = (y_diag + y_off).reshape((b, nc * q) + x.shape[3:])[:, :l]
    return y, h_last


def _pool(v, prev, pos0):
    l = v.shape[1]
    ext = jnp.concatenate([prev.astype(v.dtype), v], axis=1)
    cs = jnp.cumsum(ext.astype(jnp.float32), axis=1)
    cs = jnp.pad(cs, [(0, 0), (1, 0), (0, 0)])
    t = jnp.arange(l)
    outs = []
    for gi, w in enumerate(POOL_WINDOWS):
        sl = slice(gi * POOL_GC, (gi + 1) * POOL_GC)
        s = cs[:, POOL_BUF + 1:POOL_BUF + 1 + l, sl] - cs[:, POOL_BUF + 1 - w:POOL_BUF + 1 - w + l, sl]
        cnt = jnp.minimum(pos0 + t + 1, w).astype(jnp.float32)
        outs.append(s / cnt[None, :, None])
    mean = jnp.concatenate(outs, axis=-1)
    return (mean - v.astype(jnp.float32)).astype(v.dtype), ext[:, -POOL_BUF:]


def _layer(x, p, ssm0, conv0, pool0, pos0,
           norm_ffn1, w_ffn1_gu, w_ffn1_down, norm_mix, w_in, conv_w, conv_b,
           dt_bias, a_log, d_skip, norm_ssd, w_ssd_out, w_pool_group, pool_scale,
           w_pool_out, w_o, norm_ffn2, w_ffn2_gu, w_ffn2_down, norm_ple, w_ple_gate, w_ple):
    b, l, _ = x.shape
    h = x + 0.5 * _swiglu(_rmsnorm(x, norm_ffn1), w_ffn1_gu, w_ffn1_down)
    u = _rmsnorm(h, norm_mix)
    proj = u @ w_in
    cuts = np.cumsum([D_INNER, CONV_DIM, N_HEADS, POOL_DIM]).tolist()
    z, xbc, dt_raw, v, gates = jnp.split(proj, cuts, axis=-1)
    ext = jnp.concatenate([conv0.astype(xbc.dtype), xbc], axis=1)
    conv = conv_b + sum(ext[:, k:k + l] * conv_w[k] for k in range(CONV_W))
    new_conv = ext[:, -(CONV_W - 1):]
    xbc = jax.nn.silu(conv)
    xs, Bm, Cm = jnp.split(xbc, [D_INNER, D_INNER + N_GROUPS * D_STATE], axis=-1)
    xs = xs.reshape(b, l, N_GROUPS, HEADS_PER_GROUP, HEAD_DIM)
    Bm = Bm.reshape(b, l, N_GROUPS, D_STATE)
    Cm = Cm.reshape(b, l, N_GROUPS, D_STATE)
    dt = jax.nn.softplus(dt_raw.astype(jnp.float32) + dt_bias.astype(jnp.float32))
    dt = dt.reshape(b, l, N_GROUPS, HEADS_PER_GROUP)
    A = -jnp.exp(a_log.astype(jnp.float32)).reshape(N_GROUPS, HEADS_PER_GROUP)
    h0 = ssm0.reshape(b, N_GROUPS, HEADS_PER_GROUP, HEAD_DIM, D_STATE)
    y, h_last = _ssd(xs, dt, A, Bm, Cm, h0)
    y = y + d_skip.reshape(N_GROUPS, HEADS_PER_GROUP)[..., None] * xs
    y = y.reshape(b, l, D_INNER).astype(x.dtype) * jax.nn.silu(z)
    y = _rmsnorm(y.reshape(b, l, N_GROUPS, D_INNER // N_GROUPS),
                 norm_ssd.reshape(N_GROUPS, D_INNER // N_GROUPS)).reshape(b, l, D_INNER)
    a_branch = y @ w_ssd_out
    pooled, new_pool = _pool(v, pool0, pos0)
    pooled = jnp.einsum('blgc,gcd->blgd', pooled.reshape(b, l, N_POOL_GROUPS, POOL_GC), w_pool_group)
    b_branch = (pooled.reshape(b, l, POOL_DIM) * pool_scale) @ w_pool_out
    g = jax.nn.sigmoid(gates.astype(jnp.float32)).reshape(b, l, N_BRANCH, D_MODEL).astype(x.dtype)
    h = h + (g[:, :, 0] * a_branch + g[:, :, 1] * b_branch) @ w_o
    h = h + 0.5 * _swiglu(_rmsnorm(h, norm_ffn2), w_ffn2_gu, w_ffn2_down)
    pg = jax.nn.sigmoid((_rmsnorm(h, norm_ple) @ w_ple_gate).astype(jnp.float32)).astype(x.dtype)
    h = h + pg * (p @ w_ple)
    new_ssm = h_last.reshape(b, N_HEADS, HEAD_DIM, D_STATE).astype(ssm0.dtype)
    return h, new_ssm, new_conv, new_pool


def setup_inputs(seed: int = 0) -> dict:
    key = jax.random.key(seed)
    ks = iter(jax.random.split(key, 48))
    f32 = jnp.float32
    nrm = lambda shape, s: jax.random.normal(next(ks), shape, f32) * s
    gain = lambda shape: 1.0 + nrm(shape, 0.02)
    d = {}
    d['x_prompt'] = nrm((BATCH, SEQ, D_MODEL), 1.0)
    d['x_sample'] = nrm((DEC_BATCH, DEC_SEQ, D_MODEL), 1.0)
    d['state_ssm'] = nrm((DEPTH, DEC_BATCH, N_HEADS, HEAD_DIM, D_STATE), 0.1)
    d['state_conv'] = nrm((DEPTH, DEC_BATCH, CONV_W - 1, CONV_DIM), 1.0)
    d['state_pool'] = nrm((DEPTH, DEC_BATCH, POOL_BUF, POOL_DIM), 1.0)
    d['p_prompt'] = nrm((DEPTH, BATCH, SEQ, PLE_DIM), 1.0)
    d['p_sample'] = nrm((DEPTH, DEC_BATCH, DEC_SEQ, PLE_DIM), 1.0)
    d['norm_ffn1'] = gain((DEPTH, D_MODEL))
    d['w_ffn1_gu'] = nrm((DEPTH, D_MODEL, 2 * D_FF), D_MODEL ** -0.5)
    d['w_ffn1_down'] = nrm((DEPTH, D_FF, D_MODEL), D_FF ** -0.5)
    d['norm_mix'] = gain((DEPTH, D_MODEL))
    d['w_in'] = nrm((DEPTH, D_MODEL, IN_DIM), D_MODEL ** -0.5)
    d['conv_w'] = nrm((DEPTH, CONV_W, CONV_DIM), CONV_W ** -0.5)
    d['conv_b'] = nrm((DEPTH, CONV_DIM), 0.01)
    dt0 = jnp.exp(jax.random.uniform(next(ks), (DEPTH, N_HEADS), f32,
                                     minval=math.log(1e-3), maxval=math.log(1e-1)))
    d['dt_bias'] = dt0 + jnp.log(-jnp.expm1(-dt0))
    d['a_log'] = jnp.log(jax.random.uniform(next(ks), (DEPTH, N_HEADS), f32, minval=1.0, maxval=16.0))
    d['d_skip'] = gain((DEPTH, N_HEADS))
    d['norm_ssd'] = gain((DEPTH, D_INNER))
    d['w_ssd_out'] = nrm((DEPTH, D_INNER, D_MODEL), D_INNER ** -0.5)
    d['w_pool_group'] = nrm((DEPTH, N_POOL_GROUPS, POOL_GC, POOL_GC), POOL_GC ** -0.5)
    d['pool_scale'] = gain((DEPTH, POOL_DIM))
    d['w_pool_out'] = nrm((DEPTH, POOL_DIM, D_MODEL), POOL_DIM ** -0.5)
    d['w_o'] = nrm((DEPTH, D_MODEL, D_MODEL), D_MODEL ** -0.5)
    d['norm_ffn2'] = gain((DEPTH, D_MODEL))
    d['w_ffn2_gu'] = nrm((DEPTH, D_MODEL, 2 * D_FF), D_MODEL ** -0.5)
    d['w_ffn2_down'] = nrm((DEPTH, D_FF, D_MODEL), D_FF ** -0.5)
    d['norm_ple'] = gain((DEPTH, D_MODEL))
    d['w_ple_gate'] = nrm((DEPTH, D_MODEL, D_MODEL), D_MODEL ** -0.5)
    d['w_ple'] = nrm((DEPTH, PLE_DIM, D_MODEL), PLE_DIM ** -0.5)
    d['norm_final'] = gain((D_MODEL,))
    return d


def reference(x_prompt, x_sample, state_ssm, state_conv, state_pool, p_prompt, p_sample,
              norm_ffn1, w_ffn1_gu, w_ffn1_down, norm_mix, w_in, conv_w, conv_b,
              dt_bias, a_log, d_skip, norm_ssd, w_ssd_out, w_pool_group, pool_scale,
              w_pool_out, w_o, norm_ffn2, w_ffn2_gu, w_ffn2_down, norm_ple, w_ple_gate,
              w_ple, norm_final):
    hp, hs = x_prompt, x_sample
    ssm_p, conv_p, pool_p, ssm_s, conv_s, pool_s = [], [], [], [], [], []
    for i in range(DEPTH):
        w = (norm_ffn1[i], w_ffn1_gu[i], w_ffn1_down[i], norm_mix[i], w_in[i], conv_w[i], conv_b[i],
             dt_bias[i], a_log[i], d_skip[i], norm_ssd[i], w_ssd_out[i], w_pool_group[i],
             pool_scale[i], w_pool_out[i], w_o[i], norm_ffn2[i], w_ffn2_gu[i], w_ffn2_down[i],
             norm_ple[i], w_ple_gate[i], w_ple[i])
        z_ssm = jnp.zeros((BATCH, N_HEADS, HEAD_DIM, D_STATE), state_ssm.dtype)
        z_conv = jnp.zeros((BATCH, CONV_W - 1, CONV_DIM), hp.dtype)
        z_pool = jnp.zeros((BATCH, POOL_BUF, POOL_DIM), hp.dtype)
        hp, s1, c1, q1 = _layer(hp, p_prompt[i], z_ssm, z_conv, z_pool, 0, *w)
        hs, s2, c2, q2 = _layer(hs, p_sample[i], state_ssm[i], state_conv[i], state_pool[i], PAST_LEN, *w)
        ssm_p.append(s1); conv_p.append(c1); pool_p.append(q1)
        ssm_s.append(s2); conv_s.append(c2); pool_s.append(q2)
    y_prompt = _rmsnorm(hp, norm_final)
    y_sample = _rmsnorm(hs, norm_final)
    return (y_prompt, y_sample, jnp.stack(ssm_p), jnp.stack(conv_p), jnp.stack(pool_p),
            jnp.stack(ssm_s), jnp.stack(conv_s), jnp.stack(pool_s))
```

```python
import functools
import math

import jax
import jax.numpy as jnp
from jax import lax
from jax.experimental import pallas as pl
from jax.experimental.pallas import tpu as pltpu

F32 = jnp.float32
BF16 = jnp.bfloat16
EPS = 1e-6
HIGHEST = lax.Precision.HIGHEST

LANES = 128
SUBLANES = 8
VMEM_LIMIT_BYTES = 56 * 1024 * 1024

HEAD_DIM = 64
N_GROUPS = 8
D_STATE = 128
CONV_W = 4
CHUNK = 128
POOL_WINDOWS = (2, 4, 8, 16)
POOL_BUF = max(POOL_WINDOWS) - 1
N_BRANCH = 2
PAST_LEN = 16384
HEADS_PER_TILE = LANES // HEAD_DIM
POOL_PAD = 24


def _sigmoid(x):
    return 1.0 / (1.0 + jnp.exp(-x))


def _silu(x):
    return x * _sigmoid(x)


def _softplus(x):
    return jnp.maximum(x, 0.0) + jnp.log1p(jnp.exp(-jnp.abs(x)))


def _rmsnorm(x, g):
    return x * lax.rsqrt(jnp.mean(x * x, axis=-1, keepdims=True) + EPS) * g


def _dot(a, b):
    return jnp.dot(a, b, preferred_element_type=F32)


def _dot_nt(a, b):
    return lax.dot_general(a, b, (((1,), (1,)), ((), ())), preferred_element_type=F32)


def _const_spec(shape):
    nd = len(shape)
    return pl.BlockSpec(shape, lambda *_: (0,) * nd, pipeline_mode=pl.Buffered(1))


def _params(n_axes):
    return pltpu.CompilerParams(dimension_semantics=("arbitrary",) * n_axes,
                                vmem_limit_bytes=VMEM_LIMIT_BYTES)


def _ffn_kernel(x_ref, g_ref, wgu_ref, wd_ref, o_ref, xn_ref, acc_ref, *, d_ff, fc):
    xn_ref[...] = _rmsnorm(x_ref[...], g_ref[...]).astype(BF16)
    for c in range(d_ff // fc):
        xn = xn_ref[...]
        gate = _dot(xn, wgu_ref[:, c * fc:(c + 1) * fc])
        up = _dot(xn, wgu_ref[:, d_ff + c * fc:d_ff + (c + 1) * fc])
        act = (_silu(gate) * up).astype(BF16)
        contrib = _dot(act, wd_ref[c * fc:(c + 1) * fc, :])
        if c == 0:
            acc_ref[...] = contrib
        else:
            acc_ref[...] += contrib
    o_ref[...] = x_ref[...] + 0.5 * acc_ref[...]


def _ffn(x, g, w_gu, w_down, *, tm):
    t, d = x.shape
    d_ff = w_down.shape[0]
    tm = min(tm, t)
    fc = 256
    return pl.pallas_call(
        functools.partial(_ffn_kernel, d_ff=d_ff, fc=fc),
        grid=(t // tm,),
        in_specs=[pl.BlockSpec((tm, d), lambda i: (i, 0)),
                  _const_spec((1, d)),
                  _const_spec((d, 2 * d_ff)),
                  _const_spec((d_ff, d))],
        out_specs=pl.BlockSpec((tm, d), lambda i: (i, 0)),
        out_shape=jax.ShapeDtypeStruct((t, d), F32),
        scratch_shapes=[pltpu.VMEM((tm, d), BF16), pltpu.VMEM((tm, d), F32)],
        compiler_params=_params(1),
        name="ffn",
    )(x, g, w_gu, w_down)


def _inproj_kernel(h_ref, g_ref, w_ref, *rest, widths, nc):
    out_refs, u_ref = rest[:-1], rest[-1]
    u_ref[...] = _rmsnorm(h_ref[...], g_ref[...]).astype(BF16)
    start = 0
    for o_ref, width in zip(out_refs, widths):
        for a in range(0, width, nc):
            b = min(a + nc, width)
            o_ref[:, a:b] = _dot(u_ref[...], w_ref[:, start + a:start + b])
        start += width


def _inproj(h, g, w_cat, widths, *, tm):
    t, d = h.shape
    tm = min(tm, t)
    n_all = w_cat.shape[1]
    return pl.pallas_call(
        functools.partial(_inproj_kernel, widths=widths, nc=1024),
        grid=(t // tm,),
        in_specs=[pl.BlockSpec((tm, d), lambda i: (i, 0)),
                  _const_spec((1, d)),
                  _const_spec((d, n_all))],
        out_specs=[pl.BlockSpec((tm, w), lambda i: (i, 0)) for w in widths],
        out_shape=[jax.ShapeDtypeStruct((t, w), F32) for w in widths],
        scratch_shapes=[pltpu.VMEM((tm, d), BF16)],
        compiler_params=_params(1),
        name="inproj",
    )(h, g, w_cat)


def _chunk_scalars(dt_raw, dt_bias, a_log, seg_len):
    q = dt_raw.shape[0]
    dt = _softplus(dt_raw + dt_bias)
    da = dt * (-jnp.exp(a_log))
    shift = int(math.log2(seg_len))
    qi = lax.broadcasted_iota(jnp.int32, (q, q), 0)
    si = lax.broadcasted_iota(jnp.int32, (q, q), 1)
    same = (qi >> shift) == (si >> shift)
    causal = jnp.logical_and(same, si <= qi)
    a_cs = jnp.dot(jnp.where(causal, 1.0, 0.0), da, precision=HIGHEST, preferred_element_type=F32)
    seg_tot = jnp.dot(jnp.where(same, 1.0, 0.0), da, precision=HIGHEST, preferred_element_type=F32)
    return dt, a_cs, seg_tot, causal


def _pair_cols(vals, h0, shape):
    lane = lax.broadcasted_iota(jnp.int32, shape, 1)
    return jnp.where(lane < HEAD_DIM, vals[:, h0:h0 + 1], vals[:, h0 + 1:h0 + 2])


def _diag_pair(xpair, cb, a_cs, a_cs_t, dt_t, causal, h0):
    lane = lax.broadcasted_iota(jnp.int32, xpair.shape, 1)
    out = None
    for k in range(HEADS_PER_TILE):
        h = h0 + k
        diff = a_cs[:, h:h + 1] - a_cs_t[h:h + 1, :]
        decay = jnp.exp(jnp.where(causal, diff, -jnp.inf))
        w = (cb * decay * dt_t[h:h + 1, :]).astype(BF16)
        mine = (lane < HEAD_DIM) if k == 0 else (lane >= HEAD_DIM)
        t = _dot(w, jnp.where(mine, xpair, 0.0).astype(BF16))
        out = t if out is None else out + t
    return out


def _conv_tile_rows(rows_of_tap, w, bias):
    acc = rows_of_tap(0) * w[0:1, :]
    for k in range(1, CONV_W):
        acc = acc + rows_of_tap(k) * w[k:k + 1, :]
    return _silu(bias + acc)


def _mixer_prompt_kernel(xbc_ref, dt_ref, v_ref, cw_ref, cb_ref, dtb_ref, alog_ref, dskip_ref,
                         y_ref, pooled_ref, ssm_ref, conv_ref, pool_ref,
                         head_ref, xc_ref, h_ref, pe_ref, ps_ref, *, d_inner, n_chunks):
    c = pl.program_id(1)
    q = xbc_ref.shape[0]
    conv_dim = xbc_ref.shape[1]
    pool_dim = v_ref.shape[1]
    n_state = D_STATE
    last = c == n_chunks - 1

    @pl.when(c == 0)
    def _():
        head_ref[0:SUBLANES, :] = jnp.zeros((SUBLANES, conv_dim), F32)
        h_ref[...] = jnp.zeros(h_ref.shape, F32)
        pe_ref[0:POOL_PAD, :] = jnp.zeros((POOL_PAD, pool_dim), F32)
        ps_ref[:, 0:SUBLANES, :] = jnp.zeros((2, SUBLANES, ps_ref.shape[2]), F32)

    ct = 512
    for j in range(conv_dim // ct):
        cols = slice(j * ct, (j + 1) * ct)
        w = cw_ref[:, cols]
        bias = cb_ref[:, cols]
        head_ref[SUBLANES:2 * SUBLANES, cols] = xbc_ref[0:SUBLANES, cols]
        lo = SUBLANES - (CONV_W - 1)
        xc_ref[0:SUBLANES, cols] = _conv_tile_rows(
            lambda k: head_ref[lo + k:lo + k + SUBLANES, cols], w, bias)
        xc_ref[SUBLANES:q, cols] = _conv_tile_rows(
            lambda k: xbc_ref[lo + k:lo + k + q - SUBLANES, cols], w, bias)
        tail = xbc_ref[q - (CONV_W - 1):q, cols]
        head_ref[lo:SUBLANES, cols] = tail

        @pl.when(last)
        def _():
            conv_ref[0, :, cols] = tail

    dt, a_cs, seg_tot, causal = _chunk_scalars(dt_ref[...], dtb_ref[...], alog_ref[...], q)
    a_cs_t = a_cs.T
    dt_t = dt.T
    w_state = jnp.exp(seg_tot - a_cs) * dt
    e_in = jnp.exp(a_cs)
    chunk_decay = jnp.exp(seg_tot[0:1, :])

    row128 = lax.broadcasted_iota(jnp.int32, (LANES, n_state), 0)
    for g in range(N_GROUPS):
        b_g = xc_ref[:, d_inner + g * n_state:d_inner + (g + 1) * n_state].astype(BF16)
        c_g = xc_ref[:, d_inner + (N_GROUPS + g) * n_state:
                     d_inner + (N_GROUPS + g + 1) * n_state].astype(BF16)
        cb = _dot_nt(c_g, b_g)
        tiles_per_group = d_inner // N_GROUPS // LANES
        for p in range(tiles_per_group):
            tile = g * tiles_per_group + p
            h0 = tile * HEADS_PER_TILE
            cols = slice(tile * LANES, (tile + 1) * LANES)
            xpair = xc_ref[:, cols]
            h_prev = h_ref[cols, :]
            y = _diag_pair(xpair, cb, a_cs, a_cs_t, dt_t, causal, h0)
            y = y + _dot_nt(c_g, h_prev.astype(BF16)) * _pair_cols(e_in, h0, xpair.shape)
            y_ref[:, cols] = y + dskip_ref[:, cols] * xpair
            xw_t = (xpair * _pair_cols(w_state, h0, xpair.shape)).T.astype(BF16)
            decay_rows = jnp.where(row128 < HEAD_DIM, chunk_decay[:, h0:h0 + 1],
                                   chunk_decay[:, h0 + 1:h0 + 2])
            h_ref[cols, :] = decay_rows * h_prev + _dot(xw_t, b_g)

    @pl.when(last)
    def _():
        ssm_ref[0] = h_ref[...]

    pe_ref[POOL_PAD:POOL_PAD + q, :] = v_ref[...]
    gc = pool_dim // len(POOL_WINDOWS)
    rows = q + POOL_PAD - SUBLANES
    t_idx = lax.broadcasted_iota(jnp.int32, (q, 1), 0) + c * q
    for gi, win in enumerate(POOL_WINDOWS):
        cols = slice(gi * gc, (gi + 1) * gc)
        shift = 1
        level = 0
        cur = None
        while shift < win:
            if level == 0:
                cur = (pe_ref[SUBLANES:SUBLANES + rows, cols]
                       + pe_ref[SUBLANES - shift:SUBLANES - shift + rows, cols])
            else:
                src = ps_ref.at[(level - 1) % 2]
                cur = (src[SUBLANES:SUBLANES + rows, :]
                       + src[SUBLANES - shift:SUBLANES - shift + rows, :])
            if shift * 2 < win:
                ps_ref[level % 2, SUBLANES:SUBLANES + rows, :] = cur
            shift *= 2
            level += 1
        total = cur[POOL_PAD - SUBLANES:, :]
        cnt = jnp.minimum(t_idx + 1, win).astype(F32)
        pooled_ref[:, cols] = total / cnt - v_ref[:, cols]
    pe_ref[POOL_PAD - POOL_BUF:POOL_PAD, :] = v_ref[q - POOL_BUF:q, :]

    @pl.when(last)
    def _():
        pool_ref[0] = v_ref[q - POOL_BUF:q, :]


def _mixer_prompt(xbc, dt, v, conv_w, conv_b, dt_bias, a_log, d_skip, *, batch, seq, d_inner):
    q = CHUNK
    n_chunks = seq // q
    conv_dim = xbc.shape[1]
    pool_dim = v.shape[1]
    gc = pool_dim // len(POOL_WINDOWS)
    n_rows_state = d_inner
    tok = lambda b, c: (b * n_chunks + c, 0)
    per_b = lambda b, c: (b, 0, 0)
    return pl.pallas_call(
        functools.partial(_mixer_prompt_kernel, d_inner=d_inner, n_chunks=n_chunks),
        grid=(batch, n_chunks),
        in_specs=[pl.BlockSpec((q, conv_dim), tok),
                  pl.BlockSpec((q, LANES), tok),
                  pl.BlockSpec((q, pool_dim), tok),
                  _const_spec((CONV_W, conv_dim)),
                  _const_spec((1, conv_dim)),
                  _const_spec((1, LANES)),
                  _const_spec((1, LANES)),
                  _const_spec((1, d_inner))],
        out_specs=[pl.BlockSpec((q, d_inner), tok),
                   pl.BlockSpec((q, pool_dim), tok),
                   pl.BlockSpec((1, n_rows_state, D_STATE), per_b),
                   pl.BlockSpec((1, CONV_W - 1, conv_dim), per_b),
                   pl.BlockSpec((1, POOL_BUF, pool_dim), per_b)],
        out_shape=[jax.ShapeDtypeStruct((batch * seq, d_inner), F32),
                   jax.ShapeDtypeStruct((batch * seq, pool_dim), F32),
                   jax.ShapeDtypeStruct((batch, n_rows_state, D_STATE), F32),
                   jax.ShapeDtypeStruct((batch, CONV_W - 1, conv_dim), F32),
                   jax.ShapeDtypeStruct((batch, POOL_BUF, pool_dim), F32)],
        scratch_shapes=[pltpu.VMEM((2 * SUBLANES, conv_dim), F32),
                        pltpu.VMEM((q, conv_dim), F32),
                        pltpu.VMEM((n_rows_state, D_STATE), F32),
                        pltpu.VMEM((POOL_PAD + q, pool_dim), F32),
                        pltpu.VMEM((2, POOL_PAD + q, gc), F32)],
        compiler_params=_params(2),
        name="mixer_prompt",
    )(xbc, dt, v, conv_w, conv_b, dt_bias, a_log, d_skip)


def _mixer_sample_kernel(xbc_ref, dt_ref, v_ref, conv0_ref, pool0_ref, ssm0_ref,
                         cw_ref, cb_ref, dtb_ref, alog_ref, dskip_ref,
                         y_ref, pooled_ref, ssm_ref, conv_ref, pool_ref,
                         head_ref, xc_ref, xwt_ref, e_ref, seg_ref, pe_ref,
                         *, d_inner, seq, pos0):
    j = pl.program_id(1)
    q = xbc_ref.shape[0]
    conv_dim = xbc_ref.shape[1]
    pool_dim = v_ref.shape[1]
    n_state = D_STATE
    n_seq = q // seq
    tiles_per_group = d_inner // N_GROUPS // LANES
    gc = pool_dim // len(POOL_WINDOWS)

    @pl.when(j == 0)
    def _():
        lo = SUBLANES - (CONV_W - 1)

        def per_seq(b, carry):
            r0 = pl.multiple_of(b * seq, seq)
            head_ref[lo:SUBLANES, :] = conv0_ref[b]
            ct = 1024
            for jc in range(conv_dim // ct):
                cols = slice(jc * ct, (jc + 1) * ct)
                head_ref[SUBLANES:SUBLANES + seq, cols] = xbc_ref[pl.ds(r0, seq), cols]
                xc_ref[pl.ds(r0, seq), cols] = _conv_tile_rows(
                    lambda k: head_ref[lo + k:lo + k + seq, cols], cw_ref[:, cols], cb_ref[:, cols])
            conv_ref[b] = head_ref[SUBLANES + seq - (CONV_W - 1):SUBLANES + seq, :]

            pe_ref[POOL_PAD - POOL_BUF:POOL_PAD, :] = pool0_ref[b]
            vb = v_ref[pl.ds(r0, seq), :]
            pe_ref[POOL_PAD:POOL_PAD + seq, :] = vb
            t_idx = lax.broadcasted_iota(jnp.int32, (seq, 1), 0) + pos0
            for gi, win in enumerate(POOL_WINDOWS):
                cols = slice(gi * gc, (gi + 1) * gc)
                total = pe_ref[POOL_PAD:POOL_PAD + seq, cols]
                for k in range(1, win):
                    total = total + pe_ref[POOL_PAD - k:POOL_PAD - k + seq, cols]
                cnt = jnp.minimum(t_idx + 1, win).astype(F32)
                pooled_ref[pl.ds(r0, seq), cols] = total / cnt - vb[:, cols]
            pool_ref[b] = pe_ref[POOL_PAD + seq - POOL_BUF:POOL_PAD + seq, :]
            return carry

        lax.fori_loop(0, n_seq, per_seq, 0)

        dt, a_cs, seg_tot, causal = _chunk_scalars(dt_ref[...], dtb_ref[...], alog_ref[...], seq)
        a_cs_t = a_cs.T
        dt_t = dt.T
        w_state = jnp.exp(seg_tot - a_cs) * dt
        e_ref[...] = jnp.exp(a_cs)
        seg_ref[...] = seg_tot
        for g in range(N_GROUPS):
            b_g = xc_ref[:, d_inner + g * n_state:d_inner + (g + 1) * n_state].astype(BF16)
            c_g = xc_ref[:, d_inner + (N_GROUPS + g) * n_state:
                         d_inner + (N_GROUPS + g + 1) * n_state].astype(BF16)
            cb = _dot_nt(c_g, b_g)
            for p in range(tiles_per_group):
                tile = g * tiles_per_group + p
                h0 = tile * HEADS_PER_TILE
                cols = slice(tile * LANES, (tile + 1) * LANES)
                xpair = xc_ref[:, cols]
                y = _diag_pair(xpair, cb, a_cs, a_cs_t, dt_t, causal, h0)
                y_ref[:, cols] = y + dskip_ref[:, cols] * xpair
                xwt_ref[cols, :] = (xpair * _pair_cols(w_state, h0, xpair.shape)).T.astype(BF16)

    r0 = pl.multiple_of(j * seq, seq)
    e_rows = e_ref[pl.ds(r0, seq), :]
    chunk_decay = jnp.exp(seg_ref[pl.ds(r0, 1), :])
    row_q = lax.broadcasted_iota(jnp.int32, (q, n_state), 0)
    mine = jnp.logical_and(row_q >= r0, row_q < r0 + seq)
    row128 = lax.broadcasted_iota(jnp.int32, (LANES, n_state), 0)
    for g in range(N_GROUPS):
        b_cols = slice(d_inner + g * n_state, d_inner + (g + 1) * n_state)
        c_cols = slice(d_inner + (N_GROUPS + g) * n_state, d_inner + (N_GROUPS + g + 1) * n_state)
        b_mine = jnp.where(mine, xc_ref[:, b_cols], 0.0).astype(BF16)
        c_rows = xc_ref[pl.ds(r0, seq), c_cols].astype(BF16)
        for p in range(tiles_per_group):
            tile = g * tiles_per_group + p
            h0 = tile * HEADS_PER_TILE
            cols = slice(tile * LANES, (tile + 1) * LANES)
            h_prev = ssm0_ref[0, cols, :]
            y_off = _dot_nt(c_rows, h_prev.astype(BF16)) * _pair_cols(e_rows, h0, (seq, LANES))
            y_ref[pl.ds(r0, seq), cols] += y_off
            decay_rows = jnp.where(row128 < HEAD_DIM, chunk_decay[:, h0:h0 + 1],
                                   chunk_decay[:, h0 + 1:h0 + 2])
            ssm_ref[0, cols, :] = decay_rows * h_prev + _dot(xwt_ref[cols, :], b_mine)


def _mixer_sample(xbc, dt, v, conv0, pool0, ssm0, conv_w, conv_b, dt_bias, a_log, d_skip,
                  *, batch, seq, d_inner, pos0):
    q = CHUNK
    n_seq = q // seq
    n_chunks = batch // n_seq
    conv_dim = xbc.shape[1]
    pool_dim = v.shape[1]
    tok = lambda i, j: (i, 0)
    per_chunk = lambda i, j: (i, 0, 0)
    per_seq = lambda i, j: (i * n_seq + j, 0, 0)
    return pl.pallas_call(
        functools.partial(_mixer_sample_kernel, d_inner=d_inner, seq=seq, pos0=pos0),
        grid=(n_chunks, n_seq),
        in_specs=[pl.BlockSpec((q, conv_dim), tok),
                  pl.BlockSpec((q, LANES), tok),
                  pl.BlockSpec((q, pool_dim), tok),
                  pl.BlockSpec((n_seq, CONV_W - 1, conv_dim), per_chunk),
                  pl.BlockSpec((n_seq, POOL_BUF, pool_dim), per_chunk),
                  pl.BlockSpec((1, d_inner, D_STATE), per_seq),
                  _const_spec((CONV_W, conv_dim)),
                  _const_spec((1, conv_dim)),
                  _const_spec((1, LANES)),
                  _const_spec((1, LANES)),
                  _const_spec((1, d_inner))],
        out_specs=[pl.BlockSpec((q, d_inner), tok),
                   pl.BlockSpec((q, pool_dim), tok),
                   pl.BlockSpec((1, d_inner, D_STATE), per_seq),
                   pl.BlockSpec((n_seq, CONV_W - 1, conv_dim), per_chunk),
                   pl.BlockSpec((n_seq, POOL_BUF, pool_dim), per_chunk)],
        out_shape=[jax.ShapeDtypeStruct((batch * seq, d_inner), F32),
                   jax.ShapeDtypeStruct((batch * seq, pool_dim), F32),
                   jax.ShapeDtypeStruct((batch, d_inner, D_STATE), F32),
                   jax.ShapeDtypeStruct((batch, CONV_W - 1, conv_dim), F32),
                   jax.ShapeDtypeStruct((batch, POOL_BUF, pool_dim), F32)],
        scratch_shapes=[pltpu.VMEM((2 * SUBLANES, conv_dim), F32),
                        pltpu.VMEM((q, conv_dim), F32),
                        pltpu.VMEM((d_inner, q), BF16),
                        pltpu.VMEM((q, LANES), F32),
                        pltpu.VMEM((q, LANES), F32),
                        pltpu.VMEM((POOL_PAD + seq, pool_dim), F32)],
        compiler_params=_params(2),
        name="mixer_sample",
    )(xbc, dt, v, conv0, pool0, ssm0, conv_w, conv_b, dt_bias, a_log, d_skip)


def _merge_kernel(h_ref, y_ref, z_ref, pooled_ref, gates_ref, nssd_ref, wssd_ref, wpg_ref,
                  pscale_ref, wpo_ref, wo_ref, o_ref, yn_ref, pm_ref):
    d_inner = y_ref.shape[1]
    d_model = h_ref.shape[1]
    gw = d_inner // N_GROUPS
    for g in range(N_GROUPS):
        cols = slice(g * gw, (g + 1) * gw)
        yz = y_ref[:, cols] * _silu(z_ref[:, cols])
        yn_ref[:, cols] = _rmsnorm(yz, nssd_ref[:, cols]).astype(BF16)
    a_branch = _dot(yn_ref[...], wssd_ref[...])
    gc = pooled_ref.shape[1] // len(POOL_WINDOWS)
    for gi in range(len(POOL_WINDOWS)):
        cols = slice(gi * gc, (gi + 1) * gc)
        mixed = _dot(pooled_ref[:, cols].astype(BF16), wpg_ref[gi])
        pm_ref[:, cols] = (mixed * pscale_ref[:, cols]).astype(BF16)
    b_branch = _dot(pm_ref[...], wpo_ref[...])
    merged = (_sigmoid(gates_ref[:, 0:d_model]) * a_branch
              + _sigmoid(gates_ref[:, d_model:2 * d_model]) * b_branch)
    o_ref[...] = h_ref[...] + _dot(merged.astype(BF16), wo_ref[...])


def _merge(h, y, z, pooled, gates, norm_ssd, w_ssd_out, w_pool_group, pool_scale, w_pool_out, w_o,
           *, tm):
    t, d = h.shape
    tm = min(tm, t)
    d_inner = y.shape[1]
    pool_dim = pooled.shape[1]
    row = lambda w: pl.BlockSpec((tm, w), lambda i: (i, 0))
    return pl.pallas_call(
        _merge_kernel,
        grid=(t // tm,),
        in_specs=[row(d), row(d_inner), row(d_inner), row(pool_dim), row(N_BRANCH * d),
                  _const_spec((1, d_inner)),
                  _const_spec(w_ssd_out.shape),
                  _const_spec(w_pool_group.shape),
                  _const_spec((1, pool_dim)),
                  _const_spec(w_pool_out.shape),
                  _const_spec(w_o.shape)],
        out_specs=row(d),
        out_shape=jax.ShapeDtypeStruct((t, d), F32),
        scratch_shapes=[pltpu.VMEM((tm, d_inner), BF16), pltpu.VMEM((tm, pool_dim), BF16)],
        compiler_params=_params(1),
        name="merge",
    )(h, y, z, pooled, gates, norm_ssd, w_ssd_out, w_pool_group, pool_scale, w_pool_out, w_o)


def _ple_kernel(h_ref, p_ref, nple_ref, wg_ref, wp_ref, nfin_ref, o_ref):
    h = h_ref[...]
    gate = _sigmoid(_dot(_rmsnorm(h, nple_ref[...]).astype(BF16), wg_ref[...]))
    h = h + gate * _dot(p_ref[...].astype(BF16), wp_ref[...])
    o_ref[...] = _rmsnorm(h, nfin_ref[...])


def _ple(h, p, norm_ple, w_ple_gate, w_ple, norm_final, *, tm):
    t, d = h.shape
    tm = min(tm, t)
    pd = p.shape[1]
    return pl.pallas_call(
        _ple_kernel,
        grid=(t // tm,),
        in_specs=[pl.BlockSpec((tm, d), lambda i: (i, 0)),
                  pl.BlockSpec((tm, pd), lambda i: (i, 0)),
                  _const_spec((1, d)),
                  _const_spec(w_ple_gate.shape),
                  _const_spec(w_ple.shape),
                  _const_spec((1, d))],
        out_specs=pl.BlockSpec((tm, d), lambda i: (i, 0)),
        out_shape=jax.ShapeDtypeStruct((t, d), F32),
        compiler_params=_params(1),
        name="ple",
    )(h, p, norm_ple, w_ple_gate, w_ple, norm_final)


def _pad_lanes(a):
    return jnp.pad(a.reshape(1, -1), ((0, 0), (0, LANES - a.shape[-1])))


def kernel(x_prompt, x_sample, state_ssm, state_conv, state_pool, p_prompt, p_sample, norm_ffn1, w_ffn1_gu, w_ffn1_down, norm_mix, w_in, conv_w, conv_b, dt_bias, a_log, d_skip, norm_ssd, w_ssd_out, w_pool_group, pool_scale, w_pool_out, w_o, norm_ffn2, w_ffn2_gu, w_ffn2_down, norm_ple, w_ple_gate, w_ple, norm_final):
    depth = norm_ffn1.shape[0]
    assert depth == 1, "the final norm is fused into the layer's last stage: one layer only"
    batch, seq, d_model = x_prompt.shape
    dec_batch, dec_seq, _ = x_sample.shape
    n_heads = dt_bias.shape[1]
    d_inner = n_heads * HEAD_DIM
    conv_dim = conv_w.shape[2]
    pool_dim = pool_scale.shape[1]
    assert seq % CHUNK == 0 and CHUNK % dec_seq == 0 and dec_batch % (CHUNK // dec_seq) == 0

    row = lambda a: a[0].reshape(1, -1)
    cuts = [0, d_inner, d_inner + conv_dim, d_inner + conv_dim + n_heads,
            d_inner + conv_dim + n_heads + pool_dim, w_in.shape[2]]
    wz, wxbc, wdt, wv, wgates = [w_in[0][:, a:b] for a, b in zip(cuts[:-1], cuts[1:])]
    wdt = jnp.pad(wdt, ((0, 0), (0, LANES - n_heads)))
    w_cat = jnp.concatenate([wz, wxbc, wv, wgates, wdt], axis=1).astype(BF16)
    widths = (d_inner, conv_dim, pool_dim, N_BRANCH * d_model, LANES)
    w1gu, w1d = w_ffn1_gu[0].astype(BF16), w_ffn1_down[0].astype(BF16)
    w2gu, w2d = w_ffn2_gu[0].astype(BF16), w_ffn2_down[0].astype(BF16)
    wssd, wpg = w_ssd_out[0].astype(BF16), w_pool_group[0].astype(BF16)
    wpo, wo = w_pool_out[0].astype(BF16), w_o[0].astype(BF16)
    wpleg, wple = w_ple_gate[0].astype(BF16), w_ple[0].astype(BF16)
    dtb, alog = _pad_lanes(dt_bias[0]), _pad_lanes(a_log[0])
    dskip = jnp.repeat(d_skip[0], HEAD_DIM).reshape(1, d_inner)
    cb = row(conv_b)
    nfin = norm_final.reshape(1, -1)

    def pre(h):
        h1 = _ffn(h, row(norm_ffn1), w1gu, w1d, tm=512)
        return (h1,) + tuple(_inproj(h1, row(norm_mix), w_cat, widths, tm=256))

    def post(h1, y, z, pooled, gates, p):
        h2 = _merge(h1, y, z, pooled, gates, row(norm_ssd), wssd, wpg, row(pool_scale), wpo, wo,
                    tm=256)
        h3 = _ffn(h2, row(norm_ffn2), w2gu, w2d, tm=512)
        return _ple(h3, p, row(norm_ple), wpleg, wple, nfin, tm=512)

    h1, z, xbc, v, gates, dt = pre(x_prompt.reshape(batch * seq, d_model))
    y, pooled, s1, c1, q1 = _mixer_prompt(xbc, dt, v, conv_w[0], cb, dtb, alog, dskip,
                                          batch=batch, seq=seq, d_inner=d_inner)
    y_prompt = post(h1, y, z, pooled, gates, p_prompt[0].reshape(batch * seq, -1))

    h1, z, xbc, v, gates, dt = pre(x_sample.reshape(dec_batch * dec_seq, d_model))
    y, pooled, s2, c2, q2 = _mixer_sample(
        xbc, dt, v, state_conv[0], state_pool[0], state_ssm[0].reshape(dec_batch, d_inner, D_STATE),
        conv_w[0], cb, dtb, alog, dskip,
        batch=dec_batch, seq=dec_seq, d_inner=d_inner, pos0=PAST_LEN)
    y_sample = post(h1, y, z, pooled, gates, p_sample[0].reshape(dec_batch * dec_seq, -1))

    return (y_prompt.reshape(batch, seq, d_model),
            y_sample.reshape(dec_batch, dec_seq, d_model),
            s1.reshape(1, batch, n_heads, HEAD_DIM, D_STATE), c1[None], q1[None],
            s2.reshape(1, dec_batch, n_heads, HEAD_DIM, D_STATE), c2[None], q2[None])
```

```python
import functools
import math

import jax
import jax.numpy as jnp
from jax import lax
from jax.experimental import pallas as pl
from jax.experimental.pallas import tpu as pltpu

F32 = jnp.float32
BF16 = jnp.bfloat16
EPS = 1e-6
HIGHEST = lax.Precision.HIGHEST
NEG_LOG2E = -1.4426950408889634

LANES = 128
SUBLANES = 8
VMEM_LIMIT_BYTES = 56 * 1024 * 1024

HEAD_DIM = 64
N_GROUPS = 8
D_STATE = 128
CONV_W = 4
CHUNK = 128
POOL_WINDOWS = (2, 4, 8, 16)
POOL_BUF = max(POOL_WINDOWS) - 1
N_BRANCH = 2
PAST_LEN = 16384
HEADS_PER_TILE = LANES // HEAD_DIM
POOL_PAD = 24


def _sigmoid(x):
    return 1.0 / (1.0 + jnp.exp2(x * NEG_LOG2E))


def _silu(x):
    return x * _sigmoid(x)


def _softplus(x):
    return jnp.maximum(x, 0.0) + jnp.log1p(jnp.exp(-jnp.abs(x)))


def _rmsnorm(x, g):
    return x * lax.rsqrt(jnp.mean(x * x, axis=-1, keepdims=True) + EPS) * g


def _dot(a, b):
    return jnp.dot(a, b, preferred_element_type=F32)


def _dot_nt(a, b):
    return lax.dot_general(a, b, (((1,), (1,)), ((), ())), preferred_element_type=F32)


def _const_spec(shape):
    nd = len(shape)
    return pl.BlockSpec(shape, lambda *_: (0,) * nd, pipeline_mode=pl.Buffered(1))


def _params(n_axes):
    return pltpu.CompilerParams(dimension_semantics=("arbitrary",) * n_axes,
                                vmem_limit_bytes=VMEM_LIMIT_BYTES)


def _ffn_kernel(x_ref, g_ref, wgu_ref, wd_ref, o_ref, xn_ref, acc_ref, *, d_ff, fc):
    xn_ref[...] = _rmsnorm(x_ref[...], g_ref[...]).astype(BF16)
    for c in range(d_ff // fc):
        xn = xn_ref[...]
        gate = _dot(xn, wgu_ref[:, c * fc:(c + 1) * fc])
        up = _dot(xn, wgu_ref[:, d_ff + c * fc:d_ff + (c + 1) * fc])
        act = (_silu(gate) * up).astype(BF16)
        contrib = _dot(act, wd_ref[c * fc:(c + 1) * fc, :])
        if c == 0:
            acc_ref[...] = contrib
        else:
            acc_ref[...] += contrib
    o_ref[...] = x_ref[...] + 0.5 * acc_ref[...]


def _ffn(x, g, w_gu, w_down, *, tm):
    t, d = x.shape
    d_ff = w_down.shape[0]
    tm = min(tm, t)
    fc = 256
    return pl.pallas_call(
        functools.partial(_ffn_kernel, d_ff=d_ff, fc=fc),
        grid=(t // tm,),
        in_specs=[pl.BlockSpec((tm, d), lambda i: (i, 0)),
                  _const_spec((1, d)),
                  _const_spec((d, 2 * d_ff)),
                  _const_spec((d_ff, d))],
        out_specs=pl.BlockSpec((tm, d), lambda i: (i, 0)),
        out_shape=jax.ShapeDtypeStruct((t, d), F32),
        scratch_shapes=[pltpu.VMEM((tm, d), BF16), pltpu.VMEM((tm, d), F32)],
        compiler_params=_params(1),
        name="ffn",
    )(x, g, w_gu, w_down)


def _inproj_kernel(h_ref, g_ref, w_ref, *rest, widths, nc):
    out_refs, u_ref = rest[:-1], rest[-1]
    u_ref[...] = _rmsnorm(h_ref[...], g_ref[...]).astype(BF16)
    start = 0
    for o_ref, width in zip(out_refs, widths):
        for a in range(0, width, nc):
            b = min(a + nc, width)
            o_ref[:, a:b] = _dot(u_ref[...], w_ref[:, start + a:start + b])
        start += width


def _inproj(h, g, w_cat, widths, *, tm):
    t, d = h.shape
    tm = min(tm, t)
    n_all = w_cat.shape[1]
    return pl.pallas_call(
        functools.partial(_inproj_kernel, widths=widths, nc=1024),
        grid=(t // tm,),
        in_specs=[pl.BlockSpec((tm, d), lambda i: (i, 0)),
                  _const_spec((1, d)),
                  _const_spec((d, n_all))],
        out_specs=[pl.BlockSpec((tm, w), lambda i: (i, 0)) for w in widths],
        out_shape=[jax.ShapeDtypeStruct((t, w), F32) for w in widths],
        scratch_shapes=[pltpu.VMEM((tm, d), BF16)],
        compiler_params=_params(1),
        name="inproj",
    )(h, g, w_cat)


def _inproj_conv_kernel(h_ref, g_ref, w_ref, cw_ref, cb_ref,
                        z_ref, xc_ref, v_ref, gates_ref, dt_ref, conv_ref,
                        u_ref, raw_ref, *, widths, nc, tiles_per_seq):
    i = pl.program_id(0)
    tm = h_ref.shape[0]
    conv_dim = xc_ref.shape[1]

    @pl.when(i % tiles_per_seq == 0)
    def _():
        raw_ref[0:SUBLANES, :] = jnp.zeros((SUBLANES, conv_dim), F32)

    u_ref[...] = _rmsnorm(h_ref[...], g_ref[...]).astype(BF16)
    starts = [sum(widths[:k]) for k in range(len(widths))]

    outs = {0: (z_ref, 0), 1: (raw_ref, SUBLANES), 2: (v_ref, 0), 3: (gates_ref, 0), 4: (dt_ref, 0)}
    units = [(k, a) for k in (1, 0, 2, 3, 4) for a in range(0, widths[k], nc)]
    lo = SUBLANES - (CONV_W - 1)
    ct = 256
    rb = 64
    n_conv = conv_dim // ct
    done_conv = 0
    for n, (k, a) in enumerate(units):
        b = min(a + nc, widths[k])
        o_ref, row0 = outs[k]
        o_ref[row0:row0 + tm, a:b] = _dot(u_ref[...], w_ref[:, starts[k] + a:starts[k] + b])
        xbc_cols_done = min(n + 1, widths[1] // nc) * nc
        want = n_conv if n == len(units) - 1 else min(n_conv, n)
        while done_conv < want and (done_conv + 1) * ct <= xbc_cols_done:
            cols = slice(done_conv * ct, (done_conv + 1) * ct)
            for r0 in range(0, tm, rb):
                xc_ref[r0:r0 + rb, cols] = _conv_block(raw_ref[r0:r0 + rb + SUBLANES, cols],
                                                       cw_ref[:, cols], cb_ref[:, cols])
            done_conv += 1
    assert done_conv == n_conv
    conv_ref[0] = raw_ref[SUBLANES + tm - (CONV_W - 1):SUBLANES + tm, :]
    raw_ref[0:SUBLANES, :] = raw_ref[tm:tm + SUBLANES, :]


def _inproj_conv(h, g, w_cat, widths, conv_w, conv_b, *, tm, seq):
    t, d = h.shape
    tm = min(tm, seq)
    tiles_per_seq = seq // tm
    n_all = w_cat.shape[1]
    conv_dim = widths[1]
    return pl.pallas_call(
        functools.partial(_inproj_conv_kernel, widths=widths, nc=512, tiles_per_seq=tiles_per_seq),
        grid=(t // tm,),
        in_specs=[pl.BlockSpec((tm, d), lambda i: (i, 0)),
                  _const_spec((1, d)),
                  _const_spec((d, n_all)),
                  _const_spec((CONV_W, conv_dim)),
                  _const_spec((1, conv_dim))],
        out_specs=[pl.BlockSpec((tm, w), lambda i: (i, 0)) for w in widths]
        + [pl.BlockSpec((1, CONV_W - 1, conv_dim), lambda i: (i // tiles_per_seq, 0, 0))],
        out_shape=[jax.ShapeDtypeStruct((t, w), F32) for w in widths]
        + [jax.ShapeDtypeStruct((t // seq, CONV_W - 1, conv_dim), F32)],
        scratch_shapes=[pltpu.VMEM((tm, d), BF16), pltpu.VMEM((SUBLANES + tm, conv_dim), F32)],
        compiler_params=_params(1),
        name="inproj_conv",
    )(h, g, w_cat, conv_w, conv_b)


def _chunk_scalars(dt_raw, dt_bias, a_log, seg_len):
    q = dt_raw.shape[0]
    dt = _softplus(dt_raw + dt_bias)
    da = dt * (-jnp.exp(a_log))
    shift = int(math.log2(seg_len))
    qi = lax.broadcasted_iota(jnp.int32, (q, q), 0)
    si = lax.broadcasted_iota(jnp.int32, (q, q), 1)
    same = (qi >> shift) == (si >> shift)
    causal = jnp.logical_and(same, si <= qi)
    a_cs = jnp.dot(jnp.where(causal, 1.0, 0.0), da, precision=HIGHEST, preferred_element_type=F32)
    seg_tot = jnp.dot(jnp.where(same, 1.0, 0.0), da, precision=HIGHEST, preferred_element_type=F32)
    return dt, a_cs, seg_tot, causal


def _pair_cols(vals, h0, shape):
    lane = lax.broadcasted_iota(jnp.int32, shape, 1)
    return jnp.where(lane < HEAD_DIM, vals[:, h0:h0 + 1], vals[:, h0 + 1:h0 + 2])


def _diag_pair(xpair, cb, a_cs, a_cs_t, dt_t, causal, h0):
    lane = lax.broadcasted_iota(jnp.int32, xpair.shape, 1)
    out = None
    for k in range(HEADS_PER_TILE):
        h = h0 + k
        diff = a_cs[:, h:h + 1] - a_cs_t[h:h + 1, :]
        decay = jnp.exp(jnp.where(causal, diff, -jnp.inf))
        w = (cb * decay * dt_t[h:h + 1, :]).astype(BF16)
        mine = (lane < HEAD_DIM) if k == 0 else (lane >= HEAD_DIM)
        t = _dot(w, jnp.where(mine, xpair, 0.0).astype(BF16))
        out = t if out is None else out + t
    return out


def _conv_block(ext, w, bias):
    rows, width = ext.shape
    tiles = ext.reshape(rows // SUBLANES, SUBLANES, width)
    sub = lax.broadcasted_iota(jnp.int32, (1, SUBLANES, width), 1)
    tap = lambda k: w[k:k + 1, :][None]
    acc = tiles[1:] * tap(CONV_W - 1)
    for s in range(1, CONV_W):
        rot = pltpu.roll(tiles, s, axis=1)
        acc = acc + jnp.where(sub < s, rot[:-1], rot[1:]) * tap(CONV_W - 1 - s)
    return _silu(bias + acc.reshape(rows - SUBLANES, width))


def _conv_tile_rows(rows_of_tap, w, bias):
    acc = rows_of_tap(0) * w[0:1, :]
    for k in range(1, CONV_W):
        acc = acc + rows_of_tap(k) * w[k:k + 1, :]
    return _silu(bias + acc)


def _mixer_prompt_kernel(xc_ref, dt_ref, v_ref, dtb_ref, alog_ref, dskip_ref,
                         y_ref, pooled_ref, ssm_ref, pool_ref,
                         h_ref, pe_ref, ps_ref, *, d_inner, n_chunks):
    c = pl.program_id(1)
    q = xc_ref.shape[0]
    pool_dim = v_ref.shape[1]
    n_state = D_STATE
    last = c == n_chunks - 1

    @pl.when(c == 0)
    def _():
        h_ref[...] = jnp.zeros(h_ref.shape, F32)
        pe_ref[0:POOL_PAD, :] = jnp.zeros((POOL_PAD, pool_dim), F32)
        ps_ref[:, 0:SUBLANES, :] = jnp.zeros((2, SUBLANES, ps_ref.shape[2]), F32)

    dt, a_cs, seg_tot, causal = _chunk_scalars(dt_ref[...], dtb_ref[...], alog_ref[...], q)
    a_cs_t = a_cs.T
    dt_t = dt.T
    w_state = jnp.exp(seg_tot - a_cs) * dt
    e_in = jnp.exp(a_cs)
    chunk_decay = jnp.exp(seg_tot[0:1, :])

    row128 = lax.broadcasted_iota(jnp.int32, (LANES, n_state), 0)
    for g in range(N_GROUPS):
        b_g = xc_ref[:, d_inner + g * n_state:d_inner + (g + 1) * n_state].astype(BF16)
        c_g = xc_ref[:, d_inner + (N_GROUPS + g) * n_state:
                     d_inner + (N_GROUPS + g + 1) * n_state].astype(BF16)
        cb = _dot_nt(c_g, b_g)
        tiles_per_group = d_inner // N_GROUPS // LANES
        for p in range(tiles_per_group):
            tile = g * tiles_per_group + p
            h0 = tile * HEADS_PER_TILE
            cols = slice(tile * LANES, (tile + 1) * LANES)
            xpair = xc_ref[:, cols]
            h_prev = h_ref[cols, :]
            y = _diag_pair(xpair, cb, a_cs, a_cs_t, dt_t, causal, h0)
            y = y + _dot_nt(c_g, h_prev.astype(BF16)) * _pair_cols(e_in, h0, xpair.shape)
            y_ref[:, cols] = y + dskip_ref[:, cols] * xpair
            xw_t = (xpair * _pair_cols(w_state, h0, xpair.shape)).T.astype(BF16)
            decay_rows = jnp.where(row128 < HEAD_DIM, chunk_decay[:, h0:h0 + 1],
                                   chunk_decay[:, h0 + 1:h0 + 2])
            h_ref[cols, :] = decay_rows * h_prev + _dot(xw_t, b_g)

    @pl.when(last)
    def _():
        ssm_ref[0] = h_ref[...]

    pe_ref[POOL_PAD:POOL_PAD + q, :] = v_ref[...]
    gc = pool_dim // len(POOL_WINDOWS)
    rows = q + POOL_PAD - SUBLANES
    t_idx = lax.broadcasted_iota(jnp.int32, (q, 1), 0) + c * q
    for gi, win in enumerate(POOL_WINDOWS):
        cols = slice(gi * gc, (gi + 1) * gc)
        shift = 1
        level = 0
        cur = None
        while shift < win:
            if level == 0:
                cur = (pe_ref[SUBLANES:SUBLANES + rows, cols]
                       + pe_ref[SUBLANES - shift:SUBLANES - shift + rows, cols])
            else:
                src = ps_ref.at[(level - 1) % 2]
                cur = (src[SUBLANES:SUBLANES + rows, :]
                       + src[SUBLANES - shift:SUBLANES - shift + rows, :])
            if shift * 2 < win:
                ps_ref[level % 2, SUBLANES:SUBLANES + rows, :] = cur
            shift *= 2
            level += 1
        total = cur[POOL_PAD - SUBLANES:, :]
        cnt = jnp.minimum(t_idx + 1, win).astype(F32)
        pooled_ref[:, cols] = total / cnt - v_ref[:, cols]
    pe_ref[POOL_PAD - POOL_BUF:POOL_PAD, :] = v_ref[q - POOL_BUF:q, :]

    @pl.when(last)
    def _():
        pool_ref[0] = v_ref[q - POOL_BUF:q, :]


def _mixer_prompt(xc, dt, v, dt_bias, a_log, d_skip, *, batch, seq, d_inner):
    q = CHUNK
    n_chunks = seq // q
    conv_dim = xc.shape[1]
    pool_dim = v.shape[1]
    gc = pool_dim // len(POOL_WINDOWS)
    n_rows_state = d_inner
    tok = lambda b, c: (b * n_chunks + c, 0)
    per_b = lambda b, c: (b, 0, 0)
    return pl.pallas_call(
        functools.partial(_mixer_prompt_kernel, d_inner=d_inner, n_chunks=n_chunks),
        grid=(batch, n_chunks),
        in_specs=[pl.BlockSpec((q, conv_dim), tok),
                  pl.BlockSpec((q, LANES), tok),
                  pl.BlockSpec((q, pool_dim), tok),
                  _const_spec((1, LANES)),
                  _const_spec((1, LANES)),
                  _const_spec((1, d_inner))],
        out_specs=[pl.BlockSpec((q, d_inner), tok),
                   pl.BlockSpec((q, pool_dim), tok),
                   pl.BlockSpec((1, n_rows_state, D_STATE), per_b),
                   pl.BlockSpec((1, POOL_BUF, pool_dim), per_b)],
        out_shape=[jax.ShapeDtypeStruct((batch * seq, d_inner), F32),
                   jax.ShapeDtypeStruct((batch * seq, pool_dim), F32),
                   jax.ShapeDtypeStruct((batch, n_rows_state, D_STATE), F32),
                   jax.ShapeDtypeStruct((batch, POOL_BUF, pool_dim), F32)],
        scratch_shapes=[pltpu.VMEM((n_rows_state, D_STATE), F32),
                        pltpu.VMEM((POOL_PAD + q, pool_dim), F32),
                        pltpu.VMEM((2, POOL_PAD + q, gc), F32)],
        compiler_params=_params(2),
        name="mixer_prompt",
    )(xc, dt, v, dt_bias, a_log, d_skip)


def _mixer_sample_kernel(xbc_ref, dt_ref, v_ref, conv0_ref, pool0_ref, ssm0_ref,
                         cw_ref, cb_ref, dtb_ref, alog_ref, dskip_ref,
                         y_ref, pooled_ref, ssm_ref, conv_ref, pool_ref,
                         head_ref, xc_ref, xwt_ref, e_ref, seg_ref, pe_ref,
                         *, d_inner, seq, pos0):
    j = pl.program_id(1)
    q = xbc_ref.shape[0]
    conv_dim = xbc_ref.shape[1]
    pool_dim = v_ref.shape[1]
    n_state = D_STATE
    n_seq = q // seq
    tiles_per_group = d_inner // N_GROUPS // LANES
    gc = pool_dim // len(POOL_WINDOWS)

    @pl.when(j == 0)
    def _():
        lo = SUBLANES - (CONV_W - 1)

        def per_seq(b, carry):
            r0 = pl.multiple_of(b * seq, seq)
            head_ref[lo:SUBLANES, :] = conv0_ref[b]
            ct = 1024
            for jc in range(conv_dim // ct):
                cols = slice(jc * ct, (jc + 1) * ct)
                head_ref[SUBLANES:SUBLANES + seq, cols] = xbc_ref[pl.ds(r0, seq), cols]
                xc_ref[pl.ds(r0, seq), cols] = _conv_tile_rows(
                    lambda k: head_ref[lo + k:lo + k + seq, cols], cw_ref[:, cols], cb_ref[:, cols])
            conv_ref[b] = head_ref[SUBLANES + seq - (CONV_W - 1):SUBLANES + seq, :]

            pe_ref[POOL_PAD - POOL_BUF:POOL_PAD, :] = pool0_ref[b]
            vb = v_ref[pl.ds(r0, seq), :]
            pe_ref[POOL_PAD:POOL_PAD + seq, :] = vb
            t_idx = lax.broadcasted_iota(jnp.int32, (seq, 1), 0) + pos0
            for gi, win in enumerate(POOL_WINDOWS):
                cols = slice(gi * gc, (gi + 1) * gc)
                total = pe_ref[POOL_PAD:POOL_PAD + seq, cols]
                for k in range(1, win):
                    total = total + pe_ref[POOL_PAD - k:POOL_PAD - k + seq, cols]
                cnt = jnp.minimum(t_idx + 1, win).astype(F32)
                pooled_ref[pl.ds(r0, seq), cols] = total / cnt - vb[:, cols]
            pool_ref[b] = pe_ref[POOL_PAD + seq - POOL_BUF:POOL_PAD + seq, :]
            return carry

        lax.fori_loop(0, n_seq, per_seq, 0)

        dt, a_cs, seg_tot, causal = _chunk_scalars(dt_ref[...], dtb_ref[...], alog_ref[...], seq)
        a_cs_t = a_cs.T
        dt_t = dt.T
        w_state = jnp.exp(seg_tot - a_cs) * dt
        e_ref[...] = jnp.exp(a_cs)
        seg_ref[...] = seg_tot
        for g in range(N_GROUPS):
            b_g = xc_ref[:, d_inner + g * n_state:d_inner + (g + 1) * n_state].astype(BF16)
            c_g = xc_ref[:, d_inner + (N_GROUPS + g) * n_state:
                         d_inner + (N_GROUPS + g + 1) * n_state].astype(BF16)
            cb = _dot_nt(c_g, b_g)
            for p in range(tiles_per_group):
                tile = g * tiles_per_group + p
                h0 = tile * HEADS_PER_TILE
                cols = slice(tile * LANES, (tile + 1) * LANES)
                xpair = xc_ref[:, cols]
                y = _diag_pair(xpair, cb, a_cs, a_cs_t, dt_t, causal, h0)
                y_ref[:, cols] = y + dskip_ref[:, cols] * xpair
                xwt_ref[cols, :] = (xpair * _pair_cols(w_state, h0, xpair.shape)).T.astype(BF16)

    r0 = pl.multiple_of(j * seq, seq)
    e_rows = e_ref[pl.ds(r0, seq), :]
    chunk_decay = jnp.exp(seg_ref[pl.ds(r0, 1), :])
    row_q = lax.broadcasted_iota(jnp.int32, (q, n_state), 0)
    mine = jnp.logical_and(row_q >= r0, row_q < r0 + seq)
    row128 = lax.broadcasted_iota(jnp.int32, (LANES, n_state), 0)
    for g in range(N_GROUPS):
        b_cols = slice(d_inner + g * n_state, d_inner + (g + 1) * n_state)
        c_cols = slice(d_inner + (N_GROUPS + g) * n_state, d_inner + (N_GROUPS + g + 1) * n_state)
        b_mine = jnp.where(mine, xc_ref[:, b_cols], 0.0).astype(BF16)
        c_rows = xc_ref[pl.ds(r0, seq), c_cols].astype(BF16)
        for p in range(tiles_per_group):
            tile = g * tiles_per_group + p
            h0 = tile * HEADS_PER_TILE
            cols = slice(tile * LANES, (tile + 1) * LANES)
            h_prev = ssm0_ref[0, cols, :]
            y_off = _dot_nt(c_rows, h_prev.astype(BF16)) * _pair_cols(e_rows, h0, (seq, LANES))
            y_ref[pl.ds(r0, seq), cols] += y_off
            decay_rows = jnp.where(row128 < HEAD_DIM, chunk_decay[:, h0:h0 + 1],
                                   chunk_decay[:, h0 + 1:h0 + 2])
            ssm_ref[0, cols, :] = decay_rows * h_prev + _dot(xwt_ref[cols, :], b_mine)


def _mixer_sample(xbc, dt, v, conv0, pool0, ssm0, conv_w, conv_b, dt_bias, a_log, d_skip,
                  *, batch, seq, d_inner, pos0):
    q = CHUNK
    n_seq = q // seq
    n_chunks = batch // n_seq
    conv_dim = xbc.shape[1]
    pool_dim = v.shape[1]
    tok = lambda i, j: (i, 0)
    per_chunk = lambda i, j: (i, 0, 0)
    per_seq = lambda i, j: (i * n_seq + j, 0, 0)
    return pl.pallas_call(
        functools.partial(_mixer_sample_kernel, d_inner=d_inner, seq=seq, pos0=pos0),
        grid=(n_chunks, n_seq),
        in_specs=[pl.BlockSpec((q, conv_dim), tok),
                  pl.BlockSpec((q, LANES), tok),
                  pl.BlockSpec((q, pool_dim), tok),
                  pl.BlockSpec((n_seq, CONV_W - 1, conv_dim), per_chunk),
                  pl.BlockSpec((n_seq, POOL_BUF, pool_dim), per_chunk),
                  pl.BlockSpec((1, d_inner, D_STATE), per_seq),
                  _const_spec((CONV_W, conv_dim)),
                  _const_spec((1, conv_dim)),
                  _const_spec((1, LANES)),
                  _const_spec((1, LANES)),
                  _const_spec((1, d_inner))],
        out_specs=[pl.BlockSpec((q, d_inner), tok),
                   pl.BlockSpec((q, pool_dim), tok),
                   pl.BlockSpec((1, d_inner, D_STATE), per_seq),
                   pl.BlockSpec((n_seq, CONV_W - 1, conv_dim), per_chunk),
                   pl.BlockSpec((n_seq, POOL_BUF, pool_dim), per_chunk)],
        out_shape=[jax.ShapeDtypeStruct((batch * seq, d_inner), F32),
                   jax.ShapeDtypeStruct((batch * seq, pool_dim), F32),
                   jax.ShapeDtypeStruct((batch, d_inner, D_STATE), F32),
                   jax.ShapeDtypeStruct((batch, CONV_W - 1, conv_dim), F32),
                   jax.ShapeDtypeStruct((batch, POOL_BUF, pool_dim), F32)],
        scratch_shapes=[pltpu.VMEM((2 * SUBLANES, conv_dim), F32),
                        pltpu.VMEM((q, conv_dim), F32),
                        pltpu.VMEM((d_inner, q), BF16),
                        pltpu.VMEM((q, LANES), F32),
                        pltpu.VMEM((q, LANES), F32),
                        pltpu.VMEM((POOL_PAD + seq, pool_dim), F32)],
        compiler_params=_params(2),
        name="mixer_sample",
    )(xbc, dt, v, conv0, pool0, ssm0, conv_w, conv_b, dt_bias, a_log, d_skip)


def _merge_kernel(h_ref, y_ref, z_ref, pooled_ref, gates_ref, nssd_ref, wssd_ref, wpg_ref,
                  pscale_ref, wpo_ref, wo_ref, o_ref, yn_ref, pm_ref):
    d_inner = y_ref.shape[1]
    d_model = h_ref.shape[1]
    gw = d_inner // N_GROUPS
    for g in range(N_GROUPS):
        cols = slice(g * gw, (g + 1) * gw)
        yz = y_ref[:, cols] * _silu(z_ref[:, cols])
        yn_ref[:, cols] = _rmsnorm(yz, nssd_ref[:, cols]).astype(BF16)
    a_branch = _dot(yn_ref[...], wssd_ref[...])
    gc = pooled_ref.shape[1] // len(POOL_WINDOWS)
    for gi in range(len(POOL_WINDOWS)):
        cols = slice(gi * gc, (gi + 1) * gc)
        mixed = _dot(pooled_ref[:, cols].astype(BF16), wpg_ref[gi])
        pm_ref[:, cols] = (mixed * pscale_ref[:, cols]).astype(BF16)
    b_branch = _dot(pm_ref[...], wpo_ref[...])
    merged = (_sigmoid(gates_ref[:, 0:d_model]) * a_branch
              + _sigmoid(gates_ref[:, d_model:2 * d_model]) * b_branch)
    o_ref[...] = h_ref[...] + _dot(merged.astype(BF16), wo_ref[...])


def _merge(h, y, z, pooled, gates, norm_ssd, w_ssd_out, w_pool_group, pool_scale, w_pool_out, w_o,
           *, tm):
    t, d = h.shape
    tm = min(tm, t)
    d_inner = y.shape[1]
    pool_dim = pooled.shape[1]
    row = lambda w: pl.BlockSpec((tm, w), lambda i: (i, 0))
    return pl.pallas_call(
        _merge_kernel,
        grid=(t // tm,),
        in_specs=[row(d), row(d_inner), row(d_inner), row(pool_dim), row(N_BRANCH * d),
                  _const_spec((1, d_inner)),
                  _const_spec(w_ssd_out.shape),
                  _const_spec(w_pool_group.shape),
                  _const_spec((1, pool_dim)),
                  _const_spec(w_pool_out.shape),
                  _const_spec(w_o.shape)],
        out_specs=row(d),
        out_shape=jax.ShapeDtypeStruct((t, d), F32),
        scratch_shapes=[pltpu.VMEM((tm, d_inner), BF16), pltpu.VMEM((tm, pool_dim), BF16)],
        compiler_params=_params(1),
        name="merge",
    )(h, y, z, pooled, gates, norm_ssd, w_ssd_out, w_pool_group, pool_scale, w_pool_out, w_o)


def _ple_kernel(h_ref, p_ref, nple_ref, wg_ref, wp_ref, nfin_ref, o_ref):
    h = h_ref[...]
    gate = _sigmoid(_dot(_rmsnorm(h, nple_ref[...]).astype(BF16), wg_ref[...]))
    h = h + gate * _dot(p_ref[...].astype(BF16), wp_ref[...])
    o_ref[...] = _rmsnorm(h, nfin_ref[...])


def _ple(h, p, norm_ple, w_ple_gate, w_ple, norm_final, *, tm):
    t, d = h.shape
    tm = min(tm, t)
    pd = p.shape[1]
    return pl.pallas_call(
        _ple_kernel,
        grid=(t // tm,),
        in_specs=[pl.BlockSpec((tm, d), lambda i: (i, 0)),
                  pl.BlockSpec((tm, pd), lambda i: (i, 0)),
                  _const_spec((1, d)),
                  _const_spec(w_ple_gate.shape),
                  _const_spec(w_ple.shape),
                  _const_spec((1, d))],
        out_specs=pl.BlockSpec((tm, d), lambda i: (i, 0)),
        out_shape=jax.ShapeDtypeStruct((t, d), F32),
        compiler_params=_params(1),
        name="ple",
    )(h, p, norm_ple, w_ple_gate, w_ple, norm_final)


def _pad_lanes(a):
    return jnp.pad(a.reshape(1, -1), ((0, 0), (0, LANES - a.shape[-1])))


def kernel(x_prompt, x_sample, state_ssm, state_conv, state_pool, p_prompt, p_sample, norm_ffn1, w_ffn1_gu, w_ffn1_down, norm_mix, w_in, conv_w, conv_b, dt_bias, a_log, d_skip, norm_ssd, w_ssd_out, w_pool_group, pool_scale, w_pool_out, w_o, norm_ffn2, w_ffn2_gu, w_ffn2_down, norm_ple, w_ple_gate, w_ple, norm_final):
    depth = norm_ffn1.shape[0]
    assert depth == 1, "the final norm is fused into the layer's last stage: one layer only"
    batch, seq, d_model = x_prompt.shape
    dec_batch, dec_seq, _ = x_sample.shape
    n_heads = dt_bias.shape[1]
    d_inner = n_heads * HEAD_DIM
    conv_dim = conv_w.shape[2]
    pool_dim = pool_scale.shape[1]
    assert seq % CHUNK == 0 and CHUNK % dec_seq == 0 and dec_batch % (CHUNK // dec_seq) == 0

    row = lambda a: a[0].reshape(1, -1)
    cuts = [0, d_inner, d_inner + conv_dim, d_inner + conv_dim + n_heads,
            d_inner + conv_dim + n_heads + pool_dim, w_in.shape[2]]
    wz, wxbc, wdt, wv, wgates = [w_in[0][:, a:b] for a, b in zip(cuts[:-1], cuts[1:])]
    wdt = jnp.pad(wdt, ((0, 0), (0, LANES - n_heads)))
    w_cat = jnp.concatenate([wz, wxbc, wv, wgates, wdt], axis=1).astype(BF16)
    widths = (d_inner, conv_dim, pool_dim, N_BRANCH * d_model, LANES)
    w1gu, w1d = w_ffn1_gu[0].astype(BF16), w_ffn1_down[0].astype(BF16)
    w2gu, w2d = w_ffn2_gu[0].astype(BF16), w_ffn2_down[0].astype(BF16)
    wssd, wpg = w_ssd_out[0].astype(BF16), w_pool_group[0].astype(BF16)
    wpo, wo = w_pool_out[0].astype(BF16), w_o[0].astype(BF16)
    wpleg, wple = w_ple_gate[0].astype(BF16), w_ple[0].astype(BF16)
    dtb, alog = _pad_lanes(dt_bias[0]), _pad_lanes(a_log[0])
    dskip = jnp.repeat(d_skip[0], HEAD_DIM).reshape(1, d_inner)
    cb = row(conv_b)
    nfin = norm_final.reshape(1, -1)

    def pre(h):
        h1 = _ffn(h, row(norm_ffn1), w1gu, w1d, tm=512)
        return (h1,) + tuple(_inproj(h1, row(norm_mix), w_cat, widths, tm=256))

    def post(h1, y, z, pooled, gates, p):
        h2 = _merge(h1, y, z, pooled, gates, row(norm_ssd), wssd, wpg, row(pool_scale), wpo, wo,
                    tm=256)
        h3 = _ffn(h2, row(norm_ffn2), w2gu, w2d, tm=512)
        return _ple(h3, p, row(norm_ple), wpleg, wple, nfin, tm=512)

    h1 = _ffn(x_prompt.reshape(batch * seq, d_model), row(norm_ffn1), w1gu, w1d, tm=512)
    z, xc, v, gates, dt, c1 = _inproj_conv(h1, row(norm_mix), w_cat, widths, conv_w[0], cb,
                                           tm=256, seq=seq)
    y, pooled, s1, q1 = _mixer_prompt(xc, dt, v, dtb, alog, dskip,
                                      batch=batch, seq=seq, d_inner=d_inner)
    y_prompt = post(h1, y, z, pooled, gates, p_prompt[0].reshape(batch * seq, -1))

    h1, z, xbc, v, gates, dt = pre(x_sample.reshape(dec_batch * dec_seq, d_model))
    y, pooled, s2, c2, q2 = _mixer_sample(
        xbc, dt, v, state_conv[0], state_pool[0], state_ssm[0].reshape(dec_batch, d_inner, D_STATE),
        conv_w[0], cb, dtb, alog, dskip,
        batch=dec_batch, seq=dec_seq, d_inner=d_inner, pos0=PAST_LEN)
    y_sample = post(h1, y, z, pooled, gates, p_sample[0].reshape(dec_batch * dec_seq, -1))

    return (y_prompt.reshape(batch, seq, d_model),
            y_sample.reshape(dec_batch, dec_seq, d_model),
            s1.reshape(1, batch, n_heads, HEAD_DIM, D_STATE), c1[None], q1[None],
            s2.reshape(1, dec_batch, n_heads, HEAD_DIM, D_STATE), c2[None], q2[None])
```

```python
import functools
import math

import jax
import jax.numpy as jnp
from jax import lax
from jax.experimental import pallas as pl
from jax.experimental.pallas import tpu as pltpu

F32 = jnp.float32
BF16 = jnp.bfloat16
EPS = 1e-6
HIGHEST = lax.Precision.HIGHEST
NEG_LOG2E = -1.4426950408889634

LANES = 128
SUBLANES = 8
VMEM_LIMIT_BYTES = 56 * 1024 * 1024

HEAD_DIM = 64
N_GROUPS = 8
D_STATE = 128
CONV_W = 4
CHUNK = 128
POOL_WINDOWS = (2, 4, 8, 16)
POOL_BUF = max(POOL_WINDOWS) - 1
N_BRANCH = 2
PAST_LEN = 16384
HEADS_PER_TILE = LANES // HEAD_DIM
POOL_PAD = 24


def _sigmoid(x):
    return 1.0 / (1.0 + jnp.exp2(x * NEG_LOG2E))


def _silu(x):
    return x * _sigmoid(x)


def _softplus(x):
    return jnp.maximum(x, 0.0) + jnp.log(1.0 + jnp.exp(-jnp.abs(x)))


def _rmsnorm(x, g):
    return x * lax.rsqrt(jnp.mean(x * x, axis=-1, keepdims=True) + EPS) * g


def _dot(a, b):
    return jnp.dot(a, b, preferred_element_type=F32)


def _dot_nt(a, b):
    return lax.dot_general(a, b, (((1,), (1,)), ((), ())), preferred_element_type=F32)


def _const_spec(shape):
    nd = len(shape)
    return pl.BlockSpec(shape, lambda *_: (0,) * nd, pipeline_mode=pl.Buffered(1))


def _params(n_axes):
    return pltpu.CompilerParams(dimension_semantics=("arbitrary",) * n_axes,
                                vmem_limit_bytes=VMEM_LIMIT_BYTES)


def _ffn_kernel(x_ref, g_ref, wgu_ref, wd_ref, o_ref, xn_ref, acc_ref, *, d_ff, fc):
    xn_ref[...] = _rmsnorm(x_ref[...], g_ref[...]).astype(BF16)
    for c in range(d_ff // fc):
        xn = xn_ref[...]
        gate = _dot(xn, wgu_ref[:, c * fc:(c + 1) * fc])
        up = _dot(xn, wgu_ref[:, d_ff + c * fc:d_ff + (c + 1) * fc])
        act = (_silu(gate) * up).astype(BF16)
        contrib = _dot(act, wd_ref[c * fc:(c + 1) * fc, :])
        if c == 0:
            acc_ref[...] = contrib
        else:
            acc_ref[...] += contrib
    o_ref[...] = x_ref[...] + 0.5 * acc_ref[...]


def _ffn(x, g, w_gu, w_down, *, tm):
    t, d = x.shape
    d_ff = w_down.shape[0]
    tm = min(tm, t)
    fc = 256
    return pl.pallas_call(
        functools.partial(_ffn_kernel, d_ff=d_ff, fc=fc),
        grid=(t // tm,),
        in_specs=[pl.BlockSpec((tm, d), lambda i: (i, 0)),
                  _const_spec((1, d)),
                  _const_spec((d, 2 * d_ff)),
                  _const_spec((d_ff, d))],
        out_specs=pl.BlockSpec((tm, d), lambda i: (i, 0)),
        out_shape=jax.ShapeDtypeStruct((t, d), F32),
        scratch_shapes=[pltpu.VMEM((tm, d), BF16), pltpu.VMEM((tm, d), F32)],
        compiler_params=_params(1),
        name="ffn",
    )(x, g, w_gu, w_down)


def _inproj_kernel(h_ref, g_ref, w_ref, *rest, widths, nc):
    out_refs, u_ref = rest[:-1], rest[-1]
    u_ref[...] = _rmsnorm(h_ref[...], g_ref[...]).astype(BF16)
    start = 0
    for o_ref, width in zip(out_refs, widths):
        for a in range(0, width, nc):
            b = min(a + nc, width)
            o_ref[:, a:b] = _dot(u_ref[...], w_ref[:, start + a:start + b])
        start += width


def _inproj(h, g, w_cat, widths, *, tm):
    t, d = h.shape
    tm = min(tm, t)
    n_all = w_cat.shape[1]
    return pl.pallas_call(
        functools.partial(_inproj_kernel, widths=widths, nc=1024),
        grid=(t // tm,),
        in_specs=[pl.BlockSpec((tm, d), lambda i: (i, 0)),
                  _const_spec((1, d)),
                  _const_spec((d, n_all))],
        out_specs=[pl.BlockSpec((tm, w), lambda i: (i, 0)) for w in widths],
        out_shape=[jax.ShapeDtypeStruct((t, w), F32) for w in widths],
        scratch_shapes=[pltpu.VMEM((tm, d), BF16)],
        compiler_params=_params(1),
        name="inproj",
    )(h, g, w_cat)


def _inproj_conv_kernel(h_ref, g_ref, w_ref, cw_ref, cb_ref,
                        z_ref, xc_ref, v_ref, gates_ref, dt_ref, conv_ref,
                        u_ref, raw_ref, *, widths, nc, tiles_per_seq):
    i = pl.program_id(0)
    tm = h_ref.shape[0]
    conv_dim = xc_ref.shape[1]

    @pl.when(i % tiles_per_seq == 0)
    def _():
        raw_ref[0:SUBLANES, :] = jnp.zeros((SUBLANES, conv_dim), F32)

    u_ref[...] = _rmsnorm(h_ref[...], g_ref[...]).astype(BF16)
    starts = [sum(widths[:k]) for k in range(len(widths))]

    outs = {0: (z_ref, 0), 1: (raw_ref, SUBLANES), 2: (v_ref, 0), 3: (gates_ref, 0), 4: (dt_ref, 0)}
    units = [(k, a) for k in (1, 0, 2, 3, 4) for a in range(0, widths[k], nc)]
    lo = SUBLANES - (CONV_W - 1)
    ct = 256
    rb = 64
    n_conv = conv_dim // ct
    done_conv = 0
    for n, (k, a) in enumerate(units):
        b = min(a + nc, widths[k])
        o_ref, row0 = outs[k]
        o_ref[row0:row0 + tm, a:b] = _dot(u_ref[...], w_ref[:, starts[k] + a:starts[k] + b])
        xbc_cols_done = min(n + 1, widths[1] // nc) * nc
        want = n_conv if n == len(units) - 1 else min(n_conv, n)
        while done_conv < want and (done_conv + 1) * ct <= xbc_cols_done:
            cols = slice(done_conv * ct, (done_conv + 1) * ct)
            for r0 in range(0, tm, rb):
                xc_ref[r0:r0 + rb, cols] = _conv_block(raw_ref[r0:r0 + rb + SUBLANES, cols],
                                                       cw_ref[:, cols], cb_ref[:, cols])
            done_conv += 1
    assert done_conv == n_conv
    conv_ref[0] = raw_ref[SUBLANES + tm - (CONV_W - 1):SUBLANES + tm, :]
    raw_ref[0:SUBLANES, :] = raw_ref[tm:tm + SUBLANES, :]


def _inproj_conv(h, g, w_cat, widths, conv_w, conv_b, *, tm, seq):
    t, d = h.shape
    tm = min(tm, seq)
    tiles_per_seq = seq // tm
    n_all = w_cat.shape[1]
    conv_dim = widths[1]
    return pl.pallas_call(
        functools.partial(_inproj_conv_kernel, widths=widths, nc=512, tiles_per_seq=tiles_per_seq),
        grid=(t // tm,),
        in_specs=[pl.BlockSpec((tm, d), lambda i: (i, 0)),
                  _const_spec((1, d)),
                  _const_spec((d, n_all)),
                  _const_spec((CONV_W, conv_dim)),
                  _const_spec((1, conv_dim))],
        out_specs=[pl.BlockSpec((tm, w), lambda i: (i, 0)) for w in widths]
        + [pl.BlockSpec((1, CONV_W - 1, conv_dim), lambda i: (i // tiles_per_seq, 0, 0))],
        out_shape=[jax.ShapeDtypeStruct((t, w), F32) for w in widths]
        + [jax.ShapeDtypeStruct((t // seq, CONV_W - 1, conv_dim), F32)],
        scratch_shapes=[pltpu.VMEM((tm, d), BF16), pltpu.VMEM((SUBLANES + tm, conv_dim), F32)],
        compiler_params=_params(1),
        name="inproj_conv",
    )(h, g, w_cat, conv_w, conv_b)


def _chunk_scalars(dt_raw, dt_bias, a_log, seg_len):
    q = dt_raw.shape[0]
    dt = _softplus(dt_raw + dt_bias)
    da = dt * (-jnp.exp(a_log))
    shift = int(math.log2(seg_len))
    qi = lax.broadcasted_iota(jnp.int32, (q, q), 0)
    si = lax.broadcasted_iota(jnp.int32, (q, q), 1)
    same = (qi >> shift) == (si >> shift)
    causal = jnp.logical_and(same, si <= qi)
    a_cs = jnp.dot(jnp.where(causal, 1.0, 0.0), da, precision=HIGHEST, preferred_element_type=F32)
    seg_tot = jnp.dot(jnp.where(same, 1.0, 0.0), da, precision=HIGHEST, preferred_element_type=F32)
    return dt, a_cs, seg_tot, causal


def _pair_cols(vals, h0, shape):
    lane = lax.broadcasted_iota(jnp.int32, shape, 1)
    return jnp.where(lane < HEAD_DIM, vals[:, h0:h0 + 1], vals[:, h0 + 1:h0 + 2])


def _pair_rows(vals_t, h0, shape):
    row = lax.broadcasted_iota(jnp.int32, shape, 0)
    return jnp.where(row < HEAD_DIM, vals_t[h0:h0 + 1, :], vals_t[h0 + 1:h0 + 2, :])


def _diag_pair(xpair, cb, a_cs, a_cs_t, dt_t, causal, h0):
    q = xpair.shape[0]
    lane = lax.broadcasted_iota(jnp.int32, xpair.shape, 1)
    out = None
    spread = []
    for k in range(HEADS_PER_TILE):
        h = h0 + k
        a_col = jnp.broadcast_to(a_cs[:, h:h + 1], (q, LANES))
        spread.append(a_col)
        decay = jnp.exp(jnp.where(causal, a_col - a_cs_t[h:h + 1, :], -jnp.inf))
        w = (cb * decay * dt_t[h:h + 1, :]).astype(BF16)
        mine = (lane < HEAD_DIM) if k == 0 else (lane >= HEAD_DIM)
        t = _dot(w, jnp.where(mine, xpair, 0.0).astype(BF16))
        out = t if out is None else out + t
    e_pair = jnp.exp(jnp.where(lane < HEAD_DIM, spread[0], spread[1]))
    return out, e_pair


def _conv_block(ext, w, bias):
    rows, width = ext.shape
    tiles = ext.reshape(rows // SUBLANES, SUBLANES, width)
    sub = lax.broadcasted_iota(jnp.int32, (1, SUBLANES, width), 1)
    tap = lambda k: w[k:k + 1, :][None]
    acc = tiles[1:] * tap(CONV_W - 1)
    for s in range(1, CONV_W):
        rot = pltpu.roll(tiles, s, axis=1)
        acc = acc + jnp.where(sub < s, rot[:-1], rot[1:]) * tap(CONV_W - 1 - s)
    return _silu(bias + acc.reshape(rows - SUBLANES, width))


def _conv_tile_rows(rows_of_tap, w, bias):
    acc = rows_of_tap(0) * w[0:1, :]
    for k in range(1, CONV_W):
        acc = acc + rows_of_tap(k) * w[k:k + 1, :]
    return _silu(bias + acc)


def _mixer_prompt_kernel(xc_ref, dt_ref, v_ref, dtb_ref, alog_ref, dskip_ref,
                         y_ref, pooled_ref, ssm_ref, pool_ref,
                         h_ref, pe_ref, ps_ref, *, d_inner, n_chunks):
    c = pl.program_id(1)
    q = xc_ref.shape[0]
    pool_dim = v_ref.shape[1]
    n_state = D_STATE
    last = c == n_chunks - 1

    @pl.when(c == 0)
    def _():
        h_ref[...] = jnp.zeros(h_ref.shape, F32)
        pe_ref[0:POOL_PAD, :] = jnp.zeros((POOL_PAD, pool_dim), F32)
        ps_ref[:, 0:SUBLANES, :] = jnp.zeros((2, SUBLANES, ps_ref.shape[2]), F32)

    dt, a_cs, seg_tot, causal = _chunk_scalars(dt_ref[...], dtb_ref[...], alog_ref[...], q)
    assert q == LANES and n_state == LANES
    a_cs_t = a_cs.T
    dt_t = dt.T
    w_state_t = (jnp.exp(seg_tot - a_cs) * dt).T
    chunk_decay_t = jnp.exp(seg_tot).T

    for g in range(N_GROUPS):
        b_g = xc_ref[:, d_inner + g * n_state:d_inner + (g + 1) * n_state].astype(BF16)
        c_g = xc_ref[:, d_inner + (N_GROUPS + g) * n_state:
                     d_inner + (N_GROUPS + g + 1) * n_state].astype(BF16)
        cb = _dot_nt(c_g, b_g)
        tiles_per_group = d_inner // N_GROUPS // LANES
        for p in range(tiles_per_group):
            tile = g * tiles_per_group + p
            h0 = tile * HEADS_PER_TILE
            cols = slice(tile * LANES, (tile + 1) * LANES)
            xpair = xc_ref[:, cols]
            h_prev = h_ref[cols, :]
            y, e_pair = _diag_pair(xpair, cb, a_cs, a_cs_t, dt_t, causal, h0)
            y = y + _dot_nt(c_g, h_prev.astype(BF16)) * e_pair
            y_ref[:, cols] = y + dskip_ref[:, cols] * xpair
            xw_t = (xpair.T * _pair_rows(w_state_t, h0, (LANES, q))).astype(BF16)
            decay_rows = _pair_rows(chunk_decay_t, h0, (LANES, n_state))
            h_ref[cols, :] = decay_rows * h_prev + _dot(xw_t, b_g)

    @pl.when(last)
    def _():
        ssm_ref[0] = h_ref[...]

    pe_ref[POOL_PAD:POOL_PAD + q, :] = v_ref[...]
    gc = pool_dim // len(POOL_WINDOWS)
    rows = q + POOL_PAD - SUBLANES
    t_idx = lax.broadcasted_iota(jnp.int32, (q, 1), 0) + c * q
    for gi, win in enumerate(POOL_WINDOWS):
        cols = slice(gi * gc, (gi + 1) * gc)
        shift = 1
        level = 0
        cur = None
        while shift < win:
            if level == 0:
                cur = (pe_ref[SUBLANES:SUBLANES + rows, cols]
                       + pe_ref[SUBLANES - shift:SUBLANES - shift + rows, cols])
            else:
                src = ps_ref.at[(level - 1) % 2]
                cur = (src[SUBLANES:SUBLANES + rows, :]
                       + src[SUBLANES - shift:SUBLANES - shift + rows, :])
            if shift * 2 < win:
                ps_ref[level % 2, SUBLANES:SUBLANES + rows, :] = cur
            shift *= 2
            level += 1
        total = cur[POOL_PAD - SUBLANES:, :]
        cnt = jnp.minimum(t_idx + 1, win).astype(F32)
        pooled_ref[:, cols] = total / cnt - v_ref[:, cols]
    pe_ref[POOL_PAD - POOL_BUF:POOL_PAD, :] = v_ref[q - POOL_BUF:q, :]

    @pl.when(last)
    def _():
        pool_ref[0] = v_ref[q - POOL_BUF:q, :]


def _mixer_prompt(xc, dt, v, dt_bias, a_log, d_skip, *, batch, seq, d_inner):
    q = CHUNK
    n_chunks = seq // q
    conv_dim = xc.shape[1]
    pool_dim = v.shape[1]
    gc = pool_dim // len(POOL_WINDOWS)
    n_rows_state = d_inner
    tok = lambda b, c: (b * n_chunks + c, 0)
    per_b = lambda b, c: (b, 0, 0)
    return pl.pallas_call(
        functools.partial(_mixer_prompt_kernel, d_inner=d_inner, n_chunks=n_chunks),
        grid=(batch, n_chunks),
        in_specs=[pl.BlockSpec((q, conv_dim), tok),
                  pl.BlockSpec((q, LANES), tok),
                  pl.BlockSpec((q, pool_dim), tok),
                  _const_spec((1, LANES)),
                  _const_spec((1, LANES)),
                  _const_spec((1, d_inner))],
        out_specs=[pl.BlockSpec((q, d_inner), tok),
                   pl.BlockSpec((q, pool_dim), tok),
                   pl.BlockSpec((1, n_rows_state, D_STATE), per_b),
                   pl.BlockSpec((1, POOL_BUF, pool_dim), per_b)],
        out_shape=[jax.ShapeDtypeStruct((batch * seq, d_inner), F32),
                   jax.ShapeDtypeStruct((batch * seq, pool_dim), F32),
                   jax.ShapeDtypeStruct((batch, n_rows_state, D_STATE), F32),
                   jax.ShapeDtypeStruct((batch, POOL_BUF, pool_dim), F32)],
        scratch_shapes=[pltpu.VMEM((n_rows_state, D_STATE), F32),
                        pltpu.VMEM((POOL_PAD + q, pool_dim), F32),
                        pltpu.VMEM((2, POOL_PAD + q, gc), F32)],
        compiler_params=_params(2),
        name="mixer_prompt",
    )(xc, dt, v, dt_bias, a_log, d_skip)


def _mixer_sample_kernel(xbc_ref, dt_ref, v_ref, conv0_ref, pool0_ref, ssm0_ref,
                         cw_ref, cb_ref, dtb_ref, alog_ref, dskip_ref,
                         y_ref, pooled_ref, ssm_ref, conv_ref, pool_ref,
                         head_ref, xc_ref, xwt_ref, e_ref, seg_ref, pe_ref,
                         *, d_inner, seq, pos0, seqs_per_step):
    j = pl.program_id(1)
    q = xbc_ref.shape[0]
    conv_dim = xbc_ref.shape[1]
    pool_dim = v_ref.shape[1]
    n_state = D_STATE
    n_seq = q // seq
    tiles_per_group = d_inner // N_GROUPS // LANES
    gc = pool_dim // len(POOL_WINDOWS)

    @pl.when(j == 0)
    def _():
        lo = SUBLANES - (CONV_W - 1)

        def per_seq(b, carry):
            r0 = pl.multiple_of(b * seq, seq)
            head_ref[lo:SUBLANES, :] = conv0_ref[b]
            ct = 1024
            for jc in range(conv_dim // ct):
                cols = slice(jc * ct, (jc + 1) * ct)
                head_ref[SUBLANES:SUBLANES + seq, cols] = xbc_ref[pl.ds(r0, seq), cols]
                xc_ref[pl.ds(r0, seq), cols] = _conv_tile_rows(
                    lambda k: head_ref[lo + k:lo + k + seq, cols], cw_ref[:, cols], cb_ref[:, cols])
            conv_ref[b] = head_ref[SUBLANES + seq - (CONV_W - 1):SUBLANES + seq, :]

            pe_ref[POOL_PAD - POOL_BUF:POOL_PAD, :] = pool0_ref[b]
            vb = v_ref[pl.ds(r0, seq), :]
            pe_ref[POOL_PAD:POOL_PAD + seq, :] = vb
            t_idx = lax.broadcasted_iota(jnp.int32, (seq, 1), 0) + pos0
            for gi, win in enumerate(POOL_WINDOWS):
                cols = slice(gi * gc, (gi + 1) * gc)
                total = pe_ref[POOL_PAD:POOL_PAD + seq, cols]
                for k in range(1, win):
                    total = total + pe_ref[POOL_PAD - k:POOL_PAD - k + seq, cols]
                cnt = jnp.minimum(t_idx + 1, win).astype(F32)
                pooled_ref[pl.ds(r0, seq), cols] = total / cnt - vb[:, cols]
            pool_ref[b] = pe_ref[POOL_PAD + seq - POOL_BUF:POOL_PAD + seq, :]
            return carry

        lax.fori_loop(0, n_seq, per_seq, 0)

        dt, a_cs, seg_tot, causal = _chunk_scalars(dt_ref[...], dtb_ref[...], alog_ref[...], seq)
        a_cs_t = a_cs.T
        dt_t = dt.T
        w_state_t = (jnp.exp(seg_tot - a_cs) * dt).T
        seg_ref[...] = seg_tot
        for g in range(N_GROUPS):
            b_g = xc_ref[:, d_inner + g * n_state:d_inner + (g + 1) * n_state].astype(BF16)
            c_g = xc_ref[:, d_inner + (N_GROUPS + g) * n_state:
                         d_inner + (N_GROUPS + g + 1) * n_state].astype(BF16)
            cb = _dot_nt(c_g, b_g)
            for p in range(tiles_per_group):
                tile = g * tiles_per_group + p
                h0 = tile * HEADS_PER_TILE
                cols = slice(tile * LANES, (tile + 1) * LANES)
                xpair = xc_ref[:, cols]
                y, e_pair = _diag_pair(xpair, cb, a_cs, a_cs_t, dt_t, causal, h0)
                y_ref[:, cols] = y + dskip_ref[:, cols] * xpair
                e_ref[:, cols] = e_pair
                xwt_ref[cols, :] = (xpair.T * _pair_rows(w_state_t, h0, (LANES, q))).astype(BF16)

    row_q = lax.broadcasted_iota(jnp.int32, (q, n_state), 0)
    row128 = lax.broadcasted_iota(jnp.int32, (LANES, n_state), 0)
    for sj in range(seqs_per_step):
        r0 = pl.multiple_of((j * seqs_per_step + sj) * seq, seq)
        chunk_decay = jnp.exp(seg_ref[pl.ds(r0, 1), :])
        mine = jnp.logical_and(row_q >= r0, row_q < r0 + seq)
        for g in range(N_GROUPS):
            b_cols = slice(d_inner + g * n_state, d_inner + (g + 1) * n_state)
            c_cols = slice(d_inner + (N_GROUPS + g) * n_state,
                           d_inner + (N_GROUPS + g + 1) * n_state)
            b_mine = jnp.where(mine, xc_ref[:, b_cols], 0.0).astype(BF16)
            c_rows = xc_ref[pl.ds(r0, seq), c_cols].astype(BF16)
            for p in range(tiles_per_group):
                tile = g * tiles_per_group + p
                h0 = tile * HEADS_PER_TILE
                cols = slice(tile * LANES, (tile + 1) * LANES)
                h_prev = ssm0_ref[sj, cols, :]
                y_off = _dot_nt(c_rows, h_prev.astype(BF16)) * e_ref[pl.ds(r0, seq), cols]
                y_ref[pl.ds(r0, seq), cols] += y_off
                decay_rows = jnp.where(row128 < HEAD_DIM, chunk_decay[:, h0:h0 + 1],
                                       chunk_decay[:, h0 + 1:h0 + 2])
                ssm_ref[sj, cols, :] = decay_rows * h_prev + _dot(xwt_ref[cols, :], b_mine)


def _mixer_sample(xbc, dt, v, conv0, pool0, ssm0, conv_w, conv_b, dt_bias, a_log, d_skip,
                  *, batch, seq, d_inner, pos0):
    q = CHUNK
    n_seq = q // seq
    n_chunks = batch // n_seq
    conv_dim = xbc.shape[1]
    pool_dim = v.shape[1]
    tok = lambda i, j: (i, 0)
    per_chunk = lambda i, j: (i, 0, 0)
    sps = math.gcd(n_seq, 4)
    steps = n_seq // sps
    per_seq = lambda i, j: (i * steps + j, 0, 0)
    return pl.pallas_call(
        functools.partial(_mixer_sample_kernel, d_inner=d_inner, seq=seq, pos0=pos0,
                          seqs_per_step=sps),
        grid=(n_chunks, steps),
        in_specs=[pl.BlockSpec((q, conv_dim), tok),
                  pl.BlockSpec((q, LANES), tok),
                  pl.BlockSpec((q, pool_dim), tok),
                  pl.BlockSpec((n_seq, CONV_W - 1, conv_dim), per_chunk),
                  pl.BlockSpec((n_seq, POOL_BUF, pool_dim), per_chunk),
                  pl.BlockSpec((sps, d_inner, D_STATE), per_seq),
                  _const_spec((CONV_W, conv_dim)),
                  _const_spec((1, conv_dim)),
                  _const_spec((1, LANES)),
                  _const_spec((1, LANES)),
                  _const_spec((1, d_inner))],
        out_specs=[pl.BlockSpec((q, d_inner), tok),
                   pl.BlockSpec((q, pool_dim), tok),
                   pl.BlockSpec((sps, d_inner, D_STATE), per_seq),
                   pl.BlockSpec((n_seq, CONV_W - 1, conv_dim), per_chunk),
                   pl.BlockSpec((n_seq, POOL_BUF, pool_dim), per_chunk)],
        out_shape=[jax.ShapeDtypeStruct((batch * seq, d_inner), F32),
                   jax.ShapeDtypeStruct((batch * seq, pool_dim), F32),
                   jax.ShapeDtypeStruct((batch, d_inner, D_STATE), F32),
                   jax.ShapeDtypeStruct((batch, CONV_W - 1, conv_dim), F32),
                   jax.ShapeDtypeStruct((batch, POOL_BUF, pool_dim), F32)],
        scratch_shapes=[pltpu.VMEM((2 * SUBLANES, conv_dim), F32),
                        pltpu.VMEM((q, conv_dim), F32),
                        pltpu.VMEM((d_inner, q), BF16),
                        pltpu.VMEM((q, d_inner), F32),
                        pltpu.VMEM((q, LANES), F32),
                        pltpu.VMEM((POOL_PAD + seq, pool_dim), F32)],
        compiler_params=_params(2),
        name="mixer_sample",
    )(xbc, dt, v, conv0, pool0, ssm0, conv_w, conv_b, dt_bias, a_log, d_skip)


def _merge_kernel(h_ref, y_ref, z_ref, pooled_ref, gates_ref, nssd_ref, wssd_ref, wpg_ref,
                  pscale_ref, wpo_ref, wo_ref, o_ref, yn_ref, pm_ref):
    d_inner = y_ref.shape[1]
    d_model = h_ref.shape[1]
    gw = d_inner // N_GROUPS
    for g in range(N_GROUPS):
        cols = slice(g * gw, (g + 1) * gw)
        yz = y_ref[:, cols] * _silu(z_ref[:, cols])
        yn_ref[:, cols] = _rmsnorm(yz, nssd_ref[:, cols]).astype(BF16)
    a_branch = _dot(yn_ref[...], wssd_ref[...])
    gc = pooled_ref.shape[1] // len(POOL_WINDOWS)
    for gi in range(len(POOL_WINDOWS)):
        cols = slice(gi * gc, (gi + 1) * gc)
        mixed = _dot(pooled_ref[:, cols].astype(BF16), wpg_ref[gi])
        pm_ref[:, cols] = (mixed * pscale_ref[:, cols]).astype(BF16)
    b_branch = _dot(pm_ref[...], wpo_ref[...])
    merged = (_sigmoid(gates_ref[:, 0:d_model]) * a_branch
              + _sigmoid(gates_ref[:, d_model:2 * d_model]) * b_branch)
    o_ref[...] = h_ref[...] + _dot(merged.astype(BF16), wo_ref[...])


def _merge(h, y, z, pooled, gates, norm_ssd, w_ssd_out, w_pool_group, pool_scale, w_pool_out, w_o,
           *, tm):
    t, d = h.shape
    tm = min(tm, t)
    d_inner = y.shape[1]
    pool_dim = pooled.shape[1]
    row = lambda w: pl.BlockSpec((tm, w), lambda i: (i, 0))
    return pl.pallas_call(
        _merge_kernel,
        grid=(t // tm,),
        in_specs=[row(d), row(d_inner), row(d_inner), row(pool_dim), row(N_BRANCH * d),
                  _const_spec((1, d_inner)),
                  _const_spec(w_ssd_out.shape),
                  _const_spec(w_pool_group.shape),
                  _const_spec((1, pool_dim)),
                  _const_spec(w_pool_out.shape),
                  _const_spec(w_o.shape)],
        out_specs=row(d),
        out_shape=jax.ShapeDtypeStruct((t, d), F32),
        scratch_shapes=[pltpu.VMEM((tm, d_inner), BF16), pltpu.VMEM((tm, pool_dim), BF16)],
        compiler_params=_params(1),
        name="merge",
    )(h, y, z, pooled, gates, norm_ssd, w_ssd_out, w_pool_group, pool_scale, w_pool_out, w_o)


def _ple_kernel(h_ref, p_ref, nple_ref, wg_ref, wp_ref, nfin_ref, o_ref):
    h = h_ref[...]
    gate = _sigmoid(_dot(_rmsnorm(h, nple_ref[...]).astype(BF16), wg_ref[...]))
    h = h + gate * _dot(p_ref[...].astype(BF16), wp_ref[...])
    o_ref[...] = _rmsnorm(h, nfin_ref[...])


def _ple(h, p, norm_ple, w_ple_gate, w_ple, norm_final, *, tm):
    t, d = h.shape
    tm = min(tm, t)
    pd = p.shape[1]
    return pl.pallas_call(
        _ple_kernel,
        grid=(t // tm,),
        in_specs=[pl.BlockSpec((tm, d), lambda i: (i, 0)),
                  pl.BlockSpec((tm, pd), lambda i: (i, 0)),
                  _const_spec((1, d)),
                  _const_spec(w_ple_gate.shape),
                  _const_spec(w_ple.shape),
                  _const_spec((1, d))],
        out_specs=pl.BlockSpec((tm, d), lambda i: (i, 0)),
        out_shape=jax.ShapeDtypeStruct((t, d), F32),
        compiler_params=_params(1),
        name="ple",
    )(h, p, norm_ple, w_ple_gate, w_ple, norm_final)


def _pad_lanes(a):
    return jnp.pad(a.reshape(1, -1), ((0, 0), (0, LANES - a.shape[-1])))


def kernel(x_prompt, x_sample, state_ssm, state_conv, state_pool, p_prompt, p_sample, norm_ffn1, w_ffn1_gu, w_ffn1_down, norm_mix, w_in, conv_w, conv_b, dt_bias, a_log, d_skip, norm_ssd, w_ssd_out, w_pool_group, pool_scale, w_pool_out, w_o, norm_ffn2, w_ffn2_gu, w_ffn2_down, norm_ple, w_ple_gate, w_ple, norm_final):
    depth = norm_ffn1.shape[0]
    assert depth == 1, "the final norm is fused into the layer's last stage: one layer only"
    batch, seq, d_model = x_prompt.shape
    dec_batch, dec_seq, _ = x_sample.shape
    n_heads = dt_bias.shape[1]
    d_inner = n_heads * HEAD_DIM
    conv_dim = conv_w.shape[2]
    pool_dim = pool_scale.shape[1]
    assert seq % CHUNK == 0 and CHUNK % dec_seq == 0 and dec_batch % (CHUNK // dec_seq) == 0

    row = lambda a: a[0].reshape(1, -1)
    cuts = [0, d_inner, d_inner + conv_dim, d_inner + conv_dim + n_heads,
            d_inner + conv_dim + n_heads + pool_dim, w_in.shape[2]]
    wz, wxbc, wdt, wv, wgates = [w_in[0][:, a:b] for a, b in zip(cuts[:-1], cuts[1:])]
    wdt = jnp.pad(wdt, ((0, 0), (0, LANES - n_heads)))
    w_cat = jnp.concatenate([wz, wxbc, wv, wgates, wdt], axis=1).astype(BF16)
    widths = (d_inner, conv_dim, pool_dim, N_BRANCH * d_model, LANES)
    w1gu, w1d = w_ffn1_gu[0].astype(BF16), w_ffn1_down[0].astype(BF16)
    w2gu, w2d = w_ffn2_gu[0].astype(BF16), w_ffn2_down[0].astype(BF16)
    wssd, wpg = w_ssd_out[0].astype(BF16), w_pool_group[0].astype(BF16)
    wpo, wo = w_pool_out[0].astype(BF16), w_o[0].astype(BF16)
    wpleg, wple = w_ple_gate[0].astype(BF16), w_ple[0].astype(BF16)
    dtb, alog = _pad_lanes(dt_bias[0]), _pad_lanes(a_log[0])
    dskip = jnp.repeat(d_skip[0], HEAD_DIM).reshape(1, d_inner)
    cb = row(conv_b)
    nfin = norm_final.reshape(1, -1)

    def pre(h):
        h1 = _ffn(h, row(norm_ffn1), w1gu, w1d, tm=512)
        return (h1,) + tuple(_inproj(h1, row(norm_mix), w_cat, widths, tm=256))

    def post(h1, y, z, pooled, gates, p):
        h2 = _merge(h1, y, z, pooled, gates, row(norm_ssd), wssd, wpg, row(pool_scale), wpo, wo,
                    tm=256)
        h3 = _ffn(h2, row(norm_ffn2), w2gu, w2d, tm=512)
        return _ple(h3, p, row(norm_ple), wpleg, wple, nfin, tm=512)

    h1 = _ffn(x_prompt.reshape(batch * seq, d_model), row(norm_ffn1), w1gu, w1d, tm=512)
    z, xc, v, gates, dt, c1 = _inproj_conv(h1, row(norm_mix), w_cat, widths, conv_w[0], cb,
                                           tm=256, seq=seq)
    y, pooled, s1, q1 = _mixer_prompt(xc, dt, v, dtb, alog, dskip,
                                      batch=batch, seq=seq, d_inner=d_inner)
    y_prompt = post(h1, y, z, pooled, gates, p_prompt[0].reshape(batch * seq, -1))

    h1, z, xbc, v, gates, dt = pre(x_sample.reshape(dec_batch * dec_seq, d_model))
    y, pooled, s2, c2, q2 = _mixer_sample(
        xbc, dt, v, state_conv[0], state_pool[0], state_ssm[0].reshape(dec_batch, d_inner, D_STATE),
        conv_w[0], cb, dtb, alog, dskip,
        batch=dec_batch, seq=dec_seq, d_inner=d_inner, pos0=PAST_LEN)
    y_sample = post(h1, y, z, pooled, gates, p_sample[0].reshape(dec_batch * dec_seq, -1))

    return (y_prompt.reshape(batch, seq, d_model),
            y_sample.reshape(dec_batch, dec_seq, d_model),
            s1.reshape(1, batch, n_heads, HEAD_DIM, D_STATE), c1[None], q1[None],
            s2.reshape(1, dec_batch, n_heads, HEAD_DIM, D_STATE), c2[None], q2[None])
```

```python
import functools
import math

import jax
import jax.numpy as jnp
from jax import lax
from jax.experimental import pallas as pl
from jax.experimental.pallas import tpu as pltpu

F32 = jnp.float32
BF16 = jnp.bfloat16
EPS = 1e-6
HIGHEST = lax.Precision.HIGHEST
NEG_LOG2E = -1.4426950408889634

LANES = 128
SUBLANES = 8
VMEM_LIMIT_BYTES = 56 * 1024 * 1024

HEAD_DIM = 64
N_GROUPS = 8
D_STATE = 128
CONV_W = 4
CHUNK = 128
POOL_WINDOWS = (2, 4, 8, 16)
POOL_BUF = max(POOL_WINDOWS) - 1
N_BRANCH = 2
PAST_LEN = 16384
HEADS_PER_TILE = LANES // HEAD_DIM
ROW_STRIDE = 4
POOL_PAD = 24


def _sigmoid(x):
    return 1.0 / (1.0 + jnp.exp2(x * NEG_LOG2E))


def _silu(x):
    half = 0.5 * x
    return half + half * jnp.tanh(half)


def _softplus(x):
    return jnp.maximum(x, 0.0) + jnp.log(1.0 + jnp.exp(-jnp.abs(x)))


def _rmsnorm(x, g):
    return x * lax.rsqrt(jnp.mean(x * x, axis=-1, keepdims=True) + EPS) * g


def _dot(a, b):
    return jnp.dot(a, b, preferred_element_type=F32)


def _dot_nt(a, b):
    return lax.dot_general(a, b, (((1,), (1,)), ((), ())), preferred_element_type=F32)


def _const_spec(shape):
    nd = len(shape)
    return pl.BlockSpec(shape, lambda *_: (0,) * nd, pipeline_mode=pl.Buffered(1))


def _params(n_axes):
    return pltpu.CompilerParams(dimension_semantics=("arbitrary",) * n_axes,
                                vmem_limit_bytes=VMEM_LIMIT_BYTES)


def _ffn_kernel(x_ref, g_ref, wgu_ref, wd_ref, o_ref, xn_ref, acc_ref, *, d_ff, fc):
    xn_ref[...] = _rmsnorm(x_ref[...], g_ref[...]).astype(BF16)
    for c in range(d_ff // fc):
        xn = xn_ref[...]
        gate = _dot(xn, wgu_ref[:, c * fc:(c + 1) * fc])
        up = _dot(xn, wgu_ref[:, d_ff + c * fc:d_ff + (c + 1) * fc])
        act = (_silu(gate) * up).astype(BF16)
        contrib = _dot(act, wd_ref[c * fc:(c + 1) * fc, :])
        if c == 0:
            acc_ref[...] = contrib
        else:
            acc_ref[...] += contrib
    o_ref[...] = x_ref[...] + 0.5 * acc_ref[...]


def _ffn(x, g, w_gu, w_down, *, tm):
    t, d = x.shape
    d_ff = w_down.shape[0]
    tm = min(tm, t)
    fc = 256
    return pl.pallas_call(
        functools.partial(_ffn_kernel, d_ff=d_ff, fc=fc),
        grid=(t // tm,),
        in_specs=[pl.BlockSpec((tm, d), lambda i: (i, 0)),
                  _const_spec((1, d)),
                  _const_spec((d, 2 * d_ff)),
                  _const_spec((d_ff, d))],
        out_specs=pl.BlockSpec((tm, d), lambda i: (i, 0)),
        out_shape=jax.ShapeDtypeStruct((t, d), F32),
        scratch_shapes=[pltpu.VMEM((tm, d), BF16), pltpu.VMEM((tm, d), F32)],
        compiler_params=_params(1),
        name="ffn",
    )(x, g, w_gu, w_down)


def _inproj_kernel(h_ref, g_ref, w_ref, *rest, widths, nc):
    out_refs, u_ref = rest[:-1], rest[-1]
    u_ref[...] = _rmsnorm(h_ref[...], g_ref[...]).astype(BF16)
    start = 0
    for o_ref, width in zip(out_refs, widths):
        for a in range(0, width, nc):
            b = min(a + nc, width)
            o_ref[:, a:b] = _dot(u_ref[...], w_ref[:, start + a:start + b])
        start += width


def _inproj(h, g, w_cat, widths, *, tm):
    t, d = h.shape
    tm = min(tm, t)
    n_all = w_cat.shape[1]
    return pl.pallas_call(
        functools.partial(_inproj_kernel, widths=widths, nc=1024),
        grid=(t // tm,),
        in_specs=[pl.BlockSpec((tm, d), lambda i: (i, 0)),
                  _const_spec((1, d)),
                  _const_spec((d, n_all))],
        out_specs=[pl.BlockSpec((tm, w), lambda i: (i, 0)) for w in widths],
        out_shape=[jax.ShapeDtypeStruct((t, w), F32) for w in widths],
        scratch_shapes=[pltpu.VMEM((tm, d), BF16)],
        compiler_params=_params(1),
        name="inproj",
    )(h, g, w_cat)


def _inproj_conv_kernel(h_ref, g_ref, w_ref, cw_ref, cb_ref,
                        z_ref, xc_ref, v_ref, gates_ref, dt_ref, conv_ref,
                        u_ref, raw_ref, *, widths, nc, tiles_per_seq):
    i = pl.program_id(0)
    tm = h_ref.shape[0]
    n_slabs = raw_ref.shape[0]

    @pl.when(i % tiles_per_seq == 0)
    def _():
        raw_ref[:, 0:SUBLANES, :] = jnp.zeros((n_slabs, SUBLANES, LANES), F32)

    u_ref[...] = _rmsnorm(h_ref[...], g_ref[...]).astype(BF16)
    starts = [sum(widths[:k]) for k in range(len(widths))]

    def conv_slab(s):
        cols = slice(s * LANES, (s + 1) * LANES)
        taps = [jnp.broadcast_to(cw_ref[k:k + 1, cols], (SUBLANES, LANES)) for k in range(CONV_W)]
        bias = jnp.broadcast_to(cb_ref[:, cols], (SUBLANES, LANES))
        span = SUBLANES * ROW_STRIDE
        for base in range(0, tm, span):
            shifted = [raw_ref[s, pl.ds(SUBLANES + base - (CONV_W - 1) + m, SUBLANES,
                                        stride=ROW_STRIDE), :]
                       for m in range(ROW_STRIDE + CONV_W - 1)]
            for j in range(ROW_STRIDE):
                acc = bias
                for k in range(CONV_W):
                    acc = acc + shifted[j + k] * taps[k]
                xc_ref[s, pl.ds(base + j, SUBLANES, stride=ROW_STRIDE), :] = _silu(acc)
        conv_ref[0, :, cols] = raw_ref[s, SUBLANES + tm - (CONV_W - 1):SUBLANES + tm, :]
        raw_ref[s, 0:SUBLANES, :] = raw_ref[s, tm:tm + SUBLANES, :]

    outs = {0: z_ref, 2: v_ref, 3: gates_ref, 4: dt_ref}
    units = [(k, a) for k in (1, 0, 2, 3, 4) for a in range(0, widths[k], nc)]
    slabs_per_unit = -(-n_slabs // (len(units) - 1))
    done = 0
    for n, (k, a) in enumerate(units):
        b = min(a + nc, widths[k])
        res = _dot(u_ref[...], w_ref[:, starts[k] + a:starts[k] + b])
        if k == 1:
            for s in range((b - a) // LANES):
                raw_ref[a // LANES + s, SUBLANES:SUBLANES + tm, :] = res[:, s * LANES:(s + 1) * LANES]
        else:
            outs[k][:, a:b] = res
        ready = min(n + 1, widths[1] // nc) * (nc // LANES)
        want = n_slabs if n == len(units) - 1 else min(n_slabs, n * slabs_per_unit)
        while done < min(want, ready):
            conv_slab(done)
            done += 1
    assert done == n_slabs


def _inproj_conv(h, g, w_cat, widths, conv_w, conv_b, *, tm, seq):
    t, d = h.shape
    tm = min(tm, seq)
    tiles_per_seq = seq // tm
    n_all = w_cat.shape[1]
    conv_dim = widths[1]
    n_slabs = conv_dim // LANES
    assert tm % (SUBLANES * ROW_STRIDE) == 0

    def out_spec(k):
        if k == 1:
            return pl.BlockSpec((n_slabs, tm, LANES), lambda i: (0, i, 0))
        return pl.BlockSpec((tm, widths[k]), lambda i: (i, 0))

    def out_shape(k):
        return jax.ShapeDtypeStruct((n_slabs, t, LANES) if k == 1 else (t, widths[k]), F32)

    return pl.pallas_call(
        functools.partial(_inproj_conv_kernel, widths=widths, nc=512, tiles_per_seq=tiles_per_seq),
        grid=(t // tm,),
        in_specs=[pl.BlockSpec((tm, d), lambda i: (i, 0)),
                  _const_spec((1, d)),
                  _const_spec((d, n_all)),
                  _const_spec((CONV_W, conv_dim)),
                  _const_spec((1, conv_dim))],
        out_specs=[out_spec(k) for k in range(len(widths))]
        + [pl.BlockSpec((1, CONV_W - 1, conv_dim), lambda i: (i // tiles_per_seq, 0, 0))],
        out_shape=[out_shape(k) for k in range(len(widths))]
        + [jax.ShapeDtypeStruct((t // seq, CONV_W - 1, conv_dim), F32)],
        scratch_shapes=[pltpu.VMEM((tm, d), BF16),
                        pltpu.VMEM((n_slabs, SUBLANES + tm, LANES), F32)],
        compiler_params=_params(1),
        name="inproj_conv",
    )(h, g, w_cat, conv_w, conv_b)


def _chunk_scalars(dt_raw, dt_bias, a_log, seg_len):
    q = dt_raw.shape[0]
    dt = _softplus(dt_raw + dt_bias)
    da = dt * (-jnp.exp(a_log))
    shift = int(math.log2(seg_len))
    qi = lax.broadcasted_iota(jnp.int32, (q, q), 0)
    si = lax.broadcasted_iota(jnp.int32, (q, q), 1)
    same = (qi >> shift) == (si >> shift)
    causal = jnp.logical_and(same, si <= qi)
    a_cs = jnp.dot(jnp.where(causal, 1.0, 0.0), da, precision=HIGHEST, preferred_element_type=F32)
    seg_tot = jnp.dot(jnp.where(same, 1.0, 0.0), da, precision=HIGHEST, preferred_element_type=F32)
    return dt, a_cs, seg_tot, causal


def _pair_cols(vals, h0, shape):
    lane = lax.broadcasted_iota(jnp.int32, shape, 1)
    return jnp.where(lane < HEAD_DIM, vals[:, h0:h0 + 1], vals[:, h0 + 1:h0 + 2])


def _pair_rows(vals_t, h0, shape):
    row = lax.broadcasted_iota(jnp.int32, shape, 0)
    return jnp.where(row < HEAD_DIM, vals_t[h0:h0 + 1, :], vals_t[h0 + 1:h0 + 2, :])


def _diag_pair(xpair, cb, a_cs, a_cs_t, dt_t, causal, h0):
    q = xpair.shape[0]
    lane = lax.broadcasted_iota(jnp.int32, xpair.shape, 1)
    out = None
    spread = []
    for k in range(HEADS_PER_TILE):
        h = h0 + k
        a_col = jnp.broadcast_to(a_cs[:, h:h + 1], (q, LANES))
        spread.append(a_col)
        decay = jnp.exp(jnp.where(causal, a_col - a_cs_t[h:h + 1, :], -jnp.inf))
        w = (cb * decay * dt_t[h:h + 1, :]).astype(BF16)
        mine = (lane < HEAD_DIM) if k == 0 else (lane >= HEAD_DIM)
        t = _dot(w, jnp.where(mine, xpair, 0.0).astype(BF16))
        out = t if out is None else out + t
    e_pair = jnp.exp(jnp.where(lane < HEAD_DIM, spread[0], spread[1]))
    return out, e_pair


def _shift_rows(cur, prev, s):
    sub = lax.broadcasted_iota(jnp.int32, (1,) + cur.shape[1:], 1)
    return jnp.where(sub < s, pltpu.roll(prev, s, axis=1), pltpu.roll(cur, s, axis=1))


def _conv_tiles(cur, prev, w, bias):
    tap = lambda k: w[k:k + 1, :][None]
    acc = cur * tap(CONV_W - 1)
    for s in range(1, CONV_W):
        acc = acc + _shift_rows(cur, prev, s) * tap(CONV_W - 1 - s)
    return _silu(bias[None] + acc)


def _conv_block(ext, w, bias):
    rows, width = ext.shape
    tiles = ext.reshape(rows // SUBLANES, SUBLANES, width)
    return _conv_tiles(tiles[1:], tiles[:-1], w, bias).reshape(rows - SUBLANES, width)


def _conv_tile_rows(rows_of_tap, w, bias):
    acc = rows_of_tap(0) * w[0:1, :]
    for k in range(1, CONV_W):
        acc = acc + rows_of_tap(k) * w[k:k + 1, :]
    return _silu(bias + acc)


def _mixer_prompt_kernel(xc_ref, dt_ref, v_ref, dtb_ref, alog_ref, dskip_ref,
                         y_ref, pooled_ref, ssm_ref, pool_ref,
                         h_ref, pe_ref, ps_ref, *, d_inner, n_chunks):
    c = pl.program_id(1)
    q = xc_ref.shape[1]
    pool_dim = v_ref.shape[1]
    n_state = D_STATE
    last = c == n_chunks - 1

    @pl.when(c == 0)
    def _():
        h_ref[...] = jnp.zeros(h_ref.shape, F32)
        pe_ref[0:POOL_PAD, :] = jnp.zeros((POOL_PAD, pool_dim), F32)
        ps_ref[:, 0:SUBLANES, :] = jnp.zeros((2, SUBLANES, ps_ref.shape[2]), F32)

    dt, a_cs, seg_tot, causal = _chunk_scalars(dt_ref[...], dtb_ref[...], alog_ref[...], q)
    assert q == LANES and n_state == LANES
    a_cs_t = a_cs.T
    dt_t = dt.T
    w_state_t = (jnp.exp(seg_tot - a_cs) * dt).T
    chunk_decay_t = jnp.exp(seg_tot).T

    for g in range(N_GROUPS):
        b_g = xc_ref[d_inner // LANES + g].astype(BF16)
        c_g = xc_ref[d_inner // LANES + N_GROUPS + g].astype(BF16)
        cb = _dot_nt(c_g, b_g)
        tiles_per_group = d_inner // N_GROUPS // LANES
        for p in range(tiles_per_group):
            tile = g * tiles_per_group + p
            h0 = tile * HEADS_PER_TILE
            cols = slice(tile * LANES, (tile + 1) * LANES)
            xpair = xc_ref[tile]
            h_prev = h_ref[cols, :]
            y, e_pair = _diag_pair(xpair, cb, a_cs, a_cs_t, dt_t, causal, h0)
            y = y + _dot_nt(c_g, h_prev.astype(BF16)) * e_pair
            y_ref[:, cols] = y + dskip_ref[:, cols] * xpair
            xw_t = (xpair.T * _pair_rows(w_state_t, h0, (LANES, q))).astype(BF16)
            decay_rows = _pair_rows(chunk_decay_t, h0, (LANES, n_state))
            h_ref[cols, :] = decay_rows * h_prev + _dot(xw_t, b_g)

    @pl.when(last)
    def _():
        ssm_ref[0] = h_ref[...]

    pe_ref[POOL_PAD:POOL_PAD + q, :] = v_ref[...]
    gc = pool_dim // len(POOL_WINDOWS)
    rows = q + POOL_PAD - SUBLANES
    t_idx = lax.broadcasted_iota(jnp.int32, (q, 1), 0) + c * q
    for gi, win in enumerate(POOL_WINDOWS):
        cols = slice(gi * gc, (gi + 1) * gc)
        shift = 1
        level = 0
        cur = None
        while shift < win:
            if level == 0:
                cur = (pe_ref[SUBLANES:SUBLANES + rows, cols]
                       + pe_ref[SUBLANES - shift:SUBLANES - shift + rows, cols])
            else:
                src = ps_ref.at[(level - 1) % 2]
                cur = (src[SUBLANES:SUBLANES + rows, :]
                       + src[SUBLANES - shift:SUBLANES - shift + rows, :])
            if shift * 2 < win:
                ps_ref[level % 2, SUBLANES:SUBLANES + rows, :] = cur
            shift *= 2
            level += 1
        total = cur[POOL_PAD - SUBLANES:, :]
        cnt = jnp.minimum(t_idx + 1, win).astype(F32)
        pooled_ref[:, cols] = total / cnt - v_ref[:, cols]
    pe_ref[POOL_PAD - POOL_BUF:POOL_PAD, :] = v_ref[q - POOL_BUF:q, :]

    @pl.when(last)
    def _():
        pool_ref[0] = v_ref[q - POOL_BUF:q, :]


def _mixer_prompt(xc, dt, v, dt_bias, a_log, d_skip, *, batch, seq, d_inner):
    q = CHUNK
    n_chunks = seq // q
    n_slabs = xc.shape[0]
    pool_dim = v.shape[1]
    gc = pool_dim // len(POOL_WINDOWS)
    n_rows_state = d_inner
    tok = lambda b, c: (b * n_chunks + c, 0)
    per_b = lambda b, c: (b, 0, 0)
    return pl.pallas_call(
        functools.partial(_mixer_prompt_kernel, d_inner=d_inner, n_chunks=n_chunks),
        grid=(batch, n_chunks),
        in_specs=[pl.BlockSpec((n_slabs, q, LANES), lambda b, c: (0, b * n_chunks + c, 0)),
                  pl.BlockSpec((q, LANES), tok),
                  pl.BlockSpec((q, pool_dim), tok),
                  _const_spec((1, LANES)),
                  _const_spec((1, LANES)),
                  _const_spec((1, d_inner))],
        out_specs=[pl.BlockSpec((q, d_inner), tok),
                   pl.BlockSpec((q, pool_dim), tok),
                   pl.BlockSpec((1, n_rows_state, D_STATE), per_b),
                   pl.BlockSpec((1, POOL_BUF, pool_dim), per_b)],
        out_shape=[jax.ShapeDtypeStruct((batch * seq, d_inner), F32),
                   jax.ShapeDtypeStruct((batch * seq, pool_dim), F32),
                   jax.ShapeDtypeStruct((batch, n_rows_state, D_STATE), F32),
                   jax.ShapeDtypeStruct((batch, POOL_BUF, pool_dim), F32)],
        scratch_shapes=[pltpu.VMEM((n_rows_state, D_STATE), F32),
                        pltpu.VMEM((POOL_PAD + q, pool_dim), F32),
                        pltpu.VMEM((2, POOL_PAD + q, gc), F32)],
        compiler_params=_params(2),
        name="mixer_prompt",
    )(xc, dt, v, dt_bias, a_log, d_skip)


def _mixer_sample_kernel(xbc_ref, dt_ref, v_ref, conv0_ref, pool0_ref, ssm0_ref,
                         cw_ref, cb_ref, dtb_ref, alog_ref, dskip_ref,
                         y_ref, pooled_ref, ssm_ref, conv_ref, pool_ref,
                         head_ref, xc_ref, xwt_ref, e_ref, seg_ref, pe_ref,
                         *, d_inner, seq, pos0, seqs_per_step):
    j = pl.program_id(1)
    q = xbc_ref.shape[0]
    conv_dim = xbc_ref.shape[1]
    pool_dim = v_ref.shape[1]
    n_state = D_STATE
    n_seq = q // seq
    tiles_per_group = d_inner // N_GROUPS // LANES
    gc = pool_dim // len(POOL_WINDOWS)

    @pl.when(j == 0)
    def _():
        lo = SUBLANES - (CONV_W - 1)

        def per_seq(b, carry):
            r0 = pl.multiple_of(b * seq, seq)
            head_ref[lo:SUBLANES, :] = conv0_ref[b]
            ct = 1024
            for jc in range(conv_dim // ct):
                cols = slice(jc * ct, (jc + 1) * ct)
                head_ref[SUBLANES:SUBLANES + seq, cols] = xbc_ref[pl.ds(r0, seq), cols]
                xc_ref[pl.ds(r0, seq), cols] = _conv_tile_rows(
                    lambda k: head_ref[lo + k:lo + k + seq, cols], cw_ref[:, cols], cb_ref[:, cols])
            conv_ref[b] = head_ref[SUBLANES + seq - (CONV_W - 1):SUBLANES + seq, :]

            pe_ref[POOL_PAD - POOL_BUF:POOL_PAD, :] = pool0_ref[b]
            vb = v_ref[pl.ds(r0, seq), :]
            pe_ref[POOL_PAD:POOL_PAD + seq, :] = vb
            t_idx = lax.broadcasted_iota(jnp.int32, (seq, 1), 0) + pos0
            for gi, win in enumerate(POOL_WINDOWS):
                cols = slice(gi * gc, (gi + 1) * gc)
                total = pe_ref[POOL_PAD:POOL_PAD + seq, cols]
                for k in range(1, win):
                    total = total + pe_ref[POOL_PAD - k:POOL_PAD - k + seq, cols]
                cnt = jnp.minimum(t_idx + 1, win).astype(F32)
                pooled_ref[pl.ds(r0, seq), cols] = total / cnt - vb[:, cols]
            pool_ref[b] = pe_ref[POOL_PAD + seq - POOL_BUF:POOL_PAD + seq, :]
            return carry

        lax.fori_loop(0, n_seq, per_seq, 0)

        dt, a_cs, seg_tot, causal = _chunk_scalars(dt_ref[...], dtb_ref[...], alog_ref[...], seq)
        a_cs_t = a_cs.T
        dt_t = dt.T
        w_state_t = (jnp.exp(seg_tot - a_cs) * dt).T
        seg_ref[...] = seg_tot
        for g in range(N_GROUPS):
            b_g = xc_ref[:, d_inner + g * n_state:d_inner + (g + 1) * n_state].astype(BF16)
            c_g = xc_ref[:, d_inner + (N_GROUPS + g) * n_state:
                         d_inner + (N_GROUPS + g + 1) * n_state].astype(BF16)
            cb = _dot_nt(c_g, b_g)
            for p in range(tiles_per_group):
                tile = g * tiles_per_group + p
                h0 = tile * HEADS_PER_TILE
                cols = slice(tile * LANES, (tile + 1) * LANES)
                xpair = xc_ref[:, cols]
                y, e_pair = _diag_pair(xpair, cb, a_cs, a_cs_t, dt_t, causal, h0)
                y_ref[:, cols] = y + dskip_ref[:, cols] * xpair
                e_ref[:, cols] = e_pair
                xwt_ref[cols, :] = (xpair.T * _pair_rows(w_state_t, h0, (LANES, q))).astype(BF16)

    row_q = lax.broadcasted_iota(jnp.int32, (q, n_state), 0)
    row128 = lax.broadcasted_iota(jnp.int32, (LANES, n_state), 0)
    for sj in range(seqs_per_step):
        r0 = pl.multiple_of((j * seqs_per_step + sj) * seq, seq)
        chunk_decay = jnp.exp(seg_ref[pl.ds(r0, 1), :])
        mine = jnp.logical_and(row_q >= r0, row_q < r0 + seq)
        for g in range(N_GROUPS):
            b_cols = slice(d_inner + g * n_state, d_inner + (g + 1) * n_state)
            c_cols = slice(d_inner + (N_GROUPS + g) * n_state,
                           d_inner + (N_GROUPS + g + 1) * n_state)
            b_mine = jnp.where(mine, xc_ref[:, b_cols], 0.0).astype(BF16)
            c_rows = xc_ref[pl.ds(r0, seq), c_cols].astype(BF16)
            for p in range(tiles_per_group):
                tile = g * tiles_per_group + p
                h0 = tile * HEADS_PER_TILE
                cols = slice(tile * LANES, (tile + 1) * LANES)
                h_prev = ssm0_ref[sj, cols, :]
                y_off = _dot_nt(c_rows, h_prev.astype(BF16)) * e_ref[pl.ds(r0, seq), cols]
                y_ref[pl.ds(r0, seq), cols] += y_off
                decay_rows = jnp.where(row128 < HEAD_DIM, chunk_decay[:, h0:h0 + 1],
                                       chunk_decay[:, h0 + 1:h0 + 2])
                ssm_ref[sj, cols, :] = decay_rows * h_prev + _dot(xwt_ref[cols, :], b_mine)


def _mixer_sample(xbc, dt, v, conv0, pool0, ssm0, conv_w, conv_b, dt_bias, a_log, d_skip,
                  *, batch, seq, d_inner, pos0):
    q = CHUNK
    n_seq = q // seq
    n_chunks = batch // n_seq
    conv_dim = xbc.shape[1]
    pool_dim = v.shape[1]
    tok = lambda i, j: (i, 0)
    per_chunk = lambda i, j: (i, 0, 0)
    sps = math.gcd(n_seq, 4)
    steps = n_seq // sps
    per_seq = lambda i, j: (i * steps + j, 0, 0)
    return pl.pallas_call(
        functools.partial(_mixer_sample_kernel, d_inner=d_inner, seq=seq, pos0=pos0,
                          seqs_per_step=sps),
        grid=(n_chunks, steps),
        in_specs=[pl.BlockSpec((q, conv_dim), tok),
                  pl.BlockSpec((q, LANES), tok),
                  pl.BlockSpec((q, pool_dim), tok),
                  pl.BlockSpec((n_seq, CONV_W - 1, conv_dim), per_chunk),
                  pl.BlockSpec((n_seq, POOL_BUF, pool_dim), per_chunk),
                  pl.BlockSpec((sps, d_inner, D_STATE), per_seq),
                  _const_spec((CONV_W, conv_dim)),
                  _const_spec((1, conv_dim)),
                  _const_spec((1, LANES)),
                  _const_spec((1, LANES)),
                  _const_spec((1, d_inner))],
        out_specs=[pl.BlockSpec((q, d_inner), tok),
                   pl.BlockSpec((q, pool_dim), tok),
                   pl.BlockSpec((sps, d_inner, D_STATE), per_seq),
                   pl.BlockSpec((n_seq, CONV_W - 1, conv_dim), per_chunk),
                   pl.BlockSpec((n_seq, POOL_BUF, pool_dim), per_chunk)],
        out_shape=[jax.ShapeDtypeStruct((batch * seq, d_inner), F32),
                   jax.ShapeDtypeStruct((batch * seq, pool_dim), F32),
                   jax.ShapeDtypeStruct((batch, d_inner, D_STATE), F32),
                   jax.ShapeDtypeStruct((batch, CONV_W - 1, conv_dim), F32),
                   jax.ShapeDtypeStruct((batch, POOL_BUF, pool_dim), F32)],
        scratch_shapes=[pltpu.VMEM((2 * SUBLANES, conv_dim), F32),
                        pltpu.VMEM((q, conv_dim), F32),
                        pltpu.VMEM((d_inner, q), BF16),
                        pltpu.VMEM((q, d_inner), F32),
                        pltpu.VMEM((q, LANES), F32),
                        pltpu.VMEM((POOL_PAD + seq, pool_dim), F32)],
        compiler_params=_params(2),
        name="mixer_sample",
    )(xbc, dt, v, conv0, pool0, ssm0, conv_w, conv_b, dt_bias, a_log, d_skip)


def _merge_kernel(h_ref, y_ref, z_ref, pooled_ref, gates_ref, nssd_ref, wssd_ref, wpg_ref,
                  pscale_ref, wpo_ref, wo_ref, o_ref, yn_ref, pm_ref):
    d_inner = y_ref.shape[1]
    d_model = h_ref.shape[1]
    gw = d_inner // N_GROUPS
    for g in range(N_GROUPS):
        cols = slice(g * gw, (g + 1) * gw)
        yz = y_ref[:, cols] * _silu(z_ref[:, cols])
        yn_ref[:, cols] = _rmsnorm(yz, nssd_ref[:, cols]).astype(BF16)
    a_branch = _dot(yn_ref[...], wssd_ref[...])
    gc = pooled_ref.shape[1] // len(POOL_WINDOWS)
    for gi in range(len(POOL_WINDOWS)):
        cols = slice(gi * gc, (gi + 1) * gc)
        mixed = _dot(pooled_ref[:, cols].astype(BF16), wpg_ref[gi])
        pm_ref[:, cols] = (mixed * pscale_ref[:, cols]).astype(BF16)
    b_branch = _dot(pm_ref[...], wpo_ref[...])
    merged = (_sigmoid(gates_ref[:, 0:d_model]) * a_branch
              + _sigmoid(gates_ref[:, d_model:2 * d_model]) * b_branch)
    o_ref[...] = h_ref[...] + _dot(merged.astype(BF16), wo_ref[...])


def _merge(h, y, z, pooled, gates, norm_ssd, w_ssd_out, w_pool_group, pool_scale, w_pool_out, w_o,
           *, tm):
    t, d = h.shape
    tm = min(tm, t)
    d_inner = y.shape[1]
    pool_dim = pooled.shape[1]
    row = lambda w: pl.BlockSpec((tm, w), lambda i: (i, 0))
    return pl.pallas_call(
        _merge_kernel,
        grid=(t // tm,),
        in_specs=[row(d), row(d_inner), row(d_inner), row(pool_dim), row(N_BRANCH * d),
                  _const_spec((1, d_inner)),
                  _const_spec(w_ssd_out.shape),
                  _const_spec(w_pool_group.shape),
                  _const_spec((1, pool_dim)),
                  _const_spec(w_pool_out.shape),
                  _const_spec(w_o.shape)],
        out_specs=row(d),
        out_shape=jax.ShapeDtypeStruct((t, d), F32),
        scratch_shapes=[pltpu.VMEM((tm, d_inner), BF16), pltpu.VMEM((tm, pool_dim), BF16)],
        compiler_params=_params(1),
        name="merge",
    )(h, y, z, pooled, gates, norm_ssd, w_ssd_out, w_pool_group, pool_scale, w_pool_out, w_o)


def _ple_kernel(h_ref, p_ref, nple_ref, wg_ref, wp_ref, nfin_ref, o_ref):
    h = h_ref[...]
    gate = _sigmoid(_dot(_rmsnorm(h, nple_ref[...]).astype(BF16), wg_ref[...]))
    h = h + gate * _dot(p_ref[...].astype(BF16), wp_ref[...])
    o_ref[...] = _rmsnorm(h, nfin_ref[...])


def _ple(h, p, norm_ple, w_ple_gate, w_ple, norm_final, *, tm):
    t, d = h.shape
    tm = min(tm, t)
    pd = p.shape[1]
    return pl.pallas_call(
        _ple_kernel,
        grid=(t // tm,),
        in_specs=[pl.BlockSpec((tm, d), lambda i: (i, 0)),
                  pl.BlockSpec((tm, pd), lambda i: (i, 0)),
                  _const_spec((1, d)),
                  _const_spec(w_ple_gate.shape),
                  _const_spec(w_ple.shape),
                  _const_spec((1, d))],
        out_specs=pl.BlockSpec((tm, d), lambda i: (i, 0)),
        out_shape=jax.ShapeDtypeStruct((t, d), F32),
        compiler_params=_params(1),
        name="ple",
    )(h, p, norm_ple, w_ple_gate, w_ple, norm_final)


def _pad_lanes(a):
    return jnp.pad(a.reshape(1, -1), ((0, 0), (0, LANES - a.shape[-1])))


def kernel(x_prompt, x_sample, state_ssm, state_conv, state_pool, p_prompt, p_sample, norm_ffn1, w_ffn1_gu, w_ffn1_down, norm_mix, w_in, conv_w, conv_b, dt_bias, a_log, d_skip, norm_ssd, w_ssd_out, w_pool_group, pool_scale, w_pool_out, w_o, norm_ffn2, w_ffn2_gu, w_ffn2_down, norm_ple, w_ple_gate, w_ple, norm_final):
    depth = norm_ffn1.shape[0]
    assert depth == 1, "the final norm is fused into the layer's last stage: one layer only"
    batch, seq, d_model = x_prompt.shape
    dec_batch, dec_seq, _ = x_sample.shape
    n_heads = dt_bias.shape[1]
    d_inner = n_heads * HEAD_DIM
    conv_dim = conv_w.shape[2]
    pool_dim = pool_scale.shape[1]
    assert seq % CHUNK == 0 and CHUNK % dec_seq == 0 and dec_batch % (CHUNK // dec_seq) == 0

    row = lambda a: a[0].reshape(1, -1)
    cuts = [0, d_inner, d_inner + conv_dim, d_inner + conv_dim + n_heads,
            d_inner + conv_dim + n_heads + pool_dim, w_in.shape[2]]
    wz, wxbc, wdt, wv, wgates = [w_in[0][:, a:b] for a, b in zip(cuts[:-1], cuts[1:])]
    wdt = jnp.pad(wdt, ((0, 0), (0, LANES - n_heads)))
    w_cat = jnp.concatenate([wz, wxbc, wv, wgates, wdt], axis=1).astype(BF16)
    widths = (d_inner, conv_dim, pool_dim, N_BRANCH * d_model, LANES)
    w1gu, w1d = w_ffn1_gu[0].astype(BF16), w_ffn1_down[0].astype(BF16)
    w2gu, w2d = w_ffn2_gu[0].astype(BF16), w_ffn2_down[0].astype(BF16)
    wssd, wpg = w_ssd_out[0].astype(BF16), w_pool_group[0].astype(BF16)
    wpo, wo = w_pool_out[0].astype(BF16), w_o[0].astype(BF16)
    wpleg, wple = w_ple_gate[0].astype(BF16), w_ple[0].astype(BF16)
    dtb, alog = _pad_lanes(dt_bias[0]), _pad_lanes(a_log[0])
    dskip = jnp.repeat(d_skip[0], HEAD_DIM).reshape(1, d_inner)
    cb = row(conv_b)
    nfin = norm_final.reshape(1, -1)

    def pre(h):
        h1 = _ffn(h, row(norm_ffn1), w1gu, w1d, tm=512)
        return (h1,) + tuple(_inproj(h1, row(norm_mix), w_cat, widths, tm=256))

    def post(h1, y, z, pooled, gates, p):
        h2 = _merge(h1, y, z, pooled, gates, row(norm_ssd), wssd, wpg, row(pool_scale), wpo, wo,
                    tm=256)
        h3 = _ffn(h2, row(norm_ffn2), w2gu, w2d, tm=512)
        return _ple(h3, p, row(norm_ple), wpleg, wple, nfin, tm=512)

    h1 = _ffn(x_prompt.reshape(batch * seq, d_model), row(norm_ffn1), w1gu, w1d, tm=512)
    z, xc, v, gates, dt, c1 = _inproj_conv(h1, row(norm_mix), w_cat, widths, conv_w[0], cb,
                                           tm=256, seq=seq)
    y, pooled, s1, q1 = _mixer_prompt(xc, dt, v, dtb, alog, dskip,
                                      batch=batch, seq=seq, d_inner=d_inner)
    y_prompt = post(h1, y, z, pooled, gates, p_prompt[0].reshape(batch * seq, -1))

    h1, z, xbc, v, gates, dt = pre(x_sample.reshape(dec_batch * dec_seq, d_model))
    y, pooled, s2, c2, q2 = _mixer_sample(
        xbc, dt, v, state_conv[0], state_pool[0], state_ssm[0].reshape(dec_batch, d_inner, D_STATE),
        conv_w[0], cb, dtb, alog, dskip,
        batch=dec_batch, seq=dec_seq, d_inner=d_inner, pos0=PAST_LEN)
    y_sample = post(h1, y, z, pooled, gates, p_sample[0].reshape(dec_batch * dec_seq, -1))

    return (y_prompt.reshape(batch, seq, d_model),
            y_sample.reshape(dec_batch, dec_seq, d_model),
            s1.reshape(1, batch, n_heads, HEAD_DIM, D_STATE), c1[None], q1[None],
            s2.reshape(1, dec_batch, n_heads, HEAD_DIM, D_STATE), c2[None], q2[None])
```

```python
import functools
import math

import jax
import jax.numpy as jnp
from jax import lax
from jax.experimental import pallas as pl
from jax.experimental.pallas import tpu as pltpu

F32 = jnp.float32
BF16 = jnp.bfloat16
EPS = 1e-6
HIGHEST = lax.Precision.HIGHEST
NEG_LOG2E = -1.4426950408889634

LANES = 128
SUBLANES = 8
VMEM_LIMIT_BYTES = 56 * 1024 * 1024

HEAD_DIM = 64
N_GROUPS = 8
D_STATE = 128
CONV_W = 4
CHUNK = 128
POOL_WINDOWS = (2, 4, 8, 16)
POOL_BUF = max(POOL_WINDOWS) - 1
N_BRANCH = 2
PAST_LEN = 16384
HEADS_PER_TILE = LANES // HEAD_DIM
ROW_STRIDE = 4
POOL_PAD = 24
POOL_PRE = 16


def _sigmoid(x):
    return 1.0 / (1.0 + jnp.exp2(x * NEG_LOG2E))


def _silu(x):
    half = 0.5 * x
    return half + half * jnp.tanh(half)


def _softplus(x):
    return jnp.maximum(x, 0.0) + jnp.log(1.0 + jnp.exp(-jnp.abs(x)))


def _rmsnorm(x, g):
    return x * lax.rsqrt(jnp.mean(x * x, axis=-1, keepdims=True) + EPS) * g


def _dot(a, b):
    return jnp.dot(a, b, preferred_element_type=F32)


def _dot_nt(a, b):
    return lax.dot_general(a, b, (((1,), (1,)), ((), ())), preferred_element_type=F32)


def _const_spec(shape):
    nd = len(shape)
    return pl.BlockSpec(shape, lambda *_: (0,) * nd, pipeline_mode=pl.Buffered(1))


def _params(n_axes):
    return pltpu.CompilerParams(dimension_semantics=("arbitrary",) * n_axes,
                                vmem_limit_bytes=VMEM_LIMIT_BYTES)


def _ffn_kernel(x_ref, g_ref, wgu_ref, wd_ref, *rest, d_ff, fc, with_ple):
    if with_ple:
        p_ref, nple_ref, wg_ref, wp_ref, nfin_ref, o_ref, xn_ref, acc_ref = rest
    else:
        o_ref, xn_ref, acc_ref = rest
    xn_ref[...] = _rmsnorm(x_ref[...], g_ref[...]).astype(BF16)
    for c in range(d_ff // fc):
        xn = xn_ref[...]
        gate = _dot(xn, wgu_ref[:, c * fc:(c + 1) * fc])
        up = _dot(xn, wgu_ref[:, d_ff + c * fc:d_ff + (c + 1) * fc])
        act = (_silu(gate) * up).astype(BF16)
        contrib = _dot(act, wd_ref[c * fc:(c + 1) * fc, :])
        if c == 0:
            acc_ref[...] = contrib
        else:
            acc_ref[...] += contrib
    h = x_ref[...] + 0.5 * acc_ref[...]
    if with_ple:
        gate = _sigmoid(_dot(_rmsnorm(h, nple_ref[...]).astype(BF16), wg_ref[...]))
        h = h + gate * _dot(p_ref[...].astype(BF16), wp_ref[...])
        h = _rmsnorm(h, nfin_ref[...])
    o_ref[...] = h


def _ffn(x, g, w_gu, w_down, ple=None, *, tm):
    t, d = x.shape
    d_ff = w_down.shape[0]
    tm = min(tm, t)
    fc = 256
    in_specs = [pl.BlockSpec((tm, d), lambda i: (i, 0)),
                _const_spec((1, d)),
                _const_spec((d, 2 * d_ff)),
                _const_spec((d_ff, d))]
    args = [x, g, w_gu, w_down]
    if ple is not None:
        p, norm_ple, w_gate, w_ple, norm_final = ple
        in_specs += [pl.BlockSpec((tm, p.shape[1]), lambda i: (i, 0)),
                     _const_spec((1, d)), _const_spec(w_gate.shape), _const_spec(w_ple.shape),
                     _const_spec((1, d))]
        args += [p, norm_ple, w_gate, w_ple, norm_final]
    return pl.pallas_call(
        functools.partial(_ffn_kernel, d_ff=d_ff, fc=fc, with_ple=ple is not None),
        grid=(t // tm,),
        in_specs=in_specs,
        out_specs=pl.BlockSpec((tm, d), lambda i: (i, 0)),
        out_shape=jax.ShapeDtypeStruct((t, d), F32),
        scratch_shapes=[pltpu.VMEM((tm, d), BF16), pltpu.VMEM((tm, d), F32)],
        compiler_params=_params(1),
        name="ffn_ple" if ple is not None else "ffn",
    )(*args)


def _inproj_kernel(h_ref, g_ref, w_ref, *rest, widths, nc):
    out_refs, u_ref = rest[:-1], rest[-1]
    u_ref[...] = _rmsnorm(h_ref[...], g_ref[...]).astype(BF16)
    start = 0
    for o_ref, width in zip(out_refs, widths):
        for a in range(0, width, nc):
            b = min(a + nc, width)
            o_ref[:, a:b] = _dot(u_ref[...], w_ref[:, start + a:start + b])
        start += width


def _inproj(h, g, w_cat, widths, *, tm):
    t, d = h.shape
    tm = min(tm, t)
    n_all = w_cat.shape[1]
    return pl.pallas_call(
        functools.partial(_inproj_kernel, widths=widths, nc=1024),
        grid=(t // tm,),
        in_specs=[pl.BlockSpec((tm, d), lambda i: (i, 0)),
                  _const_spec((1, d)),
                  _const_spec((d, n_all))],
        out_specs=[pl.BlockSpec((tm, w), lambda i: (i, 0)) for w in widths],
        out_shape=[jax.ShapeDtypeStruct((t, w), F32) for w in widths],
        scratch_shapes=[pltpu.VMEM((tm, d), BF16)],
        compiler_params=_params(1),
        name="inproj",
    )(h, g, w_cat)


def _inproj_conv_kernel(h_ref, g_ref, w_ref, cw_ref, cb_ref,
                        z_ref, xc_ref, pooled_ref, gates_ref, dt_ref, conv_ref, pool_ref,
                        u_ref, raw_ref, vraw_ref, *, widths, nc, tiles_per_seq):
    i = pl.program_id(0)
    tm = h_ref.shape[0]
    n_slabs = raw_ref.shape[0]

    n_vslabs = vraw_ref.shape[0]
    tile_in_seq = i % tiles_per_seq
    span = SUBLANES * ROW_STRIDE

    @pl.when(tile_in_seq == 0)
    def _():
        raw_ref[:, 0:SUBLANES, :] = jnp.zeros((n_slabs, SUBLANES, LANES), F32)
        vraw_ref[:, 0:POOL_PRE, :] = jnp.zeros((n_vslabs, POOL_PRE, LANES), F32)

    u_ref[...] = _rmsnorm(h_ref[...], g_ref[...]).astype(BF16)
    starts = [sum(widths[:k]) for k in range(len(widths))]

    def pool_slab(s):
        cols = slice(s * LANES, (s + 1) * LANES)
        win = POOL_WINDOWS[s * len(POOL_WINDOWS) // n_vslabs]
        for base in range(0, tm, span):
            rows = {d: vraw_ref[s, pl.ds(POOL_PRE + base + d, SUBLANES, stride=ROW_STRIDE), :]
                    for d in range(1 - win, ROW_STRIDE)}
            for j in range(ROW_STRIDE):
                total = rows[j]
                for k in range(1, win):
                    total = total + rows[j - k]
                if base >= POOL_BUF:
                    mean = total * (1.0 / win)
                else:
                    pos = (tile_in_seq * tm + base + j
                           + ROW_STRIDE * lax.broadcasted_iota(jnp.int32, (SUBLANES, LANES), 0))
                    mean = total / jnp.minimum(pos + 1, win).astype(F32)
                pooled_ref[s, pl.ds(base + j, SUBLANES, stride=ROW_STRIDE), :] = mean - rows[j]
        pool_ref[0, :, cols] = vraw_ref[s, POOL_PRE + tm - POOL_BUF:POOL_PRE + tm, :]
        vraw_ref[s, 0:POOL_PRE, :] = vraw_ref[s, tm:tm + POOL_PRE, :]

    def conv_slab(s):
        cols = slice(s * LANES, (s + 1) * LANES)
        taps = [jnp.broadcast_to(cw_ref[k:k + 1, cols], (SUBLANES, LANES)) for k in range(CONV_W)]
        bias = jnp.broadcast_to(cb_ref[:, cols], (SUBLANES, LANES))
        for base in range(0, tm, span):
            shifted = [raw_ref[s, pl.ds(SUBLANES + base - (CONV_W - 1) + m, SUBLANES,
                                        stride=ROW_STRIDE), :]
                       for m in range(ROW_STRIDE + CONV_W - 1)]
            for j in range(ROW_STRIDE):
                acc = bias
                for k in range(CONV_W):
                    acc = acc + shifted[j + k] * taps[k]
                xc_ref[s, pl.ds(base + j, SUBLANES, stride=ROW_STRIDE), :] = _silu(acc)
        conv_ref[0, :, cols] = raw_ref[s, SUBLANES + tm - (CONV_W - 1):SUBLANES + tm, :]
        raw_ref[s, 0:SUBLANES, :] = raw_ref[s, tm:tm + SUBLANES, :]

    outs = {0: z_ref, 3: gates_ref, 4: dt_ref}
    units = [(k, a) for k in (1, 2, 0, 3, 4) for a in range(0, widths[k], nc)]
    slabs_per_unit = -(-n_slabs // (len(units) - 2))
    n_xbc_units = widths[1] // nc
    done = 0
    for n, (k, a) in enumerate(units):
        b = min(a + nc, widths[k])
        res = _dot(u_ref[...], w_ref[:, starts[k] + a:starts[k] + b])
        if k == 1:
            for s in range((b - a) // LANES):
                raw_ref[a // LANES + s, SUBLANES:SUBLANES + tm, :] = res[:, s * LANES:(s + 1) * LANES]
        elif k == 2:
            for s in range((b - a) // LANES):
                vraw_ref[a // LANES + s, POOL_PRE:POOL_PRE + tm, :] = res[:, s * LANES:(s + 1) * LANES]
                pool_slab(a // LANES + s)
        else:
            outs[k][:, a:b] = res
        ready = min(n + 1, n_xbc_units) * (nc // LANES)
        want = n_slabs if n == len(units) - 1 else min(n_slabs, n * slabs_per_unit)
        while done < min(want, ready):
            conv_slab(done)
            done += 1
    assert done == n_slabs


def _inproj_conv(h, g, w_cat, widths, conv_w, conv_b, *, tm, seq):
    t, d = h.shape
    tm = min(tm, seq)
    tiles_per_seq = seq // tm
    n_all = w_cat.shape[1]
    conv_dim = widths[1]
    n_slabs = conv_dim // LANES
    assert tm % (SUBLANES * ROW_STRIDE) == 0

    pool_dim = widths[2]
    slabbed = (1, 2)

    def out_spec(k):
        if k in slabbed:
            return pl.BlockSpec((widths[k] // LANES, tm, LANES), lambda i: (0, i, 0))
        return pl.BlockSpec((tm, widths[k]), lambda i: (i, 0))

    def out_shape(k):
        return jax.ShapeDtypeStruct((widths[k] // LANES, t, LANES) if k in slabbed
                                    else (t, widths[k]), F32)

    per_seq = lambda i: (i // tiles_per_seq, 0, 0)

    return pl.pallas_call(
        functools.partial(_inproj_conv_kernel, widths=widths, nc=512, tiles_per_seq=tiles_per_seq),
        grid=(t // tm,),
        in_specs=[pl.BlockSpec((tm, d), lambda i: (i, 0)),
                  _const_spec((1, d)),
                  _const_spec((d, n_all)),
                  _const_spec((CONV_W, conv_dim)),
                  _const_spec((1, conv_dim))],
        out_specs=[out_spec(k) for k in range(len(widths))]
        + [pl.BlockSpec((1, CONV_W - 1, conv_dim), per_seq),
           pl.BlockSpec((1, POOL_BUF, pool_dim), per_seq)],
        out_shape=[out_shape(k) for k in range(len(widths))]
        + [jax.ShapeDtypeStruct((t // seq, CONV_W - 1, conv_dim), F32),
           jax.ShapeDtypeStruct((t // seq, POOL_BUF, pool_dim), F32)],
        scratch_shapes=[pltpu.VMEM((tm, d), BF16),
                        pltpu.VMEM((n_slabs, SUBLANES + tm, LANES), F32),
                        pltpu.VMEM((pool_dim // LANES, POOL_PRE + tm, LANES), F32)],
        compiler_params=_params(1),
        name="inproj_conv",
    )(h, g, w_cat, conv_w, conv_b)


def _chunk_scalars(dt_raw, dt_bias, a_log, seg_len):
    q = dt_raw.shape[0]
    dt = _softplus(dt_raw + dt_bias)
    da = dt * (-jnp.exp(a_log))
    shift = int(math.log2(seg_len))
    qi = lax.broadcasted_iota(jnp.int32, (q, q), 0)
    si = lax.broadcasted_iota(jnp.int32, (q, q), 1)
    same = (qi >> shift) == (si >> shift)
    causal = jnp.logical_and(same, si <= qi)
    a_cs = jnp.dot(jnp.where(causal, 1.0, 0.0), da, precision=HIGHEST, preferred_element_type=F32)
    seg_tot = jnp.dot(jnp.where(same, 1.0, 0.0), da, precision=HIGHEST, preferred_element_type=F32)
    return dt, a_cs, seg_tot, causal


def _pair_cols(vals, h0, shape):
    lane = lax.broadcasted_iota(jnp.int32, shape, 1)
    return jnp.where(lane < HEAD_DIM, vals[:, h0:h0 + 1], vals[:, h0 + 1:h0 + 2])


def _pair_rows(vals_t, h0, shape):
    row = lax.broadcasted_iota(jnp.int32, shape, 0)
    return jnp.where(row < HEAD_DIM, vals_t[h0:h0 + 1, :], vals_t[h0 + 1:h0 + 2, :])


def _diag_pair(xpair, cb, a_cs, a_cs_t, dt_t, causal, h0):
    q = xpair.shape[0]
    lane = lax.broadcasted_iota(jnp.int32, xpair.shape, 1)
    out = None
    spread = []
    for k in range(HEADS_PER_TILE):
        h = h0 + k
        a_col = jnp.broadcast_to(a_cs[:, h:h + 1], (q, LANES))
        spread.append(a_col)
        decay = jnp.exp(jnp.where(causal, a_col - a_cs_t[h:h + 1, :], -jnp.inf))
        w = (cb * decay * dt_t[h:h + 1, :]).astype(BF16)
        mine = (lane < HEAD_DIM) if k == 0 else (lane >= HEAD_DIM)
        t = _dot(w, jnp.where(mine, xpair, 0.0).astype(BF16))
        out = t if out is None else out + t
    e_pair = jnp.exp(jnp.where(lane < HEAD_DIM, spread[0], spread[1]))
    return out, e_pair


def _shift_rows(cur, prev, s):
    sub = lax.broadcasted_iota(jnp.int32, (1,) + cur.shape[1:], 1)
    return jnp.where(sub < s, pltpu.roll(prev, s, axis=1), pltpu.roll(cur, s, axis=1))


def _conv_tiles(cur, prev, w, bias):
    tap = lambda k: w[k:k + 1, :][None]
    acc = cur * tap(CONV_W - 1)
    for s in range(1, CONV_W):
        acc = acc + _shift_rows(cur, prev, s) * tap(CONV_W - 1 - s)
    return _silu(bias[None] + acc)


def _conv_block(ext, w, bias):
    rows, width = ext.shape
    tiles = ext.reshape(rows // SUBLANES, SUBLANES, width)
    return _conv_tiles(tiles[1:], tiles[:-1], w, bias).reshape(rows - SUBLANES, width)


def _conv_tile_rows(rows_of_tap, w, bias):
    acc = rows_of_tap(0) * w[0:1, :]
    for k in range(1, CONV_W):
        acc = acc + rows_of_tap(k) * w[k:k + 1, :]
    return _silu(bias + acc)


def _mixer_prompt_kernel(xc_ref, dt_ref, dtb_ref, alog_ref, dskip_ref, y_ref, ssm_ref, h_ref,
                         *, d_inner, n_chunks):
    c = pl.program_id(1)
    q = xc_ref.shape[1]
    n_state = D_STATE
    last = c == n_chunks - 1

    @pl.when(c == 0)
    def _():
        h_ref[...] = jnp.zeros(h_ref.shape, F32)

    dt, a_cs, seg_tot, causal = _chunk_scalars(dt_ref[...], dtb_ref[...], alog_ref[...], q)
    assert q == LANES and n_state == LANES
    a_cs_t = a_cs.T
    dt_t = dt.T
    w_state_t = (jnp.exp(seg_tot - a_cs) * dt).T
    chunk_decay_t = jnp.exp(seg_tot).T

    for g in range(N_GROUPS):
        b_g = xc_ref[d_inner // LANES + g].astype(BF16)
        c_g = xc_ref[d_inner // LANES + N_GROUPS + g].astype(BF16)
        cb = _dot_nt(c_g, b_g)
        tiles_per_group = d_inner // N_GROUPS // LANES
        for p in range(tiles_per_group):
            tile = g * tiles_per_group + p
            h0 = tile * HEADS_PER_TILE
            cols = slice(tile * LANES, (tile + 1) * LANES)
            xpair = xc_ref[tile]
            h_prev = h_ref[cols, :]
            y, e_pair = _diag_pair(xpair, cb, a_cs, a_cs_t, dt_t, causal, h0)
            y = y + _dot_nt(c_g, h_prev.astype(BF16)) * e_pair
            y_ref[:, cols] = y + dskip_ref[:, cols] * xpair
            xw_t = (xpair.T * _pair_rows(w_state_t, h0, (LANES, q))).astype(BF16)
            decay_rows = _pair_rows(chunk_decay_t, h0, (LANES, n_state))
            h_ref[cols, :] = decay_rows * h_prev + _dot(xw_t, b_g)

    @pl.when(last)
    def _():
        ssm_ref[0] = h_ref[...]


def _mixer_prompt(xc, dt, dt_bias, a_log, d_skip, *, batch, seq, d_inner):
    q = CHUNK
    n_chunks = seq // q
    n_slabs = xc.shape[0]
    n_rows_state = d_inner
    tok = lambda b, c: (b * n_chunks + c, 0)
    per_b = lambda b, c: (b, 0, 0)
    return pl.pallas_call(
        functools.partial(_mixer_prompt_kernel, d_inner=d_inner, n_chunks=n_chunks),
        grid=(batch, n_chunks),
        in_specs=[pl.BlockSpec((n_slabs, q, LANES), lambda b, c: (0, b * n_chunks + c, 0)),
                  pl.BlockSpec((q, LANES), tok),
                  _const_spec((1, LANES)),
                  _const_spec((1, LANES)),
                  _const_spec((1, d_inner))],
        out_specs=[pl.BlockSpec((q, d_inner), tok),
                   pl.BlockSpec((1, n_rows_state, D_STATE), per_b)],
        out_shape=[jax.ShapeDtypeStruct((batch * seq, d_inner), F32),
                   jax.ShapeDtypeStruct((batch, n_rows_state, D_STATE), F32)],
        scratch_shapes=[pltpu.VMEM((n_rows_state, D_STATE), F32)],
        compiler_params=_params(2),
        name="mixer_prompt",
    )(xc, dt, dt_bias, a_log, d_skip)


def _mixer_sample_kernel(xbc_ref, dt_ref, v_ref, conv0_ref, pool0_ref, ssm0_ref,
                         cw_ref, cb_ref, dtb_ref, alog_ref, dskip_ref,
                         y_ref, pooled_ref, ssm_ref, conv_ref, pool_ref,
                         head_ref, xc_ref, xwt_ref, e_ref, seg_ref, pe_ref,
                         *, d_inner, seq, pos0, seqs_per_step):
    j = pl.program_id(1)
    q = xbc_ref.shape[0]
    conv_dim = xbc_ref.shape[1]
    pool_dim = v_ref.shape[1]
    n_state = D_STATE
    n_seq = q // seq
    tiles_per_group = d_inner // N_GROUPS // LANES
    gc = pool_dim // len(POOL_WINDOWS)

    @pl.when(j == 0)
    def _():
        lo = SUBLANES - (CONV_W - 1)

        def per_seq(b, carry):
            r0 = pl.multiple_of(b * seq, seq)
            head_ref[lo:SUBLANES, :] = conv0_ref[b]
            ct = 1024
            for jc in range(conv_dim // ct):
                cols = slice(jc * ct, (jc + 1) * ct)
                head_ref[SUBLANES:SUBLANES + seq, cols] = xbc_ref[pl.ds(r0, seq), cols]
                xc_ref[pl.ds(r0, seq), cols] = _conv_tile_rows(
                    lambda k: head_ref[lo + k:lo + k + seq, cols], cw_ref[:, cols], cb_ref[:, cols])
            conv_ref[b] = head_ref[SUBLANES + seq - (CONV_W - 1):SUBLANES + seq, :]

            pe_ref[POOL_PAD - POOL_BUF:POOL_PAD, :] = pool0_ref[b]
            vb = v_ref[pl.ds(r0, seq), :]
            pe_ref[POOL_PAD:POOL_PAD + seq, :] = vb
            t_idx = lax.broadcasted_iota(jnp.int32, (seq, 1), 0) + pos0
            for gi, win in enumerate(POOL_WINDOWS):
                cols = slice(gi * gc, (gi + 1) * gc)
                total = pe_ref[POOL_PAD:POOL_PAD + seq, cols]
                for k in range(1, win):
                    total = total + pe_ref[POOL_PAD - k:POOL_PAD - k + seq, cols]
                cnt = jnp.minimum(t_idx + 1, win).astype(F32)
                pooled = total / cnt - vb[:, cols]
                for s in range(gc // LANES):
                    pooled_ref[gi * (gc // LANES) + s, pl.ds(r0, seq), :] = (
                        pooled[:, s * LANES:(s + 1) * LANES])
            pool_ref[b] = pe_ref[POOL_PAD + seq - POOL_BUF:POOL_PAD + seq, :]
            return carry

        lax.fori_loop(0, n_seq, per_seq, 0)

        dt, a_cs, seg_tot, causal = _chunk_scalars(dt_ref[...], dtb_ref[...], alog_ref[...], seq)
        a_cs_t = a_cs.T
        dt_t = dt.T
        w_state_t = (jnp.exp(seg_tot - a_cs) * dt).T
        seg_ref[...] = seg_tot
        for g in range(N_GROUPS):
            b_g = xc_ref[:, d_inner + g * n_state:d_inner + (g + 1) * n_state].astype(BF16)
            c_g = xc_ref[:, d_inner + (N_GROUPS + g) * n_state:
                         d_inner + (N_GROUPS + g + 1) * n_state].astype(BF16)
            cb = _dot_nt(c_g, b_g)
            for p in range(tiles_per_group):
                tile = g * tiles_per_group + p
                h0 = tile * HEADS_PER_TILE
                cols = slice(tile * LANES, (tile + 1) * LANES)
                xpair = xc_ref[:, cols]
                y, e_pair = _diag_pair(xpair, cb, a_cs, a_cs_t, dt_t, causal, h0)
                y_ref[:, cols] = y + dskip_ref[:, cols] * xpair
                e_ref[:, cols] = e_pair
                xwt_ref[cols, :] = (xpair.T * _pair_rows(w_state_t, h0, (LANES, q))).astype(BF16)

    row_q = lax.broadcasted_iota(jnp.int32, (q, n_state), 0)
    row128 = lax.broadcasted_iota(jnp.int32, (LANES, n_state), 0)
    for sj in range(seqs_per_step):
        r0 = pl.multiple_of((j * seqs_per_step + sj) * seq, seq)
        chunk_decay = jnp.exp(seg_ref[pl.ds(r0, 1), :])
        mine = jnp.logical_and(row_q >= r0, row_q < r0 + seq)
        for g in range(N_GROUPS):
            b_cols = slice(d_inner + g * n_state, d_inner + (g + 1) * n_state)
            c_cols = slice(d_inner + (N_GROUPS + g) * n_state,
                           d_inner + (N_GROUPS + g + 1) * n_state)
            b_mine = jnp.where(mine, xc_ref[:, b_cols], 0.0).astype(BF16)
            c_rows = xc_ref[pl.ds(r0, seq), c_cols].astype(BF16)
            for p in range(tiles_per_group):
                tile = g * tiles_per_group + p
                h0 = tile * HEADS_PER_TILE
                cols = slice(tile * LANES, (tile + 1) * LANES)
                h_prev = ssm0_ref[sj, cols, :]
                y_off = _dot_nt(c_rows, h_prev.astype(BF16)) * e_ref[pl.ds(r0, seq), cols]
                y_ref[pl.ds(r0, seq), cols] += y_off
                decay_rows = jnp.where(row128 < HEAD_DIM, chunk_decay[:, h0:h0 + 1],
                                       chunk_decay[:, h0 + 1:h0 + 2])
                ssm_ref[sj, cols, :] = decay_rows * h_prev + _dot(xwt_ref[cols, :], b_mine)


def _mixer_sample(xbc, dt, v, conv0, pool0, ssm0, conv_w, conv_b, dt_bias, a_log, d_skip,
                  *, batch, seq, d_inner, pos0):
    q = CHUNK
    n_seq = q // seq
    n_chunks = batch // n_seq
    conv_dim = xbc.shape[1]
    pool_dim = v.shape[1]
    tok = lambda i, j: (i, 0)
    per_chunk = lambda i, j: (i, 0, 0)
    sps = math.gcd(n_seq, 4)
    steps = n_seq // sps
    per_seq = lambda i, j: (i * steps + j, 0, 0)
    return pl.pallas_call(
        functools.partial(_mixer_sample_kernel, d_inner=d_inner, seq=seq, pos0=pos0,
                          seqs_per_step=sps),
        grid=(n_chunks, steps),
        in_specs=[pl.BlockSpec((q, conv_dim), tok),
                  pl.BlockSpec((q, LANES), tok),
                  pl.BlockSpec((q, pool_dim), tok),
                  pl.BlockSpec((n_seq, CONV_W - 1, conv_dim), per_chunk),
                  pl.BlockSpec((n_seq, POOL_BUF, pool_dim), per_chunk),
                  pl.BlockSpec((sps, d_inner, D_STATE), per_seq),
                  _const_spec((CONV_W, conv_dim)),
                  _const_spec((1, conv_dim)),
                  _const_spec((1, LANES)),
                  _const_spec((1, LANES)),
                  _const_spec((1, d_inner))],
        out_specs=[pl.BlockSpec((q, d_inner), tok),
                   pl.BlockSpec((pool_dim // LANES, q, LANES), lambda i, j: (0, i, 0)),
                   pl.BlockSpec((sps, d_inner, D_STATE), per_seq),
                   pl.BlockSpec((n_seq, CONV_W - 1, conv_dim), per_chunk),
                   pl.BlockSpec((n_seq, POOL_BUF, pool_dim), per_chunk)],
        out_shape=[jax.ShapeDtypeStruct((batch * seq, d_inner), F32),
                   jax.ShapeDtypeStruct((pool_dim // LANES, batch * seq, LANES), F32),
                   jax.ShapeDtypeStruct((batch, d_inner, D_STATE), F32),
                   jax.ShapeDtypeStruct((batch, CONV_W - 1, conv_dim), F32),
                   jax.ShapeDtypeStruct((batch, POOL_BUF, pool_dim), F32)],
        scratch_shapes=[pltpu.VMEM((2 * SUBLANES, conv_dim), F32),
                        pltpu.VMEM((q, conv_dim), F32),
                        pltpu.VMEM((d_inner, q), BF16),
                        pltpu.VMEM((q, d_inner), F32),
                        pltpu.VMEM((q, LANES), F32),
                        pltpu.VMEM((POOL_PAD + seq, pool_dim), F32)],
        compiler_params=_params(2),
        name="mixer_sample",
    )(xbc, dt, v, conv0, pool0, ssm0, conv_w, conv_b, dt_bias, a_log, d_skip)


def _merge_kernel(h_ref, y_ref, z_ref, pooled_ref, gates_ref, nssd_ref, wssd_ref, wpg_ref,
                  pscale_ref, wpo_ref, wo_ref, o_ref, yn_ref, pm_ref):
    d_inner = y_ref.shape[1]
    d_model = h_ref.shape[1]
    gw = d_inner // N_GROUPS
    for g in range(N_GROUPS):
        cols = slice(g * gw, (g + 1) * gw)
        yz = y_ref[:, cols] * _silu(z_ref[:, cols])
        yn_ref[:, cols] = _rmsnorm(yz, nssd_ref[:, cols]).astype(BF16)
    a_branch = _dot(yn_ref[...], wssd_ref[...])
    slabs_per_group = pooled_ref.shape[0] // len(POOL_WINDOWS)
    gc = slabs_per_group * LANES
    for gi in range(len(POOL_WINDOWS)):
        cols = slice(gi * gc, (gi + 1) * gc)
        pooled = jnp.concatenate([pooled_ref[gi * slabs_per_group + s]
                                  for s in range(slabs_per_group)], axis=1)
        mixed = _dot(pooled.astype(BF16), wpg_ref[gi])
        pm_ref[:, cols] = (mixed * pscale_ref[:, cols]).astype(BF16)
    b_branch = _dot(pm_ref[...], wpo_ref[...])
    merged = (_sigmoid(gates_ref[:, 0:d_model]) * a_branch
              + _sigmoid(gates_ref[:, d_model:2 * d_model]) * b_branch)
    o_ref[...] = h_ref[...] + _dot(merged.astype(BF16), wo_ref[...])


def _merge(h, y, z, pooled, gates, norm_ssd, w_ssd_out, w_pool_group, pool_scale, w_pool_out, w_o,
           *, tm):
    t, d = h.shape
    tm = min(tm, t)
    d_inner = y.shape[1]
    pool_slabs = pooled.shape[0]
    pool_dim = pool_slabs * LANES
    row = lambda w: pl.BlockSpec((tm, w), lambda i: (i, 0))
    return pl.pallas_call(
        _merge_kernel,
        grid=(t // tm,),
        in_specs=[row(d), row(d_inner), row(d_inner),
                  pl.BlockSpec((pool_slabs, tm, LANES), lambda i: (0, i, 0)),
                  row(N_BRANCH * d),
                  _const_spec((1, d_inner)),
                  _const_spec(w_ssd_out.shape),
                  _const_spec(w_pool_group.shape),
                  _const_spec((1, pool_dim)),
                  _const_spec(w_pool_out.shape),
                  _const_spec(w_o.shape)],
        out_specs=row(d),
        out_shape=jax.ShapeDtypeStruct((t, d), F32),
        scratch_shapes=[pltpu.VMEM((tm, d_inner), BF16), pltpu.VMEM((tm, pool_dim), BF16)],
        compiler_params=_params(1),
        name="merge",
    )(h, y, z, pooled, gates, norm_ssd, w_ssd_out, w_pool_group, pool_scale, w_pool_out, w_o)


def _pad_lanes(a):
    return jnp.pad(a.reshape(1, -1), ((0, 0), (0, LANES - a.shape[-1])))


def kernel(x_prompt, x_sample, state_ssm, state_conv, state_pool, p_prompt, p_sample, norm_ffn1, w_ffn1_gu, w_ffn1_down, norm_mix, w_in, conv_w, conv_b, dt_bias, a_log, d_skip, norm_ssd, w_ssd_out, w_pool_group, pool_scale, w_pool_out, w_o, norm_ffn2, w_ffn2_gu, w_ffn2_down, norm_ple, w_ple_gate, w_ple, norm_final):
    depth = norm_ffn1.shape[0]
    assert depth == 1, "the final norm is fused into the layer's last stage: one layer only"
    batch, seq, d_model = x_prompt.shape
    dec_batch, dec_seq, _ = x_sample.shape
    n_heads = dt_bias.shape[1]
    d_inner = n_heads * HEAD_DIM
    conv_dim = conv_w.shape[2]
    pool_dim = pool_scale.shape[1]
    assert seq % CHUNK == 0 and CHUNK % dec_seq == 0 and dec_batch % (CHUNK // dec_seq) == 0

    row = lambda a: a[0].reshape(1, -1)
    cuts = [0, d_inner, d_inner + conv_dim, d_inner + conv_dim + n_heads,
            d_inner + conv_dim + n_heads + pool_dim, w_in.shape[2]]
    wz, wxbc, wdt, wv, wgates = [w_in[0][:, a:b] for a, b in zip(cuts[:-1], cuts[1:])]
    wdt = jnp.pad(wdt, ((0, 0), (0, LANES - n_heads)))
    w_cat = jnp.concatenate([wz, wxbc, wv, wgates, wdt], axis=1).astype(BF16)
    widths = (d_inner, conv_dim, pool_dim, N_BRANCH * d_model, LANES)
    w1gu, w1d = w_ffn1_gu[0].astype(BF16), w_ffn1_down[0].astype(BF16)
    w2gu, w2d = w_ffn2_gu[0].astype(BF16), w_ffn2_down[0].astype(BF16)
    wssd, wpg = w_ssd_out[0].astype(BF16), w_pool_group[0].astype(BF16)
    wpo, wo = w_pool_out[0].astype(BF16), w_o[0].astype(BF16)
    wpleg, wple = w_ple_gate[0].astype(BF16), w_ple[0].astype(BF16)
    dtb, alog = _pad_lanes(dt_bias[0]), _pad_lanes(a_log[0])
    dskip = jnp.repeat(d_skip[0], HEAD_DIM).reshape(1, d_inner)
    cb = row(conv_b)
    nfin = norm_final.reshape(1, -1)

    def pre(h):
        h1 = _ffn(h, row(norm_ffn1), w1gu, w1d, tm=512)
        return (h1,) + tuple(_inproj(h1, row(norm_mix), w_cat, widths, tm=256))

    def post(h1, y, z, pooled, gates, p):
        h2 = _merge(h1, y, z, pooled, gates, row(norm_ssd), wssd, wpg, row(pool_scale), wpo, wo,
                    tm=256)
        return _ffn(h2, row(norm_ffn2), w2gu, w2d, (p, row(norm_ple), wpleg, wple, nfin), tm=512)

    h1 = _ffn(x_prompt.reshape(batch * seq, d_model), row(norm_ffn1), w1gu, w1d, tm=512)
    z, xc, pooled, gates, dt, c1, q1 = _inproj_conv(h1, row(norm_mix), w_cat, widths, conv_w[0], cb,
                                                    tm=256, seq=seq)
    y, s1 = _mixer_prompt(xc, dt, dtb, alog, dskip, batch=batch, seq=seq, d_inner=d_inner)
    y_prompt = post(h1, y, z, pooled, gates, p_prompt[0].reshape(batch * seq, -1))

    h1, z, xbc, v, gates, dt = pre(x_sample.reshape(dec_batch * dec_seq, d_model))
    y, pooled, s2, c2, q2 = _mixer_sample(
        xbc, dt, v, state_conv[0], state_pool[0], state_ssm[0].reshape(dec_batch, d_inner, D_STATE),
        conv_w[0], cb, dtb, alog, dskip,
        batch=dec_batch, seq=dec_seq, d_inner=d_inner, pos0=PAST_LEN)
    y_sample = post(h1, y, z, pooled, gates, p_sample[0].reshape(dec_batch * dec_seq, -1))

    return (y_prompt.reshape(batch, seq, d_model),
            y_sample.reshape(dec_batch, dec_seq, d_model),
            s1.reshape(1, batch, n_heads, HEAD_DIM, D_STATE), c1[None], q1[None],
            s2.reshape(1, dec_batch, n_heads, HEAD_DIM, D_STATE), c2[None], q2[None])
```

```python
import functools
import math

import jax
import jax.numpy as jnp
from jax import lax
from jax.experimental import pallas as pl
from jax.experimental.pallas import tpu as pltpu

F32 = jnp.float32
BF16 = jnp.bfloat16
EPS = 1e-6
HIGHEST = lax.Precision.HIGHEST
NEG_LOG2E = -1.4426950408889634

LANES = 128
SUBLANES = 8
VMEM_LIMIT_BYTES = 56 * 1024 * 1024

HEAD_DIM = 64
N_GROUPS = 8
D_STATE = 128
CONV_W = 4
CHUNK = 128
POOL_WINDOWS = (2, 4, 8, 16)
POOL_BUF = max(POOL_WINDOWS) - 1
N_BRANCH = 2
PAST_LEN = 16384
HEADS_PER_TILE = LANES // HEAD_DIM
ROW_STRIDE = 4
POOL_PAD = 24
POOL_PRE = 16


def _sigmoid(x):
    return 1.0 / (1.0 + jnp.exp2(x * NEG_LOG2E))


def _silu(x):
    half = 0.5 * x
    return half + half * jnp.tanh(half)


def _softplus(x):
    return jnp.maximum(x, 0.0) + jnp.log(1.0 + jnp.exp(-jnp.abs(x)))


def _rmsnorm(x, g):
    return x * lax.rsqrt(jnp.mean(x * x, axis=-1, keepdims=True) + EPS) * g


def _dot(a, b):
    return jnp.dot(a, b, preferred_element_type=F32)


def _dot_nt(a, b):
    return lax.dot_general(a, b, (((1,), (1,)), ((), ())), preferred_element_type=F32)


def _const_spec(shape):
    nd = len(shape)
    return pl.BlockSpec(shape, lambda *_: (0,) * nd, pipeline_mode=pl.Buffered(1))


def _params(n_axes):
    return pltpu.CompilerParams(dimension_semantics=("arbitrary",) * n_axes,
                                vmem_limit_bytes=VMEM_LIMIT_BYTES)


def _ffn_kernel(x_ref, g_ref, wgu_ref, wd_ref, *rest, d_ff, fc, with_ple):
    if with_ple:
        p_ref, nple_ref, wg_ref, wp_ref, nfin_ref, o_ref, xn_ref, acc_ref = rest
    else:
        o_ref, xn_ref, acc_ref = rest
    xn_ref[...] = _rmsnorm(x_ref[...], g_ref[...]).astype(BF16)
    for c in range(d_ff // fc):
        xn = xn_ref[...]
        gate = _dot(xn, wgu_ref[:, c * fc:(c + 1) * fc])
        up = _dot(xn, wgu_ref[:, d_ff + c * fc:d_ff + (c + 1) * fc])
        act = (_silu(gate) * up).astype(BF16)
        contrib = _dot(act, wd_ref[c * fc:(c + 1) * fc, :])
        if c == 0:
            acc_ref[...] = contrib
        else:
            acc_ref[...] += contrib
    h = x_ref[...] + 0.5 * acc_ref[...]
    if with_ple:
        gate = _sigmoid(_dot(_rmsnorm(h, nple_ref[...]).astype(BF16), wg_ref[...]))
        h = h + gate * _dot(p_ref[...].astype(BF16), wp_ref[...])
        h = _rmsnorm(h, nfin_ref[...])
    o_ref[...] = h


def _ffn(x, g, w_gu, w_down, ple=None, *, tm):
    t, d = x.shape
    d_ff = w_down.shape[0]
    tm = min(tm, t)
    fc = 256
    in_specs = [pl.BlockSpec((tm, d), lambda i: (i, 0)),
                _const_spec((1, d)),
                _const_spec((d, 2 * d_ff)),
                _const_spec((d_ff, d))]
    args = [x, g, w_gu, w_down]
    if ple is not None:
        p, norm_ple, w_gate, w_ple, norm_final = ple
        in_specs += [pl.BlockSpec((tm, p.shape[1]), lambda i: (i, 0)),
                     _const_spec((1, d)), _const_spec(w_gate.shape), _const_spec(w_ple.shape),
                     _const_spec((1, d))]
        args += [p, norm_ple, w_gate, w_ple, norm_final]
    return pl.pallas_call(
        functools.partial(_ffn_kernel, d_ff=d_ff, fc=fc, with_ple=ple is not None),
        grid=(t // tm,),
        in_specs=in_specs,
        out_specs=pl.BlockSpec((tm, d), lambda i: (i, 0)),
        out_shape=jax.ShapeDtypeStruct((t, d), F32),
        scratch_shapes=[pltpu.VMEM((tm, d), BF16), pltpu.VMEM((tm, d), F32)],
        compiler_params=_params(1),
        name="ffn_ple" if ple is not None else "ffn",
    )(*args)


def _inproj_kernel(h_ref, g_ref, w_ref, *rest, widths, nc):
    out_refs, u_ref = rest[:-1], rest[-1]
    u_ref[...] = _rmsnorm(h_ref[...], g_ref[...]).astype(BF16)
    start = 0
    for o_ref, width in zip(out_refs, widths):
        for a in range(0, width, nc):
            b = min(a + nc, width)
            o_ref[:, a:b] = _dot(u_ref[...], w_ref[:, start + a:start + b])
        start += width


def _inproj(h, g, w_cat, widths, *, tm):
    t, d = h.shape
    tm = min(tm, t)
    n_all = w_cat.shape[1]
    return pl.pallas_call(
        functools.partial(_inproj_kernel, widths=widths, nc=1024),
        grid=(t // tm,),
        in_specs=[pl.BlockSpec((tm, d), lambda i: (i, 0)),
                  _const_spec((1, d)),
                  _const_spec((d, n_all))],
        out_specs=[pl.BlockSpec((tm, w), lambda i: (i, 0)) for w in widths],
        out_shape=[jax.ShapeDtypeStruct((t, w), F32) for w in widths],
        scratch_shapes=[pltpu.VMEM((tm, d), BF16)],
        compiler_params=_params(1),
        name="inproj",
    )(h, g, w_cat)


def _inproj_mixer_kernel(h_ref, g_ref, w_ref, cw_ref, cb_ref, dtb_ref, alog_ref, dskip_ref,
                         z_ref, y_ref, pooled_ref, gates_ref, conv_ref, pool_ref, ssm_ref,
                         u_ref, raw_ref, vraw_ref, xc_ref, state_ref,
                         *, widths, nc, tiles_per_seq, d_inner):
    i = pl.program_id(0)
    tm = h_ref.shape[0]
    n_slabs = raw_ref.shape[0]

    n_vslabs = vraw_ref.shape[0]
    tile_in_seq = i % tiles_per_seq
    span = SUBLANES * ROW_STRIDE

    @pl.when(tile_in_seq == 0)
    def _():
        raw_ref[:, 0:SUBLANES, :] = jnp.zeros((n_slabs, SUBLANES, LANES), F32)
        vraw_ref[:, 0:POOL_PRE, :] = jnp.zeros((n_vslabs, POOL_PRE, LANES), F32)
        state_ref[...] = jnp.zeros(state_ref.shape, F32)

    u_ref[...] = _rmsnorm(h_ref[...], g_ref[...]).astype(BF16)
    starts = [sum(widths[:k]) for k in range(len(widths))]

    def pool_slab(s):
        cols = slice(s * LANES, (s + 1) * LANES)
        win = POOL_WINDOWS[s * len(POOL_WINDOWS) // n_vslabs]
        for base in range(0, tm, span):
            rows = {d: vraw_ref[s, pl.ds(POOL_PRE + base + d, SUBLANES, stride=ROW_STRIDE), :]
                    for d in range(1 - win, ROW_STRIDE)}
            for j in range(ROW_STRIDE):
                total = rows[j]
                for k in range(1, win):
                    total = total + rows[j - k]
                if base >= POOL_BUF:
                    mean = total * (1.0 / win)
                else:
                    pos = (tile_in_seq * tm + base + j
                           + ROW_STRIDE * lax.broadcasted_iota(jnp.int32, (SUBLANES, LANES), 0))
                    mean = total / jnp.minimum(pos + 1, win).astype(F32)
                pooled_ref[s, pl.ds(base + j, SUBLANES, stride=ROW_STRIDE), :] = mean - rows[j]
        pool_ref[0, :, cols] = vraw_ref[s, POOL_PRE + tm - POOL_BUF:POOL_PRE + tm, :]
        vraw_ref[s, 0:POOL_PRE, :] = vraw_ref[s, tm:tm + POOL_PRE, :]

    def conv_slab(s):
        cols = slice(s * LANES, (s + 1) * LANES)
        taps = [jnp.broadcast_to(cw_ref[k:k + 1, cols], (SUBLANES, LANES)) for k in range(CONV_W)]
        bias = jnp.broadcast_to(cb_ref[:, cols], (SUBLANES, LANES))
        for base in range(0, tm, span):
            shifted = [raw_ref[s, pl.ds(SUBLANES + base - (CONV_W - 1) + m, SUBLANES,
                                        stride=ROW_STRIDE), :]
                       for m in range(ROW_STRIDE + CONV_W - 1)]
            for j in range(ROW_STRIDE):
                acc = bias
                for k in range(CONV_W):
                    acc = acc + shifted[j + k] * taps[k]
                xc_ref[s, pl.ds(base + j, SUBLANES, stride=ROW_STRIDE), :] = _silu(acc)
        conv_ref[0, :, cols] = raw_ref[s, SUBLANES + tm - (CONV_W - 1):SUBLANES + tm, :]
        raw_ref[s, 0:SUBLANES, :] = raw_ref[s, tm:tm + SUBLANES, :]

    def project(k, a):
        b = min(a + nc, widths[k])
        return _dot(u_ref[...], w_ref[:, starts[k] + a:starts[k] + b])

    n_xbc_units = widths[1] // nc
    slabs_per_unit = nc // LANES
    done = 0
    for n, (k, a) in enumerate([(k, a) for k in (1, 2) for a in range(0, widths[k], nc)]):
        res = project(k, a)
        for s in range(res.shape[1] // LANES):
            piece = res[:, s * LANES:(s + 1) * LANES]
            if k == 1:
                raw_ref[a // LANES + s, SUBLANES:SUBLANES + tm, :] = piece
            else:
                vraw_ref[a // LANES + s, POOL_PRE:POOL_PRE + tm, :] = piece
                pool_slab(a // LANES + s)
        while done < min(n_slabs, n * slabs_per_unit, min(n + 1, n_xbc_units) * slabs_per_unit):
            conv_slab(done)
            done += 1
    dt_raw = project(4, 0)
    while done < n_slabs:
        conv_slab(done)
        done += 1

    q = CHUNK
    n_state = D_STATE
    assert q == LANES and n_state == LANES
    late_units = [(k, a) for k in (0, 3) for a in range(0, widths[k], nc)]
    late_outs = {0: z_ref, 3: gates_ref}
    n_blocks = (tm // q) * N_GROUPS
    emitted = 0
    scalars = []
    for ci in range(tm // q):
        dt, a_cs, seg_tot, causal = _chunk_scalars(dt_raw[ci * q:(ci + 1) * q, :], dtb_ref[...],
                                                   alog_ref[...], q)
        scalars.append((a_cs, a_cs.T, dt.T, (jnp.exp(seg_tot - a_cs) * dt).T, jnp.exp(seg_tot).T,
                        causal))
    tiles_per_group = d_inner // N_GROUPS // LANES
    for ci in range(tm // q):
        rows = slice(ci * q, (ci + 1) * q)
        a_cs, a_cs_t, dt_t, w_state_t, chunk_decay_t, causal = scalars[ci]
        for g in range(N_GROUPS):
            b_g = xc_ref[d_inner // LANES + g, rows, :].astype(BF16)
            c_g = xc_ref[d_inner // LANES + N_GROUPS + g, rows, :].astype(BF16)
            cb = _dot_nt(c_g, b_g)
            for p in range(tiles_per_group):
                tile = g * tiles_per_group + p
                h0 = tile * HEADS_PER_TILE
                cols = slice(tile * LANES, (tile + 1) * LANES)
                xpair = xc_ref[tile, rows, :]
                h_prev = state_ref[cols, :]
                y, e_pair = _diag_pair(xpair, cb, a_cs, a_cs_t, dt_t, causal, h0)
                y = y + _dot_nt(c_g, h_prev.astype(BF16)) * e_pair
                y_ref[rows, cols] = y + dskip_ref[:, cols] * xpair
                xw_t = (xpair.T * _pair_rows(w_state_t, h0, (LANES, q))).astype(BF16)
                decay_rows = _pair_rows(chunk_decay_t, h0, (LANES, n_state))
                state_ref[cols, :] = decay_rows * h_prev + _dot(xw_t, b_g)
            block = ci * N_GROUPS + g + 1
            while emitted < len(late_units) and emitted * n_blocks < block * len(late_units):
                k, a = late_units[emitted]
                res = project(k, a)
                late_outs[k][:, a:a + res.shape[1]] = res
                emitted += 1
    assert emitted == len(late_units)

    @pl.when(tile_in_seq == tiles_per_seq - 1)
    def _():
        ssm_ref[0] = state_ref[...]


def _inproj_mixer(h, g, w_cat, widths, conv_w, conv_b, dt_bias, a_log, d_skip, *, tm, seq, d_inner):
    t, d = h.shape
    tm = min(tm, seq)
    tiles_per_seq = seq // tm
    n_all = w_cat.shape[1]
    conv_dim, pool_dim = widths[1], widths[2]
    n_slabs = conv_dim // LANES
    assert tm % (SUBLANES * ROW_STRIDE) == 0 and tm % CHUNK == 0
    row = lambda w: pl.BlockSpec((tm, w), lambda i: (i, 0))
    per_seq = lambda i: (i // tiles_per_seq, 0, 0)
    return pl.pallas_call(
        functools.partial(_inproj_mixer_kernel, widths=widths, nc=512, tiles_per_seq=tiles_per_seq,
                          d_inner=d_inner),
        grid=(t // tm,),
        in_specs=[row(d),
                  _const_spec((1, d)),
                  _const_spec((d, n_all)),
                  _const_spec((CONV_W, conv_dim)),
                  _const_spec((1, conv_dim)),
                  _const_spec((1, LANES)),
                  _const_spec((1, LANES)),
                  _const_spec((1, d_inner))],
        out_specs=[row(widths[0]), row(d_inner),
                   pl.BlockSpec((pool_dim // LANES, tm, LANES), lambda i: (0, i, 0)),
                   row(widths[3]),
                   pl.BlockSpec((1, CONV_W - 1, conv_dim), per_seq),
                   pl.BlockSpec((1, POOL_BUF, pool_dim), per_seq),
                   pl.BlockSpec((1, d_inner, D_STATE), per_seq)],
        out_shape=[jax.ShapeDtypeStruct((t, widths[0]), F32),
                   jax.ShapeDtypeStruct((t, d_inner), F32),
                   jax.ShapeDtypeStruct((pool_dim // LANES, t, LANES), F32),
                   jax.ShapeDtypeStruct((t, widths[3]), F32),
                   jax.ShapeDtypeStruct((t // seq, CONV_W - 1, conv_dim), F32),
                   jax.ShapeDtypeStruct((t // seq, POOL_BUF, pool_dim), F32),
                   jax.ShapeDtypeStruct((t // seq, d_inner, D_STATE), F32)],
        scratch_shapes=[pltpu.VMEM((tm, d), BF16),
                        pltpu.VMEM((n_slabs, SUBLANES + tm, LANES), F32),
                        pltpu.VMEM((pool_dim // LANES, POOL_PRE + tm, LANES), F32),
                        pltpu.VMEM((n_slabs, tm, LANES), F32),
                        pltpu.VMEM((d_inner, D_STATE), F32)],
        compiler_params=_params(1),
        name="inproj_mixer",
    )(h, g, w_cat, conv_w, conv_b, dt_bias, a_log, d_skip)


def _chunk_scalars(dt_raw, dt_bias, a_log, seg_len):
    q = dt_raw.shape[0]
    dt = _softplus(dt_raw + dt_bias)
    da = dt * (-jnp.exp(a_log))
    shift = int(math.log2(seg_len))
    qi = lax.broadcasted_iota(jnp.int32, (q, q), 0)
    si = lax.broadcasted_iota(jnp.int32, (q, q), 1)
    same = (qi >> shift) == (si >> shift)
    causal = jnp.logical_and(same, si <= qi)
    a_cs = jnp.dot(jnp.where(causal, 1.0, 0.0), da, precision=HIGHEST, preferred_element_type=F32)
    seg_tot = jnp.dot(jnp.where(same, 1.0, 0.0), da, precision=HIGHEST, preferred_element_type=F32)
    return dt, a_cs, seg_tot, causal


def _pair_cols(vals, h0, shape):
    lane = lax.broadcasted_iota(jnp.int32, shape, 1)
    return jnp.where(lane < HEAD_DIM, vals[:, h0:h0 + 1], vals[:, h0 + 1:h0 + 2])


def _pair_rows(vals_t, h0, shape):
    row = lax.broadcasted_iota(jnp.int32, shape, 0)
    return jnp.where(row < HEAD_DIM, vals_t[h0:h0 + 1, :], vals_t[h0 + 1:h0 + 2, :])


def _diag_pair(xpair, cb, a_cs, a_cs_t, dt_t, causal, h0):
    q = xpair.shape[0]
    lane = lax.broadcasted_iota(jnp.int32, xpair.shape, 1)
    out = None
    spread = []
    for k in range(HEADS_PER_TILE):
        h = h0 + k
        a_col = jnp.broadcast_to(a_cs[:, h:h + 1], (q, LANES))
        spread.append(a_col)
        decay = jnp.exp(jnp.where(causal, a_col - a_cs_t[h:h + 1, :], -jnp.inf))
        w = (cb * decay * dt_t[h:h + 1, :]).astype(BF16)
        mine = (lane < HEAD_DIM) if k == 0 else (lane >= HEAD_DIM)
        t = _dot(w, jnp.where(mine, xpair, 0.0).astype(BF16))
        out = t if out is None else out + t
    e_pair = jnp.exp(jnp.where(lane < HEAD_DIM, spread[0], spread[1]))
    return out, e_pair


def _shift_rows(cur, prev, s):
    sub = lax.broadcasted_iota(jnp.int32, (1,) + cur.shape[1:], 1)
    return jnp.where(sub < s, pltpu.roll(prev, s, axis=1), pltpu.roll(cur, s, axis=1))


def _conv_tiles(cur, prev, w, bias):
    tap = lambda k: w[k:k + 1, :][None]
    acc = cur * tap(CONV_W - 1)
    for s in range(1, CONV_W):
        acc = acc + _shift_rows(cur, prev, s) * tap(CONV_W - 1 - s)
    return _silu(bias[None] + acc)


def _conv_block(ext, w, bias):
    rows, width = ext.shape
    tiles = ext.reshape(rows // SUBLANES, SUBLANES, width)
    return _conv_tiles(tiles[1:], tiles[:-1], w, bias).reshape(rows - SUBLANES, width)


def _conv_tile_rows(rows_of_tap, w, bias):
    acc = rows_of_tap(0) * w[0:1, :]
    for k in range(1, CONV_W):
        acc = acc + rows_of_tap(k) * w[k:k + 1, :]
    return _silu(bias + acc)


def _mixer_sample_kernel(xbc_ref, dt_ref, v_ref, conv0_ref, pool0_ref, ssm0_ref,
                         cw_ref, cb_ref, dtb_ref, alog_ref, dskip_ref,
                         y_ref, pooled_ref, ssm_ref, conv_ref, pool_ref,
                         head_ref, xc_ref, xwt_ref, e_ref, seg_ref, pe_ref,
                         *, d_inner, seq, pos0, seqs_per_step):
    j = pl.program_id(1)
    q = xbc_ref.shape[0]
    conv_dim = xbc_ref.shape[1]
    pool_dim = v_ref.shape[1]
    n_state = D_STATE
    n_seq = q // seq
    tiles_per_group = d_inner // N_GROUPS // LANES
    gc = pool_dim // len(POOL_WINDOWS)

    @pl.when(j == 0)
    def _():
        lo = SUBLANES - (CONV_W - 1)

        def per_seq(b, carry):
            r0 = pl.multiple_of(b * seq, seq)
            head_ref[lo:SUBLANES, :] = conv0_ref[b]
            ct = 1024
            for jc in range(conv_dim // ct):
                cols = slice(jc * ct, (jc + 1) * ct)
                head_ref[SUBLANES:SUBLANES + seq, cols] = xbc_ref[pl.ds(r0, seq), cols]
                xc_ref[pl.ds(r0, seq), cols] = _conv_tile_rows(
                    lambda k: head_ref[lo + k:lo + k + seq, cols], cw_ref[:, cols], cb_ref[:, cols])
            conv_ref[b] = head_ref[SUBLANES + seq - (CONV_W - 1):SUBLANES + seq, :]

            pe_ref[POOL_PAD - POOL_BUF:POOL_PAD, :] = pool0_ref[b]
            vb = v_ref[pl.ds(r0, seq), :]
            pe_ref[POOL_PAD:POOL_PAD + seq, :] = vb
            t_idx = lax.broadcasted_iota(jnp.int32, (seq, 1), 0) + pos0
            for gi, win in enumerate(POOL_WINDOWS):
                cols = slice(gi * gc, (gi + 1) * gc)
                total = pe_ref[POOL_PAD:POOL_PAD + seq, cols]
                for k in range(1, win):
                    total = total + pe_ref[POOL_PAD - k:POOL_PAD - k + seq, cols]
                cnt = jnp.minimum(t_idx + 1, win).astype(F32)
                pooled = total / cnt - vb[:, cols]
                for s in range(gc // LANES):
                    pooled_ref[gi * (gc // LANES) + s, pl.ds(r0, seq), :] = (
                        pooled[:, s * LANES:(s + 1) * LANES])
            pool_ref[b] = pe_ref[POOL_PAD + seq - POOL_BUF:POOL_PAD + seq, :]
            return carry

        lax.fori_loop(0, n_seq, per_seq, 0)

        dt, a_cs, seg_tot, causal = _chunk_scalars(dt_ref[...], dtb_ref[...], alog_ref[...], seq)
        a_cs_t = a_cs.T
        dt_t = dt.T
        w_state_t = (jnp.exp(seg_tot - a_cs) * dt).T
        seg_ref[...] = seg_tot
        for g in range(N_GROUPS):
            b_g = xc_ref[:, d_inner + g * n_state:d_inner + (g + 1) * n_state].astype(BF16)
            c_g = xc_ref[:, d_inner + (N_GROUPS + g) * n_state:
                         d_inner + (N_GROUPS + g + 1) * n_state].astype(BF16)
            cb = _dot_nt(c_g, b_g)
            for p in range(tiles_per_group):
                tile = g * tiles_per_group + p
                h0 = tile * HEADS_PER_TILE
                cols = slice(tile * LANES, (tile + 1) * LANES)
                xpair = xc_ref[:, cols]
                y, e_pair = _diag_pair(xpair, cb, a_cs, a_cs_t, dt_t, causal, h0)
                y_ref[:, cols] = y + dskip_ref[:, cols] * xpair
                e_ref[:, cols] = e_pair
                xwt_ref[cols, :] = (xpair.T * _pair_rows(w_state_t, h0, (LANES, q))).astype(BF16)

    row_q = lax.broadcasted_iota(jnp.int32, (q, n_state), 0)
    row128 = lax.broadcasted_iota(jnp.int32, (LANES, n_state), 0)
    for sj in range(seqs_per_step):
        r0 = pl.multiple_of((j * seqs_per_step + sj) * seq, seq)
        chunk_decay = jnp.exp(seg_ref[pl.ds(r0, 1), :])
        mine = jnp.logical_and(row_q >= r0, row_q < r0 + seq)
        for g in range(N_GROUPS):
            b_cols = slice(d_inner + g * n_state, d_inner + (g + 1) * n_state)
            c_cols = slice(d_inner + (N_GROUPS + g) * n_state,
                           d_inner + (N_GROUPS + g + 1) * n_state)
            b_mine = jnp.where(mine, xc_ref[:, b_cols], 0.0).astype(BF16)
            c_rows = xc_ref[pl.ds(r0, seq), c_cols].astype(BF16)
            for p in range(tiles_per_group):
                tile = g * tiles_per_group + p
                h0 = tile * HEADS_PER_TILE
                cols = slice(tile * LANES, (tile + 1) * LANES)
                h_prev = ssm0_ref[sj, cols, :]
                y_off = _dot_nt(c_rows, h_prev.astype(BF16)) * e_ref[pl.ds(r0, seq), cols]
                y_ref[pl.ds(r0, seq), cols] += y_off
                decay_rows = jnp.where(row128 < HEAD_DIM, chunk_decay[:, h0:h0 + 1],
                                       chunk_decay[:, h0 + 1:h0 + 2])
                ssm_ref[sj, cols, :] = decay_rows * h_prev + _dot(xwt_ref[cols, :], b_mine)


def _mixer_sample(xbc, dt, v, conv0, pool0, ssm0, conv_w, conv_b, dt_bias, a_log, d_skip,
                  *, batch, seq, d_inner, pos0):
    q = CHUNK
    n_seq = q // seq
    n_chunks = batch // n_seq
    conv_dim = xbc.shape[1]
    pool_dim = v.shape[1]
    tok = lambda i, j: (i, 0)
    per_chunk = lambda i, j: (i, 0, 0)
    sps = math.gcd(n_seq, 4)
    steps = n_seq // sps
    per_seq = lambda i, j: (i * steps + j, 0, 0)
    return pl.pallas_call(
        functools.partial(_mixer_sample_kernel, d_inner=d_inner, seq=seq, pos0=pos0,
                          seqs_per_step=sps),
        grid=(n_chunks, steps),
        in_specs=[pl.BlockSpec((q, conv_dim), tok),
                  pl.BlockSpec((q, LANES), tok),
                  pl.BlockSpec((q, pool_dim), tok),
                  pl.BlockSpec((n_seq, CONV_W - 1, conv_dim), per_chunk),
                  pl.BlockSpec((n_seq, POOL_BUF, pool_dim), per_chunk),
                  pl.BlockSpec((sps, d_inner, D_STATE), per_seq),
                  _const_spec((CONV_W, conv_dim)),
                  _const_spec((1, conv_dim)),
                  _const_spec((1, LANES)),
                  _const_spec((1, LANES)),
                  _const_spec((1, d_inner))],
        out_specs=[pl.BlockSpec((q, d_inner), tok),
                   pl.BlockSpec((pool_dim // LANES, q, LANES), lambda i, j: (0, i, 0)),
                   pl.BlockSpec((sps, d_inner, D_STATE), per_seq),
                   pl.BlockSpec((n_seq, CONV_W - 1, conv_dim), per_chunk),
                   pl.BlockSpec((n_seq, POOL_BUF, pool_dim), per_chunk)],
        out_shape=[jax.ShapeDtypeStruct((batch * seq, d_inner), F32),
                   jax.ShapeDtypeStruct((pool_dim // LANES, batch * seq, LANES), F32),
                   jax.ShapeDtypeStruct((batch, d_inner, D_STATE), F32),
                   jax.ShapeDtypeStruct((batch, CONV_W - 1, conv_dim), F32),
                   jax.ShapeDtypeStruct((batch, POOL_BUF, pool_dim), F32)],
        scratch_shapes=[pltpu.VMEM((2 * SUBLANES, conv_dim), F32),
                        pltpu.VMEM((q, conv_dim), F32),
                        pltpu.VMEM((d_inner, q), BF16),
                        pltpu.VMEM((q, d_inner), F32),
                        pltpu.VMEM((q, LANES), F32),
                        pltpu.VMEM((POOL_PAD + seq, pool_dim), F32)],
        compiler_params=_params(2),
        name="mixer_sample",
    )(xbc, dt, v, conv0, pool0, ssm0, conv_w, conv_b, dt_bias, a_log, d_skip)


def _merge_kernel(h_ref, y_ref, z_ref, pooled_ref, gates_ref, nssd_ref, wssd_ref, wpg_ref,
                  pscale_ref, wpo_ref, wo_ref, o_ref, yn_ref, pm_ref):
    d_inner = y_ref.shape[1]
    d_model = h_ref.shape[1]
    gw = d_inner // N_GROUPS
    for g in range(N_GROUPS):
        cols = slice(g * gw, (g + 1) * gw)
        yz = y_ref[:, cols] * _silu(z_ref[:, cols])
        yn_ref[:, cols] = _rmsnorm(yz, nssd_ref[:, cols]).astype(BF16)
    a_branch = _dot(yn_ref[...], wssd_ref[...])
    slabs_per_group = pooled_ref.shape[0] // len(POOL_WINDOWS)
    gc = slabs_per_group * LANES
    for gi in range(len(POOL_WINDOWS)):
        cols = slice(gi * gc, (gi + 1) * gc)
        pooled = jnp.concatenate([pooled_ref[gi * slabs_per_group + s]
                                  for s in range(slabs_per_group)], axis=1)
        mixed = _dot(pooled.astype(BF16), wpg_ref[gi])
        pm_ref[:, cols] = (mixed * pscale_ref[:, cols]).astype(BF16)
    b_branch = _dot(pm_ref[...], wpo_ref[...])
    merged = (_sigmoid(gates_ref[:, 0:d_model]) * a_branch
              + _sigmoid(gates_ref[:, d_model:2 * d_model]) * b_branch)
    o_ref[...] = h_ref[...] + _dot(merged.astype(BF16), wo_ref[...])


def _merge(h, y, z, pooled, gates, norm_ssd, w_ssd_out, w_pool_group, pool_scale, w_pool_out, w_o,
           *, tm):
    t, d = h.shape
    tm = min(tm, t)
    d_inner = y.shape[1]
    pool_slabs = pooled.shape[0]
    pool_dim = pool_slabs * LANES
    row = lambda w: pl.BlockSpec((tm, w), lambda i: (i, 0))
    return pl.pallas_call(
        _merge_kernel,
        grid=(t // tm,),
        in_specs=[row(d), row(d_inner), row(d_inner),
                  pl.BlockSpec((pool_slabs, tm, LANES), lambda i: (0, i, 0)),
                  row(N_BRANCH * d),
                  _const_spec((1, d_inner)),
                  _const_spec(w_ssd_out.shape),
                  _const_spec(w_pool_group.shape),
                  _const_spec((1, pool_dim)),
                  _const_spec(w_pool_out.shape),
                  _const_spec(w_o.shape)],
        out_specs=row(d),
        out_shape=jax.ShapeDtypeStruct((t, d), F32),
        scratch_shapes=[pltpu.VMEM((tm, d_inner), BF16), pltpu.VMEM((tm, pool_dim), BF16)],
        compiler_params=_params(1),
        name="merge",
    )(h, y, z, pooled, gates, norm_ssd, w_ssd_out, w_pool_group, pool_scale, w_pool_out, w_o)


def _pad_lanes(a):
    return jnp.pad(a.reshape(1, -1), ((0, 0), (0, LANES - a.shape[-1])))


def kernel(x_prompt, x_sample, state_ssm, state_conv, state_pool, p_prompt, p_sample, norm_ffn1, w_ffn1_gu, w_ffn1_down, norm_mix, w_in, conv_w, conv_b, dt_bias, a_log, d_skip, norm_ssd, w_ssd_out, w_pool_group, pool_scale, w_pool_out, w_o, norm_ffn2, w_ffn2_gu, w_ffn2_down, norm_ple, w_ple_gate, w_ple, norm_final):
    depth = norm_ffn1.shape[0]
    assert depth == 1, "the final norm is fused into the layer's last stage: one layer only"
    batch, seq, d_model = x_prompt.shape
    dec_batch, dec_seq, _ = x_sample.shape
    n_heads = dt_bias.shape[1]
    d_inner = n_heads * HEAD_DIM
    conv_dim = conv_w.shape[2]
    pool_dim = pool_scale.shape[1]
    assert seq % CHUNK == 0 and CHUNK % dec_seq == 0 and dec_batch % (CHUNK // dec_seq) == 0

    row = lambda a: a[0].reshape(1, -1)
    cuts = [0, d_inner, d_inner + conv_dim, d_inner + conv_dim + n_heads,
            d_inner + conv_dim + n_heads + pool_dim, w_in.shape[2]]
    wz, wxbc, wdt, wv, wgates = [w_in[0][:, a:b] for a, b in zip(cuts[:-1], cuts[1:])]
    wdt = jnp.pad(wdt, ((0, 0), (0, LANES - n_heads)))
    w_cat = jnp.concatenate([wz, wxbc, wv, wgates, wdt], axis=1).astype(BF16)
    widths = (d_inner, conv_dim, pool_dim, N_BRANCH * d_model, LANES)
    w1gu, w1d = w_ffn1_gu[0].astype(BF16), w_ffn1_down[0].astype(BF16)
    w2gu, w2d = w_ffn2_gu[0].astype(BF16), w_ffn2_down[0].astype(BF16)
    wssd, wpg = w_ssd_out[0].astype(BF16), w_pool_group[0].astype(BF16)
    wpo, wo = w_pool_out[0].astype(BF16), w_o[0].astype(BF16)
    wpleg, wple = w_ple_gate[0].astype(BF16), w_ple[0].astype(BF16)
    dtb, alog = _pad_lanes(dt_bias[0]), _pad_lanes(a_log[0])
    dskip = jnp.repeat(d_skip[0], HEAD_DIM).reshape(1, d_inner)
    cb = row(conv_b)
    nfin = norm_final.reshape(1, -1)

    def pre(h):
        h1 = _ffn(h, row(norm_ffn1), w1gu, w1d, tm=512)
        return (h1,) + tuple(_inproj(h1, row(norm_mix), w_cat, widths, tm=256))

    def post(h1, y, z, pooled, gates, p):
        h2 = _merge(h1, y, z, pooled, gates, row(norm_ssd), wssd, wpg, row(pool_scale), wpo, wo,
                    tm=256)
        return _ffn(h2, row(norm_ffn2), w2gu, w2d, (p, row(norm_ple), wpleg, wple, nfin), tm=512)

    h1 = _ffn(x_prompt.reshape(batch * seq, d_model), row(norm_ffn1), w1gu, w1d, tm=512)
    z, y, pooled, gates, c1, q1, s1 = _inproj_mixer(h1, row(norm_mix), w_cat, widths, conv_w[0], cb,
                                                    dtb, alog, dskip, tm=256, seq=seq, d_inner=d_inner)
    y_prompt = post(h1, y, z, pooled, gates, p_prompt[0].reshape(batch * seq, -1))

    h1, z, xbc, v, gates, dt = pre(x_sample.reshape(dec_batch * dec_seq, d_model))
    y, pooled, s2, c2, q2 = _mixer_sample(
        xbc, dt, v, state_conv[0], state_pool[0], state_ssm[0].reshape(dec_batch, d_inner, D_STATE),
        conv_w[0], cb, dtb, alog, dskip,
        batch=dec_batch, seq=dec_seq, d_inner=d_inner, pos0=PAST_LEN)
    y_sample = post(h1, y, z, pooled, gates, p_sample[0].reshape(dec_batch * dec_seq, -1))

    return (y_prompt.reshape(batch, seq, d_model),
            y_sample.reshape(dec_batch, dec_seq, d_model),
            s1.reshape(1, batch, n_heads, HEAD_DIM, D_STATE), c1[None], q1[None],
            s2.reshape(1, dec_batch, n_heads, HEAD_DIM, D_STATE), c2[None], q2[None])
```

```python
import functools
import math

import jax
import jax.numpy as jnp
from jax import lax
from jax.experimental import pallas as pl
from jax.experimental.pallas import tpu as pltpu

F32 = jnp.float32
BF16 = jnp.bfloat16
EPS = 1e-6
HIGHEST = lax.Precision.HIGHEST
NEG_LOG2E = -1.4426950408889634

LANES = 128
SUBLANES = 8
VMEM_LIMIT_BYTES = 56 * 1024 * 1024
TAIL_VMEM_LIMIT_BYTES = 60 * 1024 * 1024

HEAD_DIM = 64
N_GROUPS = 8
D_STATE = 128
CONV_W = 4
CHUNK = 128
POOL_WINDOWS = (2, 4, 8, 16)
POOL_BUF = max(POOL_WINDOWS) - 1
N_BRANCH = 2
PAST_LEN = 16384
HEADS_PER_TILE = LANES // HEAD_DIM
ROW_STRIDE = 4
POOL_PAD = 24
POOL_PRE = 16


def _sigmoid(x):
    return 1.0 / (1.0 + jnp.exp2(x * NEG_LOG2E))


def _silu(x):
    half = 0.5 * x
    return half + half * jnp.tanh(half)


def _softplus(x):
    return jnp.maximum(x, 0.0) + jnp.log(1.0 + jnp.exp(-jnp.abs(x)))


def _rmsnorm(x, g):
    return x * lax.rsqrt(jnp.mean(x * x, axis=-1, keepdims=True) + EPS) * g


def _dot(a, b):
    return jnp.dot(a, b, preferred_element_type=F32)


def _dot_nt(a, b):
    return lax.dot_general(a, b, (((1,), (1,)), ((), ())), preferred_element_type=F32)


def _const_spec(shape):
    nd = len(shape)
    return pl.BlockSpec(shape, lambda *_: (0,) * nd, pipeline_mode=pl.Buffered(1))


def _params(n_axes):
    return pltpu.CompilerParams(dimension_semantics=("arbitrary",) * n_axes,
                                vmem_limit_bytes=VMEM_LIMIT_BYTES)


def _ffn_body(x_ref, g_ref, wgu_ref, wd_ref, xn_ref, acc_ref, *, d_ff, fc):
    xn_ref[...] = _rmsnorm(x_ref[...], g_ref[...]).astype(BF16)
    for c in range(d_ff // fc):
        xn = xn_ref[...]
        gate = _dot(xn, wgu_ref[:, c * fc:(c + 1) * fc])
        up = _dot(xn, wgu_ref[:, d_ff + c * fc:d_ff + (c + 1) * fc])
        act = (_silu(gate) * up).astype(BF16)
        contrib = _dot(act, wd_ref[c * fc:(c + 1) * fc, :])
        if c == 0:
            acc_ref[...] = contrib
        else:
            acc_ref[...] += contrib
    return x_ref[...] + 0.5 * acc_ref[...]


def _ffn_kernel(x_ref, g_ref, wgu_ref, wd_ref, o_ref, xn_ref, acc_ref, *, d_ff, fc):
    o_ref[...] = _ffn_body(x_ref, g_ref, wgu_ref, wd_ref, xn_ref, acc_ref, d_ff=d_ff, fc=fc)


def _tail_kernel(h_ref, y_ref, z_ref, pooled_ref, gates_ref, nssd_ref, wssd_ref, wpg_ref,
                 pscale_ref, wpo_ref, wo_ref, g_ref, wgu_ref, wd_ref,
                 p_ref, nple_ref, wg_ref, wp_ref, nfin_ref,
                 o_ref, yn_ref, pm_ref, h2_ref, xn_ref, acc_ref, *, d_ff, fc):
    _merge_body(h_ref, y_ref, z_ref, pooled_ref, gates_ref, nssd_ref, wssd_ref, wpg_ref,
                pscale_ref, wpo_ref, wo_ref, h2_ref, yn_ref, pm_ref)
    h = _ffn_body(h2_ref, g_ref, wgu_ref, wd_ref, xn_ref, acc_ref, d_ff=d_ff, fc=fc)
    gate = _sigmoid(_dot(_rmsnorm(h, nple_ref[...]).astype(BF16), wg_ref[...]))
    h = h + gate * _dot(p_ref[...].astype(BF16), wp_ref[...])
    o_ref[...] = _rmsnorm(h, nfin_ref[...])


FFN_CHUNK = 256


def _ffn(x, g, w_gu, w_down, *, tm):
    t, d = x.shape
    d_ff = w_down.shape[0]
    tm = min(tm, t)
    return pl.pallas_call(
        functools.partial(_ffn_kernel, d_ff=d_ff, fc=FFN_CHUNK),
        grid=(t // tm,),
        in_specs=[pl.BlockSpec((tm, d), lambda i: (i, 0)),
                  _const_spec((1, d)),
                  _const_spec((d, 2 * d_ff)),
                  _const_spec((d_ff, d))],
        out_specs=pl.BlockSpec((tm, d), lambda i: (i, 0)),
        out_shape=jax.ShapeDtypeStruct((t, d), F32),
        scratch_shapes=[pltpu.VMEM((tm, d), BF16), pltpu.VMEM((tm, d), F32)],
        compiler_params=_params(1),
        name="ffn",
    )(x, g, w_gu, w_down)


def _inproj_kernel(h_ref, g_ref, w_ref, *rest, widths, nc):
    out_refs, u_ref = rest[:-1], rest[-1]
    u_ref[...] = _rmsnorm(h_ref[...], g_ref[...]).astype(BF16)
    start = 0
    for o_ref, width in zip(out_refs, widths):
        for a in range(0, width, nc):
            b = min(a + nc, width)
            o_ref[:, a:b] = _dot(u_ref[...], w_ref[:, start + a:start + b])
        start += width


def _inproj(h, g, w_cat, widths, *, tm):
    t, d = h.shape
    tm = min(tm, t)
    n_all = w_cat.shape[1]
    return pl.pallas_call(
        functools.partial(_inproj_kernel, widths=widths, nc=1024),
        grid=(t // tm,),
        in_specs=[pl.BlockSpec((tm, d), lambda i: (i, 0)),
                  _const_spec((1, d)),
                  _const_spec((d, n_all))],
        out_specs=[pl.BlockSpec((tm, w), lambda i: (i, 0)) for w in widths],
        out_shape=[jax.ShapeDtypeStruct((t, w), F32) for w in widths],
        scratch_shapes=[pltpu.VMEM((tm, d), BF16)],
        compiler_params=_params(1),
        name="inproj",
    )(h, g, w_cat)


def _inproj_mixer_kernel(h_ref, g_ref, w_ref, cw_ref, cb_ref, dtb_ref, alog_ref, dskip_ref,
                         z_ref, y_ref, pooled_ref, gates_ref, conv_ref, pool_ref, ssm_ref,
                         u_ref, raw_ref, vraw_ref, xc_ref, state_ref,
                         *, widths, nc, tiles_per_seq, d_inner):
    i = pl.program_id(0)
    tm = h_ref.shape[0]
    n_slabs = raw_ref.shape[0]

    n_vslabs = vraw_ref.shape[0]
    tile_in_seq = i % tiles_per_seq
    span = SUBLANES * ROW_STRIDE

    @pl.when(tile_in_seq == 0)
    def _():
        raw_ref[:, 0:SUBLANES, :] = jnp.zeros((n_slabs, SUBLANES, LANES), F32)
        vraw_ref[:, 0:POOL_PRE, :] = jnp.zeros((n_vslabs, POOL_PRE, LANES), F32)
        state_ref[...] = jnp.zeros(state_ref.shape, F32)

    u_ref[...] = _rmsnorm(h_ref[...], g_ref[...]).astype(BF16)
    starts = [sum(widths[:k]) for k in range(len(widths))]

    def pool_slab(s):
        cols = slice(s * LANES, (s + 1) * LANES)
        win = POOL_WINDOWS[s * len(POOL_WINDOWS) // n_vslabs]
        for base in range(0, tm, span):
            rows = {d: vraw_ref[s, pl.ds(POOL_PRE + base + d, SUBLANES, stride=ROW_STRIDE), :]
                    for d in range(1 - win, ROW_STRIDE)}
            for j in range(ROW_STRIDE):
                total = rows[j]
                for k in range(1, win):
                    total = total + rows[j - k]
                if base >= POOL_BUF:
                    mean = total * (1.0 / win)
                else:
                    pos = (tile_in_seq * tm + base + j
                           + ROW_STRIDE * lax.broadcasted_iota(jnp.int32, (SUBLANES, LANES), 0))
                    mean = total / jnp.minimum(pos + 1, win).astype(F32)
                pooled_ref[s, pl.ds(base + j, SUBLANES, stride=ROW_STRIDE), :] = mean - rows[j]
        pool_ref[0, :, cols] = vraw_ref[s, POOL_PRE + tm - POOL_BUF:POOL_PRE + tm, :]
        vraw_ref[s, 0:POOL_PRE, :] = vraw_ref[s, tm:tm + POOL_PRE, :]

    def conv_slab(s):
        cols = slice(s * LANES, (s + 1) * LANES)
        taps = [jnp.broadcast_to(cw_ref[k:k + 1, cols], (SUBLANES, LANES)) for k in range(CONV_W)]
        bias = jnp.broadcast_to(cb_ref[:, cols], (SUBLANES, LANES))
        for base in range(0, tm, span):
            shifted = [raw_ref[s, pl.ds(SUBLANES + base - (CONV_W - 1) + m, SUBLANES,
                                        stride=ROW_STRIDE), :]
                       for m in range(ROW_STRIDE + CONV_W - 1)]
            for j in range(ROW_STRIDE):
                acc = bias
                for k in range(CONV_W):
                    acc = acc + shifted[j + k] * taps[k]
                xc_ref[s, pl.ds(base + j, SUBLANES, stride=ROW_STRIDE), :] = _silu(acc)
        conv_ref[0, :, cols] = raw_ref[s, SUBLANES + tm - (CONV_W - 1):SUBLANES + tm, :]
        raw_ref[s, 0:SUBLANES, :] = raw_ref[s, tm:tm + SUBLANES, :]

    def project(k, a):
        b = min(a + nc, widths[k])
        return _dot(u_ref[...], w_ref[:, starts[k] + a:starts[k] + b])

    n_xbc_units = widths[1] // nc
    slabs_per_unit = nc // LANES
    done = 0
    for n, (k, a) in enumerate([(k, a) for k in (1, 2) for a in range(0, widths[k], nc)]):
        res = project(k, a)
        for s in range(res.shape[1] // LANES):
            piece = res[:, s * LANES:(s + 1) * LANES]
            if k == 1:
                raw_ref[a // LANES + s, SUBLANES:SUBLANES + tm, :] = piece
            else:
                vraw_ref[a // LANES + s, POOL_PRE:POOL_PRE + tm, :] = piece
                pool_slab(a // LANES + s)
        while done < min(n_slabs, n * slabs_per_unit, min(n + 1, n_xbc_units) * slabs_per_unit):
            conv_slab(done)
            done += 1
    dt_raw = project(4, 0)
    while done < n_slabs:
        conv_slab(done)
        done += 1

    q = CHUNK
    n_state = D_STATE
    assert q == LANES and n_state == LANES
    late_units = [(k, a) for k in (0, 3) for a in range(0, widths[k], nc)]
    late_outs = {0: z_ref, 3: gates_ref}
    n_blocks = (tm // q) * N_GROUPS
    emitted = 0
    scalars = []
    for ci in range(tm // q):
        dt, a_cs, seg_tot, causal = _chunk_scalars(dt_raw[ci * q:(ci + 1) * q, :], dtb_ref[...],
                                                   alog_ref[...], q)
        scalars.append((a_cs, a_cs.T, dt.T, (jnp.exp(seg_tot - a_cs) * dt).T, jnp.exp(seg_tot).T,
                        causal))
    tiles_per_group = d_inner // N_GROUPS // LANES
    for ci in range(tm // q):
        rows = slice(ci * q, (ci + 1) * q)
        a_cs, a_cs_t, dt_t, w_state_t, chunk_decay_t, causal = scalars[ci]
        for g in range(N_GROUPS):
            b_g = xc_ref[d_inner // LANES + g, rows, :].astype(BF16)
            c_g = xc_ref[d_inner // LANES + N_GROUPS + g, rows, :].astype(BF16)
            cb = _dot_nt(c_g, b_g)
            for p in range(tiles_per_group):
                tile = g * tiles_per_group + p
                h0 = tile * HEADS_PER_TILE
                cols = slice(tile * LANES, (tile + 1) * LANES)
                xpair = xc_ref[tile, rows, :]
                h_prev = state_ref[cols, :]
                y, e_pair = _diag_pair(xpair, cb, a_cs, a_cs_t, dt_t, causal, h0)
                y = y + _dot_nt(c_g, h_prev.astype(BF16)) * e_pair
                y_ref[rows, cols] = y + dskip_ref[:, cols] * xpair
                xw_t = (xpair.T * _pair_rows(w_state_t, h0, (LANES, q))).astype(BF16)
                decay_rows = _pair_rows(chunk_decay_t, h0, (LANES, n_state))
                state_ref[cols, :] = decay_rows * h_prev + _dot(xw_t, b_g)
            block = ci * N_GROUPS + g + 1
            while emitted < len(late_units) and emitted * n_blocks < block * len(late_units):
                k, a = late_units[emitted]
                res = project(k, a)
                late_outs[k][:, a:a + res.shape[1]] = res
                emitted += 1
    assert emitted == len(late_units)

    @pl.when(tile_in_seq == tiles_per_seq - 1)
    def _():
        ssm_ref[0] = state_ref[...]


def _inproj_mixer(h, g, w_cat, widths, conv_w, conv_b, dt_bias, a_log, d_skip, *, tm, seq, d_inner):
    t, d = h.shape
    tm = min(tm, seq)
    tiles_per_seq = seq // tm
    n_all = w_cat.shape[1]
    conv_dim, pool_dim = widths[1], widths[2]
    n_slabs = conv_dim // LANES
    assert tm % (SUBLANES * ROW_STRIDE) == 0 and tm % CHUNK == 0
    row = lambda w: pl.BlockSpec((tm, w), lambda i: (i, 0))
    per_seq = lambda i: (i // tiles_per_seq, 0, 0)
    return pl.pallas_call(
        functools.partial(_inproj_mixer_kernel, widths=widths, nc=512, tiles_per_seq=tiles_per_seq,
                          d_inner=d_inner),
        grid=(t // tm,),
        in_specs=[row(d),
                  _const_spec((1, d)),
                  _const_spec((d, n_all)),
                  _const_spec((CONV_W, conv_dim)),
                  _const_spec((1, conv_dim)),
                  _const_spec((1, LANES)),
                  _const_spec((1, LANES)),
                  _const_spec((1, d_inner))],
        out_specs=[row(widths[0]), row(d_inner),
                   pl.BlockSpec((pool_dim // LANES, tm, LANES), lambda i: (0, i, 0)),
                   row(widths[3]),
                   pl.BlockSpec((1, CONV_W - 1, conv_dim), per_seq),
                   pl.BlockSpec((1, POOL_BUF, pool_dim), per_seq),
                   pl.BlockSpec((1, d_inner, D_STATE), per_seq)],
        out_shape=[jax.ShapeDtypeStruct((t, widths[0]), F32),
                   jax.ShapeDtypeStruct((t, d_inner), F32),
                   jax.ShapeDtypeStruct((pool_dim // LANES, t, LANES), F32),
                   jax.ShapeDtypeStruct((t, widths[3]), F32),
                   jax.ShapeDtypeStruct((t // seq, CONV_W - 1, conv_dim), F32),
                   jax.ShapeDtypeStruct((t // seq, POOL_BUF, pool_dim), F32),
                   jax.ShapeDtypeStruct((t // seq, d_inner, D_STATE), F32)],
        scratch_shapes=[pltpu.VMEM((tm, d), BF16),
                        pltpu.VMEM((n_slabs, SUBLANES + tm, LANES), F32),
                        pltpu.VMEM((pool_dim // LANES, POOL_PRE + tm, LANES), F32),
                        pltpu.VMEM((n_slabs, tm, LANES), F32),
                        pltpu.VMEM((d_inner, D_STATE), F32)],
        compiler_params=_params(1),
        name="inproj_mixer",
    )(h, g, w_cat, conv_w, conv_b, dt_bias, a_log, d_skip)


def _chunk_scalars(dt_raw, dt_bias, a_log, seg_len):
    q = dt_raw.shape[0]
    dt = _softplus(dt_raw + dt_bias)
    da = dt * (-jnp.exp(a_log))
    shift = int(math.log2(seg_len))
    qi = lax.broadcasted_iota(jnp.int32, (q, q), 0)
    si = lax.broadcasted_iota(jnp.int32, (q, q), 1)
    same = (qi >> shift) == (si >> shift)
    causal = jnp.logical_and(same, si <= qi)
    a_cs = jnp.dot(jnp.where(causal, 1.0, 0.0), da, precision=HIGHEST, preferred_element_type=F32)
    seg_tot = jnp.dot(jnp.where(same, 1.0, 0.0), da, precision=HIGHEST, preferred_element_type=F32)
    return dt, a_cs, seg_tot, causal


def _pair_cols(vals, h0, shape):
    lane = lax.broadcasted_iota(jnp.int32, shape, 1)
    return jnp.where(lane < HEAD_DIM, vals[:, h0:h0 + 1], vals[:, h0 + 1:h0 + 2])


def _pair_rows(vals_t, h0, shape):
    row = lax.broadcasted_iota(jnp.int32, shape, 0)
    return jnp.where(row < HEAD_DIM, vals_t[h0:h0 + 1, :], vals_t[h0 + 1:h0 + 2, :])


def _diag_pair(xpair, cb, a_cs, a_cs_t, dt_t, causal, h0):
    q = xpair.shape[0]
    lane = lax.broadcasted_iota(jnp.int32, xpair.shape, 1)
    out = None
    spread = []
    for k in range(HEADS_PER_TILE):
        h = h0 + k
        a_col = jnp.broadcast_to(a_cs[:, h:h + 1], (q, LANES))
        spread.append(a_col)
        decay = jnp.exp(jnp.where(causal, a_col - a_cs_t[h:h + 1, :], -jnp.inf))
        w = (cb * decay * dt_t[h:h + 1, :]).astype(BF16)
        mine = (lane < HEAD_DIM) if k == 0 else (lane >= HEAD_DIM)
        t = _dot(w, jnp.where(mine, xpair, 0.0).astype(BF16))
        out = t if out is None else out + t
    e_pair = jnp.exp(jnp.where(lane < HEAD_DIM, spread[0], spread[1]))
    return out, e_pair


def _shift_rows(cur, prev, s):
    sub = lax.broadcasted_iota(jnp.int32, (1,) + cur.shape[1:], 1)
    return jnp.where(sub < s, pltpu.roll(prev, s, axis=1), pltpu.roll(cur, s, axis=1))


def _conv_tiles(cur, prev, w, bias):
    tap = lambda k: w[k:k + 1, :][None]
    acc = cur * tap(CONV_W - 1)
    for s in range(1, CONV_W):
        acc = acc + _shift_rows(cur, prev, s) * tap(CONV_W - 1 - s)
    return _silu(bias[None] + acc)


def _conv_block(ext, w, bias):
    rows, width = ext.shape
    tiles = ext.reshape(rows // SUBLANES, SUBLANES, width)
    return _conv_tiles(tiles[1:], tiles[:-1], w, bias).reshape(rows - SUBLANES, width)


def _conv_tile_rows(rows_of_tap, w, bias):
    acc = rows_of_tap(0) * w[0:1, :]
    for k in range(1, CONV_W):
        acc = acc + rows_of_tap(k) * w[k:k + 1, :]
    return _silu(bias + acc)


def _mixer_sample_kernel(xbc_ref, dt_ref, v_ref, conv0_ref, pool0_ref, ssm0_ref,
                         cw_ref, cb_ref, dtb_ref, alog_ref, dskip_ref,
                         y_ref, pooled_ref, ssm_ref, conv_ref, pool_ref,
                         head_ref, xc_ref, xwt_ref, e_ref, seg_ref, pe_ref,
                         *, d_inner, seq, pos0, seqs_per_step):
    j = pl.program_id(1)
    q = xbc_ref.shape[0]
    conv_dim = xbc_ref.shape[1]
    pool_dim = v_ref.shape[1]
    n_state = D_STATE
    n_seq = q // seq
    tiles_per_group = d_inner // N_GROUPS // LANES
    gc = pool_dim // len(POOL_WINDOWS)

    @pl.when(j == 0)
    def _():
        lo = SUBLANES - (CONV_W - 1)

        def per_seq(b, carry):
            r0 = pl.multiple_of(b * seq, seq)
            head_ref[lo:SUBLANES, :] = conv0_ref[b]
            ct = 1024
            for jc in range(conv_dim // ct):
                cols = slice(jc * ct, (jc + 1) * ct)
                head_ref[SUBLANES:SUBLANES + seq, cols] = xbc_ref[pl.ds(r0, seq), cols]
                xc_ref[pl.ds(r0, seq), cols] = _conv_tile_rows(
                    lambda k: head_ref[lo + k:lo + k + seq, cols], cw_ref[:, cols], cb_ref[:, cols])
            conv_ref[b] = head_ref[SUBLANES + seq - (CONV_W - 1):SUBLANES + seq, :]

            pe_ref[POOL_PAD - POOL_BUF:POOL_PAD, :] = pool0_ref[b]
            vb = v_ref[pl.ds(r0, seq), :]
            pe_ref[POOL_PAD:POOL_PAD + seq, :] = vb
            t_idx = lax.broadcasted_iota(jnp.int32, (seq, 1), 0) + pos0
            for gi, win in enumerate(POOL_WINDOWS):
                cols = slice(gi * gc, (gi + 1) * gc)
                total = pe_ref[POOL_PAD:POOL_PAD + seq, cols]
                for k in range(1, win):
                    total = total + pe_ref[POOL_PAD - k:POOL_PAD - k + seq, cols]
                cnt = jnp.minimum(t_idx + 1, win).astype(F32)
                pooled = total / cnt - vb[:, cols]
                for s in range(gc // LANES):
                    pooled_ref[gi * (gc // LANES) + s, pl.ds(r0, seq), :] = (
                        pooled[:, s * LANES:(s + 1) * LANES])
            pool_ref[b] = pe_ref[POOL_PAD + seq - POOL_BUF:POOL_PAD + seq, :]
            return carry

        lax.fori_loop(0, n_seq, per_seq, 0)

        dt, a_cs, seg_tot, causal = _chunk_scalars(dt_ref[...], dtb_ref[...], alog_ref[...], seq)
        a_cs_t = a_cs.T
        dt_t = dt.T
        w_state_t = (jnp.exp(seg_tot - a_cs) * dt).T
        seg_ref[...] = seg_tot
        for g in range(N_GROUPS):
            b_g = xc_ref[:, d_inner + g * n_state:d_inner + (g + 1) * n_state].astype(BF16)
            c_g = xc_ref[:, d_inner + (N_GROUPS + g) * n_state:
                         d_inner + (N_GROUPS + g + 1) * n_state].astype(BF16)
            cb = _dot_nt(c_g, b_g)
            for p in range(tiles_per_group):
                tile = g * tiles_per_group + p
                h0 = tile * HEADS_PER_TILE
                cols = slice(tile * LANES, (tile + 1) * LANES)
                xpair = xc_ref[:, cols]
                y, e_pair = _diag_pair(xpair, cb, a_cs, a_cs_t, dt_t, causal, h0)
                y_ref[:, cols] = y + dskip_ref[:, cols] * xpair
                e_ref[:, cols] = e_pair
                xwt_ref[cols, :] = (xpair.T * _pair_rows(w_state_t, h0, (LANES, q))).astype(BF16)

    row_q = lax.broadcasted_iota(jnp.int32, (q, n_state), 0)
    row128 = lax.broadcasted_iota(jnp.int32, (LANES, n_state), 0)
    for sj in range(seqs_per_step):
        r0 = pl.multiple_of((j * seqs_per_step + sj) * seq, seq)
        chunk_decay = jnp.exp(seg_ref[pl.ds(r0, 1), :])
        mine = jnp.logical_and(row_q >= r0, row_q < r0 + seq)
        for g in range(N_GROUPS):
            b_cols = slice(d_inner + g * n_state, d_inner + (g + 1) * n_state)
            c_cols = slice(d_inner + (N_GROUPS + g) * n_state,
                           d_inner + (N_GROUPS + g + 1) * n_state)
            b_mine = jnp.where(mine, xc_ref[:, b_cols], 0.0).astype(BF16)
            c_rows = xc_ref[pl.ds(r0, seq), c_cols].astype(BF16)
            for p in range(tiles_per_group):
                tile = g * tiles_per_group + p
                h0 = tile * HEADS_PER_TILE
                cols = slice(tile * LANES, (tile + 1) * LANES)
                h_prev = ssm0_ref[sj, cols, :]
                y_off = _dot_nt(c_rows, h_prev.astype(BF16)) * e_ref[pl.ds(r0, seq), cols]
                y_ref[pl.ds(r0, seq), cols] += y_off
                decay_rows = jnp.where(row128 < HEAD_DIM, chunk_decay[:, h0:h0 + 1],
                                       chunk_decay[:, h0 + 1:h0 + 2])
                ssm_ref[sj, cols, :] = decay_rows * h_prev + _dot(xwt_ref[cols, :], b_mine)


def _mixer_sample(xbc, dt, v, conv0, pool0, ssm0, conv_w, conv_b, dt_bias, a_log, d_skip,
                  *, batch, seq, d_inner, pos0):
    q = CHUNK
    n_seq = q // seq
    n_chunks = batch // n_seq
    conv_dim = xbc.shape[1]
    pool_dim = v.shape[1]
    tok = lambda i, j: (i, 0)
    per_chunk = lambda i, j: (i, 0, 0)
    sps = math.gcd(n_seq, 4)
    steps = n_seq // sps
    per_seq = lambda i, j: (i * steps + j, 0, 0)
    return pl.pallas_call(
        functools.partial(_mixer_sample_kernel, d_inner=d_inner, seq=seq, pos0=pos0,
                          seqs_per_step=sps),
        grid=(n_chunks, steps),
        in_specs=[pl.BlockSpec((q, conv_dim), tok),
                  pl.BlockSpec((q, LANES), tok),
                  pl.BlockSpec((q, pool_dim), tok),
                  pl.BlockSpec((n_seq, CONV_W - 1, conv_dim), per_chunk),
                  pl.BlockSpec((n_seq, POOL_BUF, pool_dim), per_chunk),
                  pl.BlockSpec((sps, d_inner, D_STATE), per_seq),
                  _const_spec((CONV_W, conv_dim)),
                  _const_spec((1, conv_dim)),
                  _const_spec((1, LANES)),
                  _const_spec((1, LANES)),
                  _const_spec((1, d_inner))],
        out_specs=[pl.BlockSpec((q, d_inner), tok),
                   pl.BlockSpec((pool_dim // LANES, q, LANES), lambda i, j: (0, i, 0)),
                   pl.BlockSpec((sps, d_inner, D_STATE), per_seq),
                   pl.BlockSpec((n_seq, CONV_W - 1, conv_dim), per_chunk),
                   pl.BlockSpec((n_seq, POOL_BUF, pool_dim), per_chunk)],
        out_shape=[jax.ShapeDtypeStruct((batch * seq, d_inner), F32),
                   jax.ShapeDtypeStruct((pool_dim // LANES, batch * seq, LANES), F32),
                   jax.ShapeDtypeStruct((batch, d_inner, D_STATE), F32),
                   jax.ShapeDtypeStruct((batch, CONV_W - 1, conv_dim), F32),
                   jax.ShapeDtypeStruct((batch, POOL_BUF, pool_dim), F32)],
        scratch_shapes=[pltpu.VMEM((2 * SUBLANES, conv_dim), F32),
                        pltpu.VMEM((q, conv_dim), F32),
                        pltpu.VMEM((d_inner, q), BF16),
                        pltpu.VMEM((q, d_inner), F32),
                        pltpu.VMEM((q, LANES), F32),
                        pltpu.VMEM((POOL_PAD + seq, pool_dim), F32)],
        compiler_params=_params(2),
        name="mixer_sample",
    )(xbc, dt, v, conv0, pool0, ssm0, conv_w, conv_b, dt_bias, a_log, d_skip)


def _merge_body(h_ref, y_ref, z_ref, pooled_ref, gates_ref, nssd_ref, wssd_ref, wpg_ref,
                pscale_ref, wpo_ref, wo_ref, o_ref, yn_ref, pm_ref):
    d_inner = y_ref.shape[1]
    d_model = h_ref.shape[1]
    gw = d_inner // N_GROUPS
    for g in range(N_GROUPS):
        cols = slice(g * gw, (g + 1) * gw)
        yz = y_ref[:, cols] * _silu(z_ref[:, cols])
        yn_ref[:, cols] = _rmsnorm(yz, nssd_ref[:, cols]).astype(BF16)
    a_branch = _dot(yn_ref[...], wssd_ref[...])
    slabs_per_group = pooled_ref.shape[0] // len(POOL_WINDOWS)
    gc = slabs_per_group * LANES
    for gi in range(len(POOL_WINDOWS)):
        cols = slice(gi * gc, (gi + 1) * gc)
        pooled = jnp.concatenate([pooled_ref[gi * slabs_per_group + s]
                                  for s in range(slabs_per_group)], axis=1)
        mixed = _dot(pooled.astype(BF16), wpg_ref[gi])
        pm_ref[:, cols] = (mixed * pscale_ref[:, cols]).astype(BF16)
    b_branch = _dot(pm_ref[...], wpo_ref[...])
    merged = (_sigmoid(gates_ref[:, 0:d_model]) * a_branch
              + _sigmoid(gates_ref[:, d_model:2 * d_model]) * b_branch)
    o_ref[...] = h_ref[...] + _dot(merged.astype(BF16), wo_ref[...])


def _tail(h, y, z, pooled, gates, merge_params, ffn_params, p, ple_params, *, tm):
    t, d = h.shape
    tm = min(tm, t)
    d_inner = y.shape[1]
    pool_slabs = pooled.shape[0]
    pool_dim = pool_slabs * LANES
    d_ff = ffn_params[2].shape[0]
    row = lambda w: pl.BlockSpec((tm, w), lambda i: (i, 0))
    consts = list(merge_params) + list(ffn_params)
    return pl.pallas_call(
        functools.partial(_tail_kernel, d_ff=d_ff, fc=FFN_CHUNK),
        grid=(t // tm,),
        in_specs=[row(d), row(d_inner), row(d_inner),
                  pl.BlockSpec((pool_slabs, tm, LANES), lambda i: (0, i, 0)),
                  row(N_BRANCH * d)]
        + [_const_spec(a.shape) for a in consts]
        + [row(p.shape[1])]
        + [_const_spec(a.shape) for a in ple_params],
        out_specs=row(d),
        out_shape=jax.ShapeDtypeStruct((t, d), F32),
        scratch_shapes=[pltpu.VMEM((tm, d_inner), BF16), pltpu.VMEM((tm, pool_dim), BF16),
                        pltpu.VMEM((tm, d), F32), pltpu.VMEM((tm, d), BF16),
                        pltpu.VMEM((tm, d), F32)],
        compiler_params=pltpu.CompilerParams(dimension_semantics=("arbitrary",),
                                             vmem_limit_bytes=TAIL_VMEM_LIMIT_BYTES),
        name="tail",
    )(h, y, z, pooled, gates, *consts, p, *ple_params)


def _pad_lanes(a):
    return jnp.pad(a.reshape(1, -1), ((0, 0), (0, LANES - a.shape[-1])))


def kernel(x_prompt, x_sample, state_ssm, state_conv, state_pool, p_prompt, p_sample, norm_ffn1, w_ffn1_gu, w_ffn1_down, norm_mix, w_in, conv_w, conv_b, dt_bias, a_log, d_skip, norm_ssd, w_ssd_out, w_pool_group, pool_scale, w_pool_out, w_o, norm_ffn2, w_ffn2_gu, w_ffn2_down, norm_ple, w_ple_gate, w_ple, norm_final):
    depth = norm_ffn1.shape[0]
    assert depth == 1, "the final norm is fused into the layer's last stage: one layer only"
    batch, seq, d_model = x_prompt.shape
    dec_batch, dec_seq, _ = x_sample.shape
    n_heads = dt_bias.shape[1]
    d_inner = n_heads * HEAD_DIM
    conv_dim = conv_w.shape[2]
    pool_dim = pool_scale.shape[1]
    assert seq % CHUNK == 0 and CHUNK % dec_seq == 0 and dec_batch % (CHUNK // dec_seq) == 0

    row = lambda a: a[0].reshape(1, -1)
    cuts = [0, d_inner, d_inner + conv_dim, d_inner + conv_dim + n_heads,
            d_inner + conv_dim + n_heads + pool_dim, w_in.shape[2]]
    wz, wxbc, wdt, wv, wgates = [w_in[0][:, a:b] for a, b in zip(cuts[:-1], cuts[1:])]
    wdt = jnp.pad(wdt, ((0, 0), (0, LANES - n_heads)))
    w_cat = jnp.concatenate([wz, wxbc, wv, wgates, wdt], axis=1).astype(BF16)
    widths = (d_inner, conv_dim, pool_dim, N_BRANCH * d_model, LANES)
    w1gu, w1d = w_ffn1_gu[0].astype(BF16), w_ffn1_down[0].astype(BF16)
    w2gu, w2d = w_ffn2_gu[0].astype(BF16), w_ffn2_down[0].astype(BF16)
    wssd, wpg = w_ssd_out[0].astype(BF16), w_pool_group[0].astype(BF16)
    wpo, wo = w_pool_out[0].astype(BF16), w_o[0].astype(BF16)
    wpleg, wple = w_ple_gate[0].astype(BF16), w_ple[0].astype(BF16)
    dtb, alog = _pad_lanes(dt_bias[0]), _pad_lanes(a_log[0])
    dskip = jnp.repeat(d_skip[0], HEAD_DIM).reshape(1, d_inner)
    cb = row(conv_b)
    nfin = norm_final.reshape(1, -1)

    def pre(h):
        h1 = _ffn(h, row(norm_ffn1), w1gu, w1d, tm=512)
        return (h1,) + tuple(_inproj(h1, row(norm_mix), w_cat, widths, tm=256))

    def post(h1, y, z, pooled, gates, p):
        return _tail(h1, y, z, pooled, gates,
                     (row(norm_ssd), wssd, wpg, row(pool_scale), wpo, wo),
                     (row(norm_ffn2), w2gu, w2d), p, (row(norm_ple), wpleg, wple, nfin), tm=256)

    h1 = _ffn(x_prompt.reshape(batch * seq, d_model), row(norm_ffn1), w1gu, w1d, tm=512)
    z, y, pooled, gates, c1, q1, s1 = _inproj_mixer(h1, row(norm_mix), w_cat, widths, conv_w[0], cb,
                                                    dtb, alog, dskip, tm=256, seq=seq, d_inner=d_inner)
    y_prompt = post(h1, y, z, pooled, gates, p_prompt[0].reshape(batch * seq, -1))

    h1, z, xbc, v, gates, dt = pre(x_sample.reshape(dec_batch * dec_seq, d_model))
    y, pooled, s2, c2, q2 = _mixer_sample(
        xbc, dt, v, state_conv[0], state_pool[0], state_ssm[0].reshape(dec_batch, d_inner, D_STATE),
        conv_w[0], cb, dtb, alog, dskip,
        batch=dec_batch, seq=dec_seq, d_inner=d_inner, pos0=PAST_LEN)
    y_sample = post(h1, y, z, pooled, gates, p_sample[0].reshape(dec_batch * dec_seq, -1))

    return (y_prompt.reshape(batch, seq, d_model),
            y_sample.reshape(dec_batch, dec_seq, d_model),
            s1.reshape(1, batch, n_heads, HEAD_DIM, D_STATE), c1[None], q1[None],
            s2.reshape(1, dec_batch, n_heads, HEAD_DIM, D_STATE), c2[None], q2[None])
```

```python
import functools
import math

import jax
import jax.numpy as jnp
from jax import lax
from jax.experimental import pallas as pl
from jax.experimental.pallas import tpu as pltpu

F32 = jnp.float32
BF16 = jnp.bfloat16
EPS = 1e-6
HIGHEST = lax.Precision.HIGHEST
NEG_LOG2E = -1.4426950408889634

LANES = 128
SUBLANES = 8
VMEM_LIMIT_BYTES = 56 * 1024 * 1024

HEAD_DIM = 64
N_GROUPS = 8
D_STATE = 128
CONV_W = 4
CHUNK = 128
POOL_WINDOWS = (2, 4, 8, 16)
POOL_BUF = max(POOL_WINDOWS) - 1
N_BRANCH = 2
PAST_LEN = 16384
HEADS_PER_TILE = LANES // HEAD_DIM
ROW_STRIDE = 4
POOL_PAD = 24
POOL_PRE = 16


def _sigmoid(x):
    return 1.0 / (1.0 + jnp.exp2(x * NEG_LOG2E))


def _silu(x):
    half = 0.5 * x
    return half + half * jnp.tanh(half)


def _softplus(x):
    return jnp.maximum(x, 0.0) + jnp.log(1.0 + jnp.exp(-jnp.abs(x)))


def _rmsnorm(x, g):
    return x * lax.rsqrt(jnp.mean(x * x, axis=-1, keepdims=True) + EPS) * g


def _dot(a, b):
    return jnp.dot(a, b, preferred_element_type=F32)


def _dot_nt(a, b):
    return lax.dot_general(a, b, (((1,), (1,)), ((), ())), preferred_element_type=F32)


def _const_spec(shape):
    nd = len(shape)
    return pl.BlockSpec(shape, lambda *_: (0,) * nd, pipeline_mode=pl.Buffered(1))


def _params(n_axes):
    return pltpu.CompilerParams(dimension_semantics=("arbitrary",) * n_axes,
                                vmem_limit_bytes=VMEM_LIMIT_BYTES)


FFN_CHUNK = 256


def _ple_epilogue(h, p_ref, nple_ref, wg_ref, wp_ref, nfin_ref):
    gate = _sigmoid(_dot(_rmsnorm(h, nple_ref[...]).astype(BF16), wg_ref[...]))
    h = h + gate * _dot(p_ref[...].astype(BF16), wp_ref[...])
    return _rmsnorm(h, nfin_ref[...])


def _ffn_kernel(x_ref, g_ref, wg_ref, wu_ref, wd_ref, *rest, fc, with_ple):
    if with_ple:
        p_ref, nple_ref, wpg_ref, wp_ref, nfin_ref, o_ref, xn_ref, acc_ref = rest
    else:
        o_ref, xn_ref, acc_ref = rest
    xn_ref[...] = _rmsnorm(x_ref[...], g_ref[...]).astype(BF16)
    for c in range(wd_ref.shape[0] // fc):
        cols = slice(c * fc, (c + 1) * fc)
        xn = xn_ref[...]
        act = (_silu(_dot(xn, wg_ref[:, cols])) * _dot(xn, wu_ref[:, cols])).astype(BF16)
        contrib = _dot(act, wd_ref[cols, :])
        if c == 0:
            acc_ref[...] = contrib
        else:
            acc_ref[...] += contrib
    h = x_ref[...] + 0.5 * acc_ref[...]
    if with_ple:
        h = _ple_epilogue(h, p_ref, nple_ref, wpg_ref, wp_ref, nfin_ref)
    o_ref[...] = h


def _ffn(x, g, wg, wu, wd, ple=None, *, tm):
    t, d = x.shape
    tm = min(tm, t)
    row = lambda w: pl.BlockSpec((tm, w), lambda i: (i, 0))
    consts = [g, wg, wu, wd]
    in_specs = [row(d)] + [_const_spec(a.shape) for a in consts]
    args = [x] + consts
    if ple is not None:
        in_specs += [row(ple[0].shape[1])] + [_const_spec(a.shape) for a in ple[1:]]
        args += list(ple)
    return pl.pallas_call(
        functools.partial(_ffn_kernel, fc=FFN_CHUNK, with_ple=ple is not None),
        grid=(t // tm,),
        in_specs=in_specs,
        out_specs=row(d),
        out_shape=jax.ShapeDtypeStruct((t, d), F32),
        scratch_shapes=[pltpu.VMEM((tm, d), BF16), pltpu.VMEM((tm, d), F32)],
        compiler_params=_params(1),
        name="ffn_ple" if ple is not None else "ffn",
    )(*args)


def _ffn_stream_kernel(x_ref, g_ref, wg32_ref, wu32_ref, wd32_ref, *rest, with_ple):
    if with_ple:
        (p_ref, nple_ref, wpg32_ref, wp32_ref, nfin_ref,
         o_ref, wg_ref, wu_ref, wd_ref, wpg_ref, wp_ref, xn_ref, acc_ref) = rest
    else:
        o_ref, wg_ref, wu_ref, wd_ref, xn_ref, acc_ref = rest
    c = pl.program_id(0)

    @pl.when(c == 0)
    def _():
        xn_ref[...] = _rmsnorm(x_ref[...], g_ref[...]).astype(BF16)
        acc_ref[...] = jnp.zeros(acc_ref.shape, F32)
        if with_ple:
            wpg_ref[...] = wpg32_ref[...].astype(BF16)
            wp_ref[...] = wp32_ref[...].astype(BF16)

    wg_ref[...] = wg32_ref[...].astype(BF16)
    wu_ref[...] = wu32_ref[...].astype(BF16)
    wd_ref[...] = wd32_ref[...].astype(BF16)
    xn = xn_ref[...]
    act = (_silu(_dot(xn, wg_ref[...])) * _dot(xn, wu_ref[...])).astype(BF16)
    acc_ref[...] += _dot(act, wd_ref[...])

    @pl.when(c == pl.num_programs(0) - 1)
    def _():
        h = x_ref[...] + 0.5 * acc_ref[...]
        if with_ple:
            h = _ple_epilogue(h, p_ref, nple_ref, wpg_ref, wp_ref, nfin_ref)
        o_ref[...] = h


def _ffn_stream(x, g, w_gu, w_down, ple=None):
    t, d = x.shape
    d_ff = w_down.shape[0]
    fc = FFN_CHUNK
    n_chunks = d_ff // fc
    whole = lambda a: pl.BlockSpec(a.shape, lambda c: (0,) * a.ndim)
    in_specs = [whole(x), whole(g),
                pl.BlockSpec((d, fc), lambda c: (0, c)),
                pl.BlockSpec((d, fc), lambda c: (0, n_chunks + c)),
                pl.BlockSpec((fc, d), lambda c: (c, 0))]
    args = [x, g, w_gu, w_gu, w_down]
    out_specs = [whole(x),
                 pl.BlockSpec((d, fc), lambda c: (0, c)),
                 pl.BlockSpec((d, fc), lambda c: (0, c)),
                 pl.BlockSpec((fc, d), lambda c: (c, 0))]
    out_shape = [jax.ShapeDtypeStruct((t, d), F32),
                 jax.ShapeDtypeStruct((d, d_ff), BF16),
                 jax.ShapeDtypeStruct((d, d_ff), BF16),
                 jax.ShapeDtypeStruct((d_ff, d), BF16)]
    if ple is not None:
        in_specs += [whole(a) for a in ple]
        args += list(ple)
        out_specs += [whole(ple[2]), whole(ple[3])]
        out_shape += [jax.ShapeDtypeStruct(ple[2].shape, BF16), jax.ShapeDtypeStruct(ple[3].shape, BF16)]
    return pl.pallas_call(
        functools.partial(_ffn_stream_kernel, with_ple=ple is not None),
        grid=(n_chunks,),
        in_specs=in_specs,
        out_specs=out_specs,
        out_shape=out_shape,
        scratch_shapes=[pltpu.VMEM((t, d), BF16), pltpu.VMEM((t, d), F32)],
        compiler_params=_params(1),
        name="ffn_stream_ple" if ple is not None else "ffn_stream",
    )(*args)


PROJ_UNIT = 512


def _proj_layout(d_inner, conv_dim, pool_dim, d_model):
    xbc, z, gates, v = conv_dim, d_inner, N_BRANCH * d_model, pool_dim
    lay = {"xbc": (0, xbc), "z": (xbc, z), "gates": (xbc + z, gates), "v": (xbc + z + gates, v),
           "dt": (xbc + z + gates + v, LANES)}
    for start, width in lay.values():
        assert start % width == 0
    total = -(-(lay["dt"][0] + LANES) // PROJ_UNIT) * PROJ_UNIT
    return lay, total


def _inproj_stream_kernel(h_ref, g_ref, wa_ref, wb_ref, proj_ref, wcat_ref, u_ref, *, n_a):
    c = pl.program_id(0)

    @pl.when(c == 0)
    def _():
        u_ref[...] = _rmsnorm(h_ref[...], g_ref[...]).astype(BF16)

    @pl.when(c < n_a)
    def _():
        wcat_ref[...] = wa_ref[...].astype(BF16)

    @pl.when(c >= n_a)
    def _():
        wcat_ref[...] = wb_ref[...].astype(BF16)

    proj_ref[...] = _dot(u_ref[...], wcat_ref[...])


def _inproj_stream(h, g, w_in, w_rest, *, n_xbc_units, n_z_units, total):
    t, d = h.shape
    n_a = n_xbc_units + n_z_units
    n_b = w_rest.shape[1] // PROJ_UNIT
    assert (n_a + n_b) * PROJ_UNIT == total
    whole = lambda a: pl.BlockSpec(a.shape, lambda c: (0,) * a.ndim)

    def unit_of_w_in(c):
        return jnp.where(c < n_xbc_units, c + n_z_units, jnp.minimum(c, n_a - 1) - n_xbc_units)

    return pl.pallas_call(
        functools.partial(_inproj_stream_kernel, n_a=n_a),
        grid=(n_a + n_b,),
        in_specs=[whole(h), whole(g),
                  pl.BlockSpec((d, PROJ_UNIT), lambda c: (0, unit_of_w_in(c))),
                  pl.BlockSpec((d, PROJ_UNIT), lambda c: (0, jnp.maximum(c - n_a, 0)))],
        out_specs=[pl.BlockSpec((t, PROJ_UNIT), lambda c: (0, c)),
                   pl.BlockSpec((d, PROJ_UNIT), lambda c: (0, c))],
        out_shape=[jax.ShapeDtypeStruct((t, total), F32),
                   jax.ShapeDtypeStruct((d, total), BF16)],
        scratch_shapes=[pltpu.VMEM((t, d), BF16)],
        compiler_params=_params(1),
        name="inproj_stream",
    )(h, g, w_in, w_rest)


def _inproj_mixer_kernel(h_ref, g_ref, w_ref, cw_ref, cb_ref, dtb_ref, alog_ref, dskip_ref,
                         z_ref, y_ref, pooled_ref, gates_ref, conv_ref, pool_ref, ssm_ref,
                         u_ref, raw_ref, vraw_ref, xc_ref, state_ref,
                         *, widths, starts, nc, tiles_per_seq, d_inner):
    i = pl.program_id(0)
    tm = h_ref.shape[0]
    n_slabs = raw_ref.shape[0]

    n_vslabs = vraw_ref.shape[0]
    tile_in_seq = i % tiles_per_seq
    span = SUBLANES * ROW_STRIDE

    @pl.when(tile_in_seq == 0)
    def _():
        raw_ref[:, 0:SUBLANES, :] = jnp.zeros((n_slabs, SUBLANES, LANES), F32)
        vraw_ref[:, 0:POOL_PRE, :] = jnp.zeros((n_vslabs, POOL_PRE, LANES), F32)
        state_ref[...] = jnp.zeros(state_ref.shape, F32)

    u_ref[...] = _rmsnorm(h_ref[...], g_ref[...]).astype(BF16)

    def pool_slab(s):
        cols = slice(s * LANES, (s + 1) * LANES)
        win = POOL_WINDOWS[s * len(POOL_WINDOWS) // n_vslabs]
        for base in range(0, tm, span):
            rows = {d: vraw_ref[s, pl.ds(POOL_PRE + base + d, SUBLANES, stride=ROW_STRIDE), :]
                    for d in range(1 - win, ROW_STRIDE)}
            for j in range(ROW_STRIDE):
                total = rows[j]
                for k in range(1, win):
                    total = total + rows[j - k]
                if base >= POOL_BUF:
                    mean = total * (1.0 / win)
                else:
                    pos = (tile_in_seq * tm + base + j
                           + ROW_STRIDE * lax.broadcasted_iota(jnp.int32, (SUBLANES, LANES), 0))
                    mean = total / jnp.minimum(pos + 1, win).astype(F32)
                pooled_ref[s, pl.ds(base + j, SUBLANES, stride=ROW_STRIDE), :] = mean - rows[j]
        pool_ref[0, :, cols] = vraw_ref[s, POOL_PRE + tm - POOL_BUF:POOL_PRE + tm, :]
        vraw_ref[s, 0:POOL_PRE, :] = vraw_ref[s, tm:tm + POOL_PRE, :]

    def conv_slab(s):
        cols = slice(s * LANES, (s + 1) * LANES)
        taps = [jnp.broadcast_to(cw_ref[k:k + 1, cols], (SUBLANES, LANES)) for k in range(CONV_W)]
        bias = jnp.broadcast_to(cb_ref[:, cols], (SUBLANES, LANES))
        for base in range(0, tm, span):
            shifted = [raw_ref[s, pl.ds(SUBLANES + base - (CONV_W - 1) + m, SUBLANES,
                                        stride=ROW_STRIDE), :]
                       for m in range(ROW_STRIDE + CONV_W - 1)]
            for j in range(ROW_STRIDE):
                acc = bias
                for k in range(CONV_W):
                    acc = acc + shifted[j + k] * taps[k]
                xc_ref[s, pl.ds(base + j, SUBLANES, stride=ROW_STRIDE), :] = _silu(acc)
        conv_ref[0, :, cols] = raw_ref[s, SUBLANES + tm - (CONV_W - 1):SUBLANES + tm, :]
        raw_ref[s, 0:SUBLANES, :] = raw_ref[s, tm:tm + SUBLANES, :]

    def project(k, a):
        b = min(a + nc, widths[k])
        return _dot(u_ref[...], w_ref[:, starts[k] + a:starts[k] + b])

    n_xbc_units = widths[1] // nc
    slabs_per_unit = nc // LANES
    done = 0
    for n, (k, a) in enumerate([(k, a) for k in (1, 2) for a in range(0, widths[k], nc)]):
        res = project(k, a)
        for s in range(res.shape[1] // LANES):
            piece = res[:, s * LANES:(s + 1) * LANES]
            if k == 1:
                raw_ref[a // LANES + s, SUBLANES:SUBLANES + tm, :] = piece
            else:
                vraw_ref[a // LANES + s, POOL_PRE:POOL_PRE + tm, :] = piece
                pool_slab(a // LANES + s)
        while done < min(n_slabs, n * slabs_per_unit, min(n + 1, n_xbc_units) * slabs_per_unit):
            conv_slab(done)
            done += 1
    dt_raw = project(4, 0)
    while done < n_slabs:
        conv_slab(done)
        done += 1

    q = CHUNK
    n_state = D_STATE
    assert q == LANES and n_state == LANES
    late_units = [(k, a) for k in (0, 3) for a in range(0, widths[k], nc)]
    late_outs = {0: z_ref, 3: gates_ref}
    n_blocks = (tm // q) * N_GROUPS
    emitted = 0
    scalars = []
    for ci in range(tm // q):
        dt, a_cs, seg_tot, causal = _chunk_scalars(dt_raw[ci * q:(ci + 1) * q, :], dtb_ref[...],
                                                   alog_ref[...], q)
        scalars.append((a_cs, a_cs.T, dt.T, (jnp.exp(seg_tot - a_cs) * dt).T, jnp.exp(seg_tot).T,
                        causal))
    tiles_per_group = d_inner // N_GROUPS // LANES
    for ci in range(tm // q):
        rows = slice(ci * q, (ci + 1) * q)
        a_cs, a_cs_t, dt_t, w_state_t, chunk_decay_t, causal = scalars[ci]
        for g in range(N_GROUPS):
            b_g = xc_ref[d_inner // LANES + g, rows, :].astype(BF16)
            c_g = xc_ref[d_inner // LANES + N_GROUPS + g, rows, :].astype(BF16)
            cb = _dot_nt(c_g, b_g)
            for p in range(tiles_per_group):
                tile = g * tiles_per_group + p
                h0 = tile * HEADS_PER_TILE
                cols = slice(tile * LANES, (tile + 1) * LANES)
                xpair = xc_ref[tile, rows, :]
                h_prev = state_ref[cols, :]
                y, e_pair = _diag_pair(xpair, cb, a_cs, a_cs_t, dt_t, causal, h0)
                y = y + _dot_nt(c_g, h_prev.astype(BF16)) * e_pair
                y_ref[rows, cols] = y + dskip_ref[:, cols] * xpair
                xw_t = (xpair.T * _pair_rows(w_state_t, h0, (LANES, q))).astype(BF16)
                decay_rows = _pair_rows(chunk_decay_t, h0, (LANES, n_state))
                state_ref[cols, :] = decay_rows * h_prev + _dot(xw_t, b_g)
            block = ci * N_GROUPS + g + 1
            while emitted < len(late_units) and emitted * n_blocks < block * len(late_units):
                k, a = late_units[emitted]
                res = project(k, a)
                late_outs[k][:, a:a + res.shape[1]] = res
                emitted += 1
    assert emitted == len(late_units)

    @pl.when(tile_in_seq == tiles_per_seq - 1)
    def _():
        ssm_ref[0] = state_ref[...]


def _inproj_mixer(h, g, w_cat, layout, conv_w, conv_b, dt_bias, a_log, d_skip, *, tm, seq, d_inner):
    order = ("z", "xbc", "v", "gates", "dt")
    starts = tuple(layout[k][0] for k in order)
    widths = tuple(layout[k][1] for k in order)
    t, d = h.shape
    tm = min(tm, seq)
    tiles_per_seq = seq // tm
    n_all = w_cat.shape[1]
    conv_dim, pool_dim = widths[1], widths[2]
    n_slabs = conv_dim // LANES
    assert tm % (SUBLANES * ROW_STRIDE) == 0 and tm % CHUNK == 0
    row = lambda w: pl.BlockSpec((tm, w), lambda i: (i, 0))
    per_seq = lambda i: (i // tiles_per_seq, 0, 0)
    return pl.pallas_call(
        functools.partial(_inproj_mixer_kernel, widths=widths, starts=starts, nc=PROJ_UNIT,
                          tiles_per_seq=tiles_per_seq, d_inner=d_inner),
        grid=(t // tm,),
        in_specs=[row(d),
                  _const_spec((1, d)),
                  _const_spec((d, n_all)),
                  _const_spec((CONV_W, conv_dim)),
                  _const_spec((1, conv_dim)),
                  _const_spec((1, LANES)),
                  _const_spec((1, LANES)),
                  _const_spec((1, d_inner))],
        out_specs=[row(widths[0]), row(d_inner),
                   pl.BlockSpec((pool_dim // LANES, tm, LANES), lambda i: (0, i, 0)),
                   row(widths[3]),
                   pl.BlockSpec((1, CONV_W - 1, conv_dim), per_seq),
                   pl.BlockSpec((1, POOL_BUF, pool_dim), per_seq),
                   pl.BlockSpec((1, d_inner, D_STATE), per_seq)],
        out_shape=[jax.ShapeDtypeStruct((t, widths[0]), F32),
                   jax.ShapeDtypeStruct((t, d_inner), F32),
                   jax.ShapeDtypeStruct((pool_dim // LANES, t, LANES), F32),
                   jax.ShapeDtypeStruct((t, widths[3]), F32),
                   jax.ShapeDtypeStruct((t // seq, CONV_W - 1, conv_dim), F32),
                   jax.ShapeDtypeStruct((t // seq, POOL_BUF, pool_dim), F32),
                   jax.ShapeDtypeStruct((t // seq, d_inner, D_STATE), F32)],
        scratch_shapes=[pltpu.VMEM((tm, d), BF16),
                        pltpu.VMEM((n_slabs, SUBLANES + tm, LANES), F32),
                        pltpu.VMEM((pool_dim // LANES, POOL_PRE + tm, LANES), F32),
                        pltpu.VMEM((n_slabs, tm, LANES), F32),
                        pltpu.VMEM((d_inner, D_STATE), F32)],
        compiler_params=_params(1),
        name="inproj_mixer",
    )(h, g, w_cat, conv_w, conv_b, dt_bias, a_log, d_skip)


def _chunk_scalars(dt_raw, dt_bias, a_log, seg_len):
    q = dt_raw.shape[0]
    dt = _softplus(dt_raw + dt_bias)
    da = dt * (-jnp.exp(a_log))
    shift = int(math.log2(seg_len))
    qi = lax.broadcasted_iota(jnp.int32, (q, q), 0)
    si = lax.broadcasted_iota(jnp.int32, (q, q), 1)
    same = (qi >> shift) == (si >> shift)
    causal = jnp.logical_and(same, si <= qi)
    a_cs = jnp.dot(jnp.where(causal, 1.0, 0.0), da, precision=HIGHEST, preferred_element_type=F32)
    seg_tot = jnp.dot(jnp.where(same, 1.0, 0.0), da, precision=HIGHEST, preferred_element_type=F32)
    return dt, a_cs, seg_tot, causal


def _pair_cols(vals, h0, shape):
    lane = lax.broadcasted_iota(jnp.int32, shape, 1)
    return jnp.where(lane < HEAD_DIM, vals[:, h0:h0 + 1], vals[:, h0 + 1:h0 + 2])


def _pair_rows(vals_t, h0, shape):
    row = lax.broadcasted_iota(jnp.int32, shape, 0)
    return jnp.where(row < HEAD_DIM, vals_t[h0:h0 + 1, :], vals_t[h0 + 1:h0 + 2, :])


def _diag_pair(xpair, cb, a_cs, a_cs_t, dt_t, causal, h0):
    q = xpair.shape[0]
    lane = lax.broadcasted_iota(jnp.int32, xpair.shape, 1)
    out = None
    spread = []
    for k in range(HEADS_PER_TILE):
        h = h0 + k
        a_col = jnp.broadcast_to(a_cs[:, h:h + 1], (q, LANES))
        spread.append(a_col)
        decay = jnp.exp(jnp.where(causal, a_col - a_cs_t[h:h + 1, :], -jnp.inf))
        w = (cb * decay * dt_t[h:h + 1, :]).astype(BF16)
        mine = (lane < HEAD_DIM) if k == 0 else (lane >= HEAD_DIM)
        t = _dot(w, jnp.where(mine, xpair, 0.0).astype(BF16))
        out = t if out is None else out + t
    e_pair = jnp.exp(jnp.where(lane < HEAD_DIM, spread[0], spread[1]))
    return out, e_pair


def _shift_rows(cur, prev, s):
    sub = lax.broadcasted_iota(jnp.int32, (1,) + cur.shape[1:], 1)
    return jnp.where(sub < s, pltpu.roll(prev, s, axis=1), pltpu.roll(cur, s, axis=1))


def _conv_tiles(cur, prev, w, bias):
    tap = lambda k: w[k:k + 1, :][None]
    acc = cur * tap(CONV_W - 1)
    for s in range(1, CONV_W):
        acc = acc + _shift_rows(cur, prev, s) * tap(CONV_W - 1 - s)
    return _silu(bias[None] + acc)


def _conv_block(ext, w, bias):
    rows, width = ext.shape
    tiles = ext.reshape(rows // SUBLANES, SUBLANES, width)
    return _conv_tiles(tiles[1:], tiles[:-1], w, bias).reshape(rows - SUBLANES, width)


def _conv_tile_rows(rows_of_tap, w, bias):
    acc = rows_of_tap(0) * w[0:1, :]
    for k in range(1, CONV_W):
        acc = acc + rows_of_tap(k) * w[k:k + 1, :]
    return _silu(bias + acc)


def _mixer_sample_kernel(xbc_ref, dt_ref, v_ref, conv0_ref, pool0_ref, ssm0_ref,
                         cw_ref, cb_ref, dtb_ref, alog_ref, dskip_ref,
                         y_ref, pooled_ref, ssm_ref, conv_ref, pool_ref,
                         head_ref, xc_ref, xwt_ref, e_ref, seg_ref, pe_ref,
                         *, d_inner, seq, pos0, seqs_per_step):
    j = pl.program_id(1)
    q = xbc_ref.shape[0]
    conv_dim = xbc_ref.shape[1]
    pool_dim = v_ref.shape[1]
    n_state = D_STATE
    n_seq = q // seq
    tiles_per_group = d_inner // N_GROUPS // LANES
    gc = pool_dim // len(POOL_WINDOWS)

    @pl.when(j == 0)
    def _():
        lo = SUBLANES - (CONV_W - 1)

        def per_seq(b, carry):
            r0 = pl.multiple_of(b * seq, seq)
            head_ref[lo:SUBLANES, :] = conv0_ref[b]
            ct = 1024
            for jc in range(conv_dim // ct):
                cols = slice(jc * ct, (jc + 1) * ct)
                head_ref[SUBLANES:SUBLANES + seq, cols] = xbc_ref[pl.ds(r0, seq), cols]
                xc_ref[pl.ds(r0, seq), cols] = _conv_tile_rows(
                    lambda k: head_ref[lo + k:lo + k + seq, cols], cw_ref[:, cols], cb_ref[:, cols])
            conv_ref[b] = head_ref[SUBLANES + seq - (CONV_W - 1):SUBLANES + seq, :]

            pe_ref[POOL_PAD - POOL_BUF:POOL_PAD, :] = pool0_ref[b]
            vb = v_ref[pl.ds(r0, seq), :]
            pe_ref[POOL_PAD:POOL_PAD + seq, :] = vb
            t_idx = lax.broadcasted_iota(jnp.int32, (seq, 1), 0) + pos0
            for gi, win in enumerate(POOL_WINDOWS):
                cols = slice(gi * gc, (gi + 1) * gc)
                total = pe_ref[POOL_PAD:POOL_PAD + seq, cols]
                for k in range(1, win):
                    total = total + pe_ref[POOL_PAD - k:POOL_PAD - k + seq, cols]
                cnt = jnp.minimum(t_idx + 1, win).astype(F32)
                pooled = total / cnt - vb[:, cols]
                for s in range(gc // LANES):
                    pooled_ref[gi * (gc // LANES) + s, pl.ds(r0, seq), :] = (
                        pooled[:, s * LANES:(s + 1) * LANES])
            pool_ref[b] = pe_ref[POOL_PAD + seq - POOL_BUF:POOL_PAD + seq, :]
            return carry

        lax.fori_loop(0, n_seq, per_seq, 0)

        dt, a_cs, seg_tot, causal = _chunk_scalars(dt_ref[...], dtb_ref[...], alog_ref[...], seq)
        a_cs_t = a_cs.T
        dt_t = dt.T
        w_state_t = (jnp.exp(seg_tot - a_cs) * dt).T
        seg_ref[...] = seg_tot
        for g in range(N_GROUPS):
            b_g = xc_ref[:, d_inner + g * n_state:d_inner + (g + 1) * n_state].astype(BF16)
            c_g = xc_ref[:, d_inner + (N_GROUPS + g) * n_state:
                         d_inner + (N_GROUPS + g + 1) * n_state].astype(BF16)
            cb = _dot_nt(c_g, b_g)
            for p in range(tiles_per_group):
                tile = g * tiles_per_group + p
                h0 = tile * HEADS_PER_TILE
                cols = slice(tile * LANES, (tile + 1) * LANES)
                xpair = xc_ref[:, cols]
                y, e_pair = _diag_pair(xpair, cb, a_cs, a_cs_t, dt_t, causal, h0)
                y_ref[:, cols] = y + dskip_ref[:, cols] * xpair
                e_ref[:, cols] = e_pair
                xwt_ref[cols, :] = (xpair.T * _pair_rows(w_state_t, h0, (LANES, q))).astype(BF16)

    row_q = lax.broadcasted_iota(jnp.int32, (q, n_state), 0)
    row128 = lax.broadcasted_iota(jnp.int32, (LANES, n_state), 0)
    for sj in range(seqs_per_step):
        r0 = pl.multiple_of((j * seqs_per_step + sj) * seq, seq)
        chunk_decay = jnp.exp(seg_ref[pl.ds(r0, 1), :])
        mine = jnp.logical_and(row_q >= r0, row_q < r0 + seq)
        for g in range(N_GROUPS):
            b_cols = slice(d_inner + g * n_state, d_inner + (g + 1) * n_state)
            c_cols = slice(d_inner + (N_GROUPS + g) * n_state,
                           d_inner + (N_GROUPS + g + 1) * n_state)
            b_mine = jnp.where(mine, xc_ref[:, b_cols], 0.0).astype(BF16)
            c_rows = xc_ref[pl.ds(r0, seq), c_cols].astype(BF16)
            for p in range(tiles_per_group):
                tile = g * tiles_per_group + p
                h0 = tile * HEADS_PER_TILE
                cols = slice(tile * LANES, (tile + 1) * LANES)
                h_prev = ssm0_ref[sj, cols, :]
                y_off = _dot_nt(c_rows, h_prev.astype(BF16)) * e_ref[pl.ds(r0, seq), cols]
                y_ref[pl.ds(r0, seq), cols] += y_off
                decay_rows = jnp.where(row128 < HEAD_DIM, chunk_decay[:, h0:h0 + 1],
                                       chunk_decay[:, h0 + 1:h0 + 2])
                ssm_ref[sj, cols, :] = decay_rows * h_prev + _dot(xwt_ref[cols, :], b_mine)


def _mixer_sample(proj, layout, conv0, pool0, ssm0, conv_w, conv_b, dt_bias, a_log, d_skip,
                  *, batch, seq, d_inner, pos0):
    q = CHUNK
    n_seq = q // seq
    n_chunks = batch // n_seq
    conv_dim = layout["xbc"][1]
    pool_dim = layout["v"][1]
    window = lambda k: pl.BlockSpec((q, layout[k][1]), lambda i, j: (i, layout[k][0] // layout[k][1]))
    tok = lambda i, j: (i, 0)
    per_chunk = lambda i, j: (i, 0, 0)
    sps = math.gcd(n_seq, 4)
    steps = n_seq // sps
    per_seq = lambda i, j: (i * steps + j, 0, 0)
    return pl.pallas_call(
        functools.partial(_mixer_sample_kernel, d_inner=d_inner, seq=seq, pos0=pos0,
                          seqs_per_step=sps),
        grid=(n_chunks, steps),
        in_specs=[window("xbc"), window("dt"), window("v"),
                  pl.BlockSpec((n_seq, CONV_W - 1, conv_dim), per_chunk),
                  pl.BlockSpec((n_seq, POOL_BUF, pool_dim), per_chunk),
                  pl.BlockSpec((sps, d_inner, D_STATE), per_seq),
                  _const_spec((CONV_W, conv_dim)),
                  _const_spec((1, conv_dim)),
                  _const_spec((1, LANES)),
                  _const_spec((1, LANES)),
                  _const_spec((1, d_inner))],
        out_specs=[pl.BlockSpec((q, d_inner), tok),
                   pl.BlockSpec((pool_dim // LANES, q, LANES), lambda i, j: (0, i, 0)),
                   pl.BlockSpec((sps, d_inner, D_STATE), per_seq),
                   pl.BlockSpec((n_seq, CONV_W - 1, conv_dim), per_chunk),
                   pl.BlockSpec((n_seq, POOL_BUF, pool_dim), per_chunk)],
        out_shape=[jax.ShapeDtypeStruct((batch * seq, d_inner), F32),
                   jax.ShapeDtypeStruct((pool_dim // LANES, batch * seq, LANES), F32),
                   jax.ShapeDtypeStruct((batch, d_inner, D_STATE), F32),
                   jax.ShapeDtypeStruct((batch, CONV_W - 1, conv_dim), F32),
                   jax.ShapeDtypeStruct((batch, POOL_BUF, pool_dim), F32)],
        scratch_shapes=[pltpu.VMEM((2 * SUBLANES, conv_dim), F32),
                        pltpu.VMEM((q, conv_dim), F32),
                        pltpu.VMEM((d_inner, q), BF16),
                        pltpu.VMEM((q, d_inner), F32),
                        pltpu.VMEM((q, LANES), F32),
                        pltpu.VMEM((POOL_PAD + seq, pool_dim), F32)],
        compiler_params=_params(2),
        name="mixer_sample",
    )(proj, proj, proj, conv0, pool0, ssm0, conv_w, conv_b, dt_bias, a_log, d_skip)


def _merge_kernel(h_ref, y_ref, z_ref, pooled_ref, gates_ref, nssd_ref, pscale_ref, *rest,
                  cast_weights):
    if cast_weights:
        w32_refs, (o_ref, *w_refs), (yn_ref, pm_ref) = rest[:4], rest[4:9], rest[9:]

        @pl.when(pl.program_id(0) == 0)
        def _():
            for dst, src in zip(w_refs, w32_refs):
                dst[...] = src[...].astype(BF16)
    else:
        w_refs, o_ref, (yn_ref, pm_ref) = rest[:4], rest[4], rest[5:]
    wssd_ref, wpg_ref, wpo_ref, wo_ref = w_refs
    d_inner = y_ref.shape[1]
    d_model = h_ref.shape[1]
    gw = d_inner // N_GROUPS
    for g in range(N_GROUPS):
        cols = slice(g * gw, (g + 1) * gw)
        yz = y_ref[:, cols] * _silu(z_ref[:, cols])
        yn_ref[:, cols] = _rmsnorm(yz, nssd_ref[:, cols]).astype(BF16)
    a_branch = _dot(yn_ref[...], wssd_ref[...])
    slabs_per_group = pooled_ref.shape[0] // len(POOL_WINDOWS)
    gc = slabs_per_group * LANES
    for gi in range(len(POOL_WINDOWS)):
        cols = slice(gi * gc, (gi + 1) * gc)
        pooled = jnp.concatenate([pooled_ref[gi * slabs_per_group + s]
                                  for s in range(slabs_per_group)], axis=1)
        mixed = _dot(pooled.astype(BF16), wpg_ref[gi])
        pm_ref[:, cols] = (mixed * pscale_ref[:, cols]).astype(BF16)
    b_branch = _dot(pm_ref[...], wpo_ref[...])
    merged = (_sigmoid(gates_ref[:, 0:d_model]) * a_branch
              + _sigmoid(gates_ref[:, d_model:2 * d_model]) * b_branch)
    o_ref[...] = h_ref[...] + _dot(merged.astype(BF16), wo_ref[...])


def _merge(h, y, z_src, pooled, gates_src, norm_ssd, pool_scale, weights, *, tm, cast_weights):
    t, d = h.shape
    tm = min(tm, t)
    d_inner = y.shape[1]
    pool_slabs = pooled.shape[0]
    pool_dim = pool_slabs * LANES
    row = lambda w, blk=0: pl.BlockSpec((tm, w), lambda i: (i, blk))
    out_specs = [row(d)]
    out_shape = [jax.ShapeDtypeStruct((t, d), F32)]
    if cast_weights:
        out_specs += [pl.BlockSpec(w.shape, lambda i, nd=w.ndim: (0,) * nd) for w in weights]
        out_shape += [jax.ShapeDtypeStruct(w.shape, BF16) for w in weights]
    res = pl.pallas_call(
        functools.partial(_merge_kernel, cast_weights=cast_weights),
        grid=(t // tm,),
        in_specs=[row(d), row(d_inner), row(d_inner, z_src[1]),
                  pl.BlockSpec((pool_slabs, tm, LANES), lambda i: (0, i, 0)),
                  row(N_BRANCH * d, gates_src[1]),
                  _const_spec((1, d_inner)), _const_spec((1, pool_dim))]
        + [_const_spec(w.shape) for w in weights],
        out_specs=out_specs,
        out_shape=out_shape,
        scratch_shapes=[pltpu.VMEM((tm, d_inner), BF16), pltpu.VMEM((tm, pool_dim), BF16)],
        compiler_params=_params(1),
        name="merge_cast" if cast_weights else "merge",
    )(h, y, z_src[0], pooled, gates_src[0], norm_ssd, pool_scale, *weights)
    return res if cast_weights else res[0]


def _pad_lanes(a):
    return jnp.pad(a.reshape(1, -1), ((0, 0), (0, LANES - a.shape[-1])))


def kernel(x_prompt, x_sample, state_ssm, state_conv, state_pool, p_prompt, p_sample, norm_ffn1, w_ffn1_gu, w_ffn1_down, norm_mix, w_in, conv_w, conv_b, dt_bias, a_log, d_skip, norm_ssd, w_ssd_out, w_pool_group, pool_scale, w_pool_out, w_o, norm_ffn2, w_ffn2_gu, w_ffn2_down, norm_ple, w_ple_gate, w_ple, norm_final):
    depth = norm_ffn1.shape[0]
    assert depth == 1, "the final norm is fused into the layer's last stage: one layer only"
    batch, seq, d_model = x_prompt.shape
    dec_batch, dec_seq, _ = x_sample.shape
    n_heads = dt_bias.shape[1]
    d_inner = n_heads * HEAD_DIM
    conv_dim = conv_w.shape[2]
    pool_dim = pool_scale.shape[1]
    assert seq % CHUNK == 0 and CHUNK % dec_seq == 0 and dec_batch % (CHUNK // dec_seq) == 0

    row = lambda a: a[0].reshape(1, -1)
    layout, proj_cols = _proj_layout(d_inner, conv_dim, pool_dim, d_model)
    c_dt = d_inner + conv_dim
    c_v = c_dt + n_heads
    c_gates = c_v + pool_dim
    assert d_inner % PROJ_UNIT == 0 and conv_dim % PROJ_UNIT == 0
    rest_cols = proj_cols - d_inner - conv_dim
    w_rest = jnp.concatenate([w_in[0][:, c_gates:], w_in[0][:, c_v:c_gates], w_in[0][:, c_dt:c_v]],
                             axis=1)
    w_rest = jnp.pad(w_rest, ((0, 0), (0, rest_cols - w_rest.shape[1])))
    dtb, alog = _pad_lanes(dt_bias[0]), _pad_lanes(a_log[0])
    dskip = jnp.repeat(d_skip[0], HEAD_DIM).reshape(1, d_inner)
    cb = row(conv_b)
    ple_consts = (row(norm_ple), w_ple_gate[0], w_ple[0], norm_final.reshape(1, -1))

    h1, wg1, wu1, wd1 = _ffn_stream(x_sample.reshape(dec_batch * dec_seq, d_model), row(norm_ffn1),
                                    w_ffn1_gu[0], w_ffn1_down[0])
    proj, w_cat = _inproj_stream(h1, row(norm_mix), w_in[0], w_rest,
                                 n_xbc_units=conv_dim // PROJ_UNIT, n_z_units=d_inner // PROJ_UNIT,
                                 total=proj_cols)
    y, pooled, s2, c2, q2 = _mixer_sample(
        proj, layout, state_conv[0], state_pool[0], state_ssm[0].reshape(dec_batch, d_inner, D_STATE),
        conv_w[0], cb, dtb, alog, dskip,
        batch=dec_batch, seq=dec_seq, d_inner=d_inner, pos0=PAST_LEN)
    block_of = lambda k: layout[k][0] // layout[k][1]
    h2, wssd, wpg, wpo, wo = _merge(
        h1, y, (proj, block_of("z")), pooled, (proj, block_of("gates")), row(norm_ssd),
        row(pool_scale), (w_ssd_out[0], w_pool_group[0], w_pool_out[0], w_o[0]),
        tm=128, cast_weights=True)
    y_sample, wg2, wu2, wd2, wpleg, wple = _ffn_stream(
        h2, row(norm_ffn2), w_ffn2_gu[0], w_ffn2_down[0],
        (p_sample[0].reshape(dec_batch * dec_seq, -1),) + ple_consts)

    h1 = _ffn(x_prompt.reshape(batch * seq, d_model), row(norm_ffn1), wg1, wu1, wd1, tm=512)
    z, y, pooled, gates, c1, q1, s1 = _inproj_mixer(h1, row(norm_mix), w_cat, layout, conv_w[0], cb,
                                                    dtb, alog, dskip, tm=256, seq=seq, d_inner=d_inner)
    h2 = _merge(h1, y, (z, 0), pooled, (gates, 0), row(norm_ssd), row(pool_scale),
                (wssd, wpg, wpo, wo), tm=256, cast_weights=False)
    y_prompt = _ffn(h2, row(norm_ffn2), wg2, wu2, wd2,
                    (p_prompt[0].reshape(batch * seq, -1), ple_consts[0], wpleg, wple, ple_consts[3]),
                    tm=512)

    return (y_prompt.reshape(batch, seq, d_model),
            y_sample.reshape(dec_batch, dec_seq, d_model),
            s1.reshape(1, batch, n_heads, HEAD_DIM, D_STATE), c1[None], q1[None],
            s2.reshape(1, dec_batch, n_heads, HEAD_DIM, D_STATE), c2[None], q2[None])
```

```python
import functools
import math

import jax
import jax.numpy as jnp
from jax import lax
from jax.experimental import pallas as pl
from jax.experimental.pallas import tpu as pltpu

F32 = jnp.float32
BF16 = jnp.bfloat16
EPS = 1e-6
HIGHEST = lax.Precision.HIGHEST
NEG_LOG2E = -1.4426950408889634

LANES = 128
SUBLANES = 8
VMEM_LIMIT_BYTES = 56 * 1024 * 1024

HEAD_DIM = 64
N_GROUPS = 8
D_STATE = 128
CONV_W = 4
CHUNK = 128
POOL_WINDOWS = (2, 4, 8, 16)
POOL_BUF = max(POOL_WINDOWS) - 1
N_BRANCH = 2
PAST_LEN = 16384
HEADS_PER_TILE = LANES // HEAD_DIM
ROW_STRIDE = 4
POOL_PAD = 24
POOL_PRE = 16


def _sigmoid(x):
    return 1.0 / (1.0 + jnp.exp2(x * NEG_LOG2E))


def _silu(x):
    half = 0.5 * x
    return half + half * jnp.tanh(half)


def _softplus(x):
    return jnp.maximum(x, 0.0) + jnp.log(1.0 + jnp.exp(-jnp.abs(x)))


def _rmsnorm(x, g):
    return x * lax.rsqrt(jnp.mean(x * x, axis=-1, keepdims=True) + EPS) * g


def _dot(a, b):
    return jnp.dot(a, b, preferred_element_type=F32)


def _dot_nt(a, b):
    return lax.dot_general(a, b, (((1,), (1,)), ((), ())), preferred_element_type=F32)


def _const_spec(shape):
    nd = len(shape)
    return pl.BlockSpec(shape, lambda *_: (0,) * nd, pipeline_mode=pl.Buffered(1))


def _params(n_axes):
    return pltpu.CompilerParams(dimension_semantics=("arbitrary",) * n_axes,
                                vmem_limit_bytes=VMEM_LIMIT_BYTES)


FFN_CHUNK = 256


def _ple_epilogue(h, p_ref, nple_ref, wg_ref, wp_ref, nfin_ref):
    gate = _sigmoid(_dot(_rmsnorm(h, nple_ref[...]).astype(BF16), wg_ref[...]))
    h = h + gate * _dot(p_ref[...].astype(BF16), wp_ref[...])
    return _rmsnorm(h, nfin_ref[...])


def _ffn_kernel(x_ref, g_ref, wg_ref, wu_ref, wd_ref, *rest, fc, with_ple):
    if with_ple:
        p_ref, nple_ref, wpg_ref, wp_ref, nfin_ref, o_ref, xn_ref, acc_ref = rest
    else:
        o_ref, xn_ref, acc_ref = rest
    xn_ref[...] = _rmsnorm(x_ref[...], g_ref[...]).astype(BF16)
    for c in range(wd_ref.shape[0] // fc):
        cols = slice(c * fc, (c + 1) * fc)
        xn = xn_ref[...]
        act = (_silu(_dot(xn, wg_ref[:, cols])) * _dot(xn, wu_ref[:, cols])).astype(BF16)
        contrib = _dot(act, wd_ref[cols, :])
        if c == 0:
            acc_ref[...] = contrib
        else:
            acc_ref[...] += contrib
    h = x_ref[...] + 0.5 * acc_ref[...]
    if with_ple:
        h = _ple_epilogue(h, p_ref, nple_ref, wpg_ref, wp_ref, nfin_ref)
    o_ref[...] = h


def _ffn(x, g, wg, wu, wd, ple=None, *, tm):
    t, d = x.shape
    tm = min(tm, t)
    row = lambda w: pl.BlockSpec((tm, w), lambda i: (i, 0))
    consts = [g, wg, wu, wd]
    in_specs = [row(d)] + [_const_spec(a.shape) for a in consts]
    args = [x] + consts
    if ple is not None:
        in_specs += [row(ple[0].shape[1])] + [_const_spec(a.shape) for a in ple[1:]]
        args += list(ple)
    return pl.pallas_call(
        functools.partial(_ffn_kernel, fc=FFN_CHUNK, with_ple=ple is not None),
        grid=(t // tm,),
        in_specs=in_specs,
        out_specs=row(d),
        out_shape=jax.ShapeDtypeStruct((t, d), F32),
        scratch_shapes=[pltpu.VMEM((tm, d), BF16), pltpu.VMEM((tm, d), F32)],
        compiler_params=_params(1),
        name="ffn_ple" if ple is not None else "ffn",
    )(*args)


def _ffn_stream_kernel(x_ref, g_ref, wg32_ref, wu32_ref, wd32_ref, *rest, with_ple):
    if with_ple:
        (p_ref, nple_ref, wpg32_ref, wp32_ref, nfin_ref,
         o_ref, wg_ref, wu_ref, wd_ref, wpg_ref, wp_ref, xn_ref, acc_ref) = rest
    else:
        o_ref, wg_ref, wu_ref, wd_ref, xn_ref, acc_ref = rest
    c = pl.program_id(0)

    @pl.when(c == 0)
    def _():
        xn_ref[...] = _rmsnorm(x_ref[...], g_ref[...]).astype(BF16)
        acc_ref[...] = jnp.zeros(acc_ref.shape, F32)
        if with_ple:
            wpg_ref[...] = wpg32_ref[...].astype(BF16)
            wp_ref[...] = wp32_ref[...].astype(BF16)

    wg_ref[...] = wg32_ref[...].astype(BF16)
    wu_ref[...] = wu32_ref[...].astype(BF16)
    wd_ref[...] = wd32_ref[...].astype(BF16)
    xn = xn_ref[...]
    act = (_silu(_dot(xn, wg_ref[...])) * _dot(xn, wu_ref[...])).astype(BF16)
    acc_ref[...] += _dot(act, wd_ref[...])

    @pl.when(c == pl.num_programs(0) - 1)
    def _():
        h = x_ref[...] + 0.5 * acc_ref[...]
        if with_ple:
            h = _ple_epilogue(h, p_ref, nple_ref, wpg_ref, wp_ref, nfin_ref)
        o_ref[...] = h


def _ffn_stream(x, g, w_gu, w_down, ple=None):
    t, d = x.shape
    d_ff = w_down.shape[0]
    fc = FFN_CHUNK
    n_chunks = d_ff // fc
    whole = lambda a: pl.BlockSpec(a.shape, lambda c: (0,) * a.ndim)
    in_specs = [whole(x), whole(g),
                pl.BlockSpec((d, fc), lambda c: (0, c)),
                pl.BlockSpec((d, fc), lambda c: (0, n_chunks + c)),
                pl.BlockSpec((fc, d), lambda c: (c, 0))]
    args = [x, g, w_gu, w_gu, w_down]
    out_specs = [whole(x),
                 pl.BlockSpec((d, fc), lambda c: (0, c)),
                 pl.BlockSpec((d, fc), lambda c: (0, c)),
                 pl.BlockSpec((fc, d), lambda c: (c, 0))]
    out_shape = [jax.ShapeDtypeStruct((t, d), F32),
                 jax.ShapeDtypeStruct((d, d_ff), BF16),
                 jax.ShapeDtypeStruct((d, d_ff), BF16),
                 jax.ShapeDtypeStruct((d_ff, d), BF16)]
    if ple is not None:
        in_specs += [whole(a) for a in ple]
        args += list(ple)
        out_specs += [whole(ple[2]), whole(ple[3])]
        out_shape += [jax.ShapeDtypeStruct(ple[2].shape, BF16), jax.ShapeDtypeStruct(ple[3].shape, BF16)]
    return pl.pallas_call(
        functools.partial(_ffn_stream_kernel, with_ple=ple is not None),
        grid=(n_chunks,),
        in_specs=in_specs,
        out_specs=out_specs,
        out_shape=out_shape,
        scratch_shapes=[pltpu.VMEM((t, d), BF16), pltpu.VMEM((t, d), F32)],
        compiler_params=_params(1),
        name="ffn_stream_ple" if ple is not None else "ffn_stream",
    )(*args)


PROJ_UNIT = 512


def _proj_layout(d_inner, conv_dim, pool_dim, d_model):
    xbc, z, gates, v = conv_dim, d_inner, N_BRANCH * d_model, pool_dim
    lay = {"xbc": (0, xbc), "z": (xbc, z), "gates": (xbc + z, gates), "v": (xbc + z + gates, v),
           "dt": (xbc + z + gates + v, LANES)}
    for start, width in lay.values():
        assert start % width == 0
    total = -(-(lay["dt"][0] + LANES) // PROJ_UNIT) * PROJ_UNIT
    return lay, total


def _inproj_stream_kernel(rows_ref, valid_ref, h_ref, g_ref, wt_ref, proj_ref, wcat_ref, u_ref):
    c = pl.program_id(0)

    @pl.when(c == 0)
    def _():
        u_ref[...] = _rmsnorm(h_ref[...], g_ref[...]).astype(BF16)

    wt = wt_ref[...]
    row = lax.broadcasted_iota(jnp.int32, wt.shape, 0)
    wcat_ref[...] = jnp.where(row < valid_ref[c], wt, 0.0).T.astype(BF16)
    proj_ref[...] = _dot(u_ref[...], wcat_ref[...])


def _inproj_stream(h, g, w_t, unit_rows, unit_valid):
    t, d = h.shape
    n_units = len(unit_rows)
    total = n_units * PROJ_UNIT
    assert all(r % SUBLANES == 0 and r + PROJ_UNIT <= w_t.shape[0] for r in unit_rows)
    whole = lambda a: pl.BlockSpec(a.shape, lambda c, rows, valid: (0,) * a.ndim)
    return pl.pallas_call(
        _inproj_stream_kernel,
        grid_spec=pltpu.PrefetchScalarGridSpec(
            num_scalar_prefetch=2,
            grid=(n_units,),
            in_specs=[whole(h), whole(g),
                      pl.BlockSpec((pl.Element(PROJ_UNIT), pl.Element(d)),
                                   lambda c, rows, valid: (rows[c] * SUBLANES, 0))],
            out_specs=[pl.BlockSpec((t, PROJ_UNIT), lambda c, rows, valid: (0, c)),
                       pl.BlockSpec((d, PROJ_UNIT), lambda c, rows, valid: (0, c))],
            scratch_shapes=[pltpu.VMEM((t, d), BF16)]),
        out_shape=[jax.ShapeDtypeStruct((t, total), F32),
                   jax.ShapeDtypeStruct((d, total), BF16)],
        compiler_params=_params(1),
        name="inproj_stream",
    )(jnp.asarray([r // SUBLANES for r in unit_rows], jnp.int32),
      jnp.asarray(unit_valid, jnp.int32), h, g, w_t)


def _inproj_mixer_kernel(h_ref, g_ref, w_ref, cw_ref, cb_ref, dtb_ref, alog_ref, dskip_ref,
                         z_ref, y_ref, pooled_ref, gates_ref, conv_ref, pool_ref, ssm_ref,
                         u_ref, raw_ref, vraw_ref, xc_ref, state_ref,
                         *, widths, starts, nc, tiles_per_seq, d_inner):
    i = pl.program_id(0)
    tm = h_ref.shape[0]
    n_slabs = raw_ref.shape[0]

    n_vslabs = vraw_ref.shape[0]
    tile_in_seq = i % tiles_per_seq
    span = SUBLANES * ROW_STRIDE

    @pl.when(tile_in_seq == 0)
    def _():
        raw_ref[:, 0:SUBLANES, :] = jnp.zeros((n_slabs, SUBLANES, LANES), F32)
        vraw_ref[:, 0:POOL_PRE, :] = jnp.zeros((n_vslabs, POOL_PRE, LANES), F32)
        state_ref[...] = jnp.zeros(state_ref.shape, F32)

    u_ref[...] = _rmsnorm(h_ref[...], g_ref[...]).astype(BF16)

    def pool_slab(s):
        cols = slice(s * LANES, (s + 1) * LANES)
        win = POOL_WINDOWS[s * len(POOL_WINDOWS) // n_vslabs]
        for base in range(0, tm, span):
            rows = {d: vraw_ref[s, pl.ds(POOL_PRE + base + d, SUBLANES, stride=ROW_STRIDE), :]
                    for d in range(1 - win, ROW_STRIDE)}
            for j in range(ROW_STRIDE):
                total = rows[j]
                for k in range(1, win):
                    total = total + rows[j - k]
                if base >= POOL_BUF:
                    mean = total * (1.0 / win)
                else:
                    pos = (tile_in_seq * tm + base + j
                           + ROW_STRIDE * lax.broadcasted_iota(jnp.int32, (SUBLANES, LANES), 0))
                    mean = total / jnp.minimum(pos + 1, win).astype(F32)
                pooled_ref[s, pl.ds(base + j, SUBLANES, stride=ROW_STRIDE), :] = mean - rows[j]
        pool_ref[0, :, cols] = vraw_ref[s, POOL_PRE + tm - POOL_BUF:POOL_PRE + tm, :]
        vraw_ref[s, 0:POOL_PRE, :] = vraw_ref[s, tm:tm + POOL_PRE, :]

    def conv_slab(s):
        cols = slice(s * LANES, (s + 1) * LANES)
        taps = [jnp.broadcast_to(cw_ref[k:k + 1, cols], (SUBLANES, LANES)) for k in range(CONV_W)]
        bias = jnp.broadcast_to(cb_ref[:, cols], (SUBLANES, LANES))
        for base in range(0, tm, span):
            shifted = [raw_ref[s, pl.ds(SUBLANES + base - (CONV_W - 1) + m, SUBLANES,
                                        stride=ROW_STRIDE), :]
                       for m in range(ROW_STRIDE + CONV_W - 1)]
            for j in range(ROW_STRIDE):
                acc = bias
                for k in range(CONV_W):
                    acc = acc + shifted[j + k] * taps[k]
                xc_ref[s, pl.ds(base + j, SUBLANES, stride=ROW_STRIDE), :] = _silu(acc)
        conv_ref[0, :, cols] = raw_ref[s, SUBLANES + tm - (CONV_W - 1):SUBLANES + tm, :]
        raw_ref[s, 0:SUBLANES, :] = raw_ref[s, tm:tm + SUBLANES, :]

    def project(k, a):
        b = min(a + nc, widths[k])
        return _dot(u_ref[...], w_ref[:, starts[k] + a:starts[k] + b])

    n_xbc_units = widths[1] // nc
    slabs_per_unit = nc // LANES
    done = 0
    for n, (k, a) in enumerate([(k, a) for k in (1, 2) for a in range(0, widths[k], nc)]):
        res = project(k, a)
        for s in range(res.shape[1] // LANES):
            piece = res[:, s * LANES:(s + 1) * LANES]
            if k == 1:
                raw_ref[a // LANES + s, SUBLANES:SUBLANES + tm, :] = piece
            else:
                vraw_ref[a // LANES + s, POOL_PRE:POOL_PRE + tm, :] = piece
                pool_slab(a // LANES + s)
        while done < min(n_slabs, n * slabs_per_unit, min(n + 1, n_xbc_units) * slabs_per_unit):
            conv_slab(done)
            done += 1
    dt_raw = project(4, 0)
    while done < n_slabs:
        conv_slab(done)
        done += 1

    q = CHUNK
    n_state = D_STATE
    assert q == LANES and n_state == LANES
    late_units = [(k, a) for k in (0, 3) for a in range(0, widths[k], nc)]
    late_outs = {0: z_ref, 3: gates_ref}
    n_blocks = (tm // q) * N_GROUPS
    emitted = 0
    scalars = []
    for ci in range(tm // q):
        dt, a_cs, seg_tot, causal = _chunk_scalars(dt_raw[ci * q:(ci + 1) * q, :], dtb_ref[...],
                                                   alog_ref[...], q)
        scalars.append((a_cs, a_cs.T, dt.T, (jnp.exp(seg_tot - a_cs) * dt).T, jnp.exp(seg_tot).T,
                        causal))
    tiles_per_group = d_inner // N_GROUPS // LANES
    for ci in range(tm // q):
        rows = slice(ci * q, (ci + 1) * q)
        a_cs, a_cs_t, dt_t, w_state_t, chunk_decay_t, causal = scalars[ci]
        for g in range(N_GROUPS):
            b_g = xc_ref[d_inner // LANES + g, rows, :].astype(BF16)
            c_g = xc_ref[d_inner // LANES + N_GROUPS + g, rows, :].astype(BF16)
            cb = _dot_nt(c_g, b_g)
            for p in range(tiles_per_group):
                tile = g * tiles_per_group + p
                h0 = tile * HEADS_PER_TILE
                cols = slice(tile * LANES, (tile + 1) * LANES)
                xpair = xc_ref[tile, rows, :]
                h_prev = state_ref[cols, :]
                y, e_pair = _diag_pair(xpair, cb, a_cs, a_cs_t, dt_t, causal, h0)
                y = y + _dot_nt(c_g, h_prev.astype(BF16)) * e_pair
                y_ref[rows, cols] = y + dskip_ref[:, cols] * xpair
                xw_t = (xpair.T * _pair_rows(w_state_t, h0, (LANES, q))).astype(BF16)
                decay_rows = _pair_rows(chunk_decay_t, h0, (LANES, n_state))
                state_ref[cols, :] = decay_rows * h_prev + _dot(xw_t, b_g)
            block = ci * N_GROUPS + g + 1
            while emitted < len(late_units) and emitted * n_blocks < block * len(late_units):
                k, a = late_units[emitted]
                res = project(k, a)
                late_outs[k][:, a:a + res.shape[1]] = res
                emitted += 1
    assert emitted == len(late_units)

    @pl.when(tile_in_seq == tiles_per_seq - 1)
    def _():
        ssm_ref[0] = state_ref[...]


def _inproj_mixer(h, g, w_cat, layout, conv_w, conv_b, dt_bias, a_log, d_skip, *, tm, seq, d_inner):
    order = ("z", "xbc", "v", "gates", "dt")
    starts = tuple(layout[k][0] for k in order)
    widths = tuple(layout[k][1] for k in order)
    t, d = h.shape
    tm = min(tm, seq)
    tiles_per_seq = seq // tm
    n_all = w_cat.shape[1]
    conv_dim, pool_dim = widths[1], widths[2]
    n_slabs = conv_dim // LANES
    assert tm % (SUBLANES * ROW_STRIDE) == 0 and tm % CHUNK == 0
    row = lambda w: pl.BlockSpec((tm, w), lambda i: (i, 0))
    per_seq = lambda i: (i // tiles_per_seq, 0, 0)
    return pl.pallas_call(
        functools.partial(_inproj_mixer_kernel, widths=widths, starts=starts, nc=PROJ_UNIT,
                          tiles_per_seq=tiles_per_seq, d_inner=d_inner),
        grid=(t // tm,),
        in_specs=[row(d),
                  _const_spec((1, d)),
                  _const_spec((d, n_all)),
                  _const_spec((CONV_W, conv_dim)),
                  _const_spec((1, conv_dim)),
                  _const_spec((1, LANES)),
                  _const_spec((1, LANES)),
                  _const_spec((1, d_inner))],
        out_specs=[row(widths[0]), row(d_inner),
                   pl.BlockSpec((pool_dim // LANES, tm, LANES), lambda i: (0, i, 0)),
                   row(widths[3]),
                   pl.BlockSpec((1, CONV_W - 1, conv_dim), per_seq),
                   pl.BlockSpec((1, POOL_BUF, pool_dim), per_seq),
                   pl.BlockSpec((1, d_inner, D_STATE), per_seq)],
        out_shape=[jax.ShapeDtypeStruct((t, widths[0]), F32),
                   jax.ShapeDtypeStruct((t, d_inner), F32),
                   jax.ShapeDtypeStruct((pool_dim // LANES, t, LANES), F32),
                   jax.ShapeDtypeStruct((t, widths[3]), F32),
                   jax.ShapeDtypeStruct((t // seq, CONV_W - 1, conv_dim), F32),
                   jax.ShapeDtypeStruct((t // seq, POOL_BUF, pool_dim), F32),
                   jax.ShapeDtypeStruct((t // seq, d_inner, D_STATE), F32)],
        scratch_shapes=[pltpu.VMEM((tm, d), BF16),
                        pltpu.VMEM((n_slabs, SUBLANES + tm, LANES), F32),
                        pltpu.VMEM((pool_dim // LANES, POOL_PRE + tm, LANES), F32),
                        pltpu.VMEM((n_slabs, tm, LANES), F32),
                        pltpu.VMEM((d_inner, D_STATE), F32)],
        compiler_params=_params(1),
        name="inproj_mixer",
    )(h, g, w_cat, conv_w, conv_b, dt_bias, a_log, d_skip)


def _chunk_scalars(dt_raw, dt_bias, a_log, seg_len):
    q = dt_raw.shape[0]
    dt = _softplus(dt_raw + dt_bias)
    da = dt * (-jnp.exp(a_log))
    shift = int(math.log2(seg_len))
    qi = lax.broadcasted_iota(jnp.int32, (q, q), 0)
    si = lax.broadcasted_iota(jnp.int32, (q, q), 1)
    same = (qi >> shift) == (si >> shift)
    causal = jnp.logical_and(same, si <= qi)
    a_cs = jnp.dot(jnp.where(causal, 1.0, 0.0), da, precision=HIGHEST, preferred_element_type=F32)
    seg_tot = jnp.dot(jnp.where(same, 1.0, 0.0), da, precision=HIGHEST, preferred_element_type=F32)
    return dt, a_cs, seg_tot, causal


def _pair_cols(vals, h0, shape):
    lane = lax.broadcasted_iota(jnp.int32, shape, 1)
    return jnp.where(lane < HEAD_DIM, vals[:, h0:h0 + 1], vals[:, h0 + 1:h0 + 2])


def _pair_rows(vals_t, h0, shape):
    row = lax.broadcasted_iota(jnp.int32, shape, 0)
    return jnp.where(row < HEAD_DIM, vals_t[h0:h0 + 1, :], vals_t[h0 + 1:h0 + 2, :])


def _diag_pair(xpair, cb, a_cs, a_cs_t, dt_t, causal, h0):
    q = xpair.shape[0]
    lane = lax.broadcasted_iota(jnp.int32, xpair.shape, 1)
    out = None
    spread = []
    for k in range(HEADS_PER_TILE):
        h = h0 + k
        a_col = jnp.broadcast_to(a_cs[:, h:h + 1], (q, LANES))
        spread.append(a_col)
        decay = jnp.exp(jnp.where(causal, a_col - a_cs_t[h:h + 1, :], -jnp.inf))
        w = (cb * decay * dt_t[h:h + 1, :]).astype(BF16)
        mine = (lane < HEAD_DIM) if k == 0 else (lane >= HEAD_DIM)
        t = _dot(w, jnp.where(mine, xpair, 0.0).astype(BF16))
        out = t if out is None else out + t
    e_pair = jnp.exp(jnp.where(lane < HEAD_DIM, spread[0], spread[1]))
    return out, e_pair


def _shift_rows(cur, prev, s):
    sub = lax.broadcasted_iota(jnp.int32, (1,) + cur.shape[1:], 1)
    return jnp.where(sub < s, pltpu.roll(prev, s, axis=1), pltpu.roll(cur, s, axis=1))


def _conv_tiles(cur, prev, w, bias):
    tap = lambda k: w[k:k + 1, :][None]
    acc = cur * tap(CONV_W - 1)
    for s in range(1, CONV_W):
        acc = acc + _shift_rows(cur, prev, s) * tap(CONV_W - 1 - s)
    return _silu(bias[None] + acc)


def _conv_block(ext, w, bias):
    rows, width = ext.shape
    tiles = ext.reshape(rows // SUBLANES, SUBLANES, width)
    return _conv_tiles(tiles[1:], tiles[:-1], w, bias).reshape(rows - SUBLANES, width)


def _conv_tile_rows(rows_of_tap, w, bias):
    acc = rows_of_tap(0) * w[0:1, :]
    for k in range(1, CONV_W):
        acc = acc + rows_of_tap(k) * w[k:k + 1, :]
    return _silu(bias + acc)


def _mixer_sample_kernel(xbc_ref, dt_ref, v_ref, conv0_ref, pool0_ref, ssm0_ref,
                         cw_ref, cb_ref, dtb_ref, alog_ref, dskip_ref,
                         y_ref, pooled_ref, ssm_ref, conv_ref, pool_ref,
                         head_ref, xc_ref, xwt_ref, e_ref, seg_ref, pe_ref,
                         *, d_inner, seq, pos0, seqs_per_step):
    j = pl.program_id(1)
    q = xbc_ref.shape[0]
    conv_dim = xbc_ref.shape[1]
    pool_dim = v_ref.shape[1]
    n_state = D_STATE
    n_seq = q // seq
    tiles_per_group = d_inner // N_GROUPS // LANES
    gc = pool_dim // len(POOL_WINDOWS)

    @pl.when(j == 0)
    def _():
        lo = SUBLANES - (CONV_W - 1)

        def per_seq(b, carry):
            r0 = pl.multiple_of(b * seq, seq)
            for k in range(CONV_W - 1):
                head_ref[lo + k:lo + k + 1, :] = conv0_ref[k, pl.ds(b, 1), :]
            ct = 1024
            for jc in range(conv_dim // ct):
                cols = slice(jc * ct, (jc + 1) * ct)
                head_ref[SUBLANES:SUBLANES + seq, cols] = xbc_ref[pl.ds(r0, seq), cols]
                xc_ref[pl.ds(r0, seq), cols] = _conv_tile_rows(
                    lambda k: head_ref[lo + k:lo + k + seq, cols], cw_ref[:, cols], cb_ref[:, cols])
            for k in range(CONV_W - 1):
                r = SUBLANES + seq - (CONV_W - 1) + k
                conv_ref[k, pl.ds(b, 1), :] = head_ref[r:r + 1, :]

            for k in range(POOL_BUF):
                r = POOL_PAD - POOL_BUF + k
                pe_ref[r:r + 1, :] = pool0_ref[k, pl.ds(b, 1), :]
            vb = v_ref[pl.ds(r0, seq), :]
            pe_ref[POOL_PAD:POOL_PAD + seq, :] = vb
            t_idx = lax.broadcasted_iota(jnp.int32, (seq, 1), 0) + pos0
            for gi, win in enumerate(POOL_WINDOWS):
                cols = slice(gi * gc, (gi + 1) * gc)
                total = pe_ref[POOL_PAD:POOL_PAD + seq, cols]
                for k in range(1, win):
                    total = total + pe_ref[POOL_PAD - k:POOL_PAD - k + seq, cols]
                cnt = jnp.minimum(t_idx + 1, win).astype(F32)
                pooled = total / cnt - vb[:, cols]
                for s in range(gc // LANES):
                    pooled_ref[gi * (gc // LANES) + s, pl.ds(r0, seq), :] = (
                        pooled[:, s * LANES:(s + 1) * LANES])
            for k in range(POOL_BUF):
                r = POOL_PAD + seq - POOL_BUF + k
                pool_ref[k, pl.ds(b, 1), :] = pe_ref[r:r + 1, :]
            return carry

        lax.fori_loop(0, n_seq, per_seq, 0)

        dt, a_cs, seg_tot, causal = _chunk_scalars(dt_ref[...], dtb_ref[...], alog_ref[...], seq)
        a_cs_t = a_cs.T
        dt_t = dt.T
        w_state_t = (jnp.exp(seg_tot - a_cs) * dt).T
        seg_ref[...] = seg_tot
        for g in range(N_GROUPS):
            b_g = xc_ref[:, d_inner + g * n_state:d_inner + (g + 1) * n_state].astype(BF16)
            c_g = xc_ref[:, d_inner + (N_GROUPS + g) * n_state:
                         d_inner + (N_GROUPS + g + 1) * n_state].astype(BF16)
            cb = _dot_nt(c_g, b_g)
            for p in range(tiles_per_group):
                tile = g * tiles_per_group + p
                h0 = tile * HEADS_PER_TILE
                cols = slice(tile * LANES, (tile + 1) * LANES)
                xpair = xc_ref[:, cols]
                y, e_pair = _diag_pair(xpair, cb, a_cs, a_cs_t, dt_t, causal, h0)
                y_ref[:, cols] = y + dskip_ref[:, cols] * xpair
                e_ref[:, cols] = e_pair
                xwt_ref[cols, :] = (xpair.T * _pair_rows(w_state_t, h0, (LANES, q))).astype(BF16)

    row_q = lax.broadcasted_iota(jnp.int32, (q, n_state), 0)
    row128 = lax.broadcasted_iota(jnp.int32, (LANES, n_state), 0)
    for sj in range(seqs_per_step):
        r0 = pl.multiple_of((j * seqs_per_step + sj) * seq, seq)
        chunk_decay = jnp.exp(seg_ref[pl.ds(r0, 1), :])
        mine = jnp.logical_and(row_q >= r0, row_q < r0 + seq)
        for g in range(N_GROUPS):
            b_cols = slice(d_inner + g * n_state, d_inner + (g + 1) * n_state)
            c_cols = slice(d_inner + (N_GROUPS + g) * n_state,
                           d_inner + (N_GROUPS + g + 1) * n_state)
            b_mine = jnp.where(mine, xc_ref[:, b_cols], 0.0).astype(BF16)
            c_rows = xc_ref[pl.ds(r0, seq), c_cols].astype(BF16)
            for p in range(tiles_per_group):
                tile = g * tiles_per_group + p
                h0 = tile * HEADS_PER_TILE
                cols = slice(tile * LANES, (tile + 1) * LANES)
                h_prev = ssm0_ref[sj, cols, :]
                y_off = _dot_nt(c_rows, h_prev.astype(BF16)) * e_ref[pl.ds(r0, seq), cols]
                y_ref[pl.ds(r0, seq), cols] += y_off
                decay_rows = jnp.where(row128 < HEAD_DIM, chunk_decay[:, h0:h0 + 1],
                                       chunk_decay[:, h0 + 1:h0 + 2])
                ssm_ref[sj, cols, :] = decay_rows * h_prev + _dot(xwt_ref[cols, :], b_mine)


def _mixer_sample(proj, layout, conv0, pool0, ssm0, conv_w, conv_b, dt_bias, a_log, d_skip,
                  *, batch, seq, d_inner, pos0):
    q = CHUNK
    n_seq = q // seq
    n_chunks = batch // n_seq
    conv_dim = layout["xbc"][1]
    pool_dim = layout["v"][1]
    window = lambda k: pl.BlockSpec((q, layout[k][1]), lambda i, j: (i, layout[k][0] // layout[k][1]))
    tok = lambda i, j: (i, 0)
    per_chunk = lambda i, j: (0, i, 0)
    sps = math.gcd(n_seq, 4)
    steps = n_seq // sps
    per_seq = lambda i, j: (i * steps + j, 0, 0)
    return pl.pallas_call(
        functools.partial(_mixer_sample_kernel, d_inner=d_inner, seq=seq, pos0=pos0,
                          seqs_per_step=sps),
        grid=(n_chunks, steps),
        in_specs=[window("xbc"), window("dt"), window("v"),
                  pl.BlockSpec((CONV_W - 1, n_seq, conv_dim), per_chunk),
                  pl.BlockSpec((POOL_BUF, n_seq, pool_dim), per_chunk),
                  pl.BlockSpec((sps, d_inner, D_STATE), per_seq),
                  _const_spec((CONV_W, conv_dim)),
                  _const_spec((1, conv_dim)),
                  _const_spec((1, LANES)),
                  _const_spec((1, LANES)),
                  _const_spec((1, d_inner))],
        out_specs=[pl.BlockSpec((q, d_inner), tok),
                   pl.BlockSpec((pool_dim // LANES, q, LANES), lambda i, j: (0, i, 0)),
                   pl.BlockSpec((sps, d_inner, D_STATE), per_seq),
                   pl.BlockSpec((CONV_W - 1, n_seq, conv_dim), per_chunk),
                   pl.BlockSpec((POOL_BUF, n_seq, pool_dim), per_chunk)],
        out_shape=[jax.ShapeDtypeStruct((batch * seq, d_inner), F32),
                   jax.ShapeDtypeStruct((pool_dim // LANES, batch * seq, LANES), F32),
                   jax.ShapeDtypeStruct((batch, d_inner, D_STATE), F32),
                   jax.ShapeDtypeStruct((CONV_W - 1, batch, conv_dim), F32),
                   jax.ShapeDtypeStruct((POOL_BUF, batch, pool_dim), F32)],
        scratch_shapes=[pltpu.VMEM((2 * SUBLANES, conv_dim), F32),
                        pltpu.VMEM((q, conv_dim), F32),
                        pltpu.VMEM((d_inner, q), BF16),
                        pltpu.VMEM((q, d_inner), F32),
                        pltpu.VMEM((q, LANES), F32),
                        pltpu.VMEM((POOL_PAD + seq, pool_dim), F32)],
        compiler_params=_params(2),
        name="mixer_sample",
    )(proj, proj, proj, conv0, pool0, ssm0, conv_w, conv_b, dt_bias, a_log, d_skip)


def _merge_kernel(h_ref, y_ref, z_ref, pooled_ref, gates_ref, nssd_ref, pscale_ref, *rest,
                  cast_weights):
    if cast_weights:
        w32_refs, (o_ref, *w_refs), (yn_ref, pm_ref) = rest[:4], rest[4:9], rest[9:]

        @pl.when(pl.program_id(0) == 0)
        def _():
            for dst, src in zip(w_refs, w32_refs):
                dst[...] = src[...].astype(BF16)
    else:
        w_refs, o_ref, (yn_ref, pm_ref) = rest[:4], rest[4], rest[5:]
    wssd_ref, wpg_ref, wpo_ref, wo_ref = w_refs
    d_inner = y_ref.shape[1]
    d_model = h_ref.shape[1]
    gw = d_inner // N_GROUPS
    for g in range(N_GROUPS):
        cols = slice(g * gw, (g + 1) * gw)
        yz = y_ref[:, cols] * _silu(z_ref[:, cols])
        yn_ref[:, cols] = _rmsnorm(yz, nssd_ref[:, cols]).astype(BF16)
    a_branch = _dot(yn_ref[...], wssd_ref[...])
    slabs_per_group = pooled_ref.shape[0] // len(POOL_WINDOWS)
    gc = slabs_per_group * LANES
    for gi in range(len(POOL_WINDOWS)):
        cols = slice(gi * gc, (gi + 1) * gc)
        pooled = jnp.concatenate([pooled_ref[gi * slabs_per_group + s]
                                  for s in range(slabs_per_group)], axis=1)
        mixed = _dot(pooled.astype(BF16), wpg_ref[gi])
        pm_ref[:, cols] = (mixed * pscale_ref[:, cols]).astype(BF16)
    b_branch = _dot(pm_ref[...], wpo_ref[...])
    merged = (_sigmoid(gates_ref[:, 0:d_model]) * a_branch
              + _sigmoid(gates_ref[:, d_model:2 * d_model]) * b_branch)
    o_ref[...] = h_ref[...] + _dot(merged.astype(BF16), wo_ref[...])


def _merge(h, y, z_src, pooled, gates_src, norm_ssd, pool_scale, weights, *, tm, cast_weights):
    t, d = h.shape
    tm = min(tm, t)
    d_inner = y.shape[1]
    pool_slabs = pooled.shape[0]
    pool_dim = pool_slabs * LANES
    row = lambda w, blk=0: pl.BlockSpec((tm, w), lambda i: (i, blk))
    out_specs = [row(d)]
    out_shape = [jax.ShapeDtypeStruct((t, d), F32)]
    if cast_weights:
        out_specs += [pl.BlockSpec(w.shape, lambda i, nd=w.ndim: (0,) * nd) for w in weights]
        out_shape += [jax.ShapeDtypeStruct(w.shape, BF16) for w in weights]
    res = pl.pallas_call(
        functools.partial(_merge_kernel, cast_weights=cast_weights),
        grid=(t // tm,),
        in_specs=[row(d), row(d_inner), row(d_inner, z_src[1]),
                  pl.BlockSpec((pool_slabs, tm, LANES), lambda i: (0, i, 0)),
                  row(N_BRANCH * d, gates_src[1]),
                  _const_spec((1, d_inner)), _const_spec((1, pool_dim))]
        + [_const_spec(w.shape) for w in weights],
        out_specs=out_specs,
        out_shape=out_shape,
        scratch_shapes=[pltpu.VMEM((tm, d_inner), BF16), pltpu.VMEM((tm, pool_dim), BF16)],
        compiler_params=_params(1),
        name="merge_cast" if cast_weights else "merge",
    )(h, y, z_src[0], pooled, gates_src[0], norm_ssd, pool_scale, *weights)
    return res if cast_weights else res[0]


def _pad_lanes(a):
    return jnp.pad(a.reshape(1, -1), ((0, 0), (0, LANES - a.shape[-1])))


def kernel(x_prompt, x_sample, state_ssm, state_conv, state_pool, p_prompt, p_sample, norm_ffn1, w_ffn1_gu, w_ffn1_down, norm_mix, w_in, conv_w, conv_b, dt_bias, a_log, d_skip, norm_ssd, w_ssd_out, w_pool_group, pool_scale, w_pool_out, w_o, norm_ffn2, w_ffn2_gu, w_ffn2_down, norm_ple, w_ple_gate, w_ple, norm_final):
    depth = norm_ffn1.shape[0]
    assert depth == 1, "the final norm is fused into the layer's last stage: one layer only"
    batch, seq, d_model = x_prompt.shape
    dec_batch, dec_seq, _ = x_sample.shape
    n_heads = dt_bias.shape[1]
    d_inner = n_heads * HEAD_DIM
    conv_dim = conv_w.shape[2]
    pool_dim = pool_scale.shape[1]
    assert seq % CHUNK == 0 and CHUNK % dec_seq == 0 and dec_batch % (CHUNK // dec_seq) == 0

    row = lambda a: a[0].reshape(1, -1)
    layout, proj_cols = _proj_layout(d_inner, conv_dim, pool_dim, d_model)
    src = {"z": (0, d_inner), "xbc": (d_inner, conv_dim), "dt": (d_inner + conv_dim, n_heads),
           "v": (d_inner + conv_dim + n_heads, pool_dim),
           "gates": (d_inner + conv_dim + n_heads + pool_dim, N_BRANCH * d_model)}
    unit_rows, unit_valid = [], []
    for name in sorted(layout, key=lambda k: layout[k][0]):
        start, width = src[name]
        assert layout[name][0] == len(unit_rows) * PROJ_UNIT
        for off in range(0, width, PROJ_UNIT):
            unit_rows.append(start + off)
            unit_valid.append(min(PROJ_UNIT, width - off))
    assert len(unit_rows) * PROJ_UNIT == proj_cols
    dtb, alog = _pad_lanes(dt_bias[0]), _pad_lanes(a_log[0])
    dskip = jnp.repeat(d_skip[0], HEAD_DIM).reshape(1, d_inner)
    cb = row(conv_b)
    ple_consts = (row(norm_ple), w_ple_gate[0], w_ple[0], norm_final.reshape(1, -1))

    h1, wg1, wu1, wd1 = _ffn_stream(x_sample.reshape(dec_batch * dec_seq, d_model), row(norm_ffn1),
                                    w_ffn1_gu[0], w_ffn1_down[0])
    proj, w_cat = _inproj_stream(h1, row(norm_mix), w_in[0].T, tuple(unit_rows), tuple(unit_valid))
    y, pooled, s2, c2, q2 = _mixer_sample(
        proj, layout, jnp.transpose(state_conv[0], (1, 0, 2)), jnp.transpose(state_pool[0], (1, 0, 2)),
        state_ssm[0].reshape(dec_batch, d_inner, D_STATE),
        conv_w[0], cb, dtb, alog, dskip,
        batch=dec_batch, seq=dec_seq, d_inner=d_inner, pos0=PAST_LEN)
    block_of = lambda k: layout[k][0] // layout[k][1]
    h2, wssd, wpg, wpo, wo = _merge(
        h1, y, (proj, block_of("z")), pooled, (proj, block_of("gates")), row(norm_ssd),
        row(pool_scale), (w_ssd_out[0], w_pool_group[0], w_pool_out[0], w_o[0]),
        tm=128, cast_weights=True)
    y_sample, wg2, wu2, wd2, wpleg, wple = _ffn_stream(
        h2, row(norm_ffn2), w_ffn2_gu[0], w_ffn2_down[0],
        (p_sample[0].reshape(dec_batch * dec_seq, -1),) + ple_consts)

    h1 = _ffn(x_prompt.reshape(batch * seq, d_model), row(norm_ffn1), wg1, wu1, wd1, tm=512)
    z, y, pooled, gates, c1, q1, s1 = _inproj_mixer(h1, row(norm_mix), w_cat, layout, conv_w[0], cb,
                                                    dtb, alog, dskip, tm=256, seq=seq, d_inner=d_inner)
    h2 = _merge(h1, y, (z, 0), pooled, (gates, 0), row(norm_ssd), row(pool_scale),
                (wssd, wpg, wpo, wo), tm=256, cast_weights=False)
    y_prompt = _ffn(h2, row(norm_ffn2), wg2, wu2, wd2,
                    (p_prompt[0].reshape(batch * seq, -1), ple_consts[0], wpleg, wple, ple_consts[3]),
                    tm=512)

    return (y_prompt.reshape(batch, seq, d_model),
            y_sample.reshape(dec_batch, dec_seq, d_model),
            s1.reshape(1, batch, n_heads, HEAD_DIM, D_STATE), c1[None], q1[None],
            s2.reshape(1, dec_batch, n_heads, HEAD_DIM, D_STATE),
            jnp.transpose(c2, (1, 0, 2))[None], jnp.transpose(q2, (1, 0, 2))[None])
```

```python
import functools
import math
from typing import NamedTuple

import jax
import jax.numpy as jnp
from jax import lax
from jax.experimental import pallas as pl
from jax.experimental.pallas import tpu as pltpu

F32 = jnp.float32
BF16 = jnp.bfloat16
EPS = 1e-6
HIGHEST = lax.Precision.HIGHEST
LOG2E = 1.4426950408889634
NEG_LOG2E = -LOG2E

LANES = 128
SUBLANES = 8
VMEM_LIMIT_BYTES = 56 * 1024 * 1024

HEAD_DIM = 64
N_GROUPS = 8
D_STATE = 128
CONV_W = 4
CHUNK = 128
POOL_WINDOWS = (2, 4, 8, 16)
POOL_BUF = max(POOL_WINDOWS) - 1
N_BRANCH = 2
PAST_LEN = 16384
HEADS_PER_TILE = LANES // HEAD_DIM
ROW_STRIDE = 4
POOL_PAD = 24
POOL_PRE = 16


def _sigmoid(x):
    return 1.0 / (1.0 + jnp.exp2(x * NEG_LOG2E))


def _silu(x):
    half = 0.5 * x
    return half + half * jnp.tanh(half)


def _softplus(x):
    return jnp.maximum(x, 0.0) + jnp.log(1.0 + jnp.exp(-jnp.abs(x)))


def _rmsnorm(x, g):
    return x * lax.rsqrt(jnp.mean(x * x, axis=-1, keepdims=True) + EPS) * g


def _dot(a, b):
    return jnp.dot(a, b, preferred_element_type=F32)


def _dot_nt(a, b):
    return lax.dot_general(a, b, (((1,), (1,)), ((), ())), preferred_element_type=F32)


def _const_spec(shape):
    nd = len(shape)
    return pl.BlockSpec(shape, lambda *_: (0,) * nd, pipeline_mode=pl.Buffered(1))


def _params(n_axes):
    return pltpu.CompilerParams(dimension_semantics=("arbitrary",) * n_axes,
                                vmem_limit_bytes=VMEM_LIMIT_BYTES)


FFN_CHUNK = 256


def _ple_epilogue(h, p_ref, nple_ref, wg_ref, wp_ref, nfin_ref):
    gate = _sigmoid(_dot(_rmsnorm(h, nple_ref[...]).astype(BF16), wg_ref[...]))
    h = h + gate * _dot(p_ref[...].astype(BF16), wp_ref[...])
    return _rmsnorm(h, nfin_ref[...])


def _ffn_kernel(x_ref, g_ref, wg_ref, wu_ref, wd_ref, *rest, fc, with_ple):
    if with_ple:
        p_ref, nple_ref, wpg_ref, wp_ref, nfin_ref, o_ref, xn_ref, acc_ref = rest
    else:
        o_ref, xn_ref, acc_ref = rest
    xn_ref[...] = _rmsnorm(x_ref[...], g_ref[...]).astype(BF16)
    for c in range(wd_ref.shape[0] // fc):
        cols = slice(c * fc, (c + 1) * fc)
        xn = xn_ref[...]
        act = (_silu(_dot(xn, wg_ref[:, cols])) * _dot(xn, wu_ref[:, cols])).astype(BF16)
        contrib = _dot(act, wd_ref[cols, :])
        if c == 0:
            acc_ref[...] = contrib
        else:
            acc_ref[...] += contrib
    h = x_ref[...] + 0.5 * acc_ref[...]
    if with_ple:
        h = _ple_epilogue(h, p_ref, nple_ref, wpg_ref, wp_ref, nfin_ref)
    o_ref[...] = h


def _ffn(x, g, wg, wu, wd, ple=None, *, tm):
    t, d = x.shape
    tm = min(tm, t)
    row = lambda w: pl.BlockSpec((tm, w), lambda i: (i, 0))
    consts = [g, wg, wu, wd]
    in_specs = [row(d)] + [_const_spec(a.shape) for a in consts]
    args = [x] + consts
    if ple is not None:
        in_specs += [row(ple[0].shape[1])] + [_const_spec(a.shape) for a in ple[1:]]
        args += list(ple)
    return pl.pallas_call(
        functools.partial(_ffn_kernel, fc=FFN_CHUNK, with_ple=ple is not None),
        grid=(t // tm,),
        in_specs=in_specs,
        out_specs=row(d),
        out_shape=jax.ShapeDtypeStruct((t, d), F32),
        scratch_shapes=[pltpu.VMEM((tm, d), BF16), pltpu.VMEM((tm, d), F32)],
        compiler_params=_params(1),
        name="ffn_ple" if ple is not None else "ffn",
    )(*args)


def _ffn_stream_kernel(x_ref, g_ref, wg32_ref, wu32_ref, wd32_ref, *rest, with_ple):
    if with_ple:
        (p_ref, nple_ref, wpg32_ref, wp32_ref, nfin_ref,
         o_ref, wg_ref, wu_ref, wd_ref, wpg_ref, wp_ref, xn_ref, acc_ref) = rest
    else:
        o_ref, wg_ref, wu_ref, wd_ref, xn_ref, acc_ref = rest
    c = pl.program_id(0)

    @pl.when(c == 0)
    def _():
        xn_ref[...] = _rmsnorm(x_ref[...], g_ref[...]).astype(BF16)
        acc_ref[...] = jnp.zeros(acc_ref.shape, F32)
        if with_ple:
            wpg_ref[...] = wpg32_ref[...].astype(BF16)
            wp_ref[...] = wp32_ref[...].astype(BF16)

    wg_ref[...] = wg32_ref[...].astype(BF16)
    wu_ref[...] = wu32_ref[...].astype(BF16)
    wd_ref[...] = wd32_ref[...].astype(BF16)
    xn = xn_ref[...]
    act = (_silu(_dot(xn, wg_ref[...])) * _dot(xn, wu_ref[...])).astype(BF16)
    acc_ref[...] += _dot(act, wd_ref[...])

    @pl.when(c == pl.num_programs(0) - 1)
    def _():
        h = x_ref[...] + 0.5 * acc_ref[...]
        if with_ple:
            h = _ple_epilogue(h, p_ref, nple_ref, wpg_ref, wp_ref, nfin_ref)
        o_ref[...] = h


def _ffn_stream(x, g, w_gu, w_down, ple=None):
    t, d = x.shape
    d_ff = w_down.shape[0]
    fc = FFN_CHUNK
    n_chunks = d_ff // fc
    whole = lambda a: pl.BlockSpec(a.shape, lambda c: (0,) * a.ndim)
    in_specs = [whole(x), whole(g),
                pl.BlockSpec((d, fc), lambda c: (0, c)),
                pl.BlockSpec((d, fc), lambda c: (0, n_chunks + c)),
                pl.BlockSpec((fc, d), lambda c: (c, 0))]
    args = [x, g, w_gu, w_gu, w_down]
    out_specs = [whole(x),
                 pl.BlockSpec((d, fc), lambda c: (0, c)),
                 pl.BlockSpec((d, fc), lambda c: (0, c)),
                 pl.BlockSpec((fc, d), lambda c: (c, 0))]
    out_shape = [jax.ShapeDtypeStruct((t, d), F32),
                 jax.ShapeDtypeStruct((d, d_ff), BF16),
                 jax.ShapeDtypeStruct((d, d_ff), BF16),
                 jax.ShapeDtypeStruct((d_ff, d), BF16)]
    if ple is not None:
        in_specs += [whole(a) for a in ple]
        args += list(ple)
        out_specs += [whole(ple[2]), whole(ple[3])]
        out_shape += [jax.ShapeDtypeStruct(ple[2].shape, BF16), jax.ShapeDtypeStruct(ple[3].shape, BF16)]
    return pl.pallas_call(
        functools.partial(_ffn_stream_kernel, with_ple=ple is not None),
        grid=(n_chunks,),
        in_specs=in_specs,
        out_specs=out_specs,
        out_shape=out_shape,
        scratch_shapes=[pltpu.VMEM((t, d), BF16), pltpu.VMEM((t, d), F32)],
        compiler_params=_params(1),
        name="ffn_stream_ple" if ple is not None else "ffn_stream",
    )(*args)


PROJ_UNIT = 512


def _proj_layout(d_inner, conv_dim, pool_dim, d_model):
    xbc, z, gates, v = conv_dim, d_inner, N_BRANCH * d_model, pool_dim
    lay = {"xbc": (0, xbc), "z": (xbc, z), "gates": (xbc + z, gates), "v": (xbc + z + gates, v),
           "dt": (xbc + z + gates + v, LANES)}
    for start, width in lay.values():
        assert start % width == 0
    total = -(-(lay["dt"][0] + LANES) // PROJ_UNIT) * PROJ_UNIT
    return lay, total


def _inproj_stream_kernel(rows_ref, valid_ref, h_ref, g_ref, wt_ref, proj_ref, wcat_ref, u_ref):
    c = pl.program_id(0)

    @pl.when(c == 0)
    def _():
        u_ref[...] = _rmsnorm(h_ref[...], g_ref[...]).astype(BF16)

    wt = wt_ref[...]
    row = lax.broadcasted_iota(jnp.int32, wt.shape, 0)
    wcat_ref[...] = jnp.where(row < valid_ref[c], wt, 0.0).T.astype(BF16)
    proj_ref[...] = _dot(u_ref[...], wcat_ref[...])


def _inproj_stream(h, g, w_t, unit_rows, unit_valid):
    t, d = h.shape
    n_units = len(unit_rows)
    total = n_units * PROJ_UNIT
    assert all(r % SUBLANES == 0 and r + PROJ_UNIT <= w_t.shape[0] for r in unit_rows)
    whole = lambda a: pl.BlockSpec(a.shape, lambda c, rows, valid: (0,) * a.ndim)
    return pl.pallas_call(
        _inproj_stream_kernel,
        grid_spec=pltpu.PrefetchScalarGridSpec(
            num_scalar_prefetch=2,
            grid=(n_units,),
            in_specs=[whole(h), whole(g),
                      pl.BlockSpec((pl.Element(PROJ_UNIT), pl.Element(d)),
                                   lambda c, rows, valid: (rows[c] * SUBLANES, 0))],
            out_specs=[pl.BlockSpec((t, PROJ_UNIT), lambda c, rows, valid: (0, c)),
                       pl.BlockSpec((d, PROJ_UNIT), lambda c, rows, valid: (0, c))],
            scratch_shapes=[pltpu.VMEM((t, d), BF16)]),
        out_shape=[jax.ShapeDtypeStruct((t, total), F32),
                   jax.ShapeDtypeStruct((d, total), BF16)],
        compiler_params=_params(1),
        name="inproj_stream",
    )(jnp.asarray([r // SUBLANES for r in unit_rows], jnp.int32),
      jnp.asarray(unit_valid, jnp.int32), h, g, w_t)


def _inproj_mixer_kernel(h_ref, g_ref, w_ref, cw_ref, cb_ref, dtb_ref, alog_ref, dskip_ref, nssd_ref,
                         yn_ref, pooled_ref, gates_ref, conv_ref, pool_ref, ssm_ref,
                         u_ref, raw_ref, vraw_ref, xc_ref, state_ref, y_ref, z_ref,
                         *, widths, starts, nc, tiles_per_seq, d_inner):
    i = pl.program_id(0)
    tm = h_ref.shape[0]
    n_slabs = raw_ref.shape[0]

    n_vslabs = vraw_ref.shape[0]
    tile_in_seq = i % tiles_per_seq
    span = SUBLANES * ROW_STRIDE

    @pl.when(tile_in_seq == 0)
    def _():
        raw_ref[:, 0:SUBLANES, :] = jnp.zeros((n_slabs, SUBLANES, LANES), F32)
        vraw_ref[:, 0:POOL_PRE, :] = jnp.zeros((n_vslabs, POOL_PRE, LANES), F32)
        state_ref[...] = jnp.zeros(state_ref.shape, F32)

    u_ref[...] = _rmsnorm(h_ref[...], g_ref[...]).astype(BF16)

    def pool_slab(s):
        cols = slice(s * LANES, (s + 1) * LANES)
        win = POOL_WINDOWS[s * len(POOL_WINDOWS) // n_vslabs]
        for base in range(0, tm, span):
            rows = {d: vraw_ref[s, pl.ds(POOL_PRE + base + d, SUBLANES, stride=ROW_STRIDE), :]
                    for d in range(1 - win, ROW_STRIDE)}
            for j in range(ROW_STRIDE):
                total = rows[j]
                for k in range(1, win):
                    total = total + rows[j - k]
                if base >= POOL_BUF:
                    mean = total * (1.0 / win)
                else:
                    pos = (tile_in_seq * tm + base + j
                           + ROW_STRIDE * lax.broadcasted_iota(jnp.int32, (SUBLANES, LANES), 0))
                    mean = total / jnp.minimum(pos + 1, win).astype(F32)
                pooled_ref[s, pl.ds(base + j, SUBLANES, stride=ROW_STRIDE), :] = mean - rows[j]
        pool_ref[0, :, cols] = vraw_ref[s, POOL_PRE + tm - POOL_BUF:POOL_PRE + tm, :]
        vraw_ref[s, 0:POOL_PRE, :] = vraw_ref[s, tm:tm + POOL_PRE, :]

    def conv_slab(s):
        cols = slice(s * LANES, (s + 1) * LANES)
        taps = [jnp.broadcast_to(cw_ref[k:k + 1, cols], (SUBLANES, LANES)) for k in range(CONV_W)]
        bias = jnp.broadcast_to(cb_ref[:, cols], (SUBLANES, LANES))
        for base in range(0, tm, span):
            shifted = [raw_ref[s, pl.ds(SUBLANES + base - (CONV_W - 1) + m, SUBLANES,
                                        stride=ROW_STRIDE), :]
                       for m in range(ROW_STRIDE + CONV_W - 1)]
            for j in range(ROW_STRIDE):
                acc = bias
                for k in range(CONV_W):
                    acc = acc + shifted[j + k] * taps[k]
                xc_ref[s, pl.ds(base + j, SUBLANES, stride=ROW_STRIDE), :] = _silu(acc)
        conv_ref[0, :, cols] = raw_ref[s, SUBLANES + tm - (CONV_W - 1):SUBLANES + tm, :]
        raw_ref[s, 0:SUBLANES, :] = raw_ref[s, tm:tm + SUBLANES, :]

    def project(k, a):
        b = min(a + nc, widths[k])
        return _dot(u_ref[...], w_ref[:, starts[k] + a:starts[k] + b])

    n_xbc_units = widths[1] // nc
    slabs_per_unit = nc // LANES
    done = 0
    for n, (k, a) in enumerate([(k, a) for k in (1, 2) for a in range(0, widths[k], nc)]):
        res = project(k, a)
        for s in range(res.shape[1] // LANES):
            piece = res[:, s * LANES:(s + 1) * LANES]
            if k == 1:
                raw_ref[a // LANES + s, SUBLANES:SUBLANES + tm, :] = piece
            else:
                vraw_ref[a // LANES + s, POOL_PRE:POOL_PRE + tm, :] = piece
                pool_slab(a // LANES + s)
        while done < min(n_slabs, n * slabs_per_unit, min(n + 1, n_xbc_units) * slabs_per_unit):
            conv_slab(done)
            done += 1
    dt_raw = project(4, 0)
    while done < n_slabs:
        conv_slab(done)
        done += 1

    q = CHUNK
    n_state = D_STATE
    assert q == LANES and n_state == LANES
    late_units = [(k, a) for k in (0, 3) for a in range(0, widths[k], nc)]
    late_outs = {0: z_ref, 3: gates_ref}
    n_blocks = (tm // q) * N_GROUPS
    emitted = 0
    scalars = []
    for ci in range(tm // q):
        terms = _chunk_scalars(dt_raw[ci * q:(ci + 1) * q, :], dtb_ref[...], alog_ref[...], q)
        scalars.append((terms, jnp.exp2(terms.seg2).T))
    tiles_per_group = d_inner // N_GROUPS // LANES
    for ci in range(tm // q):
        rows = slice(ci * q, (ci + 1) * q)
        terms, chunk_decay_t = scalars[ci]
        for g in range(N_GROUPS):
            b_g = xc_ref[d_inner // LANES + g, rows, :].astype(BF16)
            c_g = xc_ref[d_inner // LANES + N_GROUPS + g, rows, :].astype(BF16)
            cb = _dot_nt(c_g, b_g)
            for p in range(tiles_per_group):
                tile = g * tiles_per_group + p
                h0 = tile * HEADS_PER_TILE
                cols = slice(tile * LANES, (tile + 1) * LANES)
                xpair = xc_ref[tile, rows, :]
                h_prev = state_ref[cols, :]
                y, e_pair = _diag_pair(xpair, cb, terms, h0)
                y = y + _dot_nt(c_g, h_prev.astype(BF16)) * e_pair
                y_ref[rows, cols] = y + dskip_ref[:, cols] * xpair
                xw_t = (xpair.T * _pair_rows(terms.w_state_t, h0, q)).astype(BF16)
                decay_rows = _pair_rows(chunk_decay_t, h0, n_state)
                state_ref[cols, :] = decay_rows * h_prev + _dot(xw_t, b_g)
            block = ci * N_GROUPS + g + 1
            while emitted < len(late_units) and emitted * n_blocks < block * len(late_units):
                k, a = late_units[emitted]
                res = project(k, a)
                late_outs[k][:, a:a + res.shape[1]] = res
                emitted += 1
    assert emitted == len(late_units)

    _gate_and_norm(y_ref, z_ref, nssd_ref, yn_ref)

    @pl.when(tile_in_seq == tiles_per_seq - 1)
    def _():
        ssm_ref[0] = state_ref[...]


def _inproj_mixer(h, g, w_cat, layout, conv_w, conv_b, dt_bias, a_log, d_skip, norm_ssd,
                  *, tm, seq, d_inner):
    order = ("z", "xbc", "v", "gates", "dt")
    starts = tuple(layout[k][0] for k in order)
    widths = tuple(layout[k][1] for k in order)
    t, d = h.shape
    tm = min(tm, seq)
    tiles_per_seq = seq // tm
    n_all = w_cat.shape[1]
    conv_dim, pool_dim = widths[1], widths[2]
    n_slabs = conv_dim // LANES
    assert tm % (SUBLANES * ROW_STRIDE) == 0 and tm % CHUNK == 0
    row = lambda w: pl.BlockSpec((tm, w), lambda i: (i, 0))
    per_seq = lambda i: (i // tiles_per_seq, 0, 0)
    return pl.pallas_call(
        functools.partial(_inproj_mixer_kernel, widths=widths, starts=starts, nc=PROJ_UNIT,
                          tiles_per_seq=tiles_per_seq, d_inner=d_inner),
        grid=(t // tm,),
        in_specs=[row(d),
                  _const_spec((1, d)),
                  _const_spec((d, n_all)),
                  _const_spec((CONV_W, conv_dim)),
                  _const_spec((1, conv_dim)),
                  _const_spec((1, LANES)),
                  _const_spec((1, LANES)),
                  _const_spec((1, d_inner)),
                  _const_spec((1, d_inner))],
        out_specs=[row(d_inner),
                   pl.BlockSpec((pool_dim // LANES, tm, LANES), lambda i: (0, i, 0)),
                   row(widths[3]),
                   pl.BlockSpec((1, CONV_W - 1, conv_dim), per_seq),
                   pl.BlockSpec((1, POOL_BUF, pool_dim), per_seq),
                   pl.BlockSpec((1, d_inner, D_STATE), per_seq)],
        out_shape=[jax.ShapeDtypeStruct((t, d_inner), BF16),
                   jax.ShapeDtypeStruct((pool_dim // LANES, t, LANES), F32),
                   jax.ShapeDtypeStruct((t, widths[3]), F32),
                   jax.ShapeDtypeStruct((t // seq, CONV_W - 1, conv_dim), F32),
                   jax.ShapeDtypeStruct((t // seq, POOL_BUF, pool_dim), F32),
                   jax.ShapeDtypeStruct((t // seq, d_inner, D_STATE), F32)],
        scratch_shapes=[pltpu.VMEM((tm, d), BF16),
                        pltpu.VMEM((n_slabs, SUBLANES + tm, LANES), F32),
                        pltpu.VMEM((pool_dim // LANES, POOL_PRE + tm, LANES), F32),
                        pltpu.VMEM((n_slabs, tm, LANES), F32),
                        pltpu.VMEM((d_inner, D_STATE), F32),
                        pltpu.VMEM((tm, d_inner), F32),
                        pltpu.VMEM((tm, widths[0]), F32)],
        compiler_params=_params(1),
        name="inproj_mixer",
    )(h, g, w_cat, conv_w, conv_b, dt_bias, a_log, d_skip, norm_ssd)


def _chunk_scalars(dt_raw, dt_bias, a_log, seg_len):
    q = dt_raw.shape[0]
    dt = _softplus(dt_raw + dt_bias)
    da = dt * (-jnp.exp(a_log))
    shift = int(math.log2(seg_len))
    qi = lax.broadcasted_iota(jnp.int32, (q, q), 0)
    si = lax.broadcasted_iota(jnp.int32, (q, q), 1)
    same = (qi >> shift) == (si >> shift)
    causal = jnp.logical_and(same, si <= qi)
    a_cs = jnp.dot(jnp.where(causal, 1.0, 0.0), da, precision=HIGHEST, preferred_element_type=F32)
    seg_tot = jnp.dot(jnp.where(same, 1.0, 0.0), da, precision=HIGHEST, preferred_element_type=F32)
    a2 = a_cs * LOG2E
    seg2 = seg_tot * LOG2E
    return _ChunkTerms(
        a2=a2,
        decay_in_t=(a2 - jnp.log2(dt)).T,
        w_state_t=(jnp.exp2(seg2 - a2) * dt).T,
        seg2=seg2,
        causal=causal)


class _ChunkTerms(NamedTuple):
    a2: jax.Array
    decay_in_t: jax.Array
    w_state_t: jax.Array
    seg2: jax.Array
    causal: jax.Array


def _pair_rows(vals_t, h0, cols):
    return jnp.concatenate([jnp.broadcast_to(vals_t[h0 + k:h0 + k + 1, :], (HEAD_DIM, cols))
                            for k in range(HEADS_PER_TILE)], axis=0)


def _diag_pair(xpair, cb, terms, h0):
    q = xpair.shape[0]
    lane = lax.broadcasted_iota(jnp.int32, xpair.shape, 1)
    out = None
    spread = []
    for k in range(HEADS_PER_TILE):
        h = h0 + k
        a_col = jnp.broadcast_to(terms.a2[:, h:h + 1], (q, LANES))
        spread.append(a_col)
        decay_dt = jnp.exp2(jnp.where(terms.causal, a_col - terms.decay_in_t[h:h + 1, :], -jnp.inf))
        w = (cb * decay_dt).astype(BF16)
        mine = (lane < HEAD_DIM) if k == 0 else (lane >= HEAD_DIM)
        t = _dot(w, jnp.where(mine, xpair, 0.0).astype(BF16))
        out = t if out is None else out + t
    e_pair = jnp.exp2(jnp.where(lane < HEAD_DIM, spread[0], spread[1]))
    return out, e_pair


def _shift_rows(cur, prev, s):
    sub = lax.broadcasted_iota(jnp.int32, (1,) + cur.shape[1:], 1)
    return jnp.where(sub < s, pltpu.roll(prev, s, axis=1), pltpu.roll(cur, s, axis=1))


def _conv_tiles(cur, prev, w, bias):
    tap = lambda k: w[k:k + 1, :][None]
    acc = cur * tap(CONV_W - 1)
    for s in range(1, CONV_W):
        acc = acc + _shift_rows(cur, prev, s) * tap(CONV_W - 1 - s)
    return _silu(bias[None] + acc)


def _conv_block(ext, w, bias):
    rows, width = ext.shape
    tiles = ext.reshape(rows // SUBLANES, SUBLANES, width)
    return _conv_tiles(tiles[1:], tiles[:-1], w, bias).reshape(rows - SUBLANES, width)


def _conv_tile_rows(rows_of_tap, w, bias):
    acc = rows_of_tap(0) * w[0:1, :]
    for k in range(1, CONV_W):
        acc = acc + rows_of_tap(k) * w[k:k + 1, :]
    return _silu(bias + acc)


def _mixer_sample_kernel(xbc_ref, dt_ref, v_ref, conv0_ref, pool0_ref, ssm0_ref,
                         cw_ref, cb_ref, dtb_ref, alog_ref, dskip_ref,
                         y_ref, pooled_ref, ssm_ref, conv_ref, pool_ref,
                         head_ref, xc_ref, xwt_ref, e_ref, seg_ref, pe_ref,
                         *, d_inner, seq, pos0, seqs_per_step):
    j = pl.program_id(1)
    q = xbc_ref.shape[0]
    conv_dim = xbc_ref.shape[1]
    pool_dim = v_ref.shape[1]
    n_state = D_STATE
    n_seq = q // seq
    tiles_per_group = d_inner // N_GROUPS // LANES
    gc = pool_dim // len(POOL_WINDOWS)

    @pl.when(j == 0)
    def _():
        lo = SUBLANES - (CONV_W - 1)

        def per_seq(b, carry):
            r0 = pl.multiple_of(b * seq, seq)
            for k in range(CONV_W - 1):
                head_ref[lo + k:lo + k + 1, :] = conv0_ref[k, pl.ds(b, 1), :]
            ct = 1024
            for jc in range(conv_dim // ct):
                cols = slice(jc * ct, (jc + 1) * ct)
                head_ref[SUBLANES:SUBLANES + seq, cols] = xbc_ref[pl.ds(r0, seq), cols]
                xc_ref[pl.ds(r0, seq), cols] = _conv_tile_rows(
                    lambda k: head_ref[lo + k:lo + k + seq, cols], cw_ref[:, cols], cb_ref[:, cols])
            for k in range(CONV_W - 1):
                r = SUBLANES + seq - (CONV_W - 1) + k
                conv_ref[k, pl.ds(b, 1), :] = head_ref[r:r + 1, :]

            for k in range(POOL_BUF):
                r = POOL_PAD - POOL_BUF + k
                pe_ref[r:r + 1, :] = pool0_ref[k, pl.ds(b, 1), :]
            vb = v_ref[pl.ds(r0, seq), :]
            pe_ref[POOL_PAD:POOL_PAD + seq, :] = vb
            t_idx = lax.broadcasted_iota(jnp.int32, (seq, 1), 0) + pos0
            for gi, win in enumerate(POOL_WINDOWS):
                cols = slice(gi * gc, (gi + 1) * gc)
                total = pe_ref[POOL_PAD:POOL_PAD + seq, cols]
                for k in range(1, win):
                    total = total + pe_ref[POOL_PAD - k:POOL_PAD - k + seq, cols]
                cnt = jnp.minimum(t_idx + 1, win).astype(F32)
                pooled = total / cnt - vb[:, cols]
                for s in range(gc // LANES):
                    pooled_ref[gi * (gc // LANES) + s, pl.ds(r0, seq), :] = (
                        pooled[:, s * LANES:(s + 1) * LANES])
            for k in range(POOL_BUF):
                r = POOL_PAD + seq - POOL_BUF + k
                pool_ref[k, pl.ds(b, 1), :] = pe_ref[r:r + 1, :]
            return carry

        lax.fori_loop(0, n_seq, per_seq, 0)

        terms = _chunk_scalars(dt_ref[...], dtb_ref[...], alog_ref[...], seq)
        seg_ref[...] = terms.seg2
        for g in range(N_GROUPS):
            b_g = xc_ref[:, d_inner + g * n_state:d_inner + (g + 1) * n_state].astype(BF16)
            c_g = xc_ref[:, d_inner + (N_GROUPS + g) * n_state:
                         d_inner + (N_GROUPS + g + 1) * n_state].astype(BF16)
            cb = _dot_nt(c_g, b_g)
            for p in range(tiles_per_group):
                tile = g * tiles_per_group + p
                h0 = tile * HEADS_PER_TILE
                cols = slice(tile * LANES, (tile + 1) * LANES)
                xpair = xc_ref[:, cols]
                y, e_pair = _diag_pair(xpair, cb, terms, h0)
                y_ref[:, cols] = y + dskip_ref[:, cols] * xpair
                e_ref[:, cols] = e_pair
                xwt_ref[cols, :] = (xpair.T * _pair_rows(terms.w_state_t, h0, q)).astype(BF16)

    row_q = lax.broadcasted_iota(jnp.int32, (q, n_state), 0)
    row128 = lax.broadcasted_iota(jnp.int32, (LANES, n_state), 0)
    for sj in range(seqs_per_step):
        r0 = pl.multiple_of((j * seqs_per_step + sj) * seq, seq)
        chunk_decay = jnp.exp2(seg_ref[pl.ds(r0, 1), :])
        mine = jnp.logical_and(row_q >= r0, row_q < r0 + seq)
        for g in range(N_GROUPS):
            b_cols = slice(d_inner + g * n_state, d_inner + (g + 1) * n_state)
            c_cols = slice(d_inner + (N_GROUPS + g) * n_state,
                           d_inner + (N_GROUPS + g + 1) * n_state)
            b_mine = jnp.where(mine, xc_ref[:, b_cols], 0.0).astype(BF16)
            c_rows = xc_ref[pl.ds(r0, seq), c_cols].astype(BF16)
            for p in range(tiles_per_group):
                tile = g * tiles_per_group + p
                h0 = tile * HEADS_PER_TILE
                cols = slice(tile * LANES, (tile + 1) * LANES)
                h_prev = ssm0_ref[sj, cols, :]
                y_off = _dot_nt(c_rows, h_prev.astype(BF16)) * e_ref[pl.ds(r0, seq), cols]
                y_ref[pl.ds(r0, seq), cols] += y_off
                decay_rows = jnp.where(row128 < HEAD_DIM, chunk_decay[:, h0:h0 + 1],
                                       chunk_decay[:, h0 + 1:h0 + 2])
                ssm_ref[sj, cols, :] = decay_rows * h_prev + _dot(xwt_ref[cols, :], b_mine)


def _mixer_sample(proj, layout, conv0, pool0, ssm0, conv_w, conv_b, dt_bias, a_log, d_skip,
                  *, batch, seq, d_inner, pos0):
    q = CHUNK
    n_seq = q // seq
    n_chunks = batch // n_seq
    conv_dim = layout["xbc"][1]
    pool_dim = layout["v"][1]
    window = lambda k: pl.BlockSpec((q, layout[k][1]), lambda i, j: (i, layout[k][0] // layout[k][1]))
    tok = lambda i, j: (i, 0)
    per_chunk = lambda i, j: (0, i, 0)
    sps = math.gcd(n_seq, 4)
    steps = n_seq // sps
    per_seq = lambda i, j: (i * steps + j, 0, 0)
    return pl.pallas_call(
        functools.partial(_mixer_sample_kernel, d_inner=d_inner, seq=seq, pos0=pos0,
                          seqs_per_step=sps),
        grid=(n_chunks, steps),
        in_specs=[window("xbc"), window("dt"), window("v"),
                  pl.BlockSpec((CONV_W - 1, n_seq, conv_dim), per_chunk),
                  pl.BlockSpec((POOL_BUF, n_seq, pool_dim), per_chunk),
                  pl.BlockSpec((sps, d_inner, D_STATE), per_seq),
                  _const_spec((CONV_W, conv_dim)),
                  _const_spec((1, conv_dim)),
                  _const_spec((1, LANES)),
                  _const_spec((1, LANES)),
                  _const_spec((1, d_inner))],
        out_specs=[pl.BlockSpec((q, d_inner), tok),
                   pl.BlockSpec((pool_dim // LANES, q, LANES), lambda i, j: (0, i, 0)),
                   pl.BlockSpec((sps, d_inner, D_STATE), per_seq),
                   pl.BlockSpec((CONV_W - 1, n_seq, conv_dim), per_chunk),
                   pl.BlockSpec((POOL_BUF, n_seq, pool_dim), per_chunk)],
        out_shape=[jax.ShapeDtypeStruct((batch * seq, d_inner), F32),
                   jax.ShapeDtypeStruct((pool_dim // LANES, batch * seq, LANES), F32),
                   jax.ShapeDtypeStruct((batch, d_inner, D_STATE), F32),
                   jax.ShapeDtypeStruct((CONV_W - 1, batch, conv_dim), F32),
                   jax.ShapeDtypeStruct((POOL_BUF, batch, pool_dim), F32)],
        scratch_shapes=[pltpu.VMEM((2 * SUBLANES, conv_dim), F32),
                        pltpu.VMEM((q, conv_dim), F32),
                        pltpu.VMEM((d_inner, q), BF16),
                        pltpu.VMEM((q, d_inner), F32),
                        pltpu.VMEM((q, LANES), F32),
                        pltpu.VMEM((POOL_PAD + seq, pool_dim), F32)],
        compiler_params=_params(2),
        name="mixer_sample",
    )(proj, proj, proj, conv0, pool0, ssm0, conv_w, conv_b, dt_bias, a_log, d_skip)


def _gate_and_norm(y_ref, z_ref, nssd_ref, yn_ref):
    gw = y_ref.shape[1] // N_GROUPS
    for g in range(N_GROUPS):
        cols = slice(g * gw, (g + 1) * gw)
        yz = y_ref[:, cols] * _silu(z_ref[:, cols])
        yn_ref[:, cols] = _rmsnorm(yz, nssd_ref[:, cols]).astype(BF16)


def _merge_kernel(h_ref, *rest, gated, cast_weights):
    if gated:
        yn_ref, pooled_ref, gates_ref, pscale_ref, *rest = rest
    else:
        y_ref, z_ref, pooled_ref, gates_ref, nssd_ref, pscale_ref, *rest = rest
    if cast_weights:
        w32_refs, (o_ref, *w_refs), scratch = rest[:4], rest[4:9], rest[9:]

        @pl.when(pl.program_id(0) == 0)
        def _():
            for dst, src in zip(w_refs, w32_refs):
                dst[...] = src[...].astype(BF16)
    else:
        w_refs, o_ref, scratch = rest[:4], rest[4], rest[5:]
    wssd_ref, wpg_ref, wpo_ref, wo_ref = w_refs
    d_model = h_ref.shape[1]
    if gated:
        (pm_ref,) = scratch
    else:
        yn_ref, pm_ref = scratch
        _gate_and_norm(y_ref, z_ref, nssd_ref, yn_ref)
    a_branch = _dot(yn_ref[...], wssd_ref[...])
    slabs_per_group = pooled_ref.shape[0] // len(POOL_WINDOWS)
    gc = slabs_per_group * LANES
    for gi in range(len(POOL_WINDOWS)):
        cols = slice(gi * gc, (gi + 1) * gc)
        pooled = jnp.concatenate([pooled_ref[gi * slabs_per_group + s]
                                  for s in range(slabs_per_group)], axis=1)
        mixed = _dot(pooled.astype(BF16), wpg_ref[gi])
        pm_ref[:, cols] = (mixed * pscale_ref[:, cols]).astype(BF16)
    b_branch = _dot(pm_ref[...], wpo_ref[...])
    merged = (_sigmoid(gates_ref[:, 0:d_model]) * a_branch
              + _sigmoid(gates_ref[:, d_model:2 * d_model]) * b_branch)
    o_ref[...] = h_ref[...] + _dot(merged.astype(BF16), wo_ref[...])


def _merge(h, ssd, pooled, gates_src, pool_scale, weights, *, tm, cast_weights):
    t, d = h.shape
    tm = min(tm, t)
    gated = not isinstance(ssd, tuple)
    pool_slabs = pooled.shape[0]
    pool_dim = pool_slabs * LANES
    row = lambda w, blk=0: pl.BlockSpec((tm, w), lambda i: (i, blk))
    pooled_spec = pl.BlockSpec((pool_slabs, tm, LANES), lambda i: (0, i, 0))
    gates_spec = row(N_BRANCH * d, gates_src[1])
    scratch = [pltpu.VMEM((tm, pool_dim), BF16)]
    if gated:
        d_inner = ssd.shape[1]
        in_specs = [row(d), row(d_inner), pooled_spec, gates_spec, _const_spec((1, pool_dim))]
        args = [h, ssd, pooled, gates_src[0], pool_scale]
    else:
        y, z_src, norm_ssd = ssd
        d_inner = y.shape[1]
        in_specs = [row(d), row(d_inner), row(d_inner, z_src[1]), pooled_spec, gates_spec,
                    _const_spec((1, d_inner)), _const_spec((1, pool_dim))]
        args = [h, y, z_src[0], pooled, gates_src[0], norm_ssd, pool_scale]
        scratch = [pltpu.VMEM((tm, d_inner), BF16)] + scratch
    out_specs = [row(d)]
    out_shape = [jax.ShapeDtypeStruct((t, d), F32)]
    if cast_weights:
        out_specs += [pl.BlockSpec(w.shape, lambda i, nd=w.ndim: (0,) * nd) for w in weights]
        out_shape += [jax.ShapeDtypeStruct(w.shape, BF16) for w in weights]
    res = pl.pallas_call(
        functools.partial(_merge_kernel, gated=gated, cast_weights=cast_weights),
        grid=(t // tm,),
        in_specs=in_specs + [_const_spec(w.shape) for w in weights],
        out_specs=out_specs,
        out_shape=out_shape,
        scratch_shapes=scratch,
        compiler_params=_params(1),
        name="merge_cast" if cast_weights else "merge",
    )(*args, *weights)
    return res if cast_weights else res[0]


def _pad_lanes(a):
    return jnp.pad(a.reshape(1, -1), ((0, 0), (0, LANES - a.shape[-1])))


def kernel(x_prompt, x_sample, state_ssm, state_conv, state_pool, p_prompt, p_sample, norm_ffn1, w_ffn1_gu, w_ffn1_down, norm_mix, w_in, conv_w, conv_b, dt_bias, a_log, d_skip, norm_ssd, w_ssd_out, w_pool_group, pool_scale, w_pool_out, w_o, norm_ffn2, w_ffn2_gu, w_ffn2_down, norm_ple, w_ple_gate, w_ple, norm_final):
    depth = norm_ffn1.shape[0]
    assert depth == 1, "the final norm is fused into the layer's last stage: one layer only"
    batch, seq, d_model = x_prompt.shape
    dec_batch, dec_seq, _ = x_sample.shape
    n_heads = dt_bias.shape[1]
    d_inner = n_heads * HEAD_DIM
    conv_dim = conv_w.shape[2]
    pool_dim = pool_scale.shape[1]
    assert seq % CHUNK == 0 and CHUNK % dec_seq == 0 and dec_batch % (CHUNK // dec_seq) == 0

    row = lambda a: a[0].reshape(1, -1)
    layout, proj_cols = _proj_layout(d_inner, conv_dim, pool_dim, d_model)
    src = {"z": (0, d_inner), "xbc": (d_inner, conv_dim), "dt": (d_inner + conv_dim, n_heads),
           "v": (d_inner + conv_dim + n_heads, pool_dim),
           "gates": (d_inner + conv_dim + n_heads + pool_dim, N_BRANCH * d_model)}
    unit_rows, unit_valid = [], []
    for name in sorted(layout, key=lambda k: layout[k][0]):
        start, width = src[name]
        assert layout[name][0] == len(unit_rows) * PROJ_UNIT
        for off in range(0, width, PROJ_UNIT):
            unit_rows.append(start + off)
            unit_valid.append(min(PROJ_UNIT, width - off))
    assert len(unit_rows) * PROJ_UNIT == proj_cols
    dtb, alog = _pad_lanes(dt_bias[0]), _pad_lanes(a_log[0])
    dskip = jnp.repeat(d_skip[0], HEAD_DIM).reshape(1, d_inner)
    cb = row(conv_b)
    ple_consts = (row(norm_ple), w_ple_gate[0], w_ple[0], norm_final.reshape(1, -1))

    h1, wg1, wu1, wd1 = _ffn_stream(x_sample.reshape(dec_batch * dec_seq, d_model), row(norm_ffn1),
                                    w_ffn1_gu[0], w_ffn1_down[0])
    proj, w_cat = _inproj_stream(h1, row(norm_mix), w_in[0].T, tuple(unit_rows), tuple(unit_valid))
    y, pooled, s2, c2, q2 = _mixer_sample(
        proj, layout, jnp.transpose(state_conv[0], (1, 0, 2)), jnp.transpose(state_pool[0], (1, 0, 2)),
        state_ssm[0].reshape(dec_batch, d_inner, D_STATE),
        conv_w[0], cb, dtb, alog, dskip,
        batch=dec_batch, seq=dec_seq, d_inner=d_inner, pos0=PAST_LEN)
    block_of = lambda k: layout[k][0] // layout[k][1]
    h2, wssd, wpg, wpo, wo = _merge(
        h1, (y, (proj, block_of("z")), row(norm_ssd)), pooled, (proj, block_of("gates")),
        row(pool_scale), (w_ssd_out[0], w_pool_group[0], w_pool_out[0], w_o[0]),
        tm=128, cast_weights=True)
    y_sample, wg2, wu2, wd2, wpleg, wple = _ffn_stream(
        h2, row(norm_ffn2), w_ffn2_gu[0], w_ffn2_down[0],
        (p_sample[0].reshape(dec_batch * dec_seq, -1),) + ple_consts)

    h1 = _ffn(x_prompt.reshape(batch * seq, d_model), row(norm_ffn1), wg1, wu1, wd1, tm=512)
    yn, pooled, gates, c1, q1, s1 = _inproj_mixer(h1, row(norm_mix), w_cat, layout, conv_w[0], cb,
                                                  dtb, alog, dskip, row(norm_ssd),
                                                  tm=256, seq=seq, d_inner=d_inner)
    h2 = _merge(h1, yn, pooled, (gates, 0), row(pool_scale), (wssd, wpg, wpo, wo),
                tm=512, cast_weights=False)
    y_prompt = _ffn(h2, row(norm_ffn2), wg2, wu2, wd2,
                    (p_prompt[0].reshape(batch * seq, -1), ple_consts[0], wpleg, wple, ple_consts[3]),
                    tm=512)

    return (y_prompt.reshape(batch, seq, d_model),
            y_sample.reshape(dec_batch, dec_seq, d_model),
            s1.reshape(1, batch, n_heads, HEAD_DIM, D_STATE), c1[None], q1[None],
            s2.reshape(1, dec_batch, n_heads, HEAD_DIM, D_STATE),
            jnp.transpose(c2, (1, 0, 2))[None], jnp.transpose(q2, (1, 0, 2))[None])
```

```python
import functools
import math
from typing import NamedTuple

import jax
import jax.numpy as jnp
from jax import lax
from jax.experimental import pallas as pl
from jax.experimental.pallas import tpu as pltpu

F32 = jnp.float32
BF16 = jnp.bfloat16
EPS = 1e-6
HIGHEST = lax.Precision.HIGHEST
LOG2E = 1.4426950408889634
NEG_LOG2E = -LOG2E

LANES = 128
SUBLANES = 8
VMEM_LIMIT_BYTES = 56 * 1024 * 1024

HEAD_DIM = 64
N_GROUPS = 8
D_STATE = 128
CONV_W = 4
CHUNK = 128
POOL_WINDOWS = (2, 4, 8, 16)
POOL_BUF = max(POOL_WINDOWS) - 1
N_BRANCH = 2
PAST_LEN = 16384
HEADS_PER_TILE = LANES // HEAD_DIM
ROW_STRIDE = 4
POOL_PAD = 24
POOL_PRE = 16


def _sigmoid(x):
    return 1.0 / (1.0 + jnp.exp2(x * NEG_LOG2E))


def _silu(x):
    half = 0.5 * x
    return half + half * jnp.tanh(half)


def _softplus(x):
    return jnp.maximum(x, 0.0) + jnp.log(1.0 + jnp.exp(-jnp.abs(x)))


def _rmsnorm(x, g):
    return x * lax.rsqrt(jnp.mean(x * x, axis=-1, keepdims=True) + EPS) * g


def _dot(a, b):
    return jnp.dot(a, b, preferred_element_type=F32)


def _dot_nt(a, b):
    return lax.dot_general(a, b, (((1,), (1,)), ((), ())), preferred_element_type=F32)


def _const_spec(shape):
    nd = len(shape)
    return pl.BlockSpec(shape, lambda *_: (0,) * nd, pipeline_mode=pl.Buffered(1))


def _params(n_axes):
    return pltpu.CompilerParams(dimension_semantics=("arbitrary",) * n_axes,
                                vmem_limit_bytes=VMEM_LIMIT_BYTES)


FFN_CHUNK = 256


def _ple_epilogue(h, p_ref, nple_ref, wg_ref, wp_ref, nfin_ref):
    gate = _sigmoid(_dot(_rmsnorm(h, nple_ref[...]).astype(BF16), wg_ref[...]))
    h = h + gate * _dot(p_ref[...].astype(BF16), wp_ref[...])
    return _rmsnorm(h, nfin_ref[...])


def _ffn_kernel(x_ref, g_ref, wg_ref, wu_ref, wd_ref, *rest, fc, with_ple):
    if with_ple:
        p_ref, nple_ref, wpg_ref, wp_ref, nfin_ref, o_ref, xn_ref, acc_ref = rest
    else:
        o_ref, xn_ref, acc_ref = rest
    xn_ref[...] = _rmsnorm(x_ref[...], g_ref[...]).astype(BF16)
    for c in range(wd_ref.shape[0] // fc):
        cols = slice(c * fc, (c + 1) * fc)
        xn = xn_ref[...]
        act = (_silu(_dot(xn, wg_ref[:, cols])) * _dot(xn, wu_ref[:, cols])).astype(BF16)
        contrib = _dot(act, wd_ref[cols, :])
        if c == 0:
            acc_ref[...] = contrib
        else:
            acc_ref[...] += contrib
    h = x_ref[...] + 0.5 * acc_ref[...]
    if with_ple:
        h = _ple_epilogue(h, p_ref, nple_ref, wpg_ref, wp_ref, nfin_ref)
    o_ref[...] = h


def _ffn(x, g, wg, wu, wd, ple=None, *, tm):
    t, d = x.shape
    tm = min(tm, t)
    row = lambda w: pl.BlockSpec((tm, w), lambda i: (i, 0))
    consts = [g, wg, wu, wd]
    in_specs = [row(d)] + [_const_spec(a.shape) for a in consts]
    args = [x] + consts
    if ple is not None:
        in_specs += [row(ple[0].shape[1])] + [_const_spec(a.shape) for a in ple[1:]]
        args += list(ple)
    return pl.pallas_call(
        functools.partial(_ffn_kernel, fc=FFN_CHUNK, with_ple=ple is not None),
        grid=(t // tm,),
        in_specs=in_specs,
        out_specs=row(d),
        out_shape=jax.ShapeDtypeStruct((t, d), F32),
        scratch_shapes=[pltpu.VMEM((tm, d), BF16), pltpu.VMEM((tm, d), F32)],
        compiler_params=_params(1),
        name="ffn_ple" if ple is not None else "ffn",
    )(*args)


def _ffn_stream_kernel(x_ref, g_ref, wg32_ref, wu32_ref, wd32_ref, *rest, with_ple):
    if with_ple:
        (p_ref, nple_ref, wpg32_ref, wp32_ref, nfin_ref,
         o_ref, wg_ref, wu_ref, wd_ref, wpg_ref, wp_ref, xn_ref, acc_ref) = rest
    else:
        o_ref, wg_ref, wu_ref, wd_ref, xn_ref, acc_ref = rest
    c = pl.program_id(0)

    @pl.when(c == 0)
    def _():
        xn_ref[...] = _rmsnorm(x_ref[...], g_ref[...]).astype(BF16)
        acc_ref[...] = jnp.zeros(acc_ref.shape, F32)
        if with_ple:
            wpg_ref[...] = wpg32_ref[...].astype(BF16)
            wp_ref[...] = wp32_ref[...].astype(BF16)

    wg_ref[...] = wg32_ref[...].astype(BF16)
    wu_ref[...] = wu32_ref[...].astype(BF16)
    wd_ref[...] = wd32_ref[...].astype(BF16)
    xn = xn_ref[...]
    act = (_silu(_dot(xn, wg_ref[...])) * _dot(xn, wu_ref[...])).astype(BF16)
    acc_ref[...] += _dot(act, wd_ref[...])

    @pl.when(c == pl.num_programs(0) - 1)
    def _():
        h = x_ref[...] + 0.5 * acc_ref[...]
        if with_ple:
            h = _ple_epilogue(h, p_ref, nple_ref, wpg_ref, wp_ref, nfin_ref)
        o_ref[...] = h


def _ffn_stream(x, g, w_gu, w_down, ple=None):
    t, d = x.shape
    d_ff = w_down.shape[0]
    fc = FFN_CHUNK
    n_chunks = d_ff // fc
    whole = lambda a: pl.BlockSpec(a.shape, lambda c: (0,) * a.ndim)
    in_specs = [whole(x), whole(g),
                pl.BlockSpec((d, fc), lambda c: (0, c)),
                pl.BlockSpec((d, fc), lambda c: (0, n_chunks + c)),
                pl.BlockSpec((fc, d), lambda c: (c, 0))]
    args = [x, g, w_gu, w_gu, w_down]
    out_specs = [whole(x),
                 pl.BlockSpec((d, fc), lambda c: (0, c)),
                 pl.BlockSpec((d, fc), lambda c: (0, c)),
                 pl.BlockSpec((fc, d), lambda c: (c, 0))]
    out_shape = [jax.ShapeDtypeStruct((t, d), F32),
                 jax.ShapeDtypeStruct((d, d_ff), BF16),
                 jax.ShapeDtypeStruct((d, d_ff), BF16),
                 jax.ShapeDtypeStruct((d_ff, d), BF16)]
    if ple is not None:
        in_specs += [whole(a) for a in ple]
        args += list(ple)
        out_specs += [whole(ple[2]), whole(ple[3])]
        out_shape += [jax.ShapeDtypeStruct(ple[2].shape, BF16), jax.ShapeDtypeStruct(ple[3].shape, BF16)]
    return pl.pallas_call(
        functools.partial(_ffn_stream_kernel, with_ple=ple is not None),
        grid=(n_chunks,),
        in_specs=in_specs,
        out_specs=out_specs,
        out_shape=out_shape,
        scratch_shapes=[pltpu.VMEM((t, d), BF16), pltpu.VMEM((t, d), F32)],
        compiler_params=_params(1),
        name="ffn_stream_ple" if ple is not None else "ffn_stream",
    )(*args)


PROJ_UNIT = 512


def _proj_layout(d_inner, conv_dim, pool_dim, d_model):
    xbc, z, gates, v = conv_dim, d_inner, N_BRANCH * d_model, pool_dim
    lay = {"xbc": (0, xbc), "z": (xbc, z), "gates": (xbc + z, gates), "v": (xbc + z + gates, v),
           "dt": (xbc + z + gates + v, LANES)}
    for start, width in lay.values():
        assert start % width == 0
    total = -(-(lay["dt"][0] + LANES) // PROJ_UNIT) * PROJ_UNIT
    return lay, total


def _inproj_stream_kernel(rows_ref, valid_ref, h_ref, g_ref, wt_ref, proj_ref, wcat_ref, u_ref):
    c = pl.program_id(0)

    @pl.when(c == 0)
    def _():
        u_ref[...] = _rmsnorm(h_ref[...], g_ref[...]).astype(BF16)

    wt = wt_ref[...]
    row = lax.broadcasted_iota(jnp.int32, wt.shape, 0)
    wcat_ref[...] = jnp.where(row < valid_ref[c], wt, 0.0).T.astype(BF16)
    proj_ref[...] = _dot(u_ref[...], wcat_ref[...])


def _inproj_stream(h, g, w_t, unit_rows, unit_valid):
    t, d = h.shape
    n_units = len(unit_rows)
    total = n_units * PROJ_UNIT
    assert all(r % SUBLANES == 0 and r + PROJ_UNIT <= w_t.shape[0] for r in unit_rows)
    whole = lambda a: pl.BlockSpec(a.shape, lambda c, rows, valid: (0,) * a.ndim)
    return pl.pallas_call(
        _inproj_stream_kernel,
        grid_spec=pltpu.PrefetchScalarGridSpec(
            num_scalar_prefetch=2,
            grid=(n_units,),
            in_specs=[whole(h), whole(g),
                      pl.BlockSpec((pl.Element(PROJ_UNIT), pl.Element(d)),
                                   lambda c, rows, valid: (rows[c] * SUBLANES, 0))],
            out_specs=[pl.BlockSpec((t, PROJ_UNIT), lambda c, rows, valid: (0, c)),
                       pl.BlockSpec((d, PROJ_UNIT), lambda c, rows, valid: (0, c))],
            scratch_shapes=[pltpu.VMEM((t, d), BF16)]),
        out_shape=[jax.ShapeDtypeStruct((t, total), F32),
                   jax.ShapeDtypeStruct((d, total), BF16)],
        compiler_params=_params(1),
        name="inproj_stream",
    )(jnp.asarray([r // SUBLANES for r in unit_rows], jnp.int32),
      jnp.asarray(unit_valid, jnp.int32), h, g, w_t)


def _inproj_mixer_kernel(h_ref, g_ref, w_ref, cw_ref, cb_ref, dtb_ref, alog_ref, dskip_ref, nssd_ref,
                         yn_ref, pooled_ref, gates_ref, conv_ref, pool_ref, ssm_ref,
                         u_ref, raw_ref, vraw_ref, xc_ref, state_ref, y_ref, z_ref,
                         *, widths, starts, nc, tiles_per_seq, d_inner):
    i = pl.program_id(0)
    tm = h_ref.shape[0]
    n_slabs = raw_ref.shape[0]

    n_vslabs = vraw_ref.shape[0]
    tile_in_seq = i % tiles_per_seq
    span = SUBLANES * ROW_STRIDE

    @pl.when(tile_in_seq == 0)
    def _():
        raw_ref[:, 0:SUBLANES, :] = jnp.zeros((n_slabs, SUBLANES, LANES), F32)
        vraw_ref[:, 0:POOL_PRE, :] = jnp.zeros((n_vslabs, POOL_PRE, LANES), F32)
        state_ref[...] = jnp.zeros(state_ref.shape, F32)

    u_ref[...] = _rmsnorm(h_ref[...], g_ref[...]).astype(BF16)

    def pool_slab(s):
        cols = slice(s * LANES, (s + 1) * LANES)
        win = POOL_WINDOWS[s * len(POOL_WINDOWS) // n_vslabs]
        for base in range(0, tm, span):
            rows = {d: vraw_ref[s, pl.ds(POOL_PRE + base + d, SUBLANES, stride=ROW_STRIDE), :]
                    for d in range(1 - win, ROW_STRIDE)}
            for j in range(ROW_STRIDE):
                total = rows[j]
                for k in range(1, win):
                    total = total + rows[j - k]
                if base >= POOL_BUF:
                    mean = total * (1.0 / win)
                else:
                    pos = (tile_in_seq * tm + base + j
                           + ROW_STRIDE * lax.broadcasted_iota(jnp.int32, (SUBLANES, LANES), 0))
                    mean = total / jnp.minimum(pos + 1, win).astype(F32)
                pooled_ref[s, pl.ds(base + j, SUBLANES, stride=ROW_STRIDE), :] = mean - rows[j]
        pool_ref[0, :, cols] = vraw_ref[s, POOL_PRE + tm - POOL_BUF:POOL_PRE + tm, :]
        vraw_ref[s, 0:POOL_PRE, :] = vraw_ref[s, tm:tm + POOL_PRE, :]

    def conv_slab(s):
        cols = slice(s * LANES, (s + 1) * LANES)
        taps = [jnp.broadcast_to(cw_ref[k:k + 1, cols], (SUBLANES, LANES)) for k in range(CONV_W)]
        bias = jnp.broadcast_to(cb_ref[:, cols], (SUBLANES, LANES))
        for base in range(0, tm, span):
            shifted = [raw_ref[s, pl.ds(SUBLANES + base - (CONV_W - 1) + m, SUBLANES,
                                        stride=ROW_STRIDE), :]
                       for m in range(ROW_STRIDE + CONV_W - 1)]
            for j in range(ROW_STRIDE):
                acc = bias
                for k in range(CONV_W):
                    acc = acc + shifted[j + k] * taps[k]
                xc_ref[s, pl.ds(base + j, SUBLANES, stride=ROW_STRIDE), :] = _silu(acc)
        conv_ref[0, :, cols] = raw_ref[s, SUBLANES + tm - (CONV_W - 1):SUBLANES + tm, :]
        raw_ref[s, 0:SUBLANES, :] = raw_ref[s, tm:tm + SUBLANES, :]

    def project(k, a, width=nc):
        b = min(a + width, widths[k])
        return _dot(u_ref[...], w_ref[:, starts[k] + a:starts[k] + b])

    n_xbc_units = widths[1] // nc
    slabs_per_unit = nc // LANES
    done = 0
    for n, (k, a) in enumerate([(k, a) for k in (1, 2) for a in range(0, widths[k], nc)]):
        res = project(k, a)
        for s in range(res.shape[1] // LANES):
            piece = res[:, s * LANES:(s + 1) * LANES]
            if k == 1:
                raw_ref[a // LANES + s, SUBLANES:SUBLANES + tm, :] = piece
            else:
                vraw_ref[a // LANES + s, POOL_PRE:POOL_PRE + tm, :] = piece
                pool_slab(a // LANES + s)
        while done < min(n_slabs, n * slabs_per_unit, min(n + 1, n_xbc_units) * slabs_per_unit):
            conv_slab(done)
            done += 1
    dt_raw = project(4, 0)
    while done < n_slabs:
        conv_slab(done)
        done += 1

    q = CHUNK
    n_state = D_STATE
    assert q == LANES and n_state == LANES
    late_nc = nc // 2
    late_units = [(k, a) for k in (0, 3) for a in range(0, widths[k], late_nc)]
    late_outs = {0: z_ref, 3: gates_ref}
    n_blocks = (tm // q) * N_GROUPS
    emitted = 0
    scalars = []
    for ci in range(tm // q):
        terms = _chunk_scalars(dt_raw[ci * q:(ci + 1) * q, :], dtb_ref[...], alog_ref[...], q)
        scalars.append((terms, jnp.exp2(terms.seg2).T))
    tiles_per_group = d_inner // N_GROUPS // LANES
    for ci in range(tm // q):
        rows = slice(ci * q, (ci + 1) * q)
        terms, chunk_decay_t = scalars[ci]
        for g in range(N_GROUPS):
            b_g = xc_ref[d_inner // LANES + g, rows, :].astype(BF16)
            c_g = xc_ref[d_inner // LANES + N_GROUPS + g, rows, :].astype(BF16)
            cb = _dot_nt(c_g, b_g)
            for p in range(tiles_per_group):
                tile = g * tiles_per_group + p
                h0 = tile * HEADS_PER_TILE
                cols = slice(tile * LANES, (tile + 1) * LANES)
                xpair = xc_ref[tile, rows, :]
                h_prev = state_ref[cols, :]
                y, e_pair = _diag_pair(xpair, cb, terms, h0)
                y = y + _dot_nt(c_g, h_prev.astype(BF16)) * e_pair
                y_ref[rows, cols] = y + dskip_ref[:, cols] * xpair
                xw_t = (xpair.T * _pair_rows(terms.w_state_t, h0, q)).astype(BF16)
                decay_rows = _pair_rows(chunk_decay_t, h0, n_state)
                state_ref[cols, :] = decay_rows * h_prev + _dot(xw_t, b_g)
            block = ci * N_GROUPS + g + 1
            while emitted < len(late_units) and emitted * n_blocks < block * len(late_units):
                k, a = late_units[emitted]
                res = project(k, a, late_nc)
                late_outs[k][:, a:a + res.shape[1]] = res
                emitted += 1
    assert emitted == len(late_units)

    _gate_and_norm(y_ref, z_ref, nssd_ref, yn_ref)

    @pl.when(tile_in_seq == tiles_per_seq - 1)
    def _():
        ssm_ref[0] = state_ref[...]


def _inproj_mixer(h, g, w_cat, layout, conv_w, conv_b, dt_bias, a_log, d_skip, norm_ssd,
                  *, tm, seq, d_inner):
    order = ("z", "xbc", "v", "gates", "dt")
    starts = tuple(layout[k][0] for k in order)
    widths = tuple(layout[k][1] for k in order)
    t, d = h.shape
    tm = min(tm, seq)
    tiles_per_seq = seq // tm
    n_all = w_cat.shape[1]
    conv_dim, pool_dim = widths[1], widths[2]
    n_slabs = conv_dim // LANES
    assert tm % (SUBLANES * ROW_STRIDE) == 0 and tm % CHUNK == 0
    row = lambda w: pl.BlockSpec((tm, w), lambda i: (i, 0))
    per_seq = lambda i: (i // tiles_per_seq, 0, 0)
    return pl.pallas_call(
        functools.partial(_inproj_mixer_kernel, widths=widths, starts=starts, nc=PROJ_UNIT,
                          tiles_per_seq=tiles_per_seq, d_inner=d_inner),
        grid=(t // tm,),
        in_specs=[row(d),
                  _const_spec((1, d)),
                  _const_spec((d, n_all)),
                  _const_spec((CONV_W, conv_dim)),
                  _const_spec((1, conv_dim)),
                  _const_spec((1, LANES)),
                  _const_spec((1, LANES)),
                  _const_spec((1, d_inner)),
                  _const_spec((1, d_inner))],
        out_specs=[row(d_inner),
                   pl.BlockSpec((pool_dim // LANES, tm, LANES), lambda i: (0, i, 0)),
                   row(widths[3]),
                   pl.BlockSpec((1, CONV_W - 1, conv_dim), per_seq),
                   pl.BlockSpec((1, POOL_BUF, pool_dim), per_seq),
                   pl.BlockSpec((1, d_inner, D_STATE), per_seq)],
        out_shape=[jax.ShapeDtypeStruct((t, d_inner), BF16),
                   jax.ShapeDtypeStruct((pool_dim // LANES, t, LANES), F32),
                   jax.ShapeDtypeStruct((t, widths[3]), F32),
                   jax.ShapeDtypeStruct((t // seq, CONV_W - 1, conv_dim), F32),
                   jax.ShapeDtypeStruct((t // seq, POOL_BUF, pool_dim), F32),
                   jax.ShapeDtypeStruct((t // seq, d_inner, D_STATE), F32)],
        scratch_shapes=[pltpu.VMEM((tm, d), BF16),
                        pltpu.VMEM((n_slabs, SUBLANES + tm, LANES), F32),
                        pltpu.VMEM((pool_dim // LANES, POOL_PRE + tm, LANES), F32),
                        pltpu.VMEM((n_slabs, tm, LANES), F32),
                        pltpu.VMEM((d_inner, D_STATE), F32),
                        pltpu.VMEM((tm, d_inner), F32),
                        pltpu.VMEM((tm, widths[0]), F32)],
        compiler_params=_params(1),
        name="inproj_mixer",
    )(h, g, w_cat, conv_w, conv_b, dt_bias, a_log, d_skip, norm_ssd)


def _chunk_scalars(dt_raw, dt_bias, a_log, seg_len):
    q = dt_raw.shape[0]
    dt = _softplus(dt_raw + dt_bias)
    da = dt * (-jnp.exp(a_log))
    shift = int(math.log2(seg_len))
    qi = lax.broadcasted_iota(jnp.int32, (q, q), 0)
    si = lax.broadcasted_iota(jnp.int32, (q, q), 1)
    same = (qi >> shift) == (si >> shift)
    causal = jnp.logical_and(same, si <= qi)
    a_cs = jnp.dot(jnp.where(causal, 1.0, 0.0), da, precision=HIGHEST, preferred_element_type=F32)
    seg_tot = jnp.dot(jnp.where(same, 1.0, 0.0), da, precision=HIGHEST, preferred_element_type=F32)
    a2 = a_cs * LOG2E
    seg2 = seg_tot * LOG2E
    return _ChunkTerms(
        a2=a2,
        decay_in_t=(a2 - jnp.log2(dt)).T,
        w_state_t=(jnp.exp2(seg2 - a2) * dt).T,
        seg2=seg2,
        causal=causal)


class _ChunkTerms(NamedTuple):
    a2: jax.Array
    decay_in_t: jax.Array
    w_state_t: jax.Array
    seg2: jax.Array
    causal: jax.Array


def _pair_rows(vals_t, h0, cols):
    return jnp.concatenate([jnp.broadcast_to(vals_t[h0 + k:h0 + k + 1, :], (HEAD_DIM, cols))
                            for k in range(HEADS_PER_TILE)], axis=0)


def _diag_pair(xpair, cb, terms, h0):
    q = xpair.shape[0]
    lane = lax.broadcasted_iota(jnp.int32, xpair.shape, 1)
    out = None
    spread = []
    for k in range(HEADS_PER_TILE):
        h = h0 + k
        a_col = jnp.broadcast_to(terms.a2[:, h:h + 1], (q, LANES))
        spread.append(a_col)
        decay_dt = jnp.exp2(jnp.where(terms.causal, a_col - terms.decay_in_t[h:h + 1, :], -jnp.inf))
        w = (cb * decay_dt).astype(BF16)
        mine = (lane < HEAD_DIM) if k == 0 else (lane >= HEAD_DIM)
        t = _dot(w, jnp.where(mine, xpair, 0.0).astype(BF16))
        out = t if out is None else out + t
    e_pair = jnp.exp2(jnp.where(lane < HEAD_DIM, spread[0], spread[1]))
    return out, e_pair


def _shift_rows(cur, prev, s):
    sub = lax.broadcasted_iota(jnp.int32, (1,) + cur.shape[1:], 1)
    return jnp.where(sub < s, pltpu.roll(prev, s, axis=1), pltpu.roll(cur, s, axis=1))


def _conv_tiles(cur, prev, w, bias):
    tap = lambda k: w[k:k + 1, :][None]
    acc = cur * tap(CONV_W - 1)
    for s in range(1, CONV_W):
        acc = acc + _shift_rows(cur, prev, s) * tap(CONV_W - 1 - s)
    return _silu(bias[None] + acc)


def _conv_block(ext, w, bias):
    rows, width = ext.shape
    tiles = ext.reshape(rows // SUBLANES, SUBLANES, width)
    return _conv_tiles(tiles[1:], tiles[:-1], w, bias).reshape(rows - SUBLANES, width)


def _conv_tile_rows(rows_of_tap, w, bias):
    acc = rows_of_tap(0) * w[0:1, :]
    for k in range(1, CONV_W):
        acc = acc + rows_of_tap(k) * w[k:k + 1, :]
    return _silu(bias + acc)


def _mixer_sample_kernel(xbc_ref, dt_ref, v_ref, conv0_ref, pool0_ref, ssm0_ref,
                         cw_ref, cb_ref, dtb_ref, alog_ref, dskip_ref,
                         y_ref, pooled_ref, ssm_ref, conv_ref, pool_ref,
                         head_ref, xc_ref, xwt_ref, e_ref, seg_ref, pe_ref,
                         *, d_inner, seq, pos0, seqs_per_step):
    j = pl.program_id(1)
    q = xbc_ref.shape[0]
    conv_dim = xbc_ref.shape[1]
    pool_dim = v_ref.shape[1]
    n_state = D_STATE
    n_seq = q // seq
    tiles_per_group = d_inner // N_GROUPS // LANES
    gc = pool_dim // len(POOL_WINDOWS)

    @pl.when(j == 0)
    def _():
        lo = SUBLANES - (CONV_W - 1)

        def per_seq(b, carry):
            r0 = pl.multiple_of(b * seq, seq)
            for k in range(CONV_W - 1):
                head_ref[lo + k:lo + k + 1, :] = conv0_ref[k, pl.ds(b, 1), :]
            ct = 1024
            for jc in range(conv_dim // ct):
                cols = slice(jc * ct, (jc + 1) * ct)
                head_ref[SUBLANES:SUBLANES + seq, cols] = xbc_ref[pl.ds(r0, seq), cols]
                xc_ref[pl.ds(r0, seq), cols] = _conv_tile_rows(
                    lambda k: head_ref[lo + k:lo + k + seq, cols], cw_ref[:, cols], cb_ref[:, cols])
            for k in range(CONV_W - 1):
                r = SUBLANES + seq - (CONV_W - 1) + k
                conv_ref[k, pl.ds(b, 1), :] = head_ref[r:r + 1, :]

            for k in range(POOL_BUF):
                r = POOL_PAD - POOL_BUF + k
                pe_ref[r:r + 1, :] = pool0_ref[k, pl.ds(b, 1), :]
            vb = v_ref[pl.ds(r0, seq), :]
            pe_ref[POOL_PAD:POOL_PAD + seq, :] = vb
            t_idx = lax.broadcasted_iota(jnp.int32, (seq, 1), 0) + pos0
            for gi, win in enumerate(POOL_WINDOWS):
                cols = slice(gi * gc, (gi + 1) * gc)
                total = pe_ref[POOL_PAD:POOL_PAD + seq, cols]
                for k in range(1, win):
                    total = total + pe_ref[POOL_PAD - k:POOL_PAD - k + seq, cols]
                cnt = jnp.minimum(t_idx + 1, win).astype(F32)
                pooled = total / cnt - vb[:, cols]
                for s in range(gc // LANES):
                    pooled_ref[gi * (gc // LANES) + s, pl.ds(r0, seq), :] = (
                        pooled[:, s * LANES:(s + 1) * LANES])
            for k in range(POOL_BUF):
                r = POOL_PAD + seq - POOL_BUF + k
                pool_ref[k, pl.ds(b, 1), :] = pe_ref[r:r + 1, :]
            return carry

        lax.fori_loop(0, n_seq, per_seq, 0)

        terms = _chunk_scalars(dt_ref[...], dtb_ref[...], alog_ref[...], seq)
        seg_ref[...] = terms.seg2
        for g in range(N_GROUPS):
            b_g = xc_ref[:, d_inner + g * n_state:d_inner + (g + 1) * n_state].astype(BF16)
            c_g = xc_ref[:, d_inner + (N_GROUPS + g) * n_state:
                         d_inner + (N_GROUPS + g + 1) * n_state].astype(BF16)
            cb = _dot_nt(c_g, b_g)
            for p in range(tiles_per_group):
                tile = g * tiles_per_group + p
                h0 = tile * HEADS_PER_TILE
                cols = slice(tile * LANES, (tile + 1) * LANES)
                xpair = xc_ref[:, cols]
                y, e_pair = _diag_pair(xpair, cb, terms, h0)
                y_ref[:, cols] = y + dskip_ref[:, cols] * xpair
                e_ref[:, cols] = e_pair
                xwt_ref[cols, :] = (xpair.T * _pair_rows(terms.w_state_t, h0, q)).astype(BF16)

    row_q = lax.broadcasted_iota(jnp.int32, (q, n_state), 0)
    row128 = lax.broadcasted_iota(jnp.int32, (LANES, n_state), 0)
    for sj in range(seqs_per_step):
        r0 = pl.multiple_of((j * seqs_per_step + sj) * seq, seq)
        chunk_decay = jnp.exp2(seg_ref[pl.ds(r0, 1), :])
        mine = jnp.logical_and(row_q >= r0, row_q < r0 + seq)
        for g in range(N_GROUPS):
            b_cols = slice(d_inner + g * n_state, d_inner + (g + 1) * n_state)
            c_cols = slice(d_inner + (N_GROUPS + g) * n_state,
                           d_inner + (N_GROUPS + g + 1) * n_state)
            b_mine = jnp.where(mine, xc_ref[:, b_cols], 0.0).astype(BF16)
            c_rows = xc_ref[pl.ds(r0, seq), c_cols].astype(BF16)
            for p in range(tiles_per_group):
                tile = g * tiles_per_group + p
                h0 = tile * HEADS_PER_TILE
                cols = slice(tile * LANES, (tile + 1) * LANES)
                h_prev = ssm0_ref[sj, cols, :]
                y_off = _dot_nt(c_rows, h_prev.astype(BF16)) * e_ref[pl.ds(r0, seq), cols]
                y_ref[pl.ds(r0, seq), cols] += y_off
                decay_rows = jnp.where(row128 < HEAD_DIM, chunk_decay[:, h0:h0 + 1],
                                       chunk_decay[:, h0 + 1:h0 + 2])
                ssm_ref[sj, cols, :] = decay_rows * h_prev + _dot(xwt_ref[cols, :], b_mine)


def _mixer_sample(proj, layout, conv0, pool0, ssm0, conv_w, conv_b, dt_bias, a_log, d_skip,
                  *, batch, seq, d_inner, pos0):
    q = CHUNK
    n_seq = q // seq
    n_chunks = batch // n_seq
    conv_dim = layout["xbc"][1]
    pool_dim = layout["v"][1]
    window = lambda k: pl.BlockSpec((q, layout[k][1]), lambda i, j: (i, layout[k][0] // layout[k][1]))
    tok = lambda i, j: (i, 0)
    per_chunk = lambda i, j: (0, i, 0)
    sps = math.gcd(n_seq, 4)
    steps = n_seq // sps
    per_seq = lambda i, j: (i * steps + j, 0, 0)
    return pl.pallas_call(
        functools.partial(_mixer_sample_kernel, d_inner=d_inner, seq=seq, pos0=pos0,
                          seqs_per_step=sps),
        grid=(n_chunks, steps),
        in_specs=[window("xbc"), window("dt"), window("v"),
                  pl.BlockSpec((CONV_W - 1, n_seq, conv_dim), per_chunk),
                  pl.BlockSpec((POOL_BUF, n_seq, pool_dim), per_chunk),
                  pl.BlockSpec((sps, d_inner, D_STATE), per_seq),
                  _const_spec((CONV_W, conv_dim)),
                  _const_spec((1, conv_dim)),
                  _const_spec((1, LANES)),
                  _const_spec((1, LANES)),
                  _const_spec((1, d_inner))],
        out_specs=[pl.BlockSpec((q, d_inner), tok),
                   pl.BlockSpec((pool_dim // LANES, q, LANES), lambda i, j: (0, i, 0)),
                   pl.BlockSpec((sps, d_inner, D_STATE), per_seq),
                   pl.BlockSpec((CONV_W - 1, n_seq, conv_dim), per_chunk),
                   pl.BlockSpec((POOL_BUF, n_seq, pool_dim), per_chunk)],
        out_shape=[jax.ShapeDtypeStruct((batch * seq, d_inner), F32),
                   jax.ShapeDtypeStruct((pool_dim // LANES, batch * seq, LANES), F32),
                   jax.ShapeDtypeStruct((batch, d_inner, D_STATE), F32),
                   jax.ShapeDtypeStruct((CONV_W - 1, batch, conv_dim), F32),
                   jax.ShapeDtypeStruct((POOL_BUF, batch, pool_dim), F32)],
        scratch_shapes=[pltpu.VMEM((2 * SUBLANES, conv_dim), F32),
                        pltpu.VMEM((q, conv_dim), F32),
                        pltpu.VMEM((d_inner, q), BF16),
                        pltpu.VMEM((q, d_inner), F32),
                        pltpu.VMEM((q, LANES), F32),
                        pltpu.VMEM((POOL_PAD + seq, pool_dim), F32)],
        compiler_params=_params(2),
        name="mixer_sample",
    )(proj, proj, proj, conv0, pool0, ssm0, conv_w, conv_b, dt_bias, a_log, d_skip)


def _gate_and_norm(y_ref, z_ref, nssd_ref, yn_ref):
    gw = y_ref.shape[1] // N_GROUPS
    for g in range(N_GROUPS):
        cols = slice(g * gw, (g + 1) * gw)
        yz = y_ref[:, cols] * _silu(z_ref[:, cols])
        yn_ref[:, cols] = _rmsnorm(yz, nssd_ref[:, cols]).astype(BF16)


def _merge_kernel(h_ref, *rest, gated, cast_weights):
    if gated:
        yn_ref, pooled_ref, gates_ref, pscale_ref, *rest = rest
    else:
        y_ref, z_ref, pooled_ref, gates_ref, nssd_ref, pscale_ref, *rest = rest
    if cast_weights:
        w32_refs, (o_ref, *w_refs), scratch = rest[:4], rest[4:9], rest[9:]

        @pl.when(pl.program_id(0) == 0)
        def _():
            for dst, src in zip(w_refs, w32_refs):
                dst[...] = src[...].astype(BF16)
    else:
        w_refs, o_ref, scratch = rest[:4], rest[4], rest[5:]
    wssd_ref, wpg_ref, wpo_ref, wo_ref = w_refs
    d_model = h_ref.shape[1]
    if gated:
        (pm_ref,) = scratch
    else:
        yn_ref, pm_ref = scratch
        _gate_and_norm(y_ref, z_ref, nssd_ref, yn_ref)
    a_branch = _dot(yn_ref[...], wssd_ref[...])
    slabs_per_group = pooled_ref.shape[0] // len(POOL_WINDOWS)
    gc = slabs_per_group * LANES
    for gi in range(len(POOL_WINDOWS)):
        cols = slice(gi * gc, (gi + 1) * gc)
        pooled = jnp.concatenate([pooled_ref[gi * slabs_per_group + s]
                                  for s in range(slabs_per_group)], axis=1)
        mixed = _dot(pooled.astype(BF16), wpg_ref[gi])
        pm_ref[:, cols] = (mixed * pscale_ref[:, cols]).astype(BF16)
    b_branch = _dot(pm_ref[...], wpo_ref[...])
    merged = (_sigmoid(gates_ref[:, 0:d_model]) * a_branch
              + _sigmoid(gates_ref[:, d_model:2 * d_model]) * b_branch)
    o_ref[...] = h_ref[...] + _dot(merged.astype(BF16), wo_ref[...])


def _merge(h, ssd, pooled, gates_src, pool_scale, weights, *, tm, cast_weights):
    t, d = h.shape
    tm = min(tm, t)
    gated = not isinstance(ssd, tuple)
    pool_slabs = pooled.shape[0]
    pool_dim = pool_slabs * LANES
    row = lambda w, blk=0: pl.BlockSpec((tm, w), lambda i: (i, blk))
    pooled_spec = pl.BlockSpec((pool_slabs, tm, LANES), lambda i: (0, i, 0))
    gates_spec = row(N_BRANCH * d, gates_src[1])
    scratch = [pltpu.VMEM((tm, pool_dim), BF16)]
    if gated:
        d_inner = ssd.shape[1]
        in_specs = [row(d), row(d_inner), pooled_spec, gates_spec, _const_spec((1, pool_dim))]
        args = [h, ssd, pooled, gates_src[0], pool_scale]
    else:
        y, z_src, norm_ssd = ssd
        d_inner = y.shape[1]
        in_specs = [row(d), row(d_inner), row(d_inner, z_src[1]), pooled_spec, gates_spec,
                    _const_spec((1, d_inner)), _const_spec((1, pool_dim))]
        args = [h, y, z_src[0], pooled, gates_src[0], norm_ssd, pool_scale]
        scratch = [pltpu.VMEM((tm, d_inner), BF16)] + scratch
    out_specs = [row(d)]
    out_shape = [jax.ShapeDtypeStruct((t, d), F32)]
    if cast_weights:
        out_specs += [pl.BlockSpec(w.shape, lambda i, nd=w.ndim: (0,) * nd) for w in weights]
        out_shape += [jax.ShapeDtypeStruct(w.shape, BF16) for w in weights]
    res = pl.pallas_call(
        functools.partial(_merge_kernel, gated=gated, cast_weights=cast_weights),
        grid=(t // tm,),
        in_specs=in_specs + [_const_spec(w.shape) for w in weights],
        out_specs=out_specs,
        out_shape=out_shape,
        scratch_shapes=scratch,
        compiler_params=_params(1),
        name="merge_cast" if cast_weights else "merge",
    )(*args, *weights)
    return res if cast_weights else res[0]


def _pad_lanes(a):
    return jnp.pad(a.reshape(1, -1), ((0, 0), (0, LANES - a.shape[-1])))


def kernel(x_prompt, x_sample, state_ssm, state_conv, state_pool, p_prompt, p_sample, norm_ffn1, w_ffn1_gu, w_ffn1_down, norm_mix, w_in, conv_w, conv_b, dt_bias, a_log, d_skip, norm_ssd, w_ssd_out, w_pool_group, pool_scale, w_pool_out, w_o, norm_ffn2, w_ffn2_gu, w_ffn2_down, norm_ple, w_ple_gate, w_ple, norm_final):
    depth = norm_ffn1.shape[0]
    assert depth == 1, "the final norm is fused into the layer's last stage: one layer only"
    batch, seq, d_model = x_prompt.shape
    dec_batch, dec_seq, _ = x_sample.shape
    n_heads = dt_bias.shape[1]
    d_inner = n_heads * HEAD_DIM
    conv_dim = conv_w.shape[2]
    pool_dim = pool_scale.shape[1]
    assert seq % CHUNK == 0 and CHUNK % dec_seq == 0 and dec_batch % (CHUNK // dec_seq) == 0

    row = lambda a: a[0].reshape(1, -1)
    layout, proj_cols = _proj_layout(d_inner, conv_dim, pool_dim, d_model)
    src = {"z": (0, d_inner), "xbc": (d_inner, conv_dim), "dt": (d_inner + conv_dim, n_heads),
           "v": (d_inner + conv_dim + n_heads, pool_dim),
           "gates": (d_inner + conv_dim + n_heads + pool_dim, N_BRANCH * d_model)}
    unit_rows, unit_valid = [], []
    for name in sorted(layout, key=lambda k: layout[k][0]):
        start, width = src[name]
        assert layout[name][0] == len(unit_rows) * PROJ_UNIT
        for off in range(0, width, PROJ_UNIT):
            unit_rows.append(start + off)
            unit_valid.append(min(PROJ_UNIT, width - off))
    assert len(unit_rows) * PROJ_UNIT == proj_cols
    dtb, alog = _pad_lanes(dt_bias[0]), _pad_lanes(a_log[0])
    dskip = jnp.repeat(d_skip[0], HEAD_DIM).reshape(1, d_inner)
    cb = row(conv_b)
    ple_consts = (row(norm_ple), w_ple_gate[0], w_ple[0], norm_final.reshape(1, -1))

    h1, wg1, wu1, wd1 = _ffn_stream(x_sample.reshape(dec_batch * dec_seq, d_model), row(norm_ffn1),
                                    w_ffn1_gu[0], w_ffn1_down[0])
    proj, w_cat = _inproj_stream(h1, row(norm_mix), w_in[0].T, tuple(unit_rows), tuple(unit_valid))
    y, pooled, s2, c2, q2 = _mixer_sample(
        proj, layout, jnp.transpose(state_conv[0], (1, 0, 2)), jnp.transpose(state_pool[0], (1, 0, 2)),
        state_ssm[0].reshape(dec_batch, d_inner, D_STATE),
        conv_w[0], cb, dtb, alog, dskip,
        batch=dec_batch, seq=dec_seq, d_inner=d_inner, pos0=PAST_LEN)
    block_of = lambda k: layout[k][0] // layout[k][1]
    h2, wssd, wpg, wpo, wo = _merge(
        h1, (y, (proj, block_of("z")), row(norm_ssd)), pooled, (proj, block_of("gates")),
        row(pool_scale), (w_ssd_out[0], w_pool_group[0], w_pool_out[0], w_o[0]),
        tm=128, cast_weights=True)
    y_sample, wg2, wu2, wd2, wpleg, wple = _ffn_stream(
        h2, row(norm_ffn2), w_ffn2_gu[0], w_ffn2_down[0],
        (p_sample[0].reshape(dec_batch * dec_seq, -1),) + ple_consts)

    h1 = _ffn(x_prompt.reshape(batch * seq, d_model), row(norm_ffn1), wg1, wu1, wd1, tm=512)
    yn, pooled, gates, c1, q1, s1 = _inproj_mixer(h1, row(norm_mix), w_cat, layout, conv_w[0], cb,
                                                  dtb, alog, dskip, row(norm_ssd),
                                                  tm=256, seq=seq, d_inner=d_inner)
    h2 = _merge(h1, yn, pooled, (gates, 0), row(pool_scale), (wssd, wpg, wpo, wo),
                tm=512, cast_weights=False)
    y_prompt = _ffn(h2, row(norm_ffn2), wg2, wu2, wd2,
                    (p_prompt[0].reshape(batch * seq, -1), ple_consts[0], wpleg, wple, ple_consts[3]),
                    tm=512)

    return (y_prompt.reshape(batch, seq, d_model),
            y_sample.reshape(dec_batch, dec_seq, d_model),
            s1.reshape(1, batch, n_heads, HEAD_DIM, D_STATE), c1[None], q1[None],
            s2.reshape(1, dec_batch, n_heads, HEAD_DIM, D_STATE),
            jnp.transpose(c2, (1, 0, 2))[None], jnp.transpose(q2, (1, 0, 2))[None])
```

```python
import functools
import math
from typing import NamedTuple

import jax
import jax.numpy as jnp
from jax import lax
from jax.experimental import pallas as pl
from jax.experimental.pallas import tpu as pltpu

F32 = jnp.float32
BF16 = jnp.bfloat16
EPS = 1e-6
HIGHEST = lax.Precision.HIGHEST
LOG2E = 1.4426950408889634
NEG_LOG2E = -LOG2E

LANES = 128
SUBLANES = 8
VMEM_LIMIT_BYTES = 56 * 1024 * 1024

HEAD_DIM = 64
N_GROUPS = 8
D_STATE = 128
CONV_W = 4
CHUNK = 128
POOL_WINDOWS = (2, 4, 8, 16)
POOL_BUF = max(POOL_WINDOWS) - 1
N_BRANCH = 2
PAST_LEN = 16384
HEADS_PER_TILE = LANES // HEAD_DIM
ROW_STRIDE = 4
POOL_PAD = 24
POOL_PRE = 16


def _sigmoid(x):
    return 1.0 / (1.0 + jnp.exp2(x * NEG_LOG2E))


def _silu(x):
    half = 0.5 * x
    return half + half * jnp.tanh(half)


def _softplus(x):
    return jnp.maximum(x, 0.0) + jnp.log(1.0 + jnp.exp(-jnp.abs(x)))


def _rmsnorm(x, g):
    return x * lax.rsqrt(jnp.mean(x * x, axis=-1, keepdims=True) + EPS) * g


def _dot(a, b):
    return jnp.dot(a, b, preferred_element_type=F32)


def _dot_nt(a, b):
    return lax.dot_general(a, b, (((1,), (1,)), ((), ())), preferred_element_type=F32)


def _const_spec(shape):
    nd = len(shape)
    return pl.BlockSpec(shape, lambda *_: (0,) * nd, pipeline_mode=pl.Buffered(1))


def _params(n_axes):
    return pltpu.CompilerParams(dimension_semantics=("arbitrary",) * n_axes,
                                vmem_limit_bytes=VMEM_LIMIT_BYTES)


FFN_CHUNK = 256


def _ple_epilogue(h, p_ref, nple_ref, wg_ref, wp_ref, nfin_ref):
    gate = _sigmoid(_dot(_rmsnorm(h, nple_ref[...]).astype(BF16), wg_ref[...]))
    h = h + gate * _dot(p_ref[...].astype(BF16), wp_ref[...])
    return _rmsnorm(h, nfin_ref[...])


def _ffn_kernel(x_ref, g_ref, wg_ref, wu_ref, wd_ref, *rest, fc, with_ple):
    if with_ple:
        p_ref, nple_ref, wpg_ref, wp_ref, nfin_ref, o_ref, xn_ref, acc_ref = rest
    else:
        o_ref, xn_ref, acc_ref = rest
    xn_ref[...] = _rmsnorm(x_ref[...], g_ref[...]).astype(BF16)
    for c in range(wd_ref.shape[0] // fc):
        cols = slice(c * fc, (c + 1) * fc)
        xn = xn_ref[...]
        act = (_silu(_dot(xn, wg_ref[:, cols])) * _dot(xn, wu_ref[:, cols])).astype(BF16)
        contrib = _dot(act, wd_ref[cols, :])
        if c == 0:
            acc_ref[...] = contrib
        else:
            acc_ref[...] += contrib
    h = x_ref[...] + 0.5 * acc_ref[...]
    if with_ple:
        h = _ple_epilogue(h, p_ref, nple_ref, wpg_ref, wp_ref, nfin_ref)
    o_ref[...] = h


def _ffn(x, g, wg, wu, wd, ple=None, *, tm):
    t, d = x.shape
    tm = min(tm, t)
    row = lambda w: pl.BlockSpec((tm, w), lambda i: (i, 0))
    consts = [g, wg, wu, wd]
    in_specs = [row(d)] + [_const_spec(a.shape) for a in consts]
    args = [x] + consts
    if ple is not None:
        in_specs += [row(ple[0].shape[1])] + [_const_spec(a.shape) for a in ple[1:]]
        args += list(ple)
    return pl.pallas_call(
        functools.partial(_ffn_kernel, fc=FFN_CHUNK, with_ple=ple is not None),
        grid=(t // tm,),
        in_specs=in_specs,
        out_specs=row(d),
        out_shape=jax.ShapeDtypeStruct((t, d), F32),
        scratch_shapes=[pltpu.VMEM((tm, d), BF16), pltpu.VMEM((tm, d), F32)],
        compiler_params=_params(1),
        name="ffn_ple" if ple is not None else "ffn",
    )(*args)


def _ffn_stream_kernel(x_ref, g_ref, wg32_ref, wu32_ref, wd32_ref, *rest, with_ple):
    if with_ple:
        (p_ref, nple_ref, wpg32_ref, wp32_ref, nfin_ref,
         o_ref, wg_ref, wu_ref, wd_ref, wpg_ref, wp_ref, xn_ref, acc_ref) = rest
    else:
        o_ref, wg_ref, wu_ref, wd_ref, xn_ref, acc_ref = rest
    c = pl.program_id(0)

    @pl.when(c == 0)
    def _():
        xn_ref[...] = _rmsnorm(x_ref[...], g_ref[...]).astype(BF16)
        acc_ref[...] = jnp.zeros(acc_ref.shape, F32)
        if with_ple:
            wpg_ref[...] = wpg32_ref[...].astype(BF16)
            wp_ref[...] = wp32_ref[...].astype(BF16)

    wg_ref[...] = wg32_ref[...].astype(BF16)
    wu_ref[...] = wu32_ref[...].astype(BF16)
    wd_ref[...] = wd32_ref[...].astype(BF16)
    xn = xn_ref[...]
    act = (_silu(_dot(xn, wg_ref[...])) * _dot(xn, wu_ref[...])).astype(BF16)
    acc_ref[...] += _dot(act, wd_ref[...])

    @pl.when(c == pl.num_programs(0) - 1)
    def _():
        h = x_ref[...] + 0.5 * acc_ref[...]
        if with_ple:
            h = _ple_epilogue(h, p_ref, nple_ref, wpg_ref, wp_ref, nfin_ref)
        o_ref[...] = h


def _ffn_stream(x, g, w_gu, w_down, ple=None):
    t, d = x.shape
    d_ff = w_down.shape[0]
    fc = FFN_CHUNK
    n_chunks = d_ff // fc
    whole = lambda a: pl.BlockSpec(a.shape, lambda c: (0,) * a.ndim)
    in_specs = [whole(x), whole(g),
                pl.BlockSpec((d, fc), lambda c: (0, c)),
                pl.BlockSpec((d, fc), lambda c: (0, n_chunks + c)),
                pl.BlockSpec((fc, d), lambda c: (c, 0))]
    args = [x, g, w_gu, w_gu, w_down]
    out_specs = [whole(x),
                 pl.BlockSpec((d, fc), lambda c: (0, c)),
                 pl.BlockSpec((d, fc), lambda c: (0, c)),
                 pl.BlockSpec((fc, d), lambda c: (c, 0))]
    out_shape = [jax.ShapeDtypeStruct((t, d), F32),
                 jax.ShapeDtypeStruct((d, d_ff), BF16),
                 jax.ShapeDtypeStruct((d, d_ff), BF16),
                 jax.ShapeDtypeStruct((d_ff, d), BF16)]
    if ple is not None:
        in_specs += [whole(a) for a in ple]
        args += list(ple)
        out_specs += [whole(ple[2]), whole(ple[3])]
        out_shape += [jax.ShapeDtypeStruct(ple[2].shape, BF16), jax.ShapeDtypeStruct(ple[3].shape, BF16)]
    return pl.pallas_call(
        functools.partial(_ffn_stream_kernel, with_ple=ple is not None),
        grid=(n_chunks,),
        in_specs=in_specs,
        out_specs=out_specs,
        out_shape=out_shape,
        scratch_shapes=[pltpu.VMEM((t, d), BF16), pltpu.VMEM((t, d), F32)],
        compiler_params=_params(1),
        name="ffn_stream_ple" if ple is not None else "ffn_stream",
    )(*args)


PROJ_UNIT = 512


def _proj_layout(d_inner, conv_dim, pool_dim, d_model):
    xbc, z, gates, v = conv_dim, d_inner, N_BRANCH * d_model, pool_dim
    lay = {"xbc": (0, xbc), "z": (xbc, z), "gates": (xbc + z, gates), "v": (xbc + z + gates, v),
           "dt": (xbc + z + gates + v, LANES)}
    for start, width in lay.values():
        assert start % width == 0
    total = -(-(lay["dt"][0] + LANES) // PROJ_UNIT) * PROJ_UNIT
    return lay, total


def _inproj_stream_kernel(rows_ref, valid_ref, h_ref, g_ref, wt_ref, proj_ref, wcat_ref, u_ref):
    c = pl.program_id(0)

    @pl.when(c == 0)
    def _():
        u_ref[...] = _rmsnorm(h_ref[...], g_ref[...]).astype(BF16)

    wt = wt_ref[...]
    row = lax.broadcasted_iota(jnp.int32, wt.shape, 0)
    wcat_ref[...] = jnp.where(row < valid_ref[c], wt, 0.0).T.astype(BF16)
    proj_ref[...] = _dot(u_ref[...], wcat_ref[...])


def _inproj_stream(h, g, w_t, unit_rows, unit_valid):
    t, d = h.shape
    n_units = len(unit_rows)
    total = n_units * PROJ_UNIT
    assert all(r % SUBLANES == 0 and r + PROJ_UNIT <= w_t.shape[0] for r in unit_rows)
    whole = lambda a: pl.BlockSpec(a.shape, lambda c, rows, valid: (0,) * a.ndim)
    return pl.pallas_call(
        _inproj_stream_kernel,
        grid_spec=pltpu.PrefetchScalarGridSpec(
            num_scalar_prefetch=2,
            grid=(n_units,),
            in_specs=[whole(h), whole(g),
                      pl.BlockSpec((pl.Element(PROJ_UNIT), pl.Element(d)),
                                   lambda c, rows, valid: (rows[c] * SUBLANES, 0))],
            out_specs=[pl.BlockSpec((t, PROJ_UNIT), lambda c, rows, valid: (0, c)),
                       pl.BlockSpec((d, PROJ_UNIT), lambda c, rows, valid: (0, c))],
            scratch_shapes=[pltpu.VMEM((t, d), BF16)]),
        out_shape=[jax.ShapeDtypeStruct((t, total), F32),
                   jax.ShapeDtypeStruct((d, total), BF16)],
        compiler_params=_params(1),
        name="inproj_stream",
    )(jnp.asarray([r // SUBLANES for r in unit_rows], jnp.int32),
      jnp.asarray(unit_valid, jnp.int32), h, g, w_t)


def _inproj_mixer_kernel(h_ref, g_ref, w_ref, cw_ref, cb_ref, dtb_ref, alog_ref, dskip_ref, nssd_ref,
                         yn_ref, pooled_ref, gates_ref, conv_ref, pool_ref, ssm_ref,
                         u_ref, raw_ref, vraw_ref, xc_ref, state_ref, y_ref, z_ref,
                         *, widths, starts, nc, tiles_per_seq, d_inner):
    i = pl.program_id(0)
    tm = h_ref.shape[0]
    n_slabs = raw_ref.shape[0]

    n_vslabs = vraw_ref.shape[0]
    tile_in_seq = i % tiles_per_seq
    span = SUBLANES * ROW_STRIDE

    @pl.when(tile_in_seq == 0)
    def _():
        raw_ref[:, 0:SUBLANES, :] = jnp.zeros((n_slabs, SUBLANES, LANES), F32)
        vraw_ref[:, 0:POOL_PRE, :] = jnp.zeros((n_vslabs, POOL_PRE, LANES), F32)
        state_ref[...] = jnp.zeros(state_ref.shape, F32)

    u_ref[...] = _rmsnorm(h_ref[...], g_ref[...]).astype(BF16)

    def pool_slab(s):
        cols = slice(s * LANES, (s + 1) * LANES)
        win = POOL_WINDOWS[s * len(POOL_WINDOWS) // n_vslabs]
        for base in range(0, tm, span):
            rows = {d: vraw_ref[s, pl.ds(POOL_PRE + base + d, SUBLANES, stride=ROW_STRIDE), :]
                    for d in range(1 - win, ROW_STRIDE)}
            for j in range(ROW_STRIDE):
                total = rows[j]
                for k in range(1, win):
                    total = total + rows[j - k]
                if base >= POOL_BUF:
                    mean = total * (1.0 / win)
                else:
                    pos = (tile_in_seq * tm + base + j
                           + ROW_STRIDE * lax.broadcasted_iota(jnp.int32, (SUBLANES, LANES), 0))
                    mean = total / jnp.minimum(pos + 1, win).astype(F32)
                pooled_ref[s, pl.ds(base + j, SUBLANES, stride=ROW_STRIDE), :] = mean - rows[j]
        pool_ref[0, :, cols] = vraw_ref[s, POOL_PRE + tm - POOL_BUF:POOL_PRE + tm, :]
        vraw_ref[s, 0:POOL_PRE, :] = vraw_ref[s, tm:tm + POOL_PRE, :]

    def conv_slab(s):
        cols = slice(s * LANES, (s + 1) * LANES)
        taps = [jnp.broadcast_to(cw_ref[k:k + 1, cols], (SUBLANES, LANES)) for k in range(CONV_W)]
        bias = jnp.broadcast_to(cb_ref[:, cols], (SUBLANES, LANES))
        for base in range(0, tm, span):
            shifted = [raw_ref[s, pl.ds(SUBLANES + base - (CONV_W - 1) + m, SUBLANES,
                                        stride=ROW_STRIDE), :]
                       for m in range(ROW_STRIDE + CONV_W - 1)]
            for j in range(ROW_STRIDE):
                acc = bias
                for k in range(CONV_W):
                    acc = acc + shifted[j + k] * taps[k]
                xc_ref[s, pl.ds(base + j, SUBLANES, stride=ROW_STRIDE), :] = _silu(acc)
        conv_ref[0, :, cols] = raw_ref[s, SUBLANES + tm - (CONV_W - 1):SUBLANES + tm, :]
        raw_ref[s, 0:SUBLANES, :] = raw_ref[s, tm:tm + SUBLANES, :]

    def project(k, a, width=nc):
        b = min(a + width, widths[k])
        return _dot(u_ref[...], w_ref[:, starts[k] + a:starts[k] + b])

    n_xbc_units = widths[1] // nc
    slabs_per_unit = nc // LANES
    done = 0
    for n, (k, a) in enumerate([(k, a) for k in (1, 2) for a in range(0, widths[k], nc)]):
        res = project(k, a)
        for s in range(res.shape[1] // LANES):
            piece = res[:, s * LANES:(s + 1) * LANES]
            if k == 1:
                raw_ref[a // LANES + s, SUBLANES:SUBLANES + tm, :] = piece
            else:
                vraw_ref[a // LANES + s, POOL_PRE:POOL_PRE + tm, :] = piece
                pool_slab(a // LANES + s)
        while done < min(n_slabs, n * slabs_per_unit, min(n + 1, n_xbc_units) * slabs_per_unit):
            conv_slab(done)
            done += 1
    dt_raw = project(4, 0)
    while done < n_slabs:
        conv_slab(done)
        done += 1

    q = CHUNK
    n_state = D_STATE
    assert q == LANES and n_state == LANES
    late_nc = nc // 2
    late_units = [(k, a) for k in (0, 3) for a in range(0, widths[k], late_nc)]
    late_outs = {0: z_ref, 3: gates_ref}
    n_blocks = (tm // q) * N_GROUPS
    emitted = 0
    scalars = []
    for ci in range(tm // q):
        terms = _chunk_scalars(dt_raw[ci * q:(ci + 1) * q, :], dtb_ref[...], alog_ref[...], q)
        scalars.append((terms, jnp.exp2(terms.seg2).T))
    tiles_per_group = d_inner // N_GROUPS // LANES
    for ci in range(tm // q):
        rows = slice(ci * q, (ci + 1) * q)
        terms, chunk_decay_t = scalars[ci]
        for g in range(N_GROUPS):
            b_g = xc_ref[d_inner // LANES + g, rows, :].astype(BF16)
            c_g = xc_ref[d_inner // LANES + N_GROUPS + g, rows, :].astype(BF16)
            cb = _dot_nt(c_g, b_g)
            for p in range(tiles_per_group):
                tile = g * tiles_per_group + p
                h0 = tile * HEADS_PER_TILE
                cols = slice(tile * LANES, (tile + 1) * LANES)
                xpair = xc_ref[tile, rows, :]
                h_prev = state_ref[cols, :]
                y, e_pair = _diag_pair(xpair, cb, terms, h0)
                y = y + _dot_nt(c_g, h_prev.astype(BF16)) * e_pair
                y_ref[rows, cols] = y + dskip_ref[:, cols] * xpair
                xw_t = (xpair.T * _pair_rows(terms.w_state_t, h0, q)).astype(BF16)
                decay_rows = _pair_rows(chunk_decay_t, h0, n_state)
                state_ref[cols, :] = decay_rows * h_prev + _dot(xw_t, b_g)
            block = ci * N_GROUPS + g + 1
            while emitted < len(late_units) and emitted * n_blocks < block * len(late_units):
                k, a = late_units[emitted]
                res = project(k, a, late_nc)
                late_outs[k][:, a:a + res.shape[1]] = res
                emitted += 1
    assert emitted == len(late_units)

    _gate_and_norm(y_ref, z_ref, nssd_ref, yn_ref)

    @pl.when(tile_in_seq == tiles_per_seq - 1)
    def _():
        ssm_ref[0] = state_ref[...]


def _inproj_mixer(h, g, w_cat, layout, conv_w, conv_b, dt_bias, a_log, d_skip, norm_ssd,
                  *, tm, seq, d_inner):
    order = ("z", "xbc", "v", "gates", "dt")
    starts = tuple(layout[k][0] for k in order)
    widths = tuple(layout[k][1] for k in order)
    t, d = h.shape
    tm = min(tm, seq)
    tiles_per_seq = seq // tm
    n_all = w_cat.shape[1]
    conv_dim, pool_dim = widths[1], widths[2]
    n_slabs = conv_dim // LANES
    assert tm % (SUBLANES * ROW_STRIDE) == 0 and tm % CHUNK == 0
    row = lambda w: pl.BlockSpec((tm, w), lambda i: (i, 0))
    per_seq = lambda i: (i // tiles_per_seq, 0, 0)
    return pl.pallas_call(
        functools.partial(_inproj_mixer_kernel, widths=widths, starts=starts, nc=PROJ_UNIT,
                          tiles_per_seq=tiles_per_seq, d_inner=d_inner),
        grid=(t // tm,),
        in_specs=[row(d),
                  _const_spec((1, d)),
                  _const_spec((d, n_all)),
                  _const_spec((CONV_W, conv_dim)),
                  _const_spec((1, conv_dim)),
                  _const_spec((1, LANES)),
                  _const_spec((1, LANES)),
                  _const_spec((1, d_inner)),
                  _const_spec((1, d_inner))],
        out_specs=[row(d_inner),
                   pl.BlockSpec((pool_dim // LANES, tm, LANES), lambda i: (0, i, 0)),
                   row(widths[3]),
                   pl.BlockSpec((1, CONV_W - 1, conv_dim), per_seq),
                   pl.BlockSpec((1, POOL_BUF, pool_dim), per_seq),
                   pl.BlockSpec((1, d_inner, D_STATE), per_seq)],
        out_shape=[jax.ShapeDtypeStruct((t, d_inner), BF16),
                   jax.ShapeDtypeStruct((pool_dim // LANES, t, LANES), F32),
                   jax.ShapeDtypeStruct((t, widths[3]), F32),
                   jax.ShapeDtypeStruct((t // seq, CONV_W - 1, conv_dim), F32),
                   jax.ShapeDtypeStruct((t // seq, POOL_BUF, pool_dim), F32),
                   jax.ShapeDtypeStruct((t // seq, d_inner, D_STATE), F32)],
        scratch_shapes=[pltpu.VMEM((tm, d), BF16),
                        pltpu.VMEM((n_slabs, SUBLANES + tm, LANES), F32),
                        pltpu.VMEM((pool_dim // LANES, POOL_PRE + tm, LANES), F32),
                        pltpu.VMEM((n_slabs, tm, LANES), F32),
                        pltpu.VMEM((d_inner, D_STATE), F32),
                        pltpu.VMEM((tm, d_inner), F32),
                        pltpu.VMEM((tm, widths[0]), F32)],
        compiler_params=_params(1),
        name="inproj_mixer",
    )(h, g, w_cat, conv_w, conv_b, dt_bias, a_log, d_skip, norm_ssd)


def _chunk_scalars(dt_raw, dt_bias, a_log, seg_len):
    q = dt_raw.shape[0]
    dt = _softplus(dt_raw + dt_bias)
    da = dt * (-jnp.exp(a_log))
    shift = int(math.log2(seg_len))
    qi = lax.broadcasted_iota(jnp.int32, (q, q), 0)
    si = lax.broadcasted_iota(jnp.int32, (q, q), 1)
    same = (qi >> shift) == (si >> shift)
    causal = jnp.logical_and(same, si <= qi)
    a_cs = jnp.dot(jnp.where(causal, 1.0, 0.0), da, precision=HIGHEST, preferred_element_type=F32)
    seg_tot = jnp.dot(jnp.where(same, 1.0, 0.0), da, precision=HIGHEST, preferred_element_type=F32)
    a2 = a_cs * LOG2E
    seg2 = seg_tot * LOG2E
    return _ChunkTerms(
        a2=a2,
        decay_in_t=(a2 - jnp.log2(dt)).T,
        w_state_t=(jnp.exp2(seg2 - a2) * dt).T,
        seg2=seg2,
        causal=causal)


class _ChunkTerms(NamedTuple):
    a2: jax.Array
    decay_in_t: jax.Array
    w_state_t: jax.Array
    seg2: jax.Array
    causal: jax.Array


def _pair_rows(vals_t, h0, cols):
    return jnp.concatenate([jnp.broadcast_to(vals_t[h0 + k:h0 + k + 1, :], (HEAD_DIM, cols))
                            for k in range(HEADS_PER_TILE)], axis=0)


def _diag_pair(xpair, cb, terms, h0):
    q = xpair.shape[0]
    lane = lax.broadcasted_iota(jnp.int32, xpair.shape, 1)
    out = None
    spread = []
    for k in range(HEADS_PER_TILE):
        h = h0 + k
        a_col = jnp.broadcast_to(terms.a2[:, h:h + 1], (q, LANES))
        spread.append(a_col)
        decay_dt = jnp.exp2(jnp.where(terms.causal, a_col - terms.decay_in_t[h:h + 1, :], -jnp.inf))
        w = (cb * decay_dt).astype(BF16)
        mine = (lane < HEAD_DIM) if k == 0 else (lane >= HEAD_DIM)
        t = _dot(w, jnp.where(mine, xpair, 0.0).astype(BF16))
        out = t if out is None else out + t
    e_pair = jnp.exp2(jnp.where(lane < HEAD_DIM, spread[0], spread[1]))
    return out, e_pair


def _shift_rows(cur, prev, s):
    sub = lax.broadcasted_iota(jnp.int32, (1,) + cur.shape[1:], 1)
    return jnp.where(sub < s, pltpu.roll(prev, s, axis=1), pltpu.roll(cur, s, axis=1))


def _conv_tiles(cur, prev, w, bias):
    tap = lambda k: w[k:k + 1, :][None]
    acc = cur * tap(CONV_W - 1)
    for s in range(1, CONV_W):
        acc = acc + _shift_rows(cur, prev, s) * tap(CONV_W - 1 - s)
    return _silu(bias[None] + acc)


def _conv_block(ext, w, bias):
    rows, width = ext.shape
    tiles = ext.reshape(rows // SUBLANES, SUBLANES, width)
    return _conv_tiles(tiles[1:], tiles[:-1], w, bias).reshape(rows - SUBLANES, width)


def _conv_tile_rows(rows_of_tap, w, bias):
    acc = rows_of_tap(0) * w[0:1, :]
    for k in range(1, CONV_W):
        acc = acc + rows_of_tap(k) * w[k:k + 1, :]
    return _silu(bias + acc)


def _mixer_sample_kernel(xbc_ref, dt_ref, v_ref, conv0_ref, pool0_ref, ssm0_ref,
                         cw_ref, cb_ref, dtb_ref, alog_ref, dskip_ref,
                         y_ref, pooled_ref, ssm_ref, conv_ref, pool_ref,
                         head_ref, xc_ref, xwt_ref, e_ref, seg_ref, pe_ref,
                         pooled_sc_ref, conv_sc_ref, pool_sc_ref,
                         *, d_inner, seq, pos0, seqs_per_step, n_chunks):
    c = pl.program_id(0)
    j = pl.program_id(1)
    q = xbc_ref.shape[0]
    conv_dim = xbc_ref.shape[1]
    pool_dim = v_ref.shape[1]
    n_state = D_STATE
    tiles_per_group = d_inner // N_GROUPS // LANES
    gc = pool_dim // len(POOL_WINDOWS)

    @pl.when(c < n_chunks)
    def _():
        slot = c % 2
        lo = SUBLANES - (CONV_W - 1)

        def per_seq(b, carry):
            r0 = pl.multiple_of(b * seq, seq)
            for k in range(CONV_W - 1):
                head_ref[lo + k:lo + k + 1, :] = conv0_ref[k, pl.ds(b, 1), :]
            ct = 1024
            for jc in range(conv_dim // ct):
                cols = slice(jc * ct, (jc + 1) * ct)
                head_ref[SUBLANES:SUBLANES + seq, cols] = xbc_ref[pl.ds(r0, seq), cols]
                xc_ref[slot, pl.ds(r0, seq), cols] = _conv_tile_rows(
                    lambda k: head_ref[lo + k:lo + k + seq, cols], cw_ref[:, cols], cb_ref[:, cols])
            for k in range(CONV_W - 1):
                r = SUBLANES + seq - (CONV_W - 1) + k
                conv_sc_ref[slot, k, pl.ds(b, 1), :] = head_ref[r:r + 1, :]

            for k in range(POOL_BUF):
                r = POOL_PAD - POOL_BUF + k
                pe_ref[r:r + 1, :] = pool0_ref[k, pl.ds(b, 1), :]
            vb = v_ref[pl.ds(r0, seq), :]
            pe_ref[POOL_PAD:POOL_PAD + seq, :] = vb
            t_idx = lax.broadcasted_iota(jnp.int32, (seq, 1), 0) + pos0
            for gi, win in enumerate(POOL_WINDOWS):
                cols = slice(gi * gc, (gi + 1) * gc)
                total = pe_ref[POOL_PAD:POOL_PAD + seq, cols]
                for k in range(1, win):
                    total = total + pe_ref[POOL_PAD - k:POOL_PAD - k + seq, cols]
                cnt = jnp.minimum(t_idx + 1, win).astype(F32)
                pooled = total / cnt - vb[:, cols]
                for s in range(gc // LANES):
                    pooled_sc_ref[slot, gi * (gc // LANES) + s, pl.ds(r0, seq), :] = (
                        pooled[:, s * LANES:(s + 1) * LANES])
            for k in range(POOL_BUF):
                r = POOL_PAD + seq - POOL_BUF + k
                pool_sc_ref[slot, k, pl.ds(b, 1), :] = pe_ref[r:r + 1, :]
            return carry

        lax.fori_loop(j * seqs_per_step, (j + 1) * seqs_per_step, per_seq, 0)

    @pl.when(c >= 1)
    def _():
        slot = (c + 1) % 2
        xc = xc_ref.at[slot]

        @pl.when(j == 0)
        def _():
            pooled_ref[...] = pooled_sc_ref[slot]
            conv_ref[...] = conv_sc_ref[slot]
            pool_ref[...] = pool_sc_ref[slot]
            terms = _chunk_scalars(dt_ref[...], dtb_ref[...], alog_ref[...], seq)
            seg_ref[...] = terms.seg2
            for g in range(N_GROUPS):
                b_g = xc[:, d_inner + g * n_state:d_inner + (g + 1) * n_state].astype(BF16)
                c_g = xc[:, d_inner + (N_GROUPS + g) * n_state:
                         d_inner + (N_GROUPS + g + 1) * n_state].astype(BF16)
                cb = _dot_nt(c_g, b_g)
                for p in range(tiles_per_group):
                    tile = g * tiles_per_group + p
                    h0 = tile * HEADS_PER_TILE
                    cols = slice(tile * LANES, (tile + 1) * LANES)
                    xpair = xc[:, cols]
                    y, e_pair = _diag_pair(xpair, cb, terms, h0)
                    y_ref[:, cols] = y + dskip_ref[:, cols] * xpair
                    e_ref[:, cols] = e_pair
                    xwt_ref[cols, :] = (xpair.T * _pair_rows(terms.w_state_t, h0, q)).astype(BF16)

        row_q = lax.broadcasted_iota(jnp.int32, (q, n_state), 0)
        row128 = lax.broadcasted_iota(jnp.int32, (LANES, n_state), 0)
        for sj in range(seqs_per_step):
            r0 = pl.multiple_of((j * seqs_per_step + sj) * seq, seq)
            chunk_decay = jnp.exp2(seg_ref[pl.ds(r0, 1), :])
            mine = jnp.logical_and(row_q >= r0, row_q < r0 + seq)
            for g in range(N_GROUPS):
                b_cols = slice(d_inner + g * n_state, d_inner + (g + 1) * n_state)
                c_cols = slice(d_inner + (N_GROUPS + g) * n_state,
                               d_inner + (N_GROUPS + g + 1) * n_state)
                b_mine = jnp.where(mine, xc[:, b_cols], 0.0).astype(BF16)
                c_rows = xc[pl.ds(r0, seq), c_cols].astype(BF16)
                for p in range(tiles_per_group):
                    tile = g * tiles_per_group + p
                    h0 = tile * HEADS_PER_TILE
                    cols = slice(tile * LANES, (tile + 1) * LANES)
                    h_prev = ssm0_ref[sj, cols, :]
                    y_off = _dot_nt(c_rows, h_prev.astype(BF16)) * e_ref[pl.ds(r0, seq), cols]
                    y_ref[pl.ds(r0, seq), cols] += y_off
                    decay_rows = jnp.where(row128 < HEAD_DIM, chunk_decay[:, h0:h0 + 1],
                                           chunk_decay[:, h0 + 1:h0 + 2])
                    ssm_ref[sj, cols, :] = decay_rows * h_prev + _dot(xwt_ref[cols, :], b_mine)


def _mixer_sample(proj, layout, conv0, pool0, ssm0, conv_w, conv_b, dt_bias, a_log, d_skip,
                  *, batch, seq, d_inner, pos0):
    q = CHUNK
    n_seq = q // seq
    n_chunks = batch // n_seq
    conv_dim = layout["xbc"][1]
    pool_dim = layout["v"][1]
    sps = math.gcd(n_seq, 4)
    steps = n_seq // sps
    prep = lambda c: jnp.minimum(c, n_chunks - 1)
    scan = lambda c: jnp.maximum(c - 1, 0)
    window = lambda k, chunk: pl.BlockSpec(
        (q, layout[k][1]), lambda c, j: (chunk(c), layout[k][0] // layout[k][1]))
    tok = lambda c, j: (scan(c), 0)
    per_chunk_in = lambda c, j: (0, prep(c), 0)
    per_chunk = lambda c, j: (0, scan(c), 0)
    per_seq = lambda c, j: (jnp.where(c == 0, 0, (c - 1) * steps + j), 0, 0)
    return pl.pallas_call(
        functools.partial(_mixer_sample_kernel, d_inner=d_inner, seq=seq, pos0=pos0,
                          seqs_per_step=sps, n_chunks=n_chunks),
        grid=(n_chunks + 1, steps),
        in_specs=[window("xbc", prep), window("dt", scan), window("v", prep),
                  pl.BlockSpec((CONV_W - 1, n_seq, conv_dim), per_chunk_in),
                  pl.BlockSpec((POOL_BUF, n_seq, pool_dim), per_chunk_in),
                  pl.BlockSpec((sps, d_inner, D_STATE), per_seq),
                  _const_spec((CONV_W, conv_dim)),
                  _const_spec((1, conv_dim)),
                  _const_spec((1, LANES)),
                  _const_spec((1, LANES)),
                  _const_spec((1, d_inner))],
        out_specs=[pl.BlockSpec((q, d_inner), tok),
                   pl.BlockSpec((pool_dim // LANES, q, LANES), per_chunk),
                   pl.BlockSpec((sps, d_inner, D_STATE), per_seq),
                   pl.BlockSpec((CONV_W - 1, n_seq, conv_dim), per_chunk),
                   pl.BlockSpec((POOL_BUF, n_seq, pool_dim), per_chunk)],
        out_shape=[jax.ShapeDtypeStruct((batch * seq, d_inner), F32),
                   jax.ShapeDtypeStruct((pool_dim // LANES, batch * seq, LANES), F32),
                   jax.ShapeDtypeStruct((batch, d_inner, D_STATE), F32),
                   jax.ShapeDtypeStruct((CONV_W - 1, batch, conv_dim), F32),
                   jax.ShapeDtypeStruct((POOL_BUF, batch, pool_dim), F32)],
        scratch_shapes=[pltpu.VMEM((2 * SUBLANES, conv_dim), F32),
                        pltpu.VMEM((2, q, conv_dim), F32),
                        pltpu.VMEM((d_inner, q), BF16),
                        pltpu.VMEM((q, d_inner), F32),
                        pltpu.VMEM((q, LANES), F32),
                        pltpu.VMEM((POOL_PAD + seq, pool_dim), F32),
                        pltpu.VMEM((2, pool_dim // LANES, q, LANES), F32),
                        pltpu.VMEM((2, CONV_W - 1, n_seq, conv_dim), F32),
                        pltpu.VMEM((2, POOL_BUF, n_seq, pool_dim), F32)],
        compiler_params=_params(2),
        name="mixer_sample",
    )(proj, proj, proj, conv0, pool0, ssm0, conv_w, conv_b, dt_bias, a_log, d_skip)


def _gate_and_norm(y_ref, z_ref, nssd_ref, yn_ref):
    gw = y_ref.shape[1] // N_GROUPS
    for g in range(N_GROUPS):
        cols = slice(g * gw, (g + 1) * gw)
        yz = y_ref[:, cols] * _silu(z_ref[:, cols])
        yn_ref[:, cols] = _rmsnorm(yz, nssd_ref[:, cols]).astype(BF16)


def _merge_kernel(h_ref, *rest, gated, cast_weights):
    if gated:
        yn_ref, pooled_ref, gates_ref, pscale_ref, *rest = rest
    else:
        y_ref, z_ref, pooled_ref, gates_ref, nssd_ref, pscale_ref, *rest = rest
    if cast_weights:
        w32_refs, (o_ref, *w_refs), scratch = rest[:4], rest[4:9], rest[9:]

        @pl.when(pl.program_id(0) == 0)
        def _():
            for dst, src in zip(w_refs, w32_refs):
                dst[...] = src[...].astype(BF16)
    else:
        w_refs, o_ref, scratch = rest[:4], rest[4], rest[5:]
    wssd_ref, wpg_ref, wpo_ref, wo_ref = w_refs
    d_model = h_ref.shape[1]
    if gated:
        (pm_ref,) = scratch
    else:
        yn_ref, pm_ref = scratch
        _gate_and_norm(y_ref, z_ref, nssd_ref, yn_ref)
    a_branch = _dot(yn_ref[...], wssd_ref[...])
    slabs_per_group = pooled_ref.shape[0] // len(POOL_WINDOWS)
    gc = slabs_per_group * LANES
    for gi in range(len(POOL_WINDOWS)):
        cols = slice(gi * gc, (gi + 1) * gc)
        pooled = jnp.concatenate([pooled_ref[gi * slabs_per_group + s]
                                  for s in range(slabs_per_group)], axis=1)
        mixed = _dot(pooled.astype(BF16), wpg_ref[gi])
        pm_ref[:, cols] = (mixed * pscale_ref[:, cols]).astype(BF16)
    b_branch = _dot(pm_ref[...], wpo_ref[...])
    merged = (_sigmoid(gates_ref[:, 0:d_model]) * a_branch
              + _sigmoid(gates_ref[:, d_model:2 * d_model]) * b_branch)
    o_ref[...] = h_ref[...] + _dot(merged.astype(BF16), wo_ref[...])


def _merge(h, ssd, pooled, gates_src, pool_scale, weights, *, tm, cast_weights):
    t, d = h.shape
    tm = min(tm, t)
    gated = not isinstance(ssd, tuple)
    pool_slabs = pooled.shape[0]
    pool_dim = pool_slabs * LANES
    row = lambda w, blk=0: pl.BlockSpec((tm, w), lambda i: (i, blk))
    pooled_spec = pl.BlockSpec((pool_slabs, tm, LANES), lambda i: (0, i, 0))
    gates_spec = row(N_BRANCH * d, gates_src[1])
    scratch = [pltpu.VMEM((tm, pool_dim), BF16)]
    if gated:
        d_inner = ssd.shape[1]
        in_specs = [row(d), row(d_inner), pooled_spec, gates_spec, _const_spec((1, pool_dim))]
        args = [h, ssd, pooled, gates_src[0], pool_scale]
    else:
        y, z_src, norm_ssd = ssd
        d_inner = y.shape[1]
        in_specs = [row(d), row(d_inner), row(d_inner, z_src[1]), pooled_spec, gates_spec,
                    _const_spec((1, d_inner)), _const_spec((1, pool_dim))]
        args = [h, y, z_src[0], pooled, gates_src[0], norm_ssd, pool_scale]
        scratch = [pltpu.VMEM((tm, d_inner), BF16)] + scratch
    out_specs = [row(d)]
    out_shape = [jax.ShapeDtypeStruct((t, d), F32)]
    if cast_weights:
        out_specs += [pl.BlockSpec(w.shape, lambda i, nd=w.ndim: (0,) * nd) for w in weights]
        out_shape += [jax.ShapeDtypeStruct(w.shape, BF16) for w in weights]
    res = pl.pallas_call(
        functools.partial(_merge_kernel, gated=gated, cast_weights=cast_weights),
        grid=(t // tm,),
        in_specs=in_specs + [_const_spec(w.shape) for w in weights],
        out_specs=out_specs,
        out_shape=out_shape,
        scratch_shapes=scratch,
        compiler_params=_params(1),
        name="merge_cast" if cast_weights else "merge",
    )(*args, *weights)
    return res if cast_weights else res[0]


def _pad_lanes(a):
    return jnp.pad(a.reshape(1, -1), ((0, 0), (0, LANES - a.shape[-1])))


def kernel(x_prompt, x_sample, state_ssm, state_conv, state_pool, p_prompt, p_sample, norm_ffn1, w_ffn1_gu, w_ffn1_down, norm_mix, w_in, conv_w, conv_b, dt_bias, a_log, d_skip, norm_ssd, w_ssd_out, w_pool_group, pool_scale, w_pool_out, w_o, norm_ffn2, w_ffn2_gu, w_ffn2_down, norm_ple, w_ple_gate, w_ple, norm_final):
    depth = norm_ffn1.shape[0]
    assert depth == 1, "the final norm is fused into the layer's last stage: one layer only"
    batch, seq, d_model = x_prompt.shape
    dec_batch, dec_seq, _ = x_sample.shape
    n_heads = dt_bias.shape[1]
    d_inner = n_heads * HEAD_DIM
    conv_dim = conv_w.shape[2]
    pool_dim = pool_scale.shape[1]
    assert seq % CHUNK == 0 and CHUNK % dec_seq == 0 and dec_batch % (CHUNK // dec_seq) == 0

    row = lambda a: a[0].reshape(1, -1)
    layout, proj_cols = _proj_layout(d_inner, conv_dim, pool_dim, d_model)
    src = {"z": (0, d_inner), "xbc": (d_inner, conv_dim), "dt": (d_inner + conv_dim, n_heads),
           "v": (d_inner + conv_dim + n_heads, pool_dim),
           "gates": (d_inner + conv_dim + n_heads + pool_dim, N_BRANCH * d_model)}
    unit_rows, unit_valid = [], []
    for name in sorted(layout, key=lambda k: layout[k][0]):
        start, width = src[name]
        assert layout[name][0] == len(unit_rows) * PROJ_UNIT
        for off in range(0, width, PROJ_UNIT):
            unit_rows.append(start + off)
            unit_valid.append(min(PROJ_UNIT, width - off))
    assert len(unit_rows) * PROJ_UNIT == proj_cols
    dtb, alog = _pad_lanes(dt_bias[0]), _pad_lanes(a_log[0])
    dskip = jnp.repeat(d_skip[0], HEAD_DIM).reshape(1, d_inner)
    cb = row(conv_b)
    ple_consts = (row(norm_ple), w_ple_gate[0], w_ple[0], norm_final.reshape(1, -1))

    h1, wg1, wu1, wd1 = _ffn_stream(x_sample.reshape(dec_batch * dec_seq, d_model), row(norm_ffn1),
                                    w_ffn1_gu[0], w_ffn1_down[0])
    proj, w_cat = _inproj_stream(h1, row(norm_mix), w_in[0].T, tuple(unit_rows), tuple(unit_valid))
    y, pooled, s2, c2, q2 = _mixer_sample(
        proj, layout, jnp.transpose(state_conv[0], (1, 0, 2)), jnp.transpose(state_pool[0], (1, 0, 2)),
        state_ssm[0].reshape(dec_batch, d_inner, D_STATE),
        conv_w[0], cb, dtb, alog, dskip,
        batch=dec_batch, seq=dec_seq, d_inner=d_inner, pos0=PAST_LEN)
    block_of = lambda k: layout[k][0] // layout[k][1]
    h2, wssd, wpg, wpo, wo = _merge(
        h1, (y, (proj, block_of("z")), row(norm_ssd)), pooled, (proj, block_of("gates")),
        row(pool_scale), (w_ssd_out[0], w_pool_group[0], w_pool_out[0], w_o[0]),
        tm=128, cast_weights=True)
    y_sample, wg2, wu2, wd2, wpleg, wple = _ffn_stream(
        h2, row(norm_ffn2), w_ffn2_gu[0], w_ffn2_down[0],
        (p_sample[0].reshape(dec_batch * dec_seq, -1),) + ple_consts)

    h1 = _ffn(x_prompt.reshape(batch * seq, d_model), row(norm_ffn1), wg1, wu1, wd1, tm=512)
    yn, pooled, gates, c1, q1, s1 = _inproj_mixer(h1, row(norm_mix), w_cat, layout, conv_w[0], cb,
                                                  dtb, alog, dskip, row(norm_ssd),
                                                  tm=256, seq=seq, d_inner=d_inner)
    h2 = _merge(h1, yn, pooled, (gates, 0), row(pool_scale), (wssd, wpg, wpo, wo),
                tm=512, cast_weights=False)
    y_prompt = _ffn(h2, row(norm_ffn2), wg2, wu2, wd2,
                    (p_prompt[0].reshape(batch * seq, -1), ple_consts[0], wpleg, wple, ple_consts[3]),
                    tm=512)

    return (y_prompt.reshape(batch, seq, d_model),
            y_sample.reshape(dec_batch, dec_seq, d_model),
            s1.reshape(1, batch, n_heads, HEAD_DIM, D_STATE), c1[None], q1[None],
            s2.reshape(1, dec_batch, n_heads, HEAD_DIM, D_STATE),
            jnp.transpose(c2, (1, 0, 2))[None], jnp.transpose(q2, (1, 0, 2))[None])
```

```python
import functools
import math
from typing import NamedTuple

import jax
import jax.numpy as jnp
from jax import lax
from jax.experimental import pallas as pl
from jax.experimental.pallas import tpu as pltpu

F32 = jnp.float32
BF16 = jnp.bfloat16
EPS = 1e-6
HIGHEST = lax.Precision.HIGHEST
LOG2E = 1.4426950408889634
NEG_LOG2E = -LOG2E

LANES = 128
SUBLANES = 8
VMEM_LIMIT_BYTES = 56 * 1024 * 1024

HEAD_DIM = 64
N_GROUPS = 8
D_STATE = 128
CONV_W = 4
CHUNK = 128
POOL_WINDOWS = (2, 4, 8, 16)
POOL_BUF = max(POOL_WINDOWS) - 1
N_BRANCH = 2
PAST_LEN = 16384
HEADS_PER_TILE = LANES // HEAD_DIM
ROW_STRIDE = 4
POOL_PAD = 24
POOL_PRE = 16


def _sigmoid(x):
    return 1.0 / (1.0 + jnp.exp2(x * NEG_LOG2E))


def _silu(x):
    half = 0.5 * x
    return half + half * jnp.tanh(half)


def _softplus(x):
    return jnp.maximum(x, 0.0) + jnp.log(1.0 + jnp.exp(-jnp.abs(x)))


def _rmsnorm(x, g):
    return x * lax.rsqrt(jnp.mean(x * x, axis=-1, keepdims=True) + EPS) * g


def _dot(a, b):
    return jnp.dot(a, b, preferred_element_type=F32)


def _dot_nt(a, b):
    return lax.dot_general(a, b, (((1,), (1,)), ((), ())), preferred_element_type=F32)


def _const_spec(shape):
    nd = len(shape)
    return pl.BlockSpec(shape, lambda *_: (0,) * nd, pipeline_mode=pl.Buffered(1))


def _params(n_axes):
    return pltpu.CompilerParams(dimension_semantics=("arbitrary",) * n_axes,
                                vmem_limit_bytes=VMEM_LIMIT_BYTES)


FFN_CHUNK = 256


def _ple_epilogue(h, p_ref, nple_ref, wg_ref, wp_ref, nfin_ref):
    gate = _sigmoid(_dot(_rmsnorm(h, nple_ref[...]).astype(BF16), wg_ref[...]))
    h = h + gate * _dot(p_ref[...].astype(BF16), wp_ref[...])
    return _rmsnorm(h, nfin_ref[...])


def _ffn_kernel(x_ref, g_ref, wg_ref, wu_ref, wd_ref, *rest, fc, with_ple):
    if with_ple:
        p_ref, nple_ref, wpg_ref, wp_ref, nfin_ref, o_ref, xn_ref, acc_ref = rest
    else:
        o_ref, xn_ref, acc_ref = rest
    xn_ref[...] = _rmsnorm(x_ref[...], g_ref[...]).astype(BF16)
    for c in range(wd_ref.shape[0] // fc):
        cols = slice(c * fc, (c + 1) * fc)
        xn = xn_ref[...]
        act = (_silu(_dot(xn, wg_ref[:, cols])) * _dot(xn, wu_ref[:, cols])).astype(BF16)
        contrib = _dot(act, wd_ref[cols, :])
        if c == 0:
            acc_ref[...] = contrib
        else:
            acc_ref[...] += contrib
    h = x_ref[...] + 0.5 * acc_ref[...]
    if with_ple:
        h = _ple_epilogue(h, p_ref, nple_ref, wpg_ref, wp_ref, nfin_ref)
    o_ref[...] = h


def _ffn(x, g, wg, wu, wd, ple=None, *, tm):
    t, d = x.shape
    tm = min(tm, t)
    row = lambda w: pl.BlockSpec((tm, w), lambda i: (i, 0))
    consts = [g, wg, wu, wd]
    in_specs = [row(d)] + [_const_spec(a.shape) for a in consts]
    args = [x] + consts
    if ple is not None:
        in_specs += [row(ple[0].shape[1])] + [_const_spec(a.shape) for a in ple[1:]]
        args += list(ple)
    return pl.pallas_call(
        functools.partial(_ffn_kernel, fc=FFN_CHUNK, with_ple=ple is not None),
        grid=(t // tm,),
        in_specs=in_specs,
        out_specs=row(d),
        out_shape=jax.ShapeDtypeStruct((t, d), F32),
        scratch_shapes=[pltpu.VMEM((tm, d), BF16), pltpu.VMEM((tm, d), F32)],
        compiler_params=_params(1),
        name="ffn_ple" if ple is not None else "ffn",
    )(*args)


def _ffn_stream_kernel(x_ref, g_ref, wg32_ref, wu32_ref, wd32_ref, *rest, with_ple):
    if with_ple:
        (p_ref, nple_ref, wpg32_ref, wp32_ref, nfin_ref,
         o_ref, wg_ref, wu_ref, wd_ref, wpg_ref, wp_ref, xn_ref, acc_ref) = rest
    else:
        o_ref, wg_ref, wu_ref, wd_ref, xn_ref, acc_ref = rest
    c = pl.program_id(0)

    @pl.when(c == 0)
    def _():
        xn_ref[...] = _rmsnorm(x_ref[...], g_ref[...]).astype(BF16)
        acc_ref[...] = jnp.zeros(acc_ref.shape, F32)
        if with_ple:
            wpg_ref[...] = wpg32_ref[...].astype(BF16)
            wp_ref[...] = wp32_ref[...].astype(BF16)

    wg_ref[...] = wg32_ref[...].astype(BF16)
    wu_ref[...] = wu32_ref[...].astype(BF16)
    wd_ref[...] = wd32_ref[...].astype(BF16)
    xn = xn_ref[...]
    act = (_silu(_dot(xn, wg_ref[...])) * _dot(xn, wu_ref[...])).astype(BF16)
    acc_ref[...] += _dot(act, wd_ref[...])

    @pl.when(c == pl.num_programs(0) - 1)
    def _():
        h = x_ref[...] + 0.5 * acc_ref[...]
        if with_ple:
            h = _ple_epilogue(h, p_ref, nple_ref, wpg_ref, wp_ref, nfin_ref)
        o_ref[...] = h


def _ffn_stream(x, g, w_gu, w_down, ple=None):
    t, d = x.shape
    d_ff = w_down.shape[0]
    fc = FFN_CHUNK
    n_chunks = d_ff // fc
    whole = lambda a: pl.BlockSpec(a.shape, lambda c: (0,) * a.ndim)
    in_specs = [whole(x), whole(g),
                pl.BlockSpec((d, fc), lambda c: (0, c)),
                pl.BlockSpec((d, fc), lambda c: (0, n_chunks + c)),
                pl.BlockSpec((fc, d), lambda c: (c, 0))]
    args = [x, g, w_gu, w_gu, w_down]
    out_specs = [whole(x),
                 pl.BlockSpec((d, fc), lambda c: (0, c)),
                 pl.BlockSpec((d, fc), lambda c: (0, c)),
                 pl.BlockSpec((fc, d), lambda c: (c, 0))]
    out_shape = [jax.ShapeDtypeStruct((t, d), F32),
                 jax.ShapeDtypeStruct((d, d_ff), BF16),
                 jax.ShapeDtypeStruct((d, d_ff), BF16),
                 jax.ShapeDtypeStruct((d_ff, d), BF16)]
    if ple is not None:
        in_specs += [whole(a) for a in ple]
        args += list(ple)
        out_specs += [whole(ple[2]), whole(ple[3])]
        out_shape += [jax.ShapeDtypeStruct(ple[2].shape, BF16), jax.ShapeDtypeStruct(ple[3].shape, BF16)]
    return pl.pallas_call(
        functools.partial(_ffn_stream_kernel, with_ple=ple is not None),
        grid=(n_chunks,),
        in_specs=in_specs,
        out_specs=out_specs,
        out_shape=out_shape,
        scratch_shapes=[pltpu.VMEM((t, d), BF16), pltpu.VMEM((t, d), F32)],
        compiler_params=_params(1),
        name="ffn_stream_ple" if ple is not None else "ffn_stream",
    )(*args)


PROJ_UNIT = 512


def _proj_layout(d_inner, conv_dim, pool_dim, d_model):
    xbc, z, gates, v = conv_dim, d_inner, N_BRANCH * d_model, pool_dim
    lay = {"xbc": (0, xbc), "z": (xbc, z), "gates": (xbc + z, gates), "v": (xbc + z + gates, v),
           "dt": (xbc + z + gates + v, LANES)}
    for start, width in lay.values():
        assert start % width == 0
    total = -(-(lay["dt"][0] + LANES) // PROJ_UNIT) * PROJ_UNIT
    return lay, total


def _inproj_stream_kernel(rows_ref, valid_ref, h_ref, g_ref, wt_ref, proj_ref, wcat_ref, u_ref):
    c = pl.program_id(0)

    @pl.when(c == 0)
    def _():
        u_ref[...] = _rmsnorm(h_ref[...], g_ref[...]).astype(BF16)

    wt = wt_ref[...]
    row = lax.broadcasted_iota(jnp.int32, wt.shape, 0)
    wcat_ref[...] = jnp.where(row < valid_ref[c], wt, 0.0).T.astype(BF16)
    proj_ref[...] = _dot(u_ref[...], wcat_ref[...])


def _inproj_stream(h, g, w_t, unit_rows, unit_valid):
    t, d = h.shape
    n_units = len(unit_rows)
    total = n_units * PROJ_UNIT
    assert all(r % SUBLANES == 0 and r + PROJ_UNIT <= w_t.shape[0] for r in unit_rows)
    whole = lambda a: pl.BlockSpec(a.shape, lambda c, rows, valid: (0,) * a.ndim)
    return pl.pallas_call(
        _inproj_stream_kernel,
        grid_spec=pltpu.PrefetchScalarGridSpec(
            num_scalar_prefetch=2,
            grid=(n_units,),
            in_specs=[whole(h), whole(g),
                      pl.BlockSpec((pl.Element(PROJ_UNIT), pl.Element(d)),
                                   lambda c, rows, valid: (rows[c] * SUBLANES, 0))],
            out_specs=[pl.BlockSpec((t, PROJ_UNIT), lambda c, rows, valid: (0, c)),
                       pl.BlockSpec((d, PROJ_UNIT), lambda c, rows, valid: (0, c))],
            scratch_shapes=[pltpu.VMEM((t, d), BF16)]),
        out_shape=[jax.ShapeDtypeStruct((t, total), F32),
                   jax.ShapeDtypeStruct((d, total), BF16)],
        compiler_params=_params(1),
        name="inproj_stream",
    )(jnp.asarray([r // SUBLANES for r in unit_rows], jnp.int32),
      jnp.asarray(unit_valid, jnp.int32), h, g, w_t)


def _inproj_mixer_kernel(h_ref, g_ref, w_ref, cw_ref, cb_ref, dtb_ref, alog_ref, dskip_ref, nssd_ref,
                         yn_ref, pooled_ref, gates_ref, conv_ref, pool_ref, ssm_ref,
                         u_ref, raw_ref, vraw_ref, xc_ref, state_ref, y_ref, z_ref, e_ref,
                         *, widths, starts, nc, tiles_per_seq, d_inner):
    i = pl.program_id(0)
    tm = h_ref.shape[0]
    n_slabs = raw_ref.shape[0]

    n_vslabs = vraw_ref.shape[0]
    tile_in_seq = i % tiles_per_seq
    span = SUBLANES * ROW_STRIDE

    @pl.when(tile_in_seq == 0)
    def _():
        raw_ref[:, 0:SUBLANES, :] = jnp.zeros((n_slabs, SUBLANES, LANES), F32)
        vraw_ref[:, 0:POOL_PRE, :] = jnp.zeros((n_vslabs, POOL_PRE, LANES), F32)
        state_ref[...] = jnp.zeros(state_ref.shape, F32)

    u_ref[...] = _rmsnorm(h_ref[...], g_ref[...]).astype(BF16)

    def pool_slab(s):
        cols = slice(s * LANES, (s + 1) * LANES)
        win = POOL_WINDOWS[s * len(POOL_WINDOWS) // n_vslabs]
        for base in range(0, tm, span):
            rows = {d: vraw_ref[s, pl.ds(POOL_PRE + base + d, SUBLANES, stride=ROW_STRIDE), :]
                    for d in range(1 - win, ROW_STRIDE)}
            for j in range(ROW_STRIDE):
                total = rows[j]
                for k in range(1, win):
                    total = total + rows[j - k]
                if base >= POOL_BUF:
                    mean = total * (1.0 / win)
                else:
                    pos = (tile_in_seq * tm + base + j
                           + ROW_STRIDE * lax.broadcasted_iota(jnp.int32, (SUBLANES, LANES), 0))
                    mean = total / jnp.minimum(pos + 1, win).astype(F32)
                pooled_ref[s, pl.ds(base + j, SUBLANES, stride=ROW_STRIDE), :] = mean - rows[j]
        pool_ref[0, :, cols] = vraw_ref[s, POOL_PRE + tm - POOL_BUF:POOL_PRE + tm, :]
        vraw_ref[s, 0:POOL_PRE, :] = vraw_ref[s, tm:tm + POOL_PRE, :]

    def conv_slab(s):
        cols = slice(s * LANES, (s + 1) * LANES)
        taps = [jnp.broadcast_to(cw_ref[k:k + 1, cols], (SUBLANES, LANES)) for k in range(CONV_W)]
        bias = jnp.broadcast_to(cb_ref[:, cols], (SUBLANES, LANES))
        for base in range(0, tm, span):
            shifted = [raw_ref[s, pl.ds(SUBLANES + base - (CONV_W - 1) + m, SUBLANES,
                                        stride=ROW_STRIDE), :]
                       for m in range(ROW_STRIDE + CONV_W - 1)]
            for j in range(ROW_STRIDE):
                acc = bias
                for k in range(CONV_W):
                    acc = acc + shifted[j + k] * taps[k]
                xc_ref[s, pl.ds(base + j, SUBLANES, stride=ROW_STRIDE), :] = _silu(acc)
        conv_ref[0, :, cols] = raw_ref[s, SUBLANES + tm - (CONV_W - 1):SUBLANES + tm, :]
        raw_ref[s, 0:SUBLANES, :] = raw_ref[s, tm:tm + SUBLANES, :]

    def project(k, a, width=nc):
        b = min(a + width, widths[k])
        return _dot(u_ref[...], w_ref[:, starts[k] + a:starts[k] + b])

    n_xbc_units = widths[1] // nc
    slabs_per_unit = nc // LANES
    done = 0
    for n, (k, a) in enumerate([(k, a) for k in (1, 2) for a in range(0, widths[k], nc)]):
        res = project(k, a)
        for s in range(res.shape[1] // LANES):
            piece = res[:, s * LANES:(s + 1) * LANES]
            if k == 1:
                raw_ref[a // LANES + s, SUBLANES:SUBLANES + tm, :] = piece
            else:
                vraw_ref[a // LANES + s, POOL_PRE:POOL_PRE + tm, :] = piece
                pool_slab(a // LANES + s)
        while done < min(n_slabs, n * slabs_per_unit, min(n + 1, n_xbc_units) * slabs_per_unit):
            conv_slab(done)
            done += 1
    dt_raw = project(4, 0)
    while done < n_slabs:
        conv_slab(done)
        done += 1

    q = CHUNK
    n_state = D_STATE
    assert q == LANES and n_state == LANES
    late_nc = nc // 2
    late_units = [(k, a) for k in (0, 3) for a in range(0, widths[k], late_nc)]
    late_outs = {0: z_ref, 3: gates_ref}
    n_blocks = (tm // q) * N_GROUPS
    emitted = 0
    scalars = []
    for ci in range(tm // q):
        terms = _chunk_scalars(dt_raw[ci * q:(ci + 1) * q, :], dtb_ref[...], alog_ref[...], q)
        scalars.append((terms, jnp.exp2(terms.seg2).T))
    tiles_per_group = d_inner // N_GROUPS // LANES

    def emit_late_units(block, of):
        nonlocal emitted
        while emitted < len(late_units) and emitted * of < block * len(late_units):
            k, a = late_units[emitted]
            res = project(k, a, late_nc)
            late_outs[k][:, a:a + res.shape[1]] = res
            emitted += 1

    for ci in range(tm // q):
        rows = slice(ci * q, (ci + 1) * q)
        terms, _ = scalars[ci]
        for g in range(N_GROUPS):
            b_g = xc_ref[d_inner // LANES + g, rows, :].astype(BF16)
            c_g = xc_ref[d_inner // LANES + N_GROUPS + g, rows, :].astype(BF16)
            cb = _dot_nt(c_g, b_g)
            for p in range(tiles_per_group):
                tile = g * tiles_per_group + p
                cols = slice(tile * LANES, (tile + 1) * LANES)
                xpair = xc_ref[tile, rows, :]
                y, e_pair = _diag_pair(xpair, cb, terms, tile * HEADS_PER_TILE)
                y_ref[rows, cols] = y + dskip_ref[:, cols] * xpair
                e_ref[rows, cols] = e_pair
            emit_late_units(ci * N_GROUPS + g + 1, 2 * n_blocks)
    for ci in range(tm // q):
        rows = slice(ci * q, (ci + 1) * q)
        terms, chunk_decay_t = scalars[ci]
        for g in range(N_GROUPS):
            b_g = xc_ref[d_inner // LANES + g, rows, :].astype(BF16)
            c_g = xc_ref[d_inner // LANES + N_GROUPS + g, rows, :].astype(BF16)
            for p in range(tiles_per_group):
                tile = g * tiles_per_group + p
                h0 = tile * HEADS_PER_TILE
                cols = slice(tile * LANES, (tile + 1) * LANES)
                xpair = xc_ref[tile, rows, :]
                h_prev = state_ref[cols, :]
                y_ref[rows, cols] += _dot_nt(c_g, h_prev.astype(BF16)) * e_ref[rows, cols]
                xw_t = (xpair.T * _pair_rows(terms.w_state_t, h0, q)).astype(BF16)
                decay_rows = _pair_rows(chunk_decay_t, h0, n_state)
                state_ref[cols, :] = decay_rows * h_prev + _dot(xw_t, b_g)
            emit_late_units(n_blocks + ci * N_GROUPS + g + 1, 2 * n_blocks)
    assert emitted == len(late_units)

    _gate_and_norm(y_ref, z_ref, nssd_ref, yn_ref)

    @pl.when(tile_in_seq == tiles_per_seq - 1)
    def _():
        ssm_ref[0] = state_ref[...]


def _inproj_mixer(h, g, w_cat, layout, conv_w, conv_b, dt_bias, a_log, d_skip, norm_ssd,
                  *, tm, seq, d_inner):
    order = ("z", "xbc", "v", "gates", "dt")
    starts = tuple(layout[k][0] for k in order)
    widths = tuple(layout[k][1] for k in order)
    t, d = h.shape
    tm = min(tm, seq)
    tiles_per_seq = seq // tm
    n_all = w_cat.shape[1]
    conv_dim, pool_dim = widths[1], widths[2]
    n_slabs = conv_dim // LANES
    assert tm % (SUBLANES * ROW_STRIDE) == 0 and tm % CHUNK == 0
    row = lambda w: pl.BlockSpec((tm, w), lambda i: (i, 0))
    per_seq = lambda i: (i // tiles_per_seq, 0, 0)
    return pl.pallas_call(
        functools.partial(_inproj_mixer_kernel, widths=widths, starts=starts, nc=PROJ_UNIT,
                          tiles_per_seq=tiles_per_seq, d_inner=d_inner),
        grid=(t // tm,),
        in_specs=[row(d),
                  _const_spec((1, d)),
                  _const_spec((d, n_all)),
                  _const_spec((CONV_W, conv_dim)),
                  _const_spec((1, conv_dim)),
                  _const_spec((1, LANES)),
                  _const_spec((1, LANES)),
                  _const_spec((1, d_inner)),
                  _const_spec((1, d_inner))],
        out_specs=[row(d_inner),
                   pl.BlockSpec((pool_dim // LANES, tm, LANES), lambda i: (0, i, 0)),
                   row(widths[3]),
                   pl.BlockSpec((1, CONV_W - 1, conv_dim), per_seq),
                   pl.BlockSpec((1, POOL_BUF, pool_dim), per_seq),
                   pl.BlockSpec((1, d_inner, D_STATE), per_seq)],
        out_shape=[jax.ShapeDtypeStruct((t, d_inner), BF16),
                   jax.ShapeDtypeStruct((pool_dim // LANES, t, LANES), F32),
                   jax.ShapeDtypeStruct((t, widths[3]), F32),
                   jax.ShapeDtypeStruct((t // seq, CONV_W - 1, conv_dim), F32),
                   jax.ShapeDtypeStruct((t // seq, POOL_BUF, pool_dim), F32),
                   jax.ShapeDtypeStruct((t // seq, d_inner, D_STATE), F32)],
        scratch_shapes=[pltpu.VMEM((tm, d), BF16),
                        pltpu.VMEM((n_slabs, SUBLANES + tm, LANES), F32),
                        pltpu.VMEM((pool_dim // LANES, POOL_PRE + tm, LANES), F32),
                        pltpu.VMEM((n_slabs, tm, LANES), F32),
                        pltpu.VMEM((d_inner, D_STATE), F32),
                        pltpu.VMEM((tm, d_inner), F32),
                        pltpu.VMEM((tm, widths[0]), F32),
                        pltpu.VMEM((tm, d_inner), F32)],
        compiler_params=_params(1),
        name="inproj_mixer",
    )(h, g, w_cat, conv_w, conv_b, dt_bias, a_log, d_skip, norm_ssd)


def _chunk_scalars(dt_raw, dt_bias, a_log, seg_len):
    q = dt_raw.shape[0]
    dt = _softplus(dt_raw + dt_bias)
    da = dt * (-jnp.exp(a_log))
    shift = int(math.log2(seg_len))
    qi = lax.broadcasted_iota(jnp.int32, (q, q), 0)
    si = lax.broadcasted_iota(jnp.int32, (q, q), 1)
    same = (qi >> shift) == (si >> shift)
    causal = jnp.logical_and(same, si <= qi)
    a_cs = jnp.dot(jnp.where(causal, 1.0, 0.0), da, precision=HIGHEST, preferred_element_type=F32)
    seg_tot = jnp.dot(jnp.where(same, 1.0, 0.0), da, precision=HIGHEST, preferred_element_type=F32)
    a2 = a_cs * LOG2E
    seg2 = seg_tot * LOG2E
    return _ChunkTerms(
        a2=a2,
        decay_in_t=(a2 - jnp.log2(dt)).T,
        w_state_t=(jnp.exp2(seg2 - a2) * dt).T,
        seg2=seg2,
        causal=causal)


class _ChunkTerms(NamedTuple):
    a2: jax.Array
    decay_in_t: jax.Array
    w_state_t: jax.Array
    seg2: jax.Array
    causal: jax.Array


def _pair_rows(vals_t, h0, cols):
    return jnp.concatenate([jnp.broadcast_to(vals_t[h0 + k:h0 + k + 1, :], (HEAD_DIM, cols))
                            for k in range(HEADS_PER_TILE)], axis=0)


def _diag_pair(xpair, cb, terms, h0):
    q = xpair.shape[0]
    lane = lax.broadcasted_iota(jnp.int32, xpair.shape, 1)
    out = None
    spread = []
    for k in range(HEADS_PER_TILE):
        h = h0 + k
        a_col = jnp.broadcast_to(terms.a2[:, h:h + 1], (q, LANES))
        spread.append(a_col)
        decay_dt = jnp.exp2(jnp.where(terms.causal, a_col - terms.decay_in_t[h:h + 1, :], -jnp.inf))
        w = (cb * decay_dt).astype(BF16)
        mine = (lane < HEAD_DIM) if k == 0 else (lane >= HEAD_DIM)
        t = _dot(w, jnp.where(mine, xpair, 0.0).astype(BF16))
        out = t if out is None else out + t
    e_pair = jnp.exp2(jnp.where(lane < HEAD_DIM, spread[0], spread[1]))
    return out, e_pair


def _shift_rows(cur, prev, s):
    sub = lax.broadcasted_iota(jnp.int32, (1,) + cur.shape[1:], 1)
    return jnp.where(sub < s, pltpu.roll(prev, s, axis=1), pltpu.roll(cur, s, axis=1))


def _conv_tiles(cur, prev, w, bias):
    tap = lambda k: w[k:k + 1, :][None]
    acc = cur * tap(CONV_W - 1)
    for s in range(1, CONV_W):
        acc = acc + _shift_rows(cur, prev, s) * tap(CONV_W - 1 - s)
    return _silu(bias[None] + acc)


def _conv_block(ext, w, bias):
    rows, width = ext.shape
    tiles = ext.reshape(rows // SUBLANES, SUBLANES, width)
    return _conv_tiles(tiles[1:], tiles[:-1], w, bias).reshape(rows - SUBLANES, width)


def _conv_tile_rows(rows_of_tap, w, bias):
    acc = rows_of_tap(0) * w[0:1, :]
    for k in range(1, CONV_W):
        acc = acc + rows_of_tap(k) * w[k:k + 1, :]
    return _silu(bias + acc)


def _mixer_sample_kernel(xbc_ref, dt_ref, v_ref, conv0_ref, pool0_ref, ssm0_ref,
                         cw_ref, cb_ref, dtb_ref, alog_ref, dskip_ref,
                         y_ref, pooled_ref, ssm_ref, conv_ref, pool_ref,
                         head_ref, xc_ref, xwt_ref, e_ref, seg_ref, pe_ref,
                         pooled_sc_ref, conv_sc_ref, pool_sc_ref,
                         *, d_inner, seq, pos0, seqs_per_step, n_chunks):
    c = pl.program_id(0)
    j = pl.program_id(1)
    q = xbc_ref.shape[0]
    conv_dim = xbc_ref.shape[1]
    pool_dim = v_ref.shape[1]
    n_state = D_STATE
    tiles_per_group = d_inner // N_GROUPS // LANES
    gc = pool_dim // len(POOL_WINDOWS)

    @pl.when(c < n_chunks)
    def _():
        slot = c % 2
        lo = SUBLANES - (CONV_W - 1)

        def per_seq(b, carry):
            r0 = pl.multiple_of(b * seq, seq)
            for k in range(CONV_W - 1):
                head_ref[lo + k:lo + k + 1, :] = conv0_ref[k, pl.ds(b, 1), :]
            ct = 1024
            for jc in range(conv_dim // ct):
                cols = slice(jc * ct, (jc + 1) * ct)
                head_ref[SUBLANES:SUBLANES + seq, cols] = xbc_ref[pl.ds(r0, seq), cols]
                xc_ref[slot, pl.ds(r0, seq), cols] = _conv_tile_rows(
                    lambda k: head_ref[lo + k:lo + k + seq, cols], cw_ref[:, cols], cb_ref[:, cols])
            for k in range(CONV_W - 1):
                r = SUBLANES + seq - (CONV_W - 1) + k
                conv_sc_ref[slot, k, pl.ds(b, 1), :] = head_ref[r:r + 1, :]

            for k in range(POOL_BUF):
                r = POOL_PAD - POOL_BUF + k
                pe_ref[r:r + 1, :] = pool0_ref[k, pl.ds(b, 1), :]
            vb = v_ref[pl.ds(r0, seq), :]
            pe_ref[POOL_PAD:POOL_PAD + seq, :] = vb
            t_idx = lax.broadcasted_iota(jnp.int32, (seq, 1), 0) + pos0
            for gi, win in enumerate(POOL_WINDOWS):
                cols = slice(gi * gc, (gi + 1) * gc)
                total = pe_ref[POOL_PAD:POOL_PAD + seq, cols]
                for k in range(1, win):
                    total = total + pe_ref[POOL_PAD - k:POOL_PAD - k + seq, cols]
                cnt = jnp.minimum(t_idx + 1, win).astype(F32)
                pooled = total / cnt - vb[:, cols]
                for s in range(gc // LANES):
                    pooled_sc_ref[slot, gi * (gc // LANES) + s, pl.ds(r0, seq), :] = (
                        pooled[:, s * LANES:(s + 1) * LANES])
            for k in range(POOL_BUF):
                r = POOL_PAD + seq - POOL_BUF + k
                pool_sc_ref[slot, k, pl.ds(b, 1), :] = pe_ref[r:r + 1, :]
            return carry

        lax.fori_loop(j * seqs_per_step, (j + 1) * seqs_per_step, per_seq, 0)

    @pl.when(c >= 1)
    def _():
        slot = (c + 1) % 2
        xc = xc_ref.at[slot]

        @pl.when(j == 0)
        def _():
            pooled_ref[...] = pooled_sc_ref[slot]
            conv_ref[...] = conv_sc_ref[slot]
            pool_ref[...] = pool_sc_ref[slot]
            terms = _chunk_scalars(dt_ref[...], dtb_ref[...], alog_ref[...], seq)
            seg_ref[...] = terms.seg2
            for g in range(N_GROUPS):
                b_g = xc[:, d_inner + g * n_state:d_inner + (g + 1) * n_state].astype(BF16)
                c_g = xc[:, d_inner + (N_GROUPS + g) * n_state:
                         d_inner + (N_GROUPS + g + 1) * n_state].astype(BF16)
                cb = _dot_nt(c_g, b_g)
                for p in range(tiles_per_group):
                    tile = g * tiles_per_group + p
                    h0 = tile * HEADS_PER_TILE
                    cols = slice(tile * LANES, (tile + 1) * LANES)
                    xpair = xc[:, cols]
                    y, e_pair = _diag_pair(xpair, cb, terms, h0)
                    y_ref[:, cols] = y + dskip_ref[:, cols] * xpair
                    e_ref[:, cols] = e_pair
                    xwt_ref[cols, :] = (xpair.T * _pair_rows(terms.w_state_t, h0, q)).astype(BF16)

        row_q = lax.broadcasted_iota(jnp.int32, (q, n_state), 0)
        row128 = lax.broadcasted_iota(jnp.int32, (LANES, n_state), 0)
        for sj in range(seqs_per_step):
            r0 = pl.multiple_of((j * seqs_per_step + sj) * seq, seq)
            chunk_decay = jnp.exp2(seg_ref[pl.ds(r0, 1), :])
            mine = jnp.logical_and(row_q >= r0, row_q < r0 + seq)
            for g in range(N_GROUPS):
                b_cols = slice(d_inner + g * n_state, d_inner + (g + 1) * n_state)
                c_cols = slice(d_inner + (N_GROUPS + g) * n_state,
                               d_inner + (N_GROUPS + g + 1) * n_state)
                b_mine = jnp.where(mine, xc[:, b_cols], 0.0).astype(BF16)
                c_rows = xc[pl.ds(r0, seq), c_cols].astype(BF16)
                for p in range(tiles_per_group):
                    tile = g * tiles_per_group + p
                    h0 = tile * HEADS_PER_TILE
                    cols = slice(tile * LANES, (tile + 1) * LANES)
                    h_prev = ssm0_ref[sj, cols, :]
                    y_off = _dot_nt(c_rows, h_prev.astype(BF16)) * e_ref[pl.ds(r0, seq), cols]
                    y_ref[pl.ds(r0, seq), cols] += y_off
                    decay_rows = jnp.where(row128 < HEAD_DIM, chunk_decay[:, h0:h0 + 1],
                                           chunk_decay[:, h0 + 1:h0 + 2])
                    ssm_ref[sj, cols, :] = decay_rows * h_prev + _dot(xwt_ref[cols, :], b_mine)


def _mixer_sample(proj, layout, conv0, pool0, ssm0, conv_w, conv_b, dt_bias, a_log, d_skip,
                  *, batch, seq, d_inner, pos0):
    q = CHUNK
    n_seq = q // seq
    n_chunks = batch // n_seq
    conv_dim = layout["xbc"][1]
    pool_dim = layout["v"][1]
    sps = math.gcd(n_seq, 4)
    steps = n_seq // sps
    prep = lambda c: jnp.minimum(c, n_chunks - 1)
    scan = lambda c: jnp.maximum(c - 1, 0)
    window = lambda k, chunk: pl.BlockSpec(
        (q, layout[k][1]), lambda c, j: (chunk(c), layout[k][0] // layout[k][1]))
    tok = lambda c, j: (scan(c), 0)
    per_chunk_in = lambda c, j: (0, prep(c), 0)
    per_chunk = lambda c, j: (0, scan(c), 0)
    per_seq = lambda c, j: (jnp.where(c == 0, 0, (c - 1) * steps + j), 0, 0)
    return pl.pallas_call(
        functools.partial(_mixer_sample_kernel, d_inner=d_inner, seq=seq, pos0=pos0,
                          seqs_per_step=sps, n_chunks=n_chunks),
        grid=(n_chunks + 1, steps),
        in_specs=[window("xbc", prep), window("dt", scan), window("v", prep),
                  pl.BlockSpec((CONV_W - 1, n_seq, conv_dim), per_chunk_in),
                  pl.BlockSpec((POOL_BUF, n_seq, pool_dim), per_chunk_in),
                  pl.BlockSpec((sps, d_inner, D_STATE), per_seq),
                  _const_spec((CONV_W, conv_dim)),
                  _const_spec((1, conv_dim)),
                  _const_spec((1, LANES)),
                  _const_spec((1, LANES)),
                  _const_spec((1, d_inner))],
        out_specs=[pl.BlockSpec((q, d_inner), tok),
                   pl.BlockSpec((pool_dim // LANES, q, LANES), per_chunk),
                   pl.BlockSpec((sps, d_inner, D_STATE), per_seq),
                   pl.BlockSpec((CONV_W - 1, n_seq, conv_dim), per_chunk),
                   pl.BlockSpec((POOL_BUF, n_seq, pool_dim), per_chunk)],
        out_shape=[jax.ShapeDtypeStruct((batch * seq, d_inner), F32),
                   jax.ShapeDtypeStruct((pool_dim // LANES, batch * seq, LANES), F32),
                   jax.ShapeDtypeStruct((batch, d_inner, D_STATE), F32),
                   jax.ShapeDtypeStruct((CONV_W - 1, batch, conv_dim), F32),
                   jax.ShapeDtypeStruct((POOL_BUF, batch, pool_dim), F32)],
        scratch_shapes=[pltpu.VMEM((2 * SUBLANES, conv_dim), F32),
                        pltpu.VMEM((2, q, conv_dim), F32),
                        pltpu.VMEM((d_inner, q), BF16),
                        pltpu.VMEM((q, d_inner), F32),
                        pltpu.VMEM((q, LANES), F32),
                        pltpu.VMEM((POOL_PAD + seq, pool_dim), F32),
                        pltpu.VMEM((2, pool_dim // LANES, q, LANES), F32),
                        pltpu.VMEM((2, CONV_W - 1, n_seq, conv_dim), F32),
                        pltpu.VMEM((2, POOL_BUF, n_seq, pool_dim), F32)],
        compiler_params=_params(2),
        name="mixer_sample",
    )(proj, proj, proj, conv0, pool0, ssm0, conv_w, conv_b, dt_bias, a_log, d_skip)


def _gate_and_norm(y_ref, z_ref, nssd_ref, yn_ref):
    gw = y_ref.shape[1] // N_GROUPS
    for g in range(N_GROUPS):
        cols = slice(g * gw, (g + 1) * gw)
        yz = y_ref[:, cols] * _silu(z_ref[:, cols])
        yn_ref[:, cols] = _rmsnorm(yz, nssd_ref[:, cols]).astype(BF16)


def _merge_kernel(h_ref, *rest, gated, cast_weights):
    if gated:
        yn_ref, pooled_ref, gates_ref, pscale_ref, *rest = rest
    else:
        y_ref, z_ref, pooled_ref, gates_ref, nssd_ref, pscale_ref, *rest = rest
    if cast_weights:
        w32_refs, (o_ref, *w_refs), scratch = rest[:4], rest[4:9], rest[9:]

        @pl.when(pl.program_id(0) == 0)
        def _():
            for dst, src in zip(w_refs, w32_refs):
                dst[...] = src[...].astype(BF16)
    else:
        w_refs, o_ref, scratch = rest[:4], rest[4], rest[5:]
    wssd_ref, wpg_ref, wpo_ref, wo_ref = w_refs
    d_model = h_ref.shape[1]
    if gated:
        (pm_ref,) = scratch
    else:
        yn_ref, pm_ref = scratch
        _gate_and_norm(y_ref, z_ref, nssd_ref, yn_ref)
    a_branch = _dot(yn_ref[...], wssd_ref[...])
    slabs_per_group = pooled_ref.shape[0] // len(POOL_WINDOWS)
    gc = slabs_per_group * LANES
    for gi in range(len(POOL_WINDOWS)):
        cols = slice(gi * gc, (gi + 1) * gc)
        pooled = jnp.concatenate([pooled_ref[gi * slabs_per_group + s]
                                  for s in range(slabs_per_group)], axis=1)
        mixed = _dot(pooled.astype(BF16), wpg_ref[gi])
        pm_ref[:, cols] = (mixed * pscale_ref[:, cols]).astype(BF16)
    b_branch = _dot(pm_ref[...], wpo_ref[...])
    merged = (_sigmoid(gates_ref[:, 0:d_model]) * a_branch
              + _sigmoid(gates_ref[:, d_model:2 * d_model]) * b_branch)
    o_ref[...] = h_ref[...] + _dot(merged.astype(BF16), wo_ref[...])


def _merge(h, ssd, pooled, gates_src, pool_scale, weights, *, tm, cast_weights):
    t, d = h.shape
    tm = min(tm, t)
    gated = not isinstance(ssd, tuple)
    pool_slabs = pooled.shape[0]
    pool_dim = pool_slabs * LANES
    row = lambda w, blk=0: pl.BlockSpec((tm, w), lambda i: (i, blk))
    pooled_spec = pl.BlockSpec((pool_slabs, tm, LANES), lambda i: (0, i, 0))
    gates_spec = row(N_BRANCH * d, gates_src[1])
    scratch = [pltpu.VMEM((tm, pool_dim), BF16)]
    if gated:
        d_inner = ssd.shape[1]
        in_specs = [row(d), row(d_inner), pooled_spec, gates_spec, _const_spec((1, pool_dim))]
        args = [h, ssd, pooled, gates_src[0], pool_scale]
    else:
        y, z_src, norm_ssd = ssd
        d_inner = y.shape[1]
        in_specs = [row(d), row(d_inner), row(d_inner, z_src[1]), pooled_spec, gates_spec,
                    _const_spec((1, d_inner)), _const_spec((1, pool_dim))]
        args = [h, y, z_src[0], pooled, gates_src[0], norm_ssd, pool_scale]
        scratch = [pltpu.VMEM((tm, d_inner), BF16)] + scratch
    out_specs = [row(d)]
    out_shape = [jax.ShapeDtypeStruct((t, d), F32)]
    if cast_weights:
        out_specs += [pl.BlockSpec(w.shape, lambda i, nd=w.ndim: (0,) * nd) for w in weights]
        out_shape += [jax.ShapeDtypeStruct(w.shape, BF16) for w in weights]
    res = pl.pallas_call(
        functools.partial(_merge_kernel, gated=gated, cast_weights=cast_weights),
        grid=(t // tm,),
        in_specs=in_specs + [_const_spec(w.shape) for w in weights],
        out_specs=out_specs,
        out_shape=out_shape,
        scratch_shapes=scratch,
        compiler_params=_params(1),
        name="merge_cast" if cast_weights else "merge",
    )(*args, *weights)
    return res if cast_weights else res[0]


def _pad_lanes(a):
    return jnp.pad(a.reshape(1, -1), ((0, 0), (0, LANES - a.shape[-1])))


def kernel(x_prompt, x_sample, state_ssm, state_conv, state_pool, p_prompt, p_sample, norm_ffn1, w_ffn1_gu, w_ffn1_down, norm_mix, w_in, conv_w, conv_b, dt_bias, a_log, d_skip, norm_ssd, w_ssd_out, w_pool_group, pool_scale, w_pool_out, w_o, norm_ffn2, w_ffn2_gu, w_ffn2_down, norm_ple, w_ple_gate, w_ple, norm_final):
    depth = norm_ffn1.shape[0]
    assert depth == 1, "the final norm is fused into the layer's last stage: one layer only"
    batch, seq, d_model = x_prompt.shape
    dec_batch, dec_seq, _ = x_sample.shape
    n_heads = dt_bias.shape[1]
    d_inner = n_heads * HEAD_DIM
    conv_dim = conv_w.shape[2]
    pool_dim = pool_scale.shape[1]
    assert seq % CHUNK == 0 and CHUNK % dec_seq == 0 and dec_batch % (CHUNK // dec_seq) == 0

    row = lambda a: a[0].reshape(1, -1)
    layout, proj_cols = _proj_layout(d_inner, conv_dim, pool_dim, d_model)
    src = {"z": (0, d_inner), "xbc": (d_inner, conv_dim), "dt": (d_inner + conv_dim, n_heads),
           "v": (d_inner + conv_dim + n_heads, pool_dim),
           "gates": (d_inner + conv_dim + n_heads + pool_dim, N_BRANCH * d_model)}
    unit_rows, unit_valid = [], []
    for name in sorted(layout, key=lambda k: layout[k][0]):
        start, width = src[name]
        assert layout[name][0] == len(unit_rows) * PROJ_UNIT
        for off in range(0, width, PROJ_UNIT):
            unit_rows.append(start + off)
            unit_valid.append(min(PROJ_UNIT, width - off))
    assert len(unit_rows) * PROJ_UNIT == proj_cols
    dtb, alog = _pad_lanes(dt_bias[0]), _pad_lanes(a_log[0])
    dskip = jnp.repeat(d_skip[0], HEAD_DIM).reshape(1, d_inner)
    cb = row(conv_b)
    ple_consts = (row(norm_ple), w_ple_gate[0], w_ple[0], norm_final.reshape(1, -1))

    h1, wg1, wu1, wd1 = _ffn_stream(x_sample.reshape(dec_batch * dec_seq, d_model), row(norm_ffn1),
                                    w_ffn1_gu[0], w_ffn1_down[0])
    proj, w_cat = _inproj_stream(h1, row(norm_mix), w_in[0].T, tuple(unit_rows), tuple(unit_valid))
    y, pooled, s2, c2, q2 = _mixer_sample(
        proj, layout, jnp.transpose(state_conv[0], (1, 0, 2)), jnp.transpose(state_pool[0], (1, 0, 2)),
        state_ssm[0].reshape(dec_batch, d_inner, D_STATE),
        conv_w[0], cb, dtb, alog, dskip,
        batch=dec_batch, seq=dec_seq, d_inner=d_inner, pos0=PAST_LEN)
    block_of = lambda k: layout[k][0] // layout[k][1]
    h2, wssd, wpg, wpo, wo = _merge(
        h1, (y, (proj, block_of("z")), row(norm_ssd)), pooled, (proj, block_of("gates")),
        row(pool_scale), (w_ssd_out[0], w_pool_group[0], w_pool_out[0], w_o[0]),
        tm=128, cast_weights=True)
    y_sample, wg2, wu2, wd2, wpleg, wple = _ffn_stream(
        h2, row(norm_ffn2), w_ffn2_gu[0], w_ffn2_down[0],
        (p_sample[0].reshape(dec_batch * dec_seq, -1),) + ple_consts)

    h1 = _ffn(x_prompt.reshape(batch * seq, d_model), row(norm_ffn1), wg1, wu1, wd1, tm=512)
    yn, pooled, gates, c1, q1, s1 = _inproj_mixer(h1, row(norm_mix), w_cat, layout, conv_w[0], cb,
                                                  dtb, alog, dskip, row(norm_ssd),
                                                  tm=256, seq=seq, d_inner=d_inner)
    h2 = _merge(h1, yn, pooled, (gates, 0), row(pool_scale), (wssd, wpg, wpo, wo),
                tm=512, cast_weights=False)
    y_prompt = _ffn(h2, row(norm_ffn2), wg2, wu2, wd2,
                    (p_prompt[0].reshape(batch * seq, -1), ple_consts[0], wpleg, wple, ple_consts[3]),
                    tm=512)

    return (y_prompt.reshape(batch, seq, d_model),
            y_sample.reshape(dec_batch, dec_seq, d_model),
            s1.reshape(1, batch, n_heads, HEAD_DIM, D_STATE), c1[None], q1[None],
            s2.reshape(1, dec_batch, n_heads, HEAD_DIM, D_STATE),
            jnp.transpose(c2, (1, 0, 2))[None], jnp.transpose(q2, (1, 0, 2))[None])
```

```python
import functools
import math
from typing import NamedTuple

import jax
import jax.numpy as jnp
from jax import lax
from jax.experimental import pallas as pl
from jax.experimental.pallas import tpu as pltpu

F32 = jnp.float32
BF16 = jnp.bfloat16
EPS = 1e-6
HIGHEST = lax.Precision.HIGHEST
LOG2E = 1.4426950408889634
NEG_LOG2E = -LOG2E

LANES = 128
SUBLANES = 8
VMEM_LIMIT_BYTES = 56 * 1024 * 1024

HEAD_DIM = 64
N_GROUPS = 8
D_STATE = 128
CONV_W = 4
CHUNK = 128
POOL_WINDOWS = (2, 4, 8, 16)
POOL_BUF = max(POOL_WINDOWS) - 1
N_BRANCH = 2
PAST_LEN = 16384
HEADS_PER_TILE = LANES // HEAD_DIM
ROW_STRIDE = 4
POOL_PAD = 24
POOL_PRE = 16


def _sigmoid(x):
    return 1.0 / (1.0 + jnp.exp2(x * NEG_LOG2E))


def _silu(x):
    half = 0.5 * x
    return half + half * jnp.tanh(half)


def _softplus(x):
    return jnp.maximum(x, 0.0) + jnp.log(1.0 + jnp.exp(-jnp.abs(x)))


def _rmsnorm(x, g):
    return x * lax.rsqrt(jnp.mean(x * x, axis=-1, keepdims=True) + EPS) * g


def _dot(a, b):
    return jnp.dot(a, b, preferred_element_type=F32)


def _dot_nt(a, b):
    return lax.dot_general(a, b, (((1,), (1,)), ((), ())), preferred_element_type=F32)


def _const_spec(shape):
    nd = len(shape)
    return pl.BlockSpec(shape, lambda *_: (0,) * nd, pipeline_mode=pl.Buffered(1))


def _params(n_axes):
    return pltpu.CompilerParams(dimension_semantics=("arbitrary",) * n_axes,
                                vmem_limit_bytes=VMEM_LIMIT_BYTES)


FFN_CHUNK = 256


def _ple_epilogue(h, p_ref, nple_ref, wg_ref, wp_ref, nfin_ref):
    gate = _sigmoid(_dot(_rmsnorm(h, nple_ref[...]).astype(BF16), wg_ref[...]))
    h = h + gate * _dot(p_ref[...].astype(BF16), wp_ref[...])
    return _rmsnorm(h, nfin_ref[...])


def _ffn_kernel(x_ref, g_ref, wg_ref, wu_ref, wd_ref, *rest, fc, with_ple):
    if with_ple:
        p_ref, nple_ref, wpg_ref, wp_ref, nfin_ref, o_ref, xn_ref, acc_ref = rest
    else:
        o_ref, xn_ref, acc_ref = rest
    xn_ref[...] = _rmsnorm(x_ref[...], g_ref[...]).astype(BF16)
    for c in range(wd_ref.shape[0] // fc):
        cols = slice(c * fc, (c + 1) * fc)
        xn = xn_ref[...]
        act = (_silu(_dot(xn, wg_ref[:, cols])) * _dot(xn, wu_ref[:, cols])).astype(BF16)
        contrib = _dot(act, wd_ref[cols, :])
        if c == 0:
            acc_ref[...] = contrib
        else:
            acc_ref[...] += contrib
    h = x_ref[...] + 0.5 * acc_ref[...]
    if with_ple:
        h = _ple_epilogue(h, p_ref, nple_ref, wpg_ref, wp_ref, nfin_ref)
    o_ref[...] = h


def _ffn(x, g, wg, wu, wd, ple=None, *, tm):
    t, d = x.shape
    tm = min(tm, t)
    row = lambda w: pl.BlockSpec((tm, w), lambda i: (i, 0))
    consts = [g, wg, wu, wd]
    in_specs = [row(d)] + [_const_spec(a.shape) for a in consts]
    args = [x] + consts
    if ple is not None:
        in_specs += [row(ple[0].shape[1])] + [_const_spec(a.shape) for a in ple[1:]]
        args += list(ple)
    return pl.pallas_call(
        functools.partial(_ffn_kernel, fc=FFN_CHUNK, with_ple=ple is not None),
        grid=(t // tm,),
        in_specs=in_specs,
        out_specs=row(d),
        out_shape=jax.ShapeDtypeStruct((t, d), F32),
        scratch_shapes=[pltpu.VMEM((tm, d), BF16), pltpu.VMEM((tm, d), F32)],
        compiler_params=_params(1),
        name="ffn_ple" if ple is not None else "ffn",
    )(*args)


def _ffn_stream_kernel(x_ref, g_ref, wg32_ref, wu32_ref, wd32_ref, *rest, with_ple):
    if with_ple:
        (p_ref, nple_ref, wpg32_ref, wp32_ref, nfin_ref,
         o_ref, wg_ref, wu_ref, wd_ref, wpg_ref, wp_ref, xn_ref, acc_ref) = rest
    else:
        o_ref, wg_ref, wu_ref, wd_ref, xn_ref, acc_ref = rest
    c = pl.program_id(0)

    @pl.when(c == 0)
    def _():
        xn_ref[...] = _rmsnorm(x_ref[...], g_ref[...]).astype(BF16)
        acc_ref[...] = jnp.zeros(acc_ref.shape, F32)
        if with_ple:
            wpg_ref[...] = wpg32_ref[...].astype(BF16)
            wp_ref[...] = wp32_ref[...].astype(BF16)

    wg_ref[...] = wg32_ref[...].astype(BF16)
    wu_ref[...] = wu32_ref[...].astype(BF16)
    wd_ref[...] = wd32_ref[...].astype(BF16)
    xn = xn_ref[...]
    act = (_silu(_dot(xn, wg_ref[...])) * _dot(xn, wu_ref[...])).astype(BF16)
    acc_ref[...] += _dot(act, wd_ref[...])

    @pl.when(c == pl.num_programs(0) - 1)
    def _():
        h = x_ref[...] + 0.5 * acc_ref[...]
        if with_ple:
            h = _ple_epilogue(h, p_ref, nple_ref, wpg_ref, wp_ref, nfin_ref)
        o_ref[...] = h


def _ffn_stream(x, g, w_gu, w_down, ple=None):
    t, d = x.shape
    d_ff = w_down.shape[0]
    fc = FFN_CHUNK
    n_chunks = d_ff // fc
    whole = lambda a: pl.BlockSpec(a.shape, lambda c: (0,) * a.ndim)
    in_specs = [whole(x), whole(g),
                pl.BlockSpec((d, fc), lambda c: (0, c)),
                pl.BlockSpec((d, fc), lambda c: (0, n_chunks + c)),
                pl.BlockSpec((fc, d), lambda c: (c, 0))]
    args = [x, g, w_gu, w_gu, w_down]
    out_specs = [whole(x),
                 pl.BlockSpec((d, fc), lambda c: (0, c)),
                 pl.BlockSpec((d, fc), lambda c: (0, c)),
                 pl.BlockSpec((fc, d), lambda c: (c, 0))]
    out_shape = [jax.ShapeDtypeStruct((t, d), F32),
                 jax.ShapeDtypeStruct((d, d_ff), BF16),
                 jax.ShapeDtypeStruct((d, d_ff), BF16),
                 jax.ShapeDtypeStruct((d_ff, d), BF16)]
    if ple is not None:
        in_specs += [whole(a) for a in ple]
        args += list(ple)
        out_specs += [whole(ple[2]), whole(ple[3])]
        out_shape += [jax.ShapeDtypeStruct(ple[2].shape, BF16), jax.ShapeDtypeStruct(ple[3].shape, BF16)]
    return pl.pallas_call(
        functools.partial(_ffn_stream_kernel, with_ple=ple is not None),
        grid=(n_chunks,),
        in_specs=in_specs,
        out_specs=out_specs,
        out_shape=out_shape,
        scratch_shapes=[pltpu.VMEM((t, d), BF16), pltpu.VMEM((t, d), F32)],
        compiler_params=_params(1),
        name="ffn_stream_ple" if ple is not None else "ffn_stream",
    )(*args)


PROJ_UNIT = 512


def _proj_layout(d_inner, conv_dim, pool_dim, d_model):
    xbc, z, gates, v = conv_dim, d_inner, N_BRANCH * d_model, pool_dim
    lay = {"xbc": (0, xbc), "z": (xbc, z), "gates": (xbc + z, gates), "v": (xbc + z + gates, v),
           "dt": (xbc + z + gates + v, LANES)}
    for start, width in lay.values():
        assert start % width == 0
    total = -(-(lay["dt"][0] + LANES) // PROJ_UNIT) * PROJ_UNIT
    return lay, total


def _inproj_stream_kernel(rows_ref, valid_ref, h_ref, g_ref, wt_ref, proj_ref, wcat_ref, u_ref):
    c = pl.program_id(0)

    @pl.when(c == 0)
    def _():
        u_ref[...] = _rmsnorm(h_ref[...], g_ref[...]).astype(BF16)

    wt = wt_ref[...]
    row = lax.broadcasted_iota(jnp.int32, wt.shape, 0)
    wcat_ref[...] = jnp.where(row < valid_ref[c], wt, 0.0).T.astype(BF16)
    proj_ref[...] = _dot(u_ref[...], wcat_ref[...])


def _inproj_stream(h, g, w_t, unit_rows, unit_valid):
    t, d = h.shape
    n_units = len(unit_rows)
    total = n_units * PROJ_UNIT
    assert all(r % SUBLANES == 0 and r + PROJ_UNIT <= w_t.shape[0] for r in unit_rows)
    whole = lambda a: pl.BlockSpec(a.shape, lambda c, rows, valid: (0,) * a.ndim)
    return pl.pallas_call(
        _inproj_stream_kernel,
        grid_spec=pltpu.PrefetchScalarGridSpec(
            num_scalar_prefetch=2,
            grid=(n_units,),
            in_specs=[whole(h), whole(g),
                      pl.BlockSpec((pl.Element(PROJ_UNIT), pl.Element(d)),
                                   lambda c, rows, valid: (rows[c] * SUBLANES, 0))],
            out_specs=[pl.BlockSpec((t, PROJ_UNIT), lambda c, rows, valid: (0, c)),
                       pl.BlockSpec((d, PROJ_UNIT), lambda c, rows, valid: (0, c))],
            scratch_shapes=[pltpu.VMEM((t, d), BF16)]),
        out_shape=[jax.ShapeDtypeStruct((t, total), F32),
                   jax.ShapeDtypeStruct((d, total), BF16)],
        compiler_params=_params(1),
        name="inproj_stream",
    )(jnp.asarray([r // SUBLANES for r in unit_rows], jnp.int32),
      jnp.asarray(unit_valid, jnp.int32), h, g, w_t)


def _inproj_mixer_kernel(h_ref, g_ref, w_ref, cw_ref, cb_ref, dtb_ref, alog_ref, dskip_ref, nssd_ref,
                         yn_ref, pooled_ref, gates_ref, conv_ref, pool_ref, ssm_ref,
                         u_ref, raw_ref, vraw_ref, xc_ref, state_ref, y_ref, z_ref, e_ref,
                         *, widths, starts, nc, tiles_per_seq, d_inner):
    i = pl.program_id(0)
    tm = h_ref.shape[0]
    n_slabs = raw_ref.shape[0]

    n_vslabs = vraw_ref.shape[0]
    tile_in_seq = i % tiles_per_seq
    span = SUBLANES * ROW_STRIDE

    @pl.when(tile_in_seq == 0)
    def _():
        raw_ref[:, 0:SUBLANES, :] = jnp.zeros((n_slabs, SUBLANES, LANES), F32)
        vraw_ref[:, 0:POOL_PRE, :] = jnp.zeros((n_vslabs, POOL_PRE, LANES), F32)
        state_ref[...] = jnp.zeros(state_ref.shape, F32)

    u_ref[...] = _rmsnorm(h_ref[...], g_ref[...]).astype(BF16)

    def pool_slab(s):
        cols = slice(s * LANES, (s + 1) * LANES)
        win = POOL_WINDOWS[s * len(POOL_WINDOWS) // n_vslabs]
        for base in range(0, tm, span):
            rows = {d: vraw_ref[s, pl.ds(POOL_PRE + base + d, SUBLANES, stride=ROW_STRIDE), :]
                    for d in range(1 - win, ROW_STRIDE)}
            for j in range(ROW_STRIDE):
                total = rows[j]
                for k in range(1, win):
                    total = total + rows[j - k]
                if base >= POOL_BUF:
                    mean = total * (1.0 / win)
                else:
                    pos = (tile_in_seq * tm + base + j
                           + ROW_STRIDE * lax.broadcasted_iota(jnp.int32, (SUBLANES, LANES), 0))
                    mean = total / jnp.minimum(pos + 1, win).astype(F32)
                pooled_ref[s, pl.ds(base + j, SUBLANES, stride=ROW_STRIDE), :] = mean - rows[j]
        pool_ref[0, :, cols] = vraw_ref[s, POOL_PRE + tm - POOL_BUF:POOL_PRE + tm, :]
        vraw_ref[s, 0:POOL_PRE, :] = vraw_ref[s, tm:tm + POOL_PRE, :]

    def conv_slab(s):
        cols = slice(s * LANES, (s + 1) * LANES)
        taps = [jnp.broadcast_to(cw_ref[k:k + 1, cols], (SUBLANES, LANES)) for k in range(CONV_W)]
        bias = jnp.broadcast_to(cb_ref[:, cols], (SUBLANES, LANES))
        for base in range(0, tm, span):
            shifted = [raw_ref[s, pl.ds(SUBLANES + base - (CONV_W - 1) + m, SUBLANES,
                                        stride=ROW_STRIDE), :]
                       for m in range(ROW_STRIDE + CONV_W - 1)]
            for j in range(ROW_STRIDE):
                acc = bias
                for k in range(CONV_W):
                    acc = acc + shifted[j + k] * taps[k]
                xc_ref[s, pl.ds(base + j, SUBLANES, stride=ROW_STRIDE), :] = _silu(acc)
        conv_ref[0, :, cols] = raw_ref[s, SUBLANES + tm - (CONV_W - 1):SUBLANES + tm, :]
        raw_ref[s, 0:SUBLANES, :] = raw_ref[s, tm:tm + SUBLANES, :]

    def project(k, a, width=nc):
        b = min(a + width, widths[k])
        return _dot(u_ref[...], w_ref[:, starts[k] + a:starts[k] + b])

    n_xbc_units = widths[1] // nc
    slabs_per_unit = nc // LANES
    done = 0
    for n, (k, a) in enumerate([(k, a) for k in (1, 2) for a in range(0, widths[k], nc)]):
        res = project(k, a)
        for s in range(res.shape[1] // LANES):
            piece = res[:, s * LANES:(s + 1) * LANES]
            if k == 1:
                raw_ref[a // LANES + s, SUBLANES:SUBLANES + tm, :] = piece
            else:
                vraw_ref[a // LANES + s, POOL_PRE:POOL_PRE + tm, :] = piece
                pool_slab(a // LANES + s)
        while done < min(n_slabs, n * slabs_per_unit, min(n + 1, n_xbc_units) * slabs_per_unit):
            conv_slab(done)
            done += 1
    dt_raw = project(4, 0)
    while done < n_slabs:
        conv_slab(done)
        done += 1

    q = CHUNK
    n_state = D_STATE
    assert q == LANES and n_state == LANES
    late_nc = nc // 2
    late_units = [(k, a) for k in (0, 3) for a in range(0, widths[k], late_nc)]
    late_outs = {0: z_ref, 3: gates_ref}
    n_blocks = (tm // q) * N_GROUPS
    emitted = 0
    scalars = []
    for ci in range(tm // q):
        terms = _chunk_scalars(dt_raw[ci * q:(ci + 1) * q, :], dtb_ref[...], alog_ref[...], q)
        scalars.append((terms, jnp.exp2(terms.seg2).T))
    tiles_per_group = d_inner // N_GROUPS // LANES

    def emit_late_units(block, of):
        nonlocal emitted
        while emitted < len(late_units) and emitted * of < block * len(late_units):
            k, a = late_units[emitted]
            res = project(k, a, late_nc)
            late_outs[k][:, a:a + res.shape[1]] = res
            emitted += 1

    for ci in range(tm // q):
        rows = slice(ci * q, (ci + 1) * q)
        terms, _ = scalars[ci]
        for g in range(N_GROUPS):
            b_g = xc_ref[d_inner // LANES + g, rows, :].astype(BF16)
            c_g = xc_ref[d_inner // LANES + N_GROUPS + g, rows, :].astype(BF16)
            cb = _dot_nt(c_g, b_g)
            for p in range(tiles_per_group):
                tile = g * tiles_per_group + p
                cols = slice(tile * LANES, (tile + 1) * LANES)
                xpair = xc_ref[tile, rows, :]
                y, e_pair = _diag_pair(xpair, cb, terms, tile * HEADS_PER_TILE)
                y_ref[rows, cols] = y + dskip_ref[:, cols] * xpair
                e_ref[rows, cols] = e_pair
            emit_late_units(ci * N_GROUPS + g + 1, 2 * n_blocks)
    for ci in range(tm // q):
        rows = slice(ci * q, (ci + 1) * q)
        terms, chunk_decay_t = scalars[ci]
        for g in range(N_GROUPS):
            b_g = xc_ref[d_inner // LANES + g, rows, :].astype(BF16)
            c_g = xc_ref[d_inner // LANES + N_GROUPS + g, rows, :].astype(BF16)
            for p in range(tiles_per_group):
                tile = g * tiles_per_group + p
                h0 = tile * HEADS_PER_TILE
                cols = slice(tile * LANES, (tile + 1) * LANES)
                xpair = xc_ref[tile, rows, :]
                h_prev = state_ref[cols, :]
                y_ref[rows, cols] += _dot_nt(c_g, h_prev.astype(BF16)) * e_ref[rows, cols]
                xw_t = (xpair.T * _pair_rows(terms.w_state_t, h0, q)).astype(BF16)
                decay_rows = _pair_rows(chunk_decay_t, h0, n_state)
                state_ref[cols, :] = decay_rows * h_prev + _dot(xw_t, b_g)
            emit_late_units(n_blocks + ci * N_GROUPS + g + 1, 2 * n_blocks)
    assert emitted == len(late_units)

    _gate_and_norm(y_ref, z_ref, nssd_ref, yn_ref)

    @pl.when(tile_in_seq == tiles_per_seq - 1)
    def _():
        ssm_ref[0] = state_ref[...]


def _inproj_mixer(h, g, w_cat, layout, conv_w, conv_b, dt_bias, a_log, d_skip, norm_ssd,
                  *, tm, seq, d_inner):
    order = ("z", "xbc", "v", "gates", "dt")
    starts = tuple(layout[k][0] for k in order)
    widths = tuple(layout[k][1] for k in order)
    t, d = h.shape
    tm = min(tm, seq)
    tiles_per_seq = seq // tm
    n_all = w_cat.shape[1]
    conv_dim, pool_dim = widths[1], widths[2]
    n_slabs = conv_dim // LANES
    assert tm % (SUBLANES * ROW_STRIDE) == 0 and tm % CHUNK == 0
    row = lambda w: pl.BlockSpec((tm, w), lambda i: (i, 0))
    per_seq = lambda i: (i // tiles_per_seq, 0, 0)
    return pl.pallas_call(
        functools.partial(_inproj_mixer_kernel, widths=widths, starts=starts, nc=PROJ_UNIT,
                          tiles_per_seq=tiles_per_seq, d_inner=d_inner),
        grid=(t // tm,),
        in_specs=[row(d),
                  _const_spec((1, d)),
                  _const_spec((d, n_all)),
                  _const_spec((CONV_W, conv_dim)),
                  _const_spec((1, conv_dim)),
                  _const_spec((1, LANES)),
                  _const_spec((1, LANES)),
                  _const_spec((1, d_inner)),
                  _const_spec((1, d_inner))],
        out_specs=[row(d_inner),
                   pl.BlockSpec((pool_dim // LANES, tm, LANES), lambda i: (0, i, 0)),
                   row(widths[3]),
                   pl.BlockSpec((1, CONV_W - 1, conv_dim), per_seq),
                   pl.BlockSpec((1, POOL_BUF, pool_dim), per_seq),
                   pl.BlockSpec((1, d_inner, D_STATE), per_seq)],
        out_shape=[jax.ShapeDtypeStruct((t, d_inner), BF16),
                   jax.ShapeDtypeStruct((pool_dim // LANES, t, LANES), F32),
                   jax.ShapeDtypeStruct((t, widths[3]), F32),
                   jax.ShapeDtypeStruct((t // seq, CONV_W - 1, conv_dim), F32),
                   jax.ShapeDtypeStruct((t // seq, POOL_BUF, pool_dim), F32),
                   jax.ShapeDtypeStruct((t // seq, d_inner, D_STATE), F32)],
        scratch_shapes=[pltpu.VMEM((tm, d), BF16),
                        pltpu.VMEM((n_slabs, SUBLANES + tm, LANES), F32),
                        pltpu.VMEM((pool_dim // LANES, POOL_PRE + tm, LANES), F32),
                        pltpu.VMEM((n_slabs, tm, LANES), F32),
                        pltpu.VMEM((d_inner, D_STATE), F32),
                        pltpu.VMEM((tm, d_inner), F32),
                        pltpu.VMEM((tm, widths[0]), F32),
                        pltpu.VMEM((tm, d_inner), F32)],
        compiler_params=_params(1),
        name="inproj_mixer",
    )(h, g, w_cat, conv_w, conv_b, dt_bias, a_log, d_skip, norm_ssd)


def _chunk_scalars(dt_raw, dt_bias, a_log, seg_len):
    q = dt_raw.shape[0]
    dt = _softplus(dt_raw + dt_bias)
    da = dt * (-jnp.exp(a_log))
    shift = int(math.log2(seg_len))
    qi = lax.broadcasted_iota(jnp.int32, (q, q), 0)
    si = lax.broadcasted_iota(jnp.int32, (q, q), 1)
    same = (qi >> shift) == (si >> shift)
    causal = jnp.logical_and(same, si <= qi)
    a_cs = jnp.dot(jnp.where(causal, 1.0, 0.0), da, precision=HIGHEST, preferred_element_type=F32)
    seg_tot = jnp.dot(jnp.where(same, 1.0, 0.0), da, precision=HIGHEST, preferred_element_type=F32)
    a2 = a_cs * LOG2E
    seg2 = seg_tot * LOG2E
    return _ChunkTerms(
        a2=a2,
        decay_in_t=(a2 - jnp.log2(dt)).T,
        w_state_t=(jnp.exp2(seg2 - a2) * dt).T,
        seg2=seg2,
        causal=causal)


class _ChunkTerms(NamedTuple):
    a2: jax.Array
    decay_in_t: jax.Array
    w_state_t: jax.Array
    seg2: jax.Array
    causal: jax.Array


def _pair_rows(vals_t, h0, cols):
    return jnp.concatenate([jnp.broadcast_to(vals_t[h0 + k:h0 + k + 1, :], (HEAD_DIM, cols))
                            for k in range(HEADS_PER_TILE)], axis=0)


def _diag_pair(xpair, cb, terms, h0):
    q = xpair.shape[0]
    lane = lax.broadcasted_iota(jnp.int32, xpair.shape, 1)
    out = None
    spread = []
    for k in range(HEADS_PER_TILE):
        h = h0 + k
        a_col = jnp.broadcast_to(terms.a2[:, h:h + 1], (q, LANES))
        spread.append(a_col)
        decay_dt = jnp.exp2(jnp.where(terms.causal, a_col - terms.decay_in_t[h:h + 1, :], -jnp.inf))
        w = (cb * decay_dt).astype(BF16)
        mine = (lane < HEAD_DIM) if k == 0 else (lane >= HEAD_DIM)
        t = _dot(w, jnp.where(mine, xpair, 0.0).astype(BF16))
        out = t if out is None else out + t
    e_pair = jnp.exp2(jnp.where(lane < HEAD_DIM, spread[0], spread[1]))
    return out, e_pair


def _conv_tile_rows(rows_of_tap, w, bias):
    acc = rows_of_tap(0) * w[0:1, :]
    for k in range(1, CONV_W):
        acc = acc + rows_of_tap(k) * w[k:k + 1, :]
    return _silu(bias + acc)


def _mixer_sample_kernel(xbc_ref, dt_ref, v_ref, conv0_ref, pool0_ref, ssm0_ref,
                         cw_ref, cb_ref, dtb_ref, alog_ref, dskip_ref,
                         y_ref, pooled_ref, ssm_ref, conv_ref, pool_ref,
                         head_ref, xc_ref, xwt_ref, e_ref, seg_ref, pe_ref,
                         pooled_sc_ref, conv_sc_ref, pool_sc_ref,
                         *, d_inner, seq, pos0, seqs_per_step, n_chunks):
    c = pl.program_id(0)
    j = pl.program_id(1)
    q = xbc_ref.shape[0]
    conv_dim = xbc_ref.shape[1]
    pool_dim = v_ref.shape[1]
    n_state = D_STATE
    tiles_per_group = d_inner // N_GROUPS // LANES
    gc = pool_dim // len(POOL_WINDOWS)

    @pl.when(c < n_chunks)
    def _():
        slot = c % 2
        lo = SUBLANES - (CONV_W - 1)

        def per_seq(b, carry):
            r0 = pl.multiple_of(b * seq, seq)
            for k in range(CONV_W - 1):
                head_ref[lo + k:lo + k + 1, :] = conv0_ref[k, pl.ds(b, 1), :]
            ct = 1024
            for jc in range(conv_dim // ct):
                cols = slice(jc * ct, (jc + 1) * ct)
                head_ref[SUBLANES:SUBLANES + seq, cols] = xbc_ref[pl.ds(r0, seq), cols]
                xc_ref[slot, pl.ds(r0, seq), cols] = _conv_tile_rows(
                    lambda k: head_ref[lo + k:lo + k + seq, cols], cw_ref[:, cols], cb_ref[:, cols])
            for k in range(CONV_W - 1):
                r = SUBLANES + seq - (CONV_W - 1) + k
                conv_sc_ref[slot, k, pl.ds(b, 1), :] = head_ref[r:r + 1, :]

            for k in range(POOL_BUF):
                r = POOL_PAD - POOL_BUF + k
                pe_ref[r:r + 1, :] = pool0_ref[k, pl.ds(b, 1), :]
            vb = v_ref[pl.ds(r0, seq), :]
            pe_ref[POOL_PAD:POOL_PAD + seq, :] = vb
            t_idx = lax.broadcasted_iota(jnp.int32, (seq, 1), 0) + pos0
            for gi, win in enumerate(POOL_WINDOWS):
                cols = slice(gi * gc, (gi + 1) * gc)
                total = pe_ref[POOL_PAD:POOL_PAD + seq, cols]
                for k in range(1, win):
                    total = total + pe_ref[POOL_PAD - k:POOL_PAD - k + seq, cols]
                cnt = jnp.minimum(t_idx + 1, win).astype(F32)
                pooled = total / cnt - vb[:, cols]
                for s in range(gc // LANES):
                    pooled_sc_ref[slot, gi * (gc // LANES) + s, pl.ds(r0, seq), :] = (
                        pooled[:, s * LANES:(s + 1) * LANES])
            for k in range(POOL_BUF):
                r = POOL_PAD + seq - POOL_BUF + k
                pool_sc_ref[slot, k, pl.ds(b, 1), :] = pe_ref[r:r + 1, :]
            return carry

        lax.fori_loop(j * seqs_per_step, (j + 1) * seqs_per_step, per_seq, 0)

    @pl.when(c >= 1)
    def _():
        slot = (c + 1) % 2
        xc = xc_ref.at[slot]

        @pl.when(j == 0)
        def _():
            pooled_ref[...] = pooled_sc_ref[slot]
            conv_ref[...] = conv_sc_ref[slot]
            pool_ref[...] = pool_sc_ref[slot]
            terms = _chunk_scalars(dt_ref[...], dtb_ref[...], alog_ref[...], seq)
            seg_ref[...] = terms.seg2
            for g in range(N_GROUPS):
                b_g = xc[:, d_inner + g * n_state:d_inner + (g + 1) * n_state].astype(BF16)
                c_g = xc[:, d_inner + (N_GROUPS + g) * n_state:
                         d_inner + (N_GROUPS + g + 1) * n_state].astype(BF16)
                cb = _dot_nt(c_g, b_g)
                for p in range(tiles_per_group):
                    tile = g * tiles_per_group + p
                    h0 = tile * HEADS_PER_TILE
                    cols = slice(tile * LANES, (tile + 1) * LANES)
                    xpair = xc[:, cols]
                    y, e_pair = _diag_pair(xpair, cb, terms, h0)
                    y_ref[:, cols] = y + dskip_ref[:, cols] * xpair
                    e_ref[:, cols] = e_pair
                    xwt_ref[cols, :] = (xpair.T * _pair_rows(terms.w_state_t, h0, q)).astype(BF16)

        row_q = lax.broadcasted_iota(jnp.int32, (q, n_state), 0)
        row128 = lax.broadcasted_iota(jnp.int32, (LANES, n_state), 0)
        for sj in range(seqs_per_step):
            r0 = pl.multiple_of((j * seqs_per_step + sj) * seq, seq)
            chunk_decay = jnp.exp2(seg_ref[pl.ds(r0, 1), :])
            mine = jnp.logical_and(row_q >= r0, row_q < r0 + seq)
            for g in range(N_GROUPS):
                b_cols = slice(d_inner + g * n_state, d_inner + (g + 1) * n_state)
                c_cols = slice(d_inner + (N_GROUPS + g) * n_state,
                               d_inner + (N_GROUPS + g + 1) * n_state)
                b_mine = jnp.where(mine, xc[:, b_cols], 0.0).astype(BF16)
                c_rows = xc[pl.ds(r0, seq), c_cols].astype(BF16)
                for p in range(tiles_per_group):
                    tile = g * tiles_per_group + p
                    h0 = tile * HEADS_PER_TILE
                    cols = slice(tile * LANES, (tile + 1) * LANES)
                    h_prev = ssm0_ref[sj, cols, :]
                    y_off = _dot_nt(c_rows, h_prev.astype(BF16)) * e_ref[pl.ds(r0, seq), cols]
                    y_ref[pl.ds(r0, seq), cols] += y_off
                    decay_rows = jnp.where(row128 < HEAD_DIM, chunk_decay[:, h0:h0 + 1],
                                           chunk_decay[:, h0 + 1:h0 + 2])
                    ssm_ref[sj, cols, :] = decay_rows * h_prev + _dot(xwt_ref[cols, :], b_mine)


def _mixer_sample(proj, layout, conv0, pool0, ssm0, conv_w, conv_b, dt_bias, a_log, d_skip,
                  *, batch, seq, d_inner, pos0):
    q = CHUNK
    n_seq = q // seq
    n_chunks = batch // n_seq
    conv_dim = layout["xbc"][1]
    pool_dim = layout["v"][1]
    sps = math.gcd(n_seq, 4)
    steps = n_seq // sps
    prep = lambda c: jnp.minimum(c, n_chunks - 1)
    scan = lambda c: jnp.maximum(c - 1, 0)
    window = lambda k, chunk: pl.BlockSpec(
        (q, layout[k][1]), lambda c, j: (chunk(c), layout[k][0] // layout[k][1]))
    tok = lambda c, j: (scan(c), 0)
    per_chunk_in = lambda c, j: (0, prep(c), 0)
    per_chunk = lambda c, j: (0, scan(c), 0)
    per_seq = lambda c, j: (jnp.where(c == 0, 0, (c - 1) * steps + j), 0, 0)
    return pl.pallas_call(
        functools.partial(_mixer_sample_kernel, d_inner=d_inner, seq=seq, pos0=pos0,
                          seqs_per_step=sps, n_chunks=n_chunks),
        grid=(n_chunks + 1, steps),
        in_specs=[window("xbc", prep), window("dt", scan), window("v", prep),
                  pl.BlockSpec((CONV_W - 1, n_seq, conv_dim), per_chunk_in),
                  pl.BlockSpec((POOL_BUF, n_seq, pool_dim), per_chunk_in),
                  pl.BlockSpec((sps, d_inner, D_STATE), per_seq),
                  _const_spec((CONV_W, conv_dim)),
                  _const_spec((1, conv_dim)),
                  _const_spec((1, LANES)),
                  _const_spec((1, LANES)),
                  _const_spec((1, d_inner))],
        out_specs=[pl.BlockSpec((q, d_inner), tok),
                   pl.BlockSpec((pool_dim // LANES, q, LANES), per_chunk),
                   pl.BlockSpec((sps, d_inner, D_STATE), per_seq),
                   pl.BlockSpec((CONV_W - 1, n_seq, conv_dim), per_chunk),
                   pl.BlockSpec((POOL_BUF, n_seq, pool_dim), per_chunk)],
        out_shape=[jax.ShapeDtypeStruct((batch * seq, d_inner), F32),
                   jax.ShapeDtypeStruct((pool_dim // LANES, batch * seq, LANES), F32),
                   jax.ShapeDtypeStruct((batch, d_inner, D_STATE), F32),
                   jax.ShapeDtypeStruct((CONV_W - 1, batch, conv_dim), F32),
                   jax.ShapeDtypeStruct((POOL_BUF, batch, pool_dim), F32)],
        scratch_shapes=[pltpu.VMEM((2 * SUBLANES, conv_dim), F32),
                        pltpu.VMEM((2, q, conv_dim), F32),
                        pltpu.VMEM((d_inner, q), BF16),
                        pltpu.VMEM((q, d_inner), F32),
                        pltpu.VMEM((q, LANES), F32),
                        pltpu.VMEM((POOL_PAD + seq, pool_dim), F32),
                        pltpu.VMEM((2, pool_dim // LANES, q, LANES), F32),
                        pltpu.VMEM((2, CONV_W - 1, n_seq, conv_dim), F32),
                        pltpu.VMEM((2, POOL_BUF, n_seq, pool_dim), F32)],
        compiler_params=_params(2),
        name="mixer_sample",
    )(proj, proj, proj, conv0, pool0, ssm0, conv_w, conv_b, dt_bias, a_log, d_skip)


def _gate_and_norm(y_ref, z_ref, nssd_ref, yn_ref):
    gw = y_ref.shape[1] // N_GROUPS
    for g in range(N_GROUPS):
        cols = slice(g * gw, (g + 1) * gw)
        yz = y_ref[:, cols] * _silu(z_ref[:, cols])
        yn_ref[:, cols] = _rmsnorm(yz, nssd_ref[:, cols]).astype(BF16)


def _merge_kernel(h_ref, *rest, gated, cast_weights):
    if gated:
        yn_ref, pooled_ref, gates_ref, pscale_ref, *rest = rest
    else:
        y_ref, z_ref, pooled_ref, gates_ref, nssd_ref, pscale_ref, *rest = rest
    if cast_weights:
        w32_refs, (o_ref, *w_refs), scratch = rest[:4], rest[4:9], rest[9:]

        @pl.when(pl.program_id(0) == 0)
        def _():
            for dst, src in zip(w_refs, w32_refs):
                dst[...] = src[...].astype(BF16)
    else:
        w_refs, o_ref, scratch = rest[:4], rest[4], rest[5:]
    wssd_ref, wpg_ref, wpo_ref, wo_ref = w_refs
    d_model = h_ref.shape[1]
    if gated:
        (pm_ref,) = scratch
    else:
        yn_ref, pm_ref = scratch
        _gate_and_norm(y_ref, z_ref, nssd_ref, yn_ref)
    a_branch = _dot(yn_ref[...], wssd_ref[...])
    slabs_per_group = pooled_ref.shape[0] // len(POOL_WINDOWS)
    gc = slabs_per_group * LANES
    for gi in range(len(POOL_WINDOWS)):
        cols = slice(gi * gc, (gi + 1) * gc)
        pooled = jnp.concatenate([pooled_ref[gi * slabs_per_group + s]
                                  for s in range(slabs_per_group)], axis=1)
        mixed = _dot(pooled.astype(BF16), wpg_ref[gi])
        pm_ref[:, cols] = (mixed * pscale_ref[:, cols]).astype(BF16)
    b_branch = _dot(pm_ref[...], wpo_ref[...])
    merged = (_sigmoid(gates_ref[:, 0:d_model]) * a_branch
              + _sigmoid(gates_ref[:, d_model:2 * d_model]) * b_branch)
    o_ref[...] = h_ref[...] + _dot(merged.astype(BF16), wo_ref[...])


def _merge(h, ssd, pooled, gates_src, pool_scale, weights, *, tm, cast_weights):
    t, d = h.shape
    tm = min(tm, t)
    gated = not isinstance(ssd, tuple)
    pool_slabs = pooled.shape[0]
    pool_dim = pool_slabs * LANES
    row = lambda w, blk=0: pl.BlockSpec((tm, w), lambda i: (i, blk))
    pooled_spec = pl.BlockSpec((pool_slabs, tm, LANES), lambda i: (0, i, 0))
    gates_spec = row(N_BRANCH * d, gates_src[1])
    scratch = [pltpu.VMEM((tm, pool_dim), BF16)]
    if gated:
        d_inner = ssd.shape[1]
        in_specs = [row(d), row(d_inner), pooled_spec, gates_spec, _const_spec((1, pool_dim))]
        args = [h, ssd, pooled, gates_src[0], pool_scale]
    else:
        y, z_src, norm_ssd = ssd
        d_inner = y.shape[1]
        in_specs = [row(d), row(d_inner), row(d_inner, z_src[1]), pooled_spec, gates_spec,
                    _const_spec((1, d_inner)), _const_spec((1, pool_dim))]
        args = [h, y, z_src[0], pooled, gates_src[0], norm_ssd, pool_scale]
        scratch = [pltpu.VMEM((tm, d_inner), BF16)] + scratch
    out_specs = [row(d)]
    out_shape = [jax.ShapeDtypeStruct((t, d), F32)]
    if cast_weights:
        out_specs += [pl.BlockSpec(w.shape, lambda i, nd=w.ndim: (0,) * nd) for w in weights]
        out_shape += [jax.ShapeDtypeStruct(w.shape, BF16) for w in weights]
    res = pl.pallas_call(
        functools.partial(_merge_kernel, gated=gated, cast_weights=cast_weights),
        grid=(t // tm,),
        in_specs=in_specs + [_const_spec(w.shape) for w in weights],
        out_specs=out_specs,
        out_shape=out_shape,
        scratch_shapes=scratch,
        compiler_params=_params(1),
        name="merge_cast" if cast_weights else "merge",
    )(*args, *weights)
    return res if cast_weights else res[0]


def _pad_lanes(a):
    return jnp.pad(a.reshape(1, -1), ((0, 0), (0, LANES - a.shape[-1])))


def kernel(x_prompt, x_sample, state_ssm, state_conv, state_pool, p_prompt, p_sample, norm_ffn1, w_ffn1_gu, w_ffn1_down, norm_mix, w_in, conv_w, conv_b, dt_bias, a_log, d_skip, norm_ssd, w_ssd_out, w_pool_group, pool_scale, w_pool_out, w_o, norm_ffn2, w_ffn2_gu, w_ffn2_down, norm_ple, w_ple_gate, w_ple, norm_final):
    depth = norm_ffn1.shape[0]
    assert depth == 1, "the final norm is fused into the layer's last stage: one layer only"
    batch, seq, d_model = x_prompt.shape
    dec_batch, dec_seq, _ = x_sample.shape
    n_heads = dt_bias.shape[1]
    d_inner = n_heads * HEAD_DIM
    conv_dim = conv_w.shape[2]
    pool_dim = pool_scale.shape[1]
    assert seq % CHUNK == 0 and CHUNK % dec_seq == 0 and dec_batch % (CHUNK // dec_seq) == 0

    row = lambda a: a[0].reshape(1, -1)
    layout, proj_cols = _proj_layout(d_inner, conv_dim, pool_dim, d_model)
    src = {"z": (0, d_inner), "xbc": (d_inner, conv_dim), "dt": (d_inner + conv_dim, n_heads),
           "v": (d_inner + conv_dim + n_heads, pool_dim),
           "gates": (d_inner + conv_dim + n_heads + pool_dim, N_BRANCH * d_model)}
    unit_rows, unit_valid = [], []
    for name in sorted(layout, key=lambda k: layout[k][0]):
        start, width = src[name]
        assert layout[name][0] == len(unit_rows) * PROJ_UNIT
        for off in range(0, width, PROJ_UNIT):
            unit_rows.append(start + off)
            unit_valid.append(min(PROJ_UNIT, width - off))
    assert len(unit_rows) * PROJ_UNIT == proj_cols
    dtb, alog = _pad_lanes(dt_bias[0]), _pad_lanes(a_log[0])
    dskip = jnp.repeat(d_skip[0], HEAD_DIM).reshape(1, d_inner)
    cb = row(conv_b)
    ple_consts = (row(norm_ple), w_ple_gate[0], w_ple[0], norm_final.reshape(1, -1))

    h1, wg1, wu1, wd1 = _ffn_stream(x_sample.reshape(dec_batch * dec_seq, d_model), row(norm_ffn1),
                                    w_ffn1_gu[0], w_ffn1_down[0])
    proj, w_cat = _inproj_stream(h1, row(norm_mix), w_in[0].T, tuple(unit_rows), tuple(unit_valid))
    y, pooled, s2, c2, q2 = _mixer_sample(
        proj, layout, jnp.transpose(state_conv[0], (1, 0, 2)), jnp.transpose(state_pool[0], (1, 0, 2)),
        state_ssm[0].reshape(dec_batch, d_inner, D_STATE),
        conv_w[0], cb, dtb, alog, dskip,
        batch=dec_batch, seq=dec_seq, d_inner=d_inner, pos0=PAST_LEN)
    block_of = lambda k: layout[k][0] // layout[k][1]
    h2, wssd, wpg, wpo, wo = _merge(
        h1, (y, (proj, block_of("z")), row(norm_ssd)), pooled, (proj, block_of("gates")),
        row(pool_scale), (w_ssd_out[0], w_pool_group[0], w_pool_out[0], w_o[0]),
        tm=128, cast_weights=True)
    y_sample, wg2, wu2, wd2, wpleg, wple = _ffn_stream(
        h2, row(norm_ffn2), w_ffn2_gu[0], w_ffn2_down[0],
        (p_sample[0].reshape(dec_batch * dec_seq, -1),) + ple_consts)

    h1 = _ffn(x_prompt.reshape(batch * seq, d_model), row(norm_ffn1), wg1, wu1, wd1, tm=1024)
    yn, pooled, gates, c1, q1, s1 = _inproj_mixer(h1, row(norm_mix), w_cat, layout, conv_w[0], cb,
                                                  dtb, alog, dskip, row(norm_ssd),
                                                  tm=256, seq=seq, d_inner=d_inner)
    h2 = _merge(h1, yn, pooled, (gates, 0), row(pool_scale), (wssd, wpg, wpo, wo),
                tm=512, cast_weights=False)
    y_prompt = _ffn(h2, row(norm_ffn2), wg2, wu2, wd2,
                    (p_prompt[0].reshape(batch * seq, -1), ple_consts[0], wpleg, wple, ple_consts[3]),
                    tm=1024)

    return (y_prompt.reshape(batch, seq, d_model),
            y_sample.reshape(dec_batch, dec_seq, d_model),
            s1.reshape(1, batch, n_heads, HEAD_DIM, D_STATE), c1[None], q1[None],
            s2.reshape(1, dec_batch, n_heads, HEAD_DIM, D_STATE),
            jnp.transpose(c2, (1, 0, 2))[None], jnp.transpose(q2, (1, 0, 2))[None])
```

```python
import functools
import math
from typing import NamedTuple

import jax
import jax.numpy as jnp
from jax import lax
from jax.experimental import pallas as pl
from jax.experimental.pallas import tpu as pltpu

F32 = jnp.float32
BF16 = jnp.bfloat16
EPS = 1e-6
HIGHEST = lax.Precision.HIGHEST
LOG2E = 1.4426950408889634
NEG_LOG2E = -LOG2E

LANES = 128
SUBLANES = 8
VMEM_LIMIT_BYTES = 56 * 1024 * 1024
SAMPLE_MIXER_VMEM_LIMIT_BYTES = 60 * 1024 * 1024

HEAD_DIM = 64
N_GROUPS = 8
D_STATE = 128
CONV_W = 4
CHUNK = 128
POOL_WINDOWS = (2, 4, 8, 16)
POOL_BUF = max(POOL_WINDOWS) - 1
N_BRANCH = 2
PAST_LEN = 16384
HEADS_PER_TILE = LANES // HEAD_DIM
ROW_STRIDE = 4
POOL_PAD = 24
POOL_PRE = 16


def _sigmoid(x):
    return 1.0 / (1.0 + jnp.exp2(x * NEG_LOG2E))


def _silu(x):
    half = 0.5 * x
    return half + half * jnp.tanh(half)


def _softplus(x):
    return jnp.maximum(x, 0.0) + jnp.log(1.0 + jnp.exp(-jnp.abs(x)))


def _rmsnorm(x, g):
    return x * lax.rsqrt(jnp.mean(x * x, axis=-1, keepdims=True) + EPS) * g


def _dot(a, b):
    return jnp.dot(a, b, preferred_element_type=F32)


def _dot_nt(a, b):
    return lax.dot_general(a, b, (((1,), (1,)), ((), ())), preferred_element_type=F32)


def _const_spec(shape):
    nd = len(shape)
    return pl.BlockSpec(shape, lambda *_: (0,) * nd, pipeline_mode=pl.Buffered(1))


def _params(n_axes, vmem_limit_bytes=VMEM_LIMIT_BYTES):
    return pltpu.CompilerParams(dimension_semantics=("arbitrary",) * n_axes,
                                vmem_limit_bytes=vmem_limit_bytes)


FFN_CHUNK = 256


def _ple_epilogue(h, p_ref, nple_ref, wg_ref, wp_ref, nfin_ref):
    gate = _sigmoid(_dot(_rmsnorm(h, nple_ref[...]).astype(BF16), wg_ref[...]))
    h = h + gate * _dot(p_ref[...].astype(BF16), wp_ref[...])
    return _rmsnorm(h, nfin_ref[...])


def _ffn_kernel(x_ref, g_ref, wg_ref, wu_ref, wd_ref, *rest, fc, with_ple):
    if with_ple:
        p_ref, nple_ref, wpg_ref, wp_ref, nfin_ref, o_ref, xn_ref, acc_ref = rest
    else:
        o_ref, xn_ref, acc_ref = rest
    xn_ref[...] = _rmsnorm(x_ref[...], g_ref[...]).astype(BF16)
    for c in range(wd_ref.shape[0] // fc):
        cols = slice(c * fc, (c + 1) * fc)
        xn = xn_ref[...]
        act = (_silu(_dot(xn, wg_ref[:, cols])) * _dot(xn, wu_ref[:, cols])).astype(BF16)
        contrib = _dot(act, wd_ref[cols, :])
        if c == 0:
            acc_ref[...] = contrib
        else:
            acc_ref[...] += contrib
    h = x_ref[...] + 0.5 * acc_ref[...]
    if with_ple:
        h = _ple_epilogue(h, p_ref, nple_ref, wpg_ref, wp_ref, nfin_ref)
    o_ref[...] = h


def _ffn(x, g, wg, wu, wd, ple=None, *, tm):
    t, d = x.shape
    tm = min(tm, t)
    row = lambda w: pl.BlockSpec((tm, w), lambda i: (i, 0))
    consts = [g, wg, wu, wd]
    in_specs = [row(d)] + [_const_spec(a.shape) for a in consts]
    args = [x] + consts
    if ple is not None:
        in_specs += [row(ple[0].shape[1])] + [_const_spec(a.shape) for a in ple[1:]]
        args += list(ple)
    return pl.pallas_call(
        functools.partial(_ffn_kernel, fc=FFN_CHUNK, with_ple=ple is not None),
        grid=(t // tm,),
        in_specs=in_specs,
        out_specs=row(d),
        out_shape=jax.ShapeDtypeStruct((t, d), F32),
        scratch_shapes=[pltpu.VMEM((tm, d), BF16), pltpu.VMEM((tm, d), F32)],
        compiler_params=_params(1),
        name="ffn_ple" if ple is not None else "ffn",
    )(*args)


def _ffn_stream_kernel(x_ref, g_ref, wg32_ref, wu32_ref, wd32_ref, *rest, with_ple):
    if with_ple:
        (p_ref, nple_ref, wpg32_ref, wp32_ref, nfin_ref,
         o_ref, wg_ref, wu_ref, wd_ref, wpg_ref, wp_ref, xn_ref, acc_ref) = rest
    else:
        o_ref, wg_ref, wu_ref, wd_ref, xn_ref, acc_ref = rest
    c = pl.program_id(0)

    @pl.when(c == 0)
    def _():
        xn_ref[...] = _rmsnorm(x_ref[...], g_ref[...]).astype(BF16)
        acc_ref[...] = jnp.zeros(acc_ref.shape, F32)
        if with_ple:
            wpg_ref[...] = wpg32_ref[...].astype(BF16)
            wp_ref[...] = wp32_ref[...].astype(BF16)

    wg_ref[...] = wg32_ref[...].astype(BF16)
    wu_ref[...] = wu32_ref[...].astype(BF16)
    wd_ref[...] = wd32_ref[...].astype(BF16)
    xn = xn_ref[...]
    act = (_silu(_dot(xn, wg_ref[...])) * _dot(xn, wu_ref[...])).astype(BF16)
    acc_ref[...] += _dot(act, wd_ref[...])

    @pl.when(c == pl.num_programs(0) - 1)
    def _():
        h = x_ref[...] + 0.5 * acc_ref[...]
        if with_ple:
            h = _ple_epilogue(h, p_ref, nple_ref, wpg_ref, wp_ref, nfin_ref)
        o_ref[...] = h


def _ffn_stream(x, g, w_gu, w_down, ple=None):
    t, d = x.shape
    d_ff = w_down.shape[0]
    fc = FFN_CHUNK
    n_chunks = d_ff // fc
    whole = lambda a: pl.BlockSpec(a.shape, lambda c: (0,) * a.ndim)
    in_specs = [whole(x), whole(g),
                pl.BlockSpec((d, fc), lambda c: (0, c)),
                pl.BlockSpec((d, fc), lambda c: (0, n_chunks + c)),
                pl.BlockSpec((fc, d), lambda c: (c, 0))]
    args = [x, g, w_gu, w_gu, w_down]
    out_specs = [whole(x),
                 pl.BlockSpec((d, fc), lambda c: (0, c)),
                 pl.BlockSpec((d, fc), lambda c: (0, c)),
                 pl.BlockSpec((fc, d), lambda c: (c, 0))]
    out_shape = [jax.ShapeDtypeStruct((t, d), F32),
                 jax.ShapeDtypeStruct((d, d_ff), BF16),
                 jax.ShapeDtypeStruct((d, d_ff), BF16),
                 jax.ShapeDtypeStruct((d_ff, d), BF16)]
    if ple is not None:
        in_specs += [whole(a) for a in ple]
        args += list(ple)
        out_specs += [whole(ple[2]), whole(ple[3])]
        out_shape += [jax.ShapeDtypeStruct(ple[2].shape, BF16), jax.ShapeDtypeStruct(ple[3].shape, BF16)]
    return pl.pallas_call(
        functools.partial(_ffn_stream_kernel, with_ple=ple is not None),
        grid=(n_chunks,),
        in_specs=in_specs,
        out_specs=out_specs,
        out_shape=out_shape,
        scratch_shapes=[pltpu.VMEM((t, d), BF16), pltpu.VMEM((t, d), F32)],
        compiler_params=_params(1),
        name="ffn_stream_ple" if ple is not None else "ffn_stream",
    )(*args)


PROJ_UNIT = 512


def _proj_layout(d_inner, conv_dim, pool_dim, d_model):
    xbc, z, gates, v = conv_dim, d_inner, N_BRANCH * d_model, pool_dim
    lay = {"xbc": (0, xbc), "z": (xbc, z), "gates": (xbc + z, gates), "v": (xbc + z + gates, v),
           "dt": (xbc + z + gates + v, LANES)}
    for start, width in lay.values():
        assert start % width == 0
    total = -(-(lay["dt"][0] + LANES) // PROJ_UNIT) * PROJ_UNIT
    return lay, total


def _inproj_stream_kernel(rows_ref, valid_ref, h_ref, g_ref, wt_ref, proj_ref, wcat_ref, u_ref):
    c = pl.program_id(0)

    @pl.when(c == 0)
    def _():
        u_ref[...] = _rmsnorm(h_ref[...], g_ref[...]).astype(BF16)

    wt = wt_ref[...]
    row = lax.broadcasted_iota(jnp.int32, wt.shape, 0)
    wcat_ref[...] = jnp.where(row < valid_ref[c], wt, 0.0).T.astype(BF16)
    proj_ref[...] = _dot(u_ref[...], wcat_ref[...])


def _inproj_stream(h, g, w_t, unit_rows, unit_valid):
    t, d = h.shape
    n_units = len(unit_rows)
    total = n_units * PROJ_UNIT
    assert all(r % SUBLANES == 0 and r + PROJ_UNIT <= w_t.shape[0] for r in unit_rows)
    whole = lambda a: pl.BlockSpec(a.shape, lambda c, rows, valid: (0,) * a.ndim)
    return pl.pallas_call(
        _inproj_stream_kernel,
        grid_spec=pltpu.PrefetchScalarGridSpec(
            num_scalar_prefetch=2,
            grid=(n_units,),
            in_specs=[whole(h), whole(g),
                      pl.BlockSpec((pl.Element(PROJ_UNIT), pl.Element(d)),
                                   lambda c, rows, valid: (rows[c] * SUBLANES, 0))],
            out_specs=[pl.BlockSpec((t, PROJ_UNIT), lambda c, rows, valid: (0, c)),
                       pl.BlockSpec((d, PROJ_UNIT), lambda c, rows, valid: (0, c))],
            scratch_shapes=[pltpu.VMEM((t, d), BF16)]),
        out_shape=[jax.ShapeDtypeStruct((t, total), F32),
                   jax.ShapeDtypeStruct((d, total), BF16)],
        compiler_params=_params(1),
        name="inproj_stream",
    )(jnp.asarray([r // SUBLANES for r in unit_rows], jnp.int32),
      jnp.asarray(unit_valid, jnp.int32), h, g, w_t)


def _inproj_mixer_kernel(h_ref, g_ref, w_ref, cw_ref, cb_ref, dtb_ref, alog_ref, dskip_ref, nssd_ref,
                         yn_ref, pooled_ref, gates_ref, conv_ref, pool_ref, ssm_ref,
                         u_ref, raw_ref, vraw_ref, xc_ref, state_ref, y_ref, z_ref, e_ref,
                         *, widths, starts, nc, tiles_per_seq, d_inner):
    i = pl.program_id(0)
    tm = h_ref.shape[0]
    n_slabs = raw_ref.shape[0]

    n_vslabs = vraw_ref.shape[0]
    tile_in_seq = i % tiles_per_seq
    span = SUBLANES * ROW_STRIDE

    @pl.when(tile_in_seq == 0)
    def _():
        raw_ref[:, 0:SUBLANES, :] = jnp.zeros((n_slabs, SUBLANES, LANES), F32)
        vraw_ref[:, 0:POOL_PRE, :] = jnp.zeros((n_vslabs, POOL_PRE, LANES), F32)
        state_ref[...] = jnp.zeros(state_ref.shape, F32)

    u_ref[...] = _rmsnorm(h_ref[...], g_ref[...]).astype(BF16)

    def pool_slab(s):
        cols = slice(s * LANES, (s + 1) * LANES)
        win = POOL_WINDOWS[s * len(POOL_WINDOWS) // n_vslabs]
        for base in range(0, tm, span):
            rows = {d: vraw_ref[s, pl.ds(POOL_PRE + base + d, SUBLANES, stride=ROW_STRIDE), :]
                    for d in range(1 - win, ROW_STRIDE)}
            for j in range(ROW_STRIDE):
                total = rows[j]
                for k in range(1, win):
                    total = total + rows[j - k]
                if base >= POOL_BUF:
                    mean = total * (1.0 / win)
                else:
                    pos = (tile_in_seq * tm + base + j
                           + ROW_STRIDE * lax.broadcasted_iota(jnp.int32, (SUBLANES, LANES), 0))
                    mean = total / jnp.minimum(pos + 1, win).astype(F32)
                pooled_ref[s, pl.ds(base + j, SUBLANES, stride=ROW_STRIDE), :] = mean - rows[j]
        pool_ref[0, :, cols] = vraw_ref[s, POOL_PRE + tm - POOL_BUF:POOL_PRE + tm, :]
        vraw_ref[s, 0:POOL_PRE, :] = vraw_ref[s, tm:tm + POOL_PRE, :]

    def conv_slab(s):
        cols = slice(s * LANES, (s + 1) * LANES)
        taps = [jnp.broadcast_to(cw_ref[k:k + 1, cols], (SUBLANES, LANES)) for k in range(CONV_W)]
        bias = jnp.broadcast_to(cb_ref[:, cols], (SUBLANES, LANES))
        for base in range(0, tm, span):
            shifted = [raw_ref[s, pl.ds(SUBLANES + base - (CONV_W - 1) + m, SUBLANES,
                                        stride=ROW_STRIDE), :]
                       for m in range(ROW_STRIDE + CONV_W - 1)]
            for j in range(ROW_STRIDE):
                acc = bias
                for k in range(CONV_W):
                    acc = acc + shifted[j + k] * taps[k]
                xc_ref[s, pl.ds(base + j, SUBLANES, stride=ROW_STRIDE), :] = _silu(acc)
        conv_ref[0, :, cols] = raw_ref[s, SUBLANES + tm - (CONV_W - 1):SUBLANES + tm, :]
        raw_ref[s, 0:SUBLANES, :] = raw_ref[s, tm:tm + SUBLANES, :]

    def project(k, a, width=nc):
        b = min(a + width, widths[k])
        return _dot(u_ref[...], w_ref[:, starts[k] + a:starts[k] + b])

    n_xbc_units = widths[1] // nc
    slabs_per_unit = nc // LANES
    done = 0
    for n, (k, a) in enumerate([(k, a) for k in (1, 2) for a in range(0, widths[k], nc)]):
        res = project(k, a)
        for s in range(res.shape[1] // LANES):
            piece = res[:, s * LANES:(s + 1) * LANES]
            if k == 1:
                raw_ref[a // LANES + s, SUBLANES:SUBLANES + tm, :] = piece
            else:
                vraw_ref[a // LANES + s, POOL_PRE:POOL_PRE + tm, :] = piece
                pool_slab(a // LANES + s)
        while done < min(n_slabs, n * slabs_per_unit, min(n + 1, n_xbc_units) * slabs_per_unit):
            conv_slab(done)
            done += 1
    dt_raw = project(4, 0)
    while done < n_slabs:
        conv_slab(done)
        done += 1

    q = CHUNK
    n_state = D_STATE
    assert q == LANES and n_state == LANES
    late_nc = nc // 2
    late_units = [(k, a) for k in (0, 3) for a in range(0, widths[k], late_nc)]
    late_outs = {0: z_ref, 3: gates_ref}
    n_blocks = (tm // q) * N_GROUPS
    emitted = 0
    scalars = []
    for ci in range(tm // q):
        terms = _chunk_scalars(dt_raw[ci * q:(ci + 1) * q, :], dtb_ref[...], alog_ref[...], q)
        scalars.append((terms, jnp.exp2(terms.seg2).T))
    tiles_per_group = d_inner // N_GROUPS // LANES

    def emit_late_units(block, of):
        nonlocal emitted
        while emitted < len(late_units) and emitted * of < block * len(late_units):
            k, a = late_units[emitted]
            res = project(k, a, late_nc)
            late_outs[k][:, a:a + res.shape[1]] = res
            emitted += 1

    for ci in range(tm // q):
        rows = slice(ci * q, (ci + 1) * q)
        terms, _ = scalars[ci]
        for g in range(N_GROUPS):
            b_g = xc_ref[d_inner // LANES + g, rows, :].astype(BF16)
            c_g = xc_ref[d_inner // LANES + N_GROUPS + g, rows, :].astype(BF16)
            cb = _dot_nt(c_g, b_g)
            for p in range(tiles_per_group):
                tile = g * tiles_per_group + p
                cols = slice(tile * LANES, (tile + 1) * LANES)
                xpair = xc_ref[tile, rows, :]
                y, e_pair = _diag_pair(xpair, cb, terms, tile * HEADS_PER_TILE)
                y_ref[rows, cols] = y + dskip_ref[:, cols] * xpair
                e_ref[rows, cols] = e_pair
            emit_late_units(ci * N_GROUPS + g + 1, 2 * n_blocks)
    for ci in range(tm // q):
        rows = slice(ci * q, (ci + 1) * q)
        terms, chunk_decay_t = scalars[ci]
        for g in range(N_GROUPS):
            b_g = xc_ref[d_inner // LANES + g, rows, :].astype(BF16)
            c_g = xc_ref[d_inner // LANES + N_GROUPS + g, rows, :].astype(BF16)
            for p in range(tiles_per_group):
                tile = g * tiles_per_group + p
                h0 = tile * HEADS_PER_TILE
                cols = slice(tile * LANES, (tile + 1) * LANES)
                xpair = xc_ref[tile, rows, :]
                h_prev = state_ref[cols, :]
                y_ref[rows, cols] += _dot_nt(c_g, h_prev.astype(BF16)) * e_ref[rows, cols]
                xw_t = (xpair.T * _pair_rows(terms.w_state_t, h0, q)).astype(BF16)
                decay_rows = _pair_rows(chunk_decay_t, h0, n_state)
                state_ref[cols, :] = decay_rows * h_prev + _dot(xw_t, b_g)
            emit_late_units(n_blocks + ci * N_GROUPS + g + 1, 2 * n_blocks)
    assert emitted == len(late_units)

    _gate_and_norm(y_ref, z_ref, nssd_ref, yn_ref)

    @pl.when(tile_in_seq == tiles_per_seq - 1)
    def _():
        ssm_ref[0] = state_ref[...]


def _inproj_mixer(h, g, w_cat, layout, conv_w, conv_b, dt_bias, a_log, d_skip, norm_ssd,
                  *, tm, seq, d_inner):
    order = ("z", "xbc", "v", "gates", "dt")
    starts = tuple(layout[k][0] for k in order)
    widths = tuple(layout[k][1] for k in order)
    t, d = h.shape
    tm = min(tm, seq)
    tiles_per_seq = seq // tm
    n_all = w_cat.shape[1]
    conv_dim, pool_dim = widths[1], widths[2]
    n_slabs = conv_dim // LANES
    assert tm % (SUBLANES * ROW_STRIDE) == 0 and tm % CHUNK == 0
    row = lambda w: pl.BlockSpec((tm, w), lambda i: (i, 0))
    per_seq = lambda i: (i // tiles_per_seq, 0, 0)
    return pl.pallas_call(
        functools.partial(_inproj_mixer_kernel, widths=widths, starts=starts, nc=PROJ_UNIT,
                          tiles_per_seq=tiles_per_seq, d_inner=d_inner),
        grid=(t // tm,),
        in_specs=[row(d),
                  _const_spec((1, d)),
                  _const_spec((d, n_all)),
                  _const_spec((CONV_W, conv_dim)),
                  _const_spec((1, conv_dim)),
                  _const_spec((1, LANES)),
                  _const_spec((1, LANES)),
                  _const_spec((1, d_inner)),
                  _const_spec((1, d_inner))],
        out_specs=[row(d_inner),
                   pl.BlockSpec((pool_dim // LANES, tm, LANES), lambda i: (0, i, 0)),
                   row(widths[3]),
                   pl.BlockSpec((1, CONV_W - 1, conv_dim), per_seq),
                   pl.BlockSpec((1, POOL_BUF, pool_dim), per_seq),
                   pl.BlockSpec((1, d_inner, D_STATE), per_seq)],
        out_shape=[jax.ShapeDtypeStruct((t, d_inner), BF16),
                   jax.ShapeDtypeStruct((pool_dim // LANES, t, LANES), F32),
                   jax.ShapeDtypeStruct((t, widths[3]), F32),
                   jax.ShapeDtypeStruct((t // seq, CONV_W - 1, conv_dim), F32),
                   jax.ShapeDtypeStruct((t // seq, POOL_BUF, pool_dim), F32),
                   jax.ShapeDtypeStruct((t // seq, d_inner, D_STATE), F32)],
        scratch_shapes=[pltpu.VMEM((tm, d), BF16),
                        pltpu.VMEM((n_slabs, SUBLANES + tm, LANES), F32),
                        pltpu.VMEM((pool_dim // LANES, POOL_PRE + tm, LANES), F32),
                        pltpu.VMEM((n_slabs, tm, LANES), F32),
                        pltpu.VMEM((d_inner, D_STATE), F32),
                        pltpu.VMEM((tm, d_inner), F32),
                        pltpu.VMEM((tm, widths[0]), F32),
                        pltpu.VMEM((tm, d_inner), F32)],
        compiler_params=_params(1),
        name="inproj_mixer",
    )(h, g, w_cat, conv_w, conv_b, dt_bias, a_log, d_skip, norm_ssd)


def _chunk_scalars(dt_raw, dt_bias, a_log, seg_len):
    q = dt_raw.shape[0]
    dt = _softplus(dt_raw + dt_bias)
    da = dt * (-jnp.exp(a_log))
    shift = int(math.log2(seg_len))
    qi = lax.broadcasted_iota(jnp.int32, (q, q), 0)
    si = lax.broadcasted_iota(jnp.int32, (q, q), 1)
    same = (qi >> shift) == (si >> shift)
    causal = jnp.logical_and(same, si <= qi)
    a_cs = jnp.dot(jnp.where(causal, 1.0, 0.0), da, precision=HIGHEST, preferred_element_type=F32)
    seg_tot = jnp.dot(jnp.where(same, 1.0, 0.0), da, precision=HIGHEST, preferred_element_type=F32)
    a2 = a_cs * LOG2E
    seg2 = seg_tot * LOG2E
    return _ChunkTerms(
        a2=a2,
        decay_in_t=(a2 - jnp.log2(dt)).T,
        w_state_t=(jnp.exp2(seg2 - a2) * dt).T,
        seg2=seg2,
        causal=causal)


class _ChunkTerms(NamedTuple):
    a2: jax.Array
    decay_in_t: jax.Array
    w_state_t: jax.Array
    seg2: jax.Array
    causal: jax.Array


def _pair_rows(vals_t, h0, cols):
    return jnp.concatenate([jnp.broadcast_to(vals_t[h0 + k:h0 + k + 1, :], (HEAD_DIM, cols))
                            for k in range(HEADS_PER_TILE)], axis=0)


def _diag_pair(xpair, cb, terms, h0):
    q = xpair.shape[0]
    lane = lax.broadcasted_iota(jnp.int32, xpair.shape, 1)
    out = None
    spread = []
    for k in range(HEADS_PER_TILE):
        h = h0 + k
        a_col = jnp.broadcast_to(terms.a2[:, h:h + 1], (q, LANES))
        spread.append(a_col)
        decay_dt = jnp.exp2(jnp.where(terms.causal, a_col - terms.decay_in_t[h:h + 1, :], -jnp.inf))
        w = (cb * decay_dt).astype(BF16)
        mine = (lane < HEAD_DIM) if k == 0 else (lane >= HEAD_DIM)
        t = _dot(w, jnp.where(mine, xpair, 0.0).astype(BF16))
        out = t if out is None else out + t
    e_pair = jnp.exp2(jnp.where(lane < HEAD_DIM, spread[0], spread[1]))
    return out, e_pair


def _conv_tile_rows(rows_of_tap, w, bias):
    acc = rows_of_tap(0) * w[0:1, :]
    for k in range(1, CONV_W):
        acc = acc + rows_of_tap(k) * w[k:k + 1, :]
    return _silu(bias + acc)


def _mixer_sample_kernel(xbc_ref, dt_ref, v_ref, conv0_ref, pool0_ref, ssm0_ref,
                         cw_ref, cb_ref, dtb_ref, alog_ref, dskip_ref,
                         y_ref, pooled_ref, ssm_ref, conv_ref, pool_ref,
                         head_ref, xc_ref, xwt_ref, e_ref, seg_ref, pe_ref,
                         pooled_sc_ref, conv_sc_ref, pool_sc_ref,
                         *, d_inner, seq, pos0, seqs_per_step, n_chunks):
    c = pl.program_id(0)
    j = pl.program_id(1)
    q = xbc_ref.shape[0]
    conv_dim = xbc_ref.shape[1]
    pool_dim = v_ref.shape[1]
    n_state = D_STATE
    tiles_per_group = d_inner // N_GROUPS // LANES
    gc = pool_dim // len(POOL_WINDOWS)

    @pl.when(c < n_chunks)
    def _():
        slot = c % 2
        lo = SUBLANES - (CONV_W - 1)

        def per_seq(b, carry):
            r0 = pl.multiple_of(b * seq, seq)
            for k in range(CONV_W - 1):
                head_ref[lo + k:lo + k + 1, :] = conv0_ref[k, pl.ds(b, 1), :]
            ct = 1024
            for jc in range(conv_dim // ct):
                cols = slice(jc * ct, (jc + 1) * ct)
                head_ref[SUBLANES:SUBLANES + seq, cols] = xbc_ref[pl.ds(r0, seq), cols]
                xc_ref[slot, pl.ds(r0, seq), cols] = _conv_tile_rows(
                    lambda k: head_ref[lo + k:lo + k + seq, cols], cw_ref[:, cols], cb_ref[:, cols])
            for k in range(CONV_W - 1):
                r = SUBLANES + seq - (CONV_W - 1) + k
                conv_sc_ref[slot, k, pl.ds(b, 1), :] = head_ref[r:r + 1, :]

            for k in range(POOL_BUF):
                r = POOL_PAD - POOL_BUF + k
                pe_ref[r:r + 1, :] = pool0_ref[k, pl.ds(b, 1), :]
            vb = v_ref[pl.ds(r0, seq), :]
            pe_ref[POOL_PAD:POOL_PAD + seq, :] = vb
            t_idx = lax.broadcasted_iota(jnp.int32, (seq, 1), 0) + pos0
            for gi, win in enumerate(POOL_WINDOWS):
                cols = slice(gi * gc, (gi + 1) * gc)
                total = pe_ref[POOL_PAD:POOL_PAD + seq, cols]
                for k in range(1, win):
                    total = total + pe_ref[POOL_PAD - k:POOL_PAD - k + seq, cols]
                cnt = jnp.minimum(t_idx + 1, win).astype(F32)
                pooled = total / cnt - vb[:, cols]
                for s in range(gc // LANES):
                    pooled_sc_ref[slot, gi * (gc // LANES) + s, pl.ds(r0, seq), :] = (
                        pooled[:, s * LANES:(s + 1) * LANES])
            for k in range(POOL_BUF):
                r = POOL_PAD + seq - POOL_BUF + k
                pool_sc_ref[slot, k, pl.ds(b, 1), :] = pe_ref[r:r + 1, :]
            return carry

        lax.fori_loop(j * seqs_per_step, (j + 1) * seqs_per_step, per_seq, 0)

    @pl.when(c >= 1)
    def _():
        slot = (c + 1) % 2
        xc = xc_ref.at[slot]

        @pl.when(j == 0)
        def _():
            pooled_ref[...] = pooled_sc_ref[slot]
            conv_ref[...] = conv_sc_ref[slot]
            pool_ref[...] = pool_sc_ref[slot]
            terms = _chunk_scalars(dt_ref[...], dtb_ref[...], alog_ref[...], seq)
            seg_ref[...] = terms.seg2
            for g in range(N_GROUPS):
                b_g = xc[:, d_inner + g * n_state:d_inner + (g + 1) * n_state].astype(BF16)
                c_g = xc[:, d_inner + (N_GROUPS + g) * n_state:
                         d_inner + (N_GROUPS + g + 1) * n_state].astype(BF16)
                cb = _dot_nt(c_g, b_g)
                for p in range(tiles_per_group):
                    tile = g * tiles_per_group + p
                    h0 = tile * HEADS_PER_TILE
                    cols = slice(tile * LANES, (tile + 1) * LANES)
                    xpair = xc[:, cols]
                    y, e_pair = _diag_pair(xpair, cb, terms, h0)
                    y_ref[:, cols] = y + dskip_ref[:, cols] * xpair
                    e_ref[:, cols] = e_pair
                    xwt_ref[cols, :] = (xpair.T * _pair_rows(terms.w_state_t, h0, q)).astype(BF16)

        row_q = lax.broadcasted_iota(jnp.int32, (q, n_state), 0)
        row128 = lax.broadcasted_iota(jnp.int32, (LANES, n_state), 0)
        for sj in range(seqs_per_step):
            r0 = pl.multiple_of((j * seqs_per_step + sj) * seq, seq)
            chunk_decay = jnp.exp2(seg_ref[pl.ds(r0, 1), :])
            mine = jnp.logical_and(row_q >= r0, row_q < r0 + seq)
            for g in range(N_GROUPS):
                b_cols = slice(d_inner + g * n_state, d_inner + (g + 1) * n_state)
                c_cols = slice(d_inner + (N_GROUPS + g) * n_state,
                               d_inner + (N_GROUPS + g + 1) * n_state)
                b_mine = jnp.where(mine, xc[:, b_cols], 0.0).astype(BF16)
                c_rows = xc[pl.ds(r0, seq), c_cols].astype(BF16)
                for p in range(tiles_per_group):
                    tile = g * tiles_per_group + p
                    h0 = tile * HEADS_PER_TILE
                    cols = slice(tile * LANES, (tile + 1) * LANES)
                    h_prev = ssm0_ref[sj, cols, :]
                    y_off = _dot_nt(c_rows, h_prev.astype(BF16)) * e_ref[pl.ds(r0, seq), cols]
                    y_ref[pl.ds(r0, seq), cols] += y_off
                    decay_rows = jnp.where(row128 < HEAD_DIM, chunk_decay[:, h0:h0 + 1],
                                           chunk_decay[:, h0 + 1:h0 + 2])
                    ssm_ref[sj, cols, :] = decay_rows * h_prev + _dot(xwt_ref[cols, :], b_mine)


def _mixer_sample(proj, layout, conv0, pool0, ssm0, conv_w, conv_b, dt_bias, a_log, d_skip,
                  *, batch, seq, d_inner, pos0):
    q = CHUNK
    n_seq = q // seq
    n_chunks = batch // n_seq
    conv_dim = layout["xbc"][1]
    pool_dim = layout["v"][1]
    sps = math.gcd(n_seq, 8)
    steps = n_seq // sps
    prep = lambda c: jnp.minimum(c, n_chunks - 1)
    scan = lambda c: jnp.maximum(c - 1, 0)
    window = lambda k, chunk: pl.BlockSpec(
        (q, layout[k][1]), lambda c, j: (chunk(c), layout[k][0] // layout[k][1]))
    tok = lambda c, j: (scan(c), 0)
    per_chunk_in = lambda c, j: (0, prep(c), 0)
    per_chunk = lambda c, j: (0, scan(c), 0)
    per_seq = lambda c, j: (jnp.where(c == 0, 0, (c - 1) * steps + j), 0, 0)
    return pl.pallas_call(
        functools.partial(_mixer_sample_kernel, d_inner=d_inner, seq=seq, pos0=pos0,
                          seqs_per_step=sps, n_chunks=n_chunks),
        grid=(n_chunks + 1, steps),
        in_specs=[window("xbc", prep), window("dt", scan), window("v", prep),
                  pl.BlockSpec((CONV_W - 1, n_seq, conv_dim), per_chunk_in),
                  pl.BlockSpec((POOL_BUF, n_seq, pool_dim), per_chunk_in),
                  pl.BlockSpec((sps, d_inner, D_STATE), per_seq),
                  _const_spec((CONV_W, conv_dim)),
                  _const_spec((1, conv_dim)),
                  _const_spec((1, LANES)),
                  _const_spec((1, LANES)),
                  _const_spec((1, d_inner))],
        out_specs=[pl.BlockSpec((q, d_inner), tok),
                   pl.BlockSpec((pool_dim // LANES, q, LANES), per_chunk),
                   pl.BlockSpec((sps, d_inner, D_STATE), per_seq),
                   pl.BlockSpec((CONV_W - 1, n_seq, conv_dim), per_chunk),
                   pl.BlockSpec((POOL_BUF, n_seq, pool_dim), per_chunk)],
        out_shape=[jax.ShapeDtypeStruct((batch * seq, d_inner), F32),
                   jax.ShapeDtypeStruct((pool_dim // LANES, batch * seq, LANES), F32),
                   jax.ShapeDtypeStruct((batch, d_inner, D_STATE), F32),
                   jax.ShapeDtypeStruct((CONV_W - 1, batch, conv_dim), F32),
                   jax.ShapeDtypeStruct((POOL_BUF, batch, pool_dim), F32)],
        scratch_shapes=[pltpu.VMEM((2 * SUBLANES, conv_dim), F32),
                        pltpu.VMEM((2, q, conv_dim), F32),
                        pltpu.VMEM((d_inner, q), BF16),
                        pltpu.VMEM((q, d_inner), F32),
                        pltpu.VMEM((q, LANES), F32),
                        pltpu.VMEM((POOL_PAD + seq, pool_dim), F32),
                        pltpu.VMEM((2, pool_dim // LANES, q, LANES), F32),
                        pltpu.VMEM((2, CONV_W - 1, n_seq, conv_dim), F32),
                        pltpu.VMEM((2, POOL_BUF, n_seq, pool_dim), F32)],
        compiler_params=_params(2, SAMPLE_MIXER_VMEM_LIMIT_BYTES),
        name="mixer_sample",
    )(proj, proj, proj, conv0, pool0, ssm0, conv_w, conv_b, dt_bias, a_log, d_skip)


def _gate_and_norm(y_ref, z_ref, nssd_ref, yn_ref):
    gw = y_ref.shape[1] // N_GROUPS
    for g in range(N_GROUPS):
        cols = slice(g * gw, (g + 1) * gw)
        yz = y_ref[:, cols] * _silu(z_ref[:, cols])
        yn_ref[:, cols] = _rmsnorm(yz, nssd_ref[:, cols]).astype(BF16)


def _merge_kernel(h_ref, *rest, gated, cast_weights):
    if gated:
        yn_ref, pooled_ref, gates_ref, pscale_ref, *rest = rest
    else:
        y_ref, z_ref, pooled_ref, gates_ref, nssd_ref, pscale_ref, *rest = rest
    if cast_weights:
        w32_refs, (o_ref, *w_refs), scratch = rest[:4], rest[4:9], rest[9:]

        @pl.when(pl.program_id(0) == 0)
        def _():
            for dst, src in zip(w_refs, w32_refs):
                dst[...] = src[...].astype(BF16)
    else:
        w_refs, o_ref, scratch = rest[:4], rest[4], rest[5:]
    wssd_ref, wpg_ref, wpo_ref, wo_ref = w_refs
    d_model = h_ref.shape[1]
    if gated:
        (pm_ref,) = scratch
    else:
        yn_ref, pm_ref = scratch
        _gate_and_norm(y_ref, z_ref, nssd_ref, yn_ref)
    a_branch = _dot(yn_ref[...], wssd_ref[...])
    slabs_per_group = pooled_ref.shape[0] // len(POOL_WINDOWS)
    gc = slabs_per_group * LANES
    for gi in range(len(POOL_WINDOWS)):
        cols = slice(gi * gc, (gi + 1) * gc)
        pooled = jnp.concatenate([pooled_ref[gi * slabs_per_group + s]
                                  for s in range(slabs_per_group)], axis=1)
        mixed = _dot(pooled.astype(BF16), wpg_ref[gi])
        pm_ref[:, cols] = (mixed * pscale_ref[:, cols]).astype(BF16)
    b_branch = _dot(pm_ref[...], wpo_ref[...])
    merged = (_sigmoid(gates_ref[:, 0:d_model]) * a_branch
              + _sigmoid(gates_ref[:, d_model:2 * d_model]) * b_branch)
    o_ref[...] = h_ref[...] + _dot(merged.astype(BF16), wo_ref[...])


def _merge(h, ssd, pooled, gates_src, pool_scale, weights, *, tm, cast_weights):
    t, d = h.shape
    tm = min(tm, t)
    gated = not isinstance(ssd, tuple)
    pool_slabs = pooled.shape[0]
    pool_dim = pool_slabs * LANES
    row = lambda w, blk=0: pl.BlockSpec((tm, w), lambda i: (i, blk))
    pooled_spec = pl.BlockSpec((pool_slabs, tm, LANES), lambda i: (0, i, 0))
    gates_spec = row(N_BRANCH * d, gates_src[1])
    scratch = [pltpu.VMEM((tm, pool_dim), BF16)]
    if gated:
        d_inner = ssd.shape[1]
        in_specs = [row(d), row(d_inner), pooled_spec, gates_spec, _const_spec((1, pool_dim))]
        args = [h, ssd, pooled, gates_src[0], pool_scale]
    else:
        y, z_src, norm_ssd = ssd
        d_inner = y.shape[1]
        in_specs = [row(d), row(d_inner), row(d_inner, z_src[1]), pooled_spec, gates_spec,
                    _const_spec((1, d_inner)), _const_spec((1, pool_dim))]
        args = [h, y, z_src[0], pooled, gates_src[0], norm_ssd, pool_scale]
        scratch = [pltpu.VMEM((tm, d_inner), BF16)] + scratch
    out_specs = [row(d)]
    out_shape = [jax.ShapeDtypeStruct((t, d), F32)]
    if cast_weights:
        out_specs += [pl.BlockSpec(w.shape, lambda i, nd=w.ndim: (0,) * nd) for w in weights]
        out_shape += [jax.ShapeDtypeStruct(w.shape, BF16) for w in weights]
    res = pl.pallas_call(
        functools.partial(_merge_kernel, gated=gated, cast_weights=cast_weights),
        grid=(t // tm,),
        in_specs=in_specs + [_const_spec(w.shape) for w in weights],
        out_specs=out_specs,
        out_shape=out_shape,
        scratch_shapes=scratch,
        compiler_params=_params(1),
        name="merge_cast" if cast_weights else "merge",
    )(*args, *weights)
    return res if cast_weights else res[0]


def _pad_lanes(a):
    return jnp.pad(a.reshape(1, -1), ((0, 0), (0, LANES - a.shape[-1])))


def kernel(x_prompt, x_sample, state_ssm, state_conv, state_pool, p_prompt, p_sample, norm_ffn1, w_ffn1_gu, w_ffn1_down, norm_mix, w_in, conv_w, conv_b, dt_bias, a_log, d_skip, norm_ssd, w_ssd_out, w_pool_group, pool_scale, w_pool_out, w_o, norm_ffn2, w_ffn2_gu, w_ffn2_down, norm_ple, w_ple_gate, w_ple, norm_final):
    depth = norm_ffn1.shape[0]
    assert depth == 1, "the final norm is fused into the layer's last stage: one layer only"
    batch, seq, d_model = x_prompt.shape
    dec_batch, dec_seq, _ = x_sample.shape
    n_heads = dt_bias.shape[1]
    d_inner = n_heads * HEAD_DIM
    conv_dim = conv_w.shape[2]
    pool_dim = pool_scale.shape[1]
    assert seq % CHUNK == 0 and CHUNK % dec_seq == 0 and dec_batch % (CHUNK // dec_seq) == 0

    row = lambda a: a[0].reshape(1, -1)
    layout, proj_cols = _proj_layout(d_inner, conv_dim, pool_dim, d_model)
    src = {"z": (0, d_inner), "xbc": (d_inner, conv_dim), "dt": (d_inner + conv_dim, n_heads),
           "v": (d_inner + conv_dim + n_heads, pool_dim),
           "gates": (d_inner + conv_dim + n_heads + pool_dim, N_BRANCH * d_model)}
    unit_rows, unit_valid = [], []
    for name in sorted(layout, key=lambda k: layout[k][0]):
        start, width = src[name]
        assert layout[name][0] == len(unit_rows) * PROJ_UNIT
        for off in range(0, width, PROJ_UNIT):
            unit_rows.append(start + off)
            unit_valid.append(min(PROJ_UNIT, width - off))
    assert len(unit_rows) * PROJ_UNIT == proj_cols
    dtb, alog = _pad_lanes(dt_bias[0]), _pad_lanes(a_log[0])
    dskip = jnp.repeat(d_skip[0], HEAD_DIM).reshape(1, d_inner)
    cb = row(conv_b)
    ple_consts = (row(norm_ple), w_ple_gate[0], w_ple[0], norm_final.reshape(1, -1))

    h1, wg1, wu1, wd1 = _ffn_stream(x_sample.reshape(dec_batch * dec_seq, d_model), row(norm_ffn1),
                                    w_ffn1_gu[0], w_ffn1_down[0])
    proj, w_cat = _inproj_stream(h1, row(norm_mix), w_in[0].T, tuple(unit_rows), tuple(unit_valid))
    y, pooled, s2, c2, q2 = _mixer_sample(
        proj, layout, jnp.transpose(state_conv[0], (1, 0, 2)), jnp.transpose(state_pool[0], (1, 0, 2)),
        state_ssm[0].reshape(dec_batch, d_inner, D_STATE),
        conv_w[0], cb, dtb, alog, dskip,
        batch=dec_batch, seq=dec_seq, d_inner=d_inner, pos0=PAST_LEN)
    block_of = lambda k: layout[k][0] // layout[k][1]
    h2, wssd, wpg, wpo, wo = _merge(
        h1, (y, (proj, block_of("z")), row(norm_ssd)), pooled, (proj, block_of("gates")),
        row(pool_scale), (w_ssd_out[0], w_pool_group[0], w_pool_out[0], w_o[0]),
        tm=128, cast_weights=True)
    y_sample, wg2, wu2, wd2, wpleg, wple = _ffn_stream(
        h2, row(norm_ffn2), w_ffn2_gu[0], w_ffn2_down[0],
        (p_sample[0].reshape(dec_batch * dec_seq, -1),) + ple_consts)

    h1 = _ffn(x_prompt.reshape(batch * seq, d_model), row(norm_ffn1), wg1, wu1, wd1, tm=1024)
    yn, pooled, gates, c1, q1, s1 = _inproj_mixer(h1, row(norm_mix), w_cat, layout, conv_w[0], cb,
                                                  dtb, alog, dskip, row(norm_ssd),
                                                  tm=256, seq=seq, d_inner=d_inner)
    h2 = _merge(h1, yn, pooled, (gates, 0), row(pool_scale), (wssd, wpg, wpo, wo),
                tm=512, cast_weights=False)
    y_prompt = _ffn(h2, row(norm_ffn2), wg2, wu2, wd2,
                    (p_prompt[0].reshape(batch * seq, -1), ple_consts[0], wpleg, wple, ple_consts[3]),
                    tm=1024)

    return (y_prompt.reshape(batch, seq, d_model),
            y_sample.reshape(dec_batch, dec_seq, d_model),
            s1.reshape(1, batch, n_heads, HEAD_DIM, D_STATE), c1[None], q1[None],
            s2.reshape(1, dec_batch, n_heads, HEAD_DIM, D_STATE),
            jnp.transpose(c2, (1, 0, 2))[None], jnp.transpose(q2, (1, 0, 2))[None])
```

```python
import functools
import math
from typing import NamedTuple

import jax
import jax.numpy as jnp
from jax import lax
from jax.experimental import pallas as pl
from jax.experimental.pallas import tpu as pltpu

F32 = jnp.float32
BF16 = jnp.bfloat16
EPS = 1e-6
HIGHEST = lax.Precision.HIGHEST
LOG2E = 1.4426950408889634
NEG_LOG2E = -LOG2E

LANES = 128
SUBLANES = 8
VMEM_LIMIT_BYTES = 56 * 1024 * 1024
SAMPLE_MIXER_VMEM_LIMIT_BYTES = 60 * 1024 * 1024

HEAD_DIM = 64
N_GROUPS = 8
D_STATE = 128
CONV_W = 4
CHUNK = 128
POOL_WINDOWS = (2, 4, 8, 16)
POOL_BUF = max(POOL_WINDOWS) - 1
N_BRANCH = 2
PAST_LEN = 16384
HEADS_PER_TILE = LANES // HEAD_DIM
ROW_STRIDE = 4
POOL_PAD = 24
POOL_PRE = 16


def _sigmoid(x):
    return 1.0 / (1.0 + jnp.exp2(x * NEG_LOG2E))


def _silu(x):
    half = 0.5 * x
    return half + half * jnp.tanh(half)


def _softplus(x):
    return jnp.maximum(x, 0.0) + jnp.log(1.0 + jnp.exp(-jnp.abs(x)))


def _rmsnorm(x, g):
    return x * lax.rsqrt(jnp.mean(x * x, axis=-1, keepdims=True) + EPS) * g


def _dot(a, b):
    return jnp.dot(a, b, preferred_element_type=F32)


def _dot_nt(a, b):
    return lax.dot_general(a, b, (((1,), (1,)), ((), ())), preferred_element_type=F32)


def _const_spec(shape):
    nd = len(shape)
    return pl.BlockSpec(shape, lambda *_: (0,) * nd, pipeline_mode=pl.Buffered(1))


def _params(n_axes, vmem_limit_bytes=VMEM_LIMIT_BYTES):
    return pltpu.CompilerParams(dimension_semantics=("arbitrary",) * n_axes,
                                vmem_limit_bytes=vmem_limit_bytes)


FFN_CHUNK = 256


def _ple_epilogue(h, p_ref, nple_ref, wg_ref, wp_ref, nfin_ref):
    gate = _sigmoid(_dot(_rmsnorm(h, nple_ref[...]).astype(BF16), wg_ref[...]))
    h = h + gate * _dot(p_ref[...].astype(BF16), wp_ref[...])
    return _rmsnorm(h, nfin_ref[...])


def _ffn_kernel(x_ref, g_ref, wg_ref, wu_ref, wd_ref, *rest, fc, with_ple):
    if with_ple:
        p_ref, nple_ref, wpg_ref, wp_ref, nfin_ref, o_ref, xn_ref, acc_ref = rest
    else:
        o_ref, xn_ref, acc_ref = rest
    xn_ref[...] = _rmsnorm(x_ref[...], g_ref[...]).astype(BF16)
    for c in range(wd_ref.shape[0] // fc):
        cols = slice(c * fc, (c + 1) * fc)
        xn = xn_ref[...]
        act = (_silu(_dot(xn, wg_ref[:, cols])) * _dot(xn, wu_ref[:, cols])).astype(BF16)
        contrib = _dot(act, wd_ref[cols, :])
        if c == 0:
            acc_ref[...] = contrib
        else:
            acc_ref[...] += contrib
    h = x_ref[...] + 0.5 * acc_ref[...]
    if with_ple:
        h = _ple_epilogue(h, p_ref, nple_ref, wpg_ref, wp_ref, nfin_ref)
    o_ref[...] = h


def _ffn(x, g, wg, wu, wd, ple=None, *, tm):
    t, d = x.shape
    tm = min(tm, t)
    row = lambda w: pl.BlockSpec((tm, w), lambda i: (i, 0))
    consts = [g, wg, wu, wd]
    in_specs = [row(d)] + [_const_spec(a.shape) for a in consts]
    args = [x] + consts
    if ple is not None:
        in_specs += [row(ple[0].shape[1])] + [_const_spec(a.shape) for a in ple[1:]]
        args += list(ple)
    return pl.pallas_call(
        functools.partial(_ffn_kernel, fc=FFN_CHUNK, with_ple=ple is not None),
        grid=(t // tm,),
        in_specs=in_specs,
        out_specs=row(d),
        out_shape=jax.ShapeDtypeStruct((t, d), F32),
        scratch_shapes=[pltpu.VMEM((tm, d), BF16), pltpu.VMEM((tm, d), F32)],
        compiler_params=_params(1),
        name="ffn_ple" if ple is not None else "ffn",
    )(*args)


def _ffn_stream_kernel(x_ref, g_ref, wg32_ref, wu32_ref, wd32_ref, *rest, with_ple):
    if with_ple:
        (p_ref, nple_ref, wpg32_ref, wp32_ref, nfin_ref,
         o_ref, wg_ref, wu_ref, wd_ref, wpg_ref, wp_ref, xn_ref, acc_ref) = rest
    else:
        o_ref, wg_ref, wu_ref, wd_ref, xn_ref, acc_ref = rest
    c = pl.program_id(0)

    @pl.when(c == 0)
    def _():
        xn_ref[...] = _rmsnorm(x_ref[...], g_ref[...]).astype(BF16)
        acc_ref[...] = jnp.zeros(acc_ref.shape, F32)
        if with_ple:
            wpg_ref[...] = wpg32_ref[...].astype(BF16)
            wp_ref[...] = wp32_ref[...].astype(BF16)

    wg_ref[...] = wg32_ref[...].astype(BF16)
    wu_ref[...] = wu32_ref[...].astype(BF16)
    wd_ref[...] = wd32_ref[...].astype(BF16)
    xn = xn_ref[...]
    act = (_silu(_dot(xn, wg_ref[...])) * _dot(xn, wu_ref[...])).astype(BF16)
    acc_ref[...] += _dot(act, wd_ref[...])

    @pl.when(c == pl.num_programs(0) - 1)
    def _():
        h = x_ref[...] + 0.5 * acc_ref[...]
        if with_ple:
            h = _ple_epilogue(h, p_ref, nple_ref, wpg_ref, wp_ref, nfin_ref)
        o_ref[...] = h


def _ffn_stream(x, g, w_gu, w_down, ple=None):
    t, d = x.shape
    d_ff = w_down.shape[0]
    fc = FFN_CHUNK
    n_chunks = d_ff // fc
    whole = lambda a: pl.BlockSpec(a.shape, lambda c: (0,) * a.ndim)
    in_specs = [whole(x), whole(g),
                pl.BlockSpec((d, fc), lambda c: (0, c)),
                pl.BlockSpec((d, fc), lambda c: (0, n_chunks + c)),
                pl.BlockSpec((fc, d), lambda c: (c, 0))]
    args = [x, g, w_gu, w_gu, w_down]
    out_specs = [whole(x),
                 pl.BlockSpec((d, fc), lambda c: (0, c)),
                 pl.BlockSpec((d, fc), lambda c: (0, c)),
                 pl.BlockSpec((fc, d), lambda c: (c, 0))]
    out_shape = [jax.ShapeDtypeStruct((t, d), F32),
                 jax.ShapeDtypeStruct((d, d_ff), BF16),
                 jax.ShapeDtypeStruct((d, d_ff), BF16),
                 jax.ShapeDtypeStruct((d_ff, d), BF16)]
    if ple is not None:
        in_specs += [whole(a) for a in ple]
        args += list(ple)
        out_specs += [whole(ple[2]), whole(ple[3])]
        out_shape += [jax.ShapeDtypeStruct(ple[2].shape, BF16), jax.ShapeDtypeStruct(ple[3].shape, BF16)]
    return pl.pallas_call(
        functools.partial(_ffn_stream_kernel, with_ple=ple is not None),
        grid=(n_chunks,),
        in_specs=in_specs,
        out_specs=out_specs,
        out_shape=out_shape,
        scratch_shapes=[pltpu.VMEM((t, d), BF16), pltpu.VMEM((t, d), F32)],
        compiler_params=_params(1),
        name="ffn_stream_ple" if ple is not None else "ffn_stream",
    )(*args)


PROJ_UNIT = 512


def _proj_layout(d_inner, conv_dim, pool_dim, d_model):
    xbc, z, gates, v = conv_dim, d_inner, N_BRANCH * d_model, pool_dim
    lay = {"xbc": (0, xbc), "z": (xbc, z), "gates": (xbc + z, gates), "v": (xbc + z + gates, v),
           "dt": (xbc + z + gates + v, LANES)}
    for start, width in lay.values():
        assert start % width == 0
    total = -(-(lay["dt"][0] + LANES) // PROJ_UNIT) * PROJ_UNIT
    return lay, total


def _inproj_stream_kernel(rows_ref, valid_ref, h_ref, g_ref, wt_ref, proj_ref, wcat_ref, u_ref):
    c = pl.program_id(0)

    @pl.when(c == 0)
    def _():
        u_ref[...] = _rmsnorm(h_ref[...], g_ref[...]).astype(BF16)

    wt = wt_ref[...]
    row = lax.broadcasted_iota(jnp.int32, wt.shape, 0)
    wcat_ref[...] = jnp.where(row < valid_ref[c], wt, 0.0).T.astype(BF16)
    proj_ref[...] = _dot(u_ref[...], wcat_ref[...])


def _inproj_stream(h, g, w_t, unit_rows, unit_valid):
    t, d = h.shape
    n_units = len(unit_rows)
    total = n_units * PROJ_UNIT
    assert all(r % SUBLANES == 0 and r + PROJ_UNIT <= w_t.shape[0] for r in unit_rows)
    whole = lambda a: pl.BlockSpec(a.shape, lambda c, rows, valid: (0,) * a.ndim)
    return pl.pallas_call(
        _inproj_stream_kernel,
        grid_spec=pltpu.PrefetchScalarGridSpec(
            num_scalar_prefetch=2,
            grid=(n_units,),
            in_specs=[whole(h), whole(g),
                      pl.BlockSpec((pl.Element(PROJ_UNIT), pl.Element(d)),
                                   lambda c, rows, valid: (rows[c] * SUBLANES, 0))],
            out_specs=[pl.BlockSpec((t, PROJ_UNIT), lambda c, rows, valid: (0, c)),
                       pl.BlockSpec((d, PROJ_UNIT), lambda c, rows, valid: (0, c))],
            scratch_shapes=[pltpu.VMEM((t, d), BF16)]),
        out_shape=[jax.ShapeDtypeStruct((t, total), F32),
                   jax.ShapeDtypeStruct((d, total), BF16)],
        compiler_params=_params(1),
        name="inproj_stream",
    )(jnp.asarray([r // SUBLANES for r in unit_rows], jnp.int32),
      jnp.asarray(unit_valid, jnp.int32), h, g, w_t)


def _inproj_mixer_kernel(h_ref, g_ref, w_ref, cw_ref, cb_ref, dtb_ref, alog_ref, dskip_ref, nssd_ref,
                         yn_ref, pooled_ref, gates_ref, conv_ref, pool_ref, ssm_ref,
                         u_ref, raw_ref, vraw_ref, xc_ref, state_ref, y_ref, z_ref, e_ref,
                         *, widths, starts, nc, tiles_per_seq, d_inner):
    i = pl.program_id(0)
    tm = h_ref.shape[0]
    n_slabs = raw_ref.shape[0]

    n_vslabs = vraw_ref.shape[0]
    tile_in_seq = i % tiles_per_seq
    span = SUBLANES * ROW_STRIDE

    @pl.when(tile_in_seq == 0)
    def _():
        raw_ref[:, 0:SUBLANES, :] = jnp.zeros((n_slabs, SUBLANES, LANES), F32)
        vraw_ref[:, 0:POOL_PRE, :] = jnp.zeros((n_vslabs, POOL_PRE, LANES), F32)
        state_ref[...] = jnp.zeros(state_ref.shape, F32)

    u_ref[...] = _rmsnorm(h_ref[...], g_ref[...]).astype(BF16)

    def pool_slab(s):
        cols = slice(s * LANES, (s + 1) * LANES)
        win = POOL_WINDOWS[s * len(POOL_WINDOWS) // n_vslabs]
        for base in range(0, tm, span):
            rows = {d: vraw_ref[s, pl.ds(POOL_PRE + base + d, SUBLANES, stride=ROW_STRIDE), :]
                    for d in range(1 - win, ROW_STRIDE)}
            for j in range(ROW_STRIDE):
                total = rows[j]
                for k in range(1, win):
                    total = total + rows[j - k]
                if base >= POOL_BUF:
                    mean = total * (1.0 / win)
                else:
                    pos = (tile_in_seq * tm + base + j
                           + ROW_STRIDE * lax.broadcasted_iota(jnp.int32, (SUBLANES, LANES), 0))
                    mean = total / jnp.minimum(pos + 1, win).astype(F32)
                pooled_ref[s, pl.ds(base + j, SUBLANES, stride=ROW_STRIDE), :] = mean - rows[j]
        pool_ref[0, :, cols] = vraw_ref[s, POOL_PRE + tm - POOL_BUF:POOL_PRE + tm, :]
        vraw_ref[s, 0:POOL_PRE, :] = vraw_ref[s, tm:tm + POOL_PRE, :]

    def conv_slab(s):
        cols = slice(s * LANES, (s + 1) * LANES)
        taps = [jnp.broadcast_to(cw_ref[k:k + 1, cols], (SUBLANES, LANES)) for k in range(CONV_W)]
        bias = jnp.broadcast_to(cb_ref[:, cols], (SUBLANES, LANES))
        for base in range(0, tm, span):
            shifted = [raw_ref[s, pl.ds(SUBLANES + base - (CONV_W - 1) + m, SUBLANES,
                                        stride=ROW_STRIDE), :]
                       for m in range(ROW_STRIDE + CONV_W - 1)]
            for j in range(ROW_STRIDE):
                acc = bias
                for k in range(CONV_W):
                    acc = acc + shifted[j + k] * taps[k]
                xc_ref[s, pl.ds(base + j, SUBLANES, stride=ROW_STRIDE), :] = _silu(acc)
        conv_ref[0, :, cols] = raw_ref[s, SUBLANES + tm - (CONV_W - 1):SUBLANES + tm, :]
        raw_ref[s, 0:SUBLANES, :] = raw_ref[s, tm:tm + SUBLANES, :]

    def project(k, a, width=nc):
        b = min(a + width, widths[k])
        return _dot(u_ref[...], w_ref[:, starts[k] + a:starts[k] + b])

    n_xbc_units = widths[1] // nc
    slabs_per_unit = nc // LANES
    done = 0
    for n, (k, a) in enumerate([(k, a) for k in (1, 2) for a in range(0, widths[k], nc)]):
        res = project(k, a)
        for s in range(res.shape[1] // LANES):
            piece = res[:, s * LANES:(s + 1) * LANES]
            if k == 1:
                raw_ref[a // LANES + s, SUBLANES:SUBLANES + tm, :] = piece
            else:
                vraw_ref[a // LANES + s, POOL_PRE:POOL_PRE + tm, :] = piece
                pool_slab(a // LANES + s)
        while done < min(n_slabs, n * slabs_per_unit, min(n + 1, n_xbc_units) * slabs_per_unit):
            conv_slab(done)
            done += 1
    dt_raw = project(4, 0)
    while done < n_slabs:
        conv_slab(done)
        done += 1

    q = CHUNK
    n_state = D_STATE
    assert q == LANES and n_state == LANES
    late_nc = nc // 2
    late_units = [(k, a) for k in (0, 3) for a in range(0, widths[k], late_nc)]
    late_outs = {0: z_ref, 3: gates_ref}
    n_blocks = (tm // q) * N_GROUPS
    emitted = 0
    scalars = []
    for ci in range(tm // q):
        terms = _chunk_scalars(dt_raw[ci * q:(ci + 1) * q, :], dtb_ref[...], alog_ref[...], q)
        scalars.append((terms, jnp.exp2(terms.seg2).T))
    tiles_per_group = d_inner // N_GROUPS // LANES

    def emit_late_units(block, of):
        nonlocal emitted
        while emitted < len(late_units) and emitted * of < block * len(late_units):
            k, a = late_units[emitted]
            res = project(k, a, late_nc)
            late_outs[k][:, a:a + res.shape[1]] = res
            emitted += 1

    for ci in range(tm // q):
        rows = slice(ci * q, (ci + 1) * q)
        terms, _ = scalars[ci]
        for g in range(N_GROUPS):
            b_g = xc_ref[d_inner // LANES + g, rows, :].astype(BF16)
            c_g = xc_ref[d_inner // LANES + N_GROUPS + g, rows, :].astype(BF16)
            cb = _dot_nt(c_g, b_g)
            for p in range(tiles_per_group):
                tile = g * tiles_per_group + p
                cols = slice(tile * LANES, (tile + 1) * LANES)
                xpair = xc_ref[tile, rows, :]
                y, e_pair = _diag_pair(xpair, cb, terms, tile * HEADS_PER_TILE)
                y_ref[rows, cols] = y + dskip_ref[:, cols] * xpair
                e_ref[rows, cols] = e_pair
            emit_late_units(ci * N_GROUPS + g + 1, 2 * n_blocks)
    for ci in range(tm // q):
        rows = slice(ci * q, (ci + 1) * q)
        terms, chunk_decay_t = scalars[ci]
        for g in range(N_GROUPS):
            b_g = xc_ref[d_inner // LANES + g, rows, :].astype(BF16)
            c_g = xc_ref[d_inner // LANES + N_GROUPS + g, rows, :].astype(BF16)
            for p in range(tiles_per_group):
                tile = g * tiles_per_group + p
                h0 = tile * HEADS_PER_TILE
                cols = slice(tile * LANES, (tile + 1) * LANES)
                xpair = xc_ref[tile, rows, :]
                h_prev = state_ref[cols, :]
                y_ref[rows, cols] += _dot_nt(c_g, h_prev.astype(BF16)) * e_ref[rows, cols]
                xw_t = (xpair.T * _pair_rows(terms.w_state_t, h0, q)).astype(BF16)
                decay_rows = _pair_rows(chunk_decay_t, h0, n_state)
                state_ref[cols, :] = decay_rows * h_prev + _dot(xw_t, b_g)
            emit_late_units(n_blocks + ci * N_GROUPS + g + 1, 2 * n_blocks)
    assert emitted == len(late_units)

    _gate_and_norm(y_ref, z_ref, nssd_ref, yn_ref)

    @pl.when(tile_in_seq == tiles_per_seq - 1)
    def _():
        ssm_ref[0] = state_ref[...]


def _inproj_mixer(h, g, w_cat, layout, conv_w, conv_b, dt_bias, a_log, d_skip, norm_ssd,
                  *, tm, seq, d_inner):
    order = ("z", "xbc", "v", "gates", "dt")
    starts = tuple(layout[k][0] for k in order)
    widths = tuple(layout[k][1] for k in order)
    t, d = h.shape
    tm = min(tm, seq)
    tiles_per_seq = seq // tm
    n_all = w_cat.shape[1]
    conv_dim, pool_dim = widths[1], widths[2]
    n_slabs = conv_dim // LANES
    assert tm % (SUBLANES * ROW_STRIDE) == 0 and tm % CHUNK == 0
    row = lambda w: pl.BlockSpec((tm, w), lambda i: (i, 0))
    per_seq = lambda i: (i // tiles_per_seq, 0, 0)
    return pl.pallas_call(
        functools.partial(_inproj_mixer_kernel, widths=widths, starts=starts, nc=PROJ_UNIT,
                          tiles_per_seq=tiles_per_seq, d_inner=d_inner),
        grid=(t // tm,),
        in_specs=[row(d),
                  _const_spec((1, d)),
                  _const_spec((d, n_all)),
                  _const_spec((CONV_W, conv_dim)),
                  _const_spec((1, conv_dim)),
                  _const_spec((1, LANES)),
                  _const_spec((1, LANES)),
                  _const_spec((1, d_inner)),
                  _const_spec((1, d_inner))],
        out_specs=[row(d_inner),
                   pl.BlockSpec((pool_dim // LANES, tm, LANES), lambda i: (0, i, 0)),
                   row(widths[3]),
                   pl.BlockSpec((1, CONV_W - 1, conv_dim), per_seq),
                   pl.BlockSpec((1, POOL_BUF, pool_dim), per_seq),
                   pl.BlockSpec((1, d_inner, D_STATE), per_seq)],
        out_shape=[jax.ShapeDtypeStruct((t, d_inner), BF16),
                   jax.ShapeDtypeStruct((pool_dim // LANES, t, LANES), F32),
                   jax.ShapeDtypeStruct((t, widths[3]), F32),
                   jax.ShapeDtypeStruct((t // seq, CONV_W - 1, conv_dim), F32),
                   jax.ShapeDtypeStruct((t // seq, POOL_BUF, pool_dim), F32),
                   jax.ShapeDtypeStruct((t // seq, d_inner, D_STATE), F32)],
        scratch_shapes=[pltpu.VMEM((tm, d), BF16),
                        pltpu.VMEM((n_slabs, SUBLANES + tm, LANES), F32),
                        pltpu.VMEM((pool_dim // LANES, POOL_PRE + tm, LANES), F32),
                        pltpu.VMEM((n_slabs, tm, LANES), F32),
                        pltpu.VMEM((d_inner, D_STATE), F32),
                        pltpu.VMEM((tm, d_inner), F32),
                        pltpu.VMEM((tm, widths[0]), F32),
                        pltpu.VMEM((tm, d_inner), F32)],
        compiler_params=_params(1),
        name="inproj_mixer",
    )(h, g, w_cat, conv_w, conv_b, dt_bias, a_log, d_skip, norm_ssd)


def _chunk_scalars(dt_raw, dt_bias, a_log, seg_len):
    q = dt_raw.shape[0]
    dt = _softplus(dt_raw + dt_bias)
    da = dt * (-jnp.exp(a_log))
    shift = int(math.log2(seg_len))
    qi = lax.broadcasted_iota(jnp.int32, (q, q), 0)
    si = lax.broadcasted_iota(jnp.int32, (q, q), 1)
    same = (qi >> shift) == (si >> shift)
    causal = jnp.logical_and(same, si <= qi)
    a_cs = jnp.dot(jnp.where(causal, 1.0, 0.0), da, precision=HIGHEST, preferred_element_type=F32)
    seg_tot = jnp.dot(jnp.where(same, 1.0, 0.0), da, precision=HIGHEST, preferred_element_type=F32)
    a2 = a_cs * LOG2E
    seg2 = seg_tot * LOG2E
    return _ChunkTerms(
        a2=a2,
        decay_in_t=(a2 - jnp.log2(dt)).T,
        w_state_t=(jnp.exp2(seg2 - a2) * dt).T,
        seg2=seg2,
        causal=causal)


class _ChunkTerms(NamedTuple):
    a2: jax.Array
    decay_in_t: jax.Array
    w_state_t: jax.Array
    seg2: jax.Array
    causal: jax.Array


def _pair_rows(vals_t, h0, cols):
    return jnp.concatenate([jnp.broadcast_to(vals_t[h0 + k:h0 + k + 1, :], (HEAD_DIM, cols))
                            for k in range(HEADS_PER_TILE)], axis=0)


def _diag_pair(xpair, cb, terms, h0):
    q = xpair.shape[0]
    lane = lax.broadcasted_iota(jnp.int32, xpair.shape, 1)
    spread, ws, xs = [], [], []
    for k in range(HEADS_PER_TILE):
        h = h0 + k
        a_col = jnp.broadcast_to(terms.a2[:, h:h + 1], (q, LANES))
        spread.append(a_col)
        decay_dt = jnp.exp2(jnp.where(terms.causal, a_col - terms.decay_in_t[h:h + 1, :], -jnp.inf))
        ws.append((cb * decay_dt).astype(BF16))
        mine = (lane < HEAD_DIM) if k == 0 else (lane >= HEAD_DIM)
        xs.append(jnp.where(mine, xpair, 0.0).astype(BF16))
    out = _dot(jnp.concatenate(ws, axis=1), jnp.concatenate(xs, axis=0))
    e_pair = jnp.exp2(jnp.where(lane < HEAD_DIM, spread[0], spread[1]))
    return out, e_pair


def _conv_tile_rows(rows_of_tap, w, bias):
    acc = rows_of_tap(0) * w[0:1, :]
    for k in range(1, CONV_W):
        acc = acc + rows_of_tap(k) * w[k:k + 1, :]
    return _silu(bias + acc)


def _mixer_sample_kernel(xbc_ref, dt_ref, v_ref, conv0_ref, pool0_ref, ssm0_ref,
                         cw_ref, cb_ref, dtb_ref, alog_ref, dskip_ref,
                         y_ref, pooled_ref, ssm_ref, conv_ref, pool_ref,
                         head_ref, xc_ref, xwt_ref, e_ref, seg_ref, pe_ref,
                         pooled_sc_ref, conv_sc_ref, pool_sc_ref,
                         *, d_inner, seq, pos0, seqs_per_step, n_chunks):
    c = pl.program_id(0)
    j = pl.program_id(1)
    q = xbc_ref.shape[0]
    conv_dim = xbc_ref.shape[1]
    pool_dim = v_ref.shape[1]
    n_state = D_STATE
    tiles_per_group = d_inner // N_GROUPS // LANES
    gc = pool_dim // len(POOL_WINDOWS)

    @pl.when(c < n_chunks)
    def _():
        slot = c % 2
        lo = SUBLANES - (CONV_W - 1)

        def per_seq(b, carry):
            r0 = pl.multiple_of(b * seq, seq)
            for k in range(CONV_W - 1):
                head_ref[lo + k:lo + k + 1, :] = conv0_ref[k, pl.ds(b, 1), :]
            ct = 1024
            for jc in range(conv_dim // ct):
                cols = slice(jc * ct, (jc + 1) * ct)
                head_ref[SUBLANES:SUBLANES + seq, cols] = xbc_ref[pl.ds(r0, seq), cols]
                xc_ref[slot, pl.ds(r0, seq), cols] = _conv_tile_rows(
                    lambda k: head_ref[lo + k:lo + k + seq, cols], cw_ref[:, cols], cb_ref[:, cols])
            for k in range(CONV_W - 1):
                r = SUBLANES + seq - (CONV_W - 1) + k
                conv_sc_ref[slot, k, pl.ds(b, 1), :] = head_ref[r:r + 1, :]

            for k in range(POOL_BUF):
                r = POOL_PAD - POOL_BUF + k
                pe_ref[r:r + 1, :] = pool0_ref[k, pl.ds(b, 1), :]
            vb = v_ref[pl.ds(r0, seq), :]
            pe_ref[POOL_PAD:POOL_PAD + seq, :] = vb
            t_idx = lax.broadcasted_iota(jnp.int32, (seq, 1), 0) + pos0
            for gi, win in enumerate(POOL_WINDOWS):
                cols = slice(gi * gc, (gi + 1) * gc)
                total = pe_ref[POOL_PAD:POOL_PAD + seq, cols]
                for k in range(1, win):
                    total = total + pe_ref[POOL_PAD - k:POOL_PAD - k + seq, cols]
                cnt = jnp.minimum(t_idx + 1, win).astype(F32)
                pooled = total / cnt - vb[:, cols]
                for s in range(gc // LANES):
                    pooled_sc_ref[slot, gi * (gc // LANES) + s, pl.ds(r0, seq), :] = (
                        pooled[:, s * LANES:(s + 1) * LANES])
            for k in range(POOL_BUF):
                r = POOL_PAD + seq - POOL_BUF + k
                pool_sc_ref[slot, k, pl.ds(b, 1), :] = pe_ref[r:r + 1, :]
            return carry

        lax.fori_loop(j * seqs_per_step, (j + 1) * seqs_per_step, per_seq, 0)

    @pl.when(c >= 1)
    def _():
        slot = (c + 1) % 2
        xc = xc_ref.at[slot]

        @pl.when(j == 0)
        def _():
            pooled_ref[...] = pooled_sc_ref[slot]
            conv_ref[...] = conv_sc_ref[slot]
            pool_ref[...] = pool_sc_ref[slot]
            terms = _chunk_scalars(dt_ref[...], dtb_ref[...], alog_ref[...], seq)
            seg_ref[...] = terms.seg2
            for g in range(N_GROUPS):
                b_g = xc[:, d_inner + g * n_state:d_inner + (g + 1) * n_state].astype(BF16)
                c_g = xc[:, d_inner + (N_GROUPS + g) * n_state:
                         d_inner + (N_GROUPS + g + 1) * n_state].astype(BF16)
                cb = _dot_nt(c_g, b_g)
                for p in range(tiles_per_group):
                    tile = g * tiles_per_group + p
                    h0 = tile * HEADS_PER_TILE
                    cols = slice(tile * LANES, (tile + 1) * LANES)
                    xpair = xc[:, cols]
                    y, e_pair = _diag_pair(xpair, cb, terms, h0)
                    y_ref[:, cols] = y + dskip_ref[:, cols] * xpair
                    e_ref[:, cols] = e_pair
                    xwt_ref[cols, :] = (xpair.T * _pair_rows(terms.w_state_t, h0, q)).astype(BF16)

        row_q = lax.broadcasted_iota(jnp.int32, (q, n_state), 0)
        row128 = lax.broadcasted_iota(jnp.int32, (LANES, n_state), 0)
        for sj in range(seqs_per_step):
            r0 = pl.multiple_of((j * seqs_per_step + sj) * seq, seq)
            chunk_decay = jnp.exp2(seg_ref[pl.ds(r0, 1), :])
            mine = jnp.logical_and(row_q >= r0, row_q < r0 + seq)
            for g in range(N_GROUPS):
                b_cols = slice(d_inner + g * n_state, d_inner + (g + 1) * n_state)
                c_cols = slice(d_inner + (N_GROUPS + g) * n_state,
                               d_inner + (N_GROUPS + g + 1) * n_state)
                b_mine = jnp.where(mine, xc[:, b_cols], 0.0).astype(BF16)
                c_rows = xc[pl.ds(r0, seq), c_cols].astype(BF16)
                for p in range(tiles_per_group):
                    tile = g * tiles_per_group + p
                    h0 = tile * HEADS_PER_TILE
                    cols = slice(tile * LANES, (tile + 1) * LANES)
                    h_prev = ssm0_ref[sj, cols, :]
                    y_off = _dot_nt(c_rows, h_prev.astype(BF16)) * e_ref[pl.ds(r0, seq), cols]
                    y_ref[pl.ds(r0, seq), cols] += y_off
                    decay_rows = jnp.where(row128 < HEAD_DIM, chunk_decay[:, h0:h0 + 1],
                                           chunk_decay[:, h0 + 1:h0 + 2])
                    ssm_ref[sj, cols, :] = decay_rows * h_prev + _dot(xwt_ref[cols, :], b_mine)


def _mixer_sample(proj, layout, conv0, pool0, ssm0, conv_w, conv_b, dt_bias, a_log, d_skip,
                  *, batch, seq, d_inner, pos0):
    q = CHUNK
    n_seq = q // seq
    n_chunks = batch // n_seq
    conv_dim = layout["xbc"][1]
    pool_dim = layout["v"][1]
    sps = math.gcd(n_seq, 8)
    steps = n_seq // sps
    prep = lambda c: jnp.minimum(c, n_chunks - 1)
    scan = lambda c: jnp.maximum(c - 1, 0)
    window = lambda k, chunk: pl.BlockSpec(
        (q, layout[k][1]), lambda c, j: (chunk(c), layout[k][0] // layout[k][1]))
    tok = lambda c, j: (scan(c), 0)
    per_chunk_in = lambda c, j: (0, prep(c), 0)
    per_chunk = lambda c, j: (0, scan(c), 0)
    per_seq = lambda c, j: (jnp.where(c == 0, 0, (c - 1) * steps + j), 0, 0)
    return pl.pallas_call(
        functools.partial(_mixer_sample_kernel, d_inner=d_inner, seq=seq, pos0=pos0,
                          seqs_per_step=sps, n_chunks=n_chunks),
        grid=(n_chunks + 1, steps),
        in_specs=[window("xbc", prep), window("dt", scan), window("v", prep),
                  pl.BlockSpec((CONV_W - 1, n_seq, conv_dim), per_chunk_in),
                  pl.BlockSpec((POOL_BUF, n_seq, pool_dim), per_chunk_in),
                  pl.BlockSpec((sps, d_inner, D_STATE), per_seq),
                  _const_spec((CONV_W, conv_dim)),
                  _const_spec((1, conv_dim)),
                  _const_spec((1, LANES)),
                  _const_spec((1, LANES)),
                  _const_spec((1, d_inner))],
        out_specs=[pl.BlockSpec((q, d_inner), tok),
                   pl.BlockSpec((pool_dim // LANES, q, LANES), per_chunk),
                   pl.BlockSpec((sps, d_inner, D_STATE), per_seq),
                   pl.BlockSpec((CONV_W - 1, n_seq, conv_dim), per_chunk),
                   pl.BlockSpec((POOL_BUF, n_seq, pool_dim), per_chunk)],
        out_shape=[jax.ShapeDtypeStruct((batch * seq, d_inner), F32),
                   jax.ShapeDtypeStruct((pool_dim // LANES, batch * seq, LANES), F32),
                   jax.ShapeDtypeStruct((batch, d_inner, D_STATE), F32),
                   jax.ShapeDtypeStruct((CONV_W - 1, batch, conv_dim), F32),
                   jax.ShapeDtypeStruct((POOL_BUF, batch, pool_dim), F32)],
        scratch_shapes=[pltpu.VMEM((2 * SUBLANES, conv_dim), F32),
                        pltpu.VMEM((2, q, conv_dim), F32),
                        pltpu.VMEM((d_inner, q), BF16),
                        pltpu.VMEM((q, d_inner), F32),
                        pltpu.VMEM((q, LANES), F32),
                        pltpu.VMEM((POOL_PAD + seq, pool_dim), F32),
                        pltpu.VMEM((2, pool_dim // LANES, q, LANES), F32),
                        pltpu.VMEM((2, CONV_W - 1, n_seq, conv_dim), F32),
                        pltpu.VMEM((2, POOL_BUF, n_seq, pool_dim), F32)],
        compiler_params=_params(2, SAMPLE_MIXER_VMEM_LIMIT_BYTES),
        name="mixer_sample",
    )(proj, proj, proj, conv0, pool0, ssm0, conv_w, conv_b, dt_bias, a_log, d_skip)


def _gate_and_norm(y_ref, z_ref, nssd_ref, yn_ref):
    gw = y_ref.shape[1] // N_GROUPS
    for g in range(N_GROUPS):
        cols = slice(g * gw, (g + 1) * gw)
        yz = y_ref[:, cols] * _silu(z_ref[:, cols])
        yn_ref[:, cols] = _rmsnorm(yz, nssd_ref[:, cols]).astype(BF16)


def _merge_kernel(h_ref, *rest, gated, cast_weights):
    if gated:
        yn_ref, pooled_ref, gates_ref, pscale_ref, *rest = rest
    else:
        y_ref, z_ref, pooled_ref, gates_ref, nssd_ref, pscale_ref, *rest = rest
    if cast_weights:
        w32_refs, (o_ref, *w_refs), scratch = rest[:4], rest[4:9], rest[9:]

        @pl.when(pl.program_id(0) == 0)
        def _():
            for dst, src in zip(w_refs, w32_refs):
                dst[...] = src[...].astype(BF16)
    else:
        w_refs, o_ref, scratch = rest[:4], rest[4], rest[5:]
    wssd_ref, wpg_ref, wpo_ref, wo_ref = w_refs
    d_model = h_ref.shape[1]
    if gated:
        (pm_ref,) = scratch
    else:
        yn_ref, pm_ref = scratch
        _gate_and_norm(y_ref, z_ref, nssd_ref, yn_ref)
    a_branch = _dot(yn_ref[...], wssd_ref[...])
    slabs_per_group = pooled_ref.shape[0] // len(POOL_WINDOWS)
    gc = slabs_per_group * LANES
    for gi in range(len(POOL_WINDOWS)):
        cols = slice(gi * gc, (gi + 1) * gc)
        pooled = jnp.concatenate([pooled_ref[gi * slabs_per_group + s]
                                  for s in range(slabs_per_group)], axis=1)
        mixed = _dot(pooled.astype(BF16), wpg_ref[gi])
        pm_ref[:, cols] = (mixed * pscale_ref[:, cols]).astype(BF16)
    b_branch = _dot(pm_ref[...], wpo_ref[...])
    merged = (_sigmoid(gates_ref[:, 0:d_model]) * a_branch
              + _sigmoid(gates_ref[:, d_model:2 * d_model]) * b_branch)
    o_ref[...] = h_ref[...] + _dot(merged.astype(BF16), wo_ref[...])


def _merge(h, ssd, pooled, gates_src, pool_scale, weights, *, tm, cast_weights):
    t, d = h.shape
    tm = min(tm, t)
    gated = not isinstance(ssd, tuple)
    pool_slabs = pooled.shape[0]
    pool_dim = pool_slabs * LANES
    row = lambda w, blk=0: pl.BlockSpec((tm, w), lambda i: (i, blk))
    pooled_spec = pl.BlockSpec((pool_slabs, tm, LANES), lambda i: (0, i, 0))
    gates_spec = row(N_BRANCH * d, gates_src[1])
    scratch = [pltpu.VMEM((tm, pool_dim), BF16)]
    if gated:
        d_inner = ssd.shape[1]
        in_specs = [row(d), row(d_inner), pooled_spec, gates_spec, _const_spec((1, pool_dim))]
        args = [h, ssd, pooled, gates_src[0], pool_scale]
    else:
        y, z_src, norm_ssd = ssd
        d_inner = y.shape[1]
        in_specs = [row(d), row(d_inner), row(d_inner, z_src[1]), pooled_spec, gates_spec,
                    _const_spec((1, d_inner)), _const_spec((1, pool_dim))]
        args = [h, y, z_src[0], pooled, gates_src[0], norm_ssd, pool_scale]
        scratch = [pltpu.VMEM((tm, d_inner), BF16)] + scratch
    out_specs = [row(d)]
    out_shape = [jax.ShapeDtypeStruct((t, d), F32)]
    if cast_weights:
        out_specs += [pl.BlockSpec(w.shape, lambda i, nd=w.ndim: (0,) * nd) for w in weights]
        out_shape += [jax.ShapeDtypeStruct(w.shape, BF16) for w in weights]
    res = pl.pallas_call(
        functools.partial(_merge_kernel, gated=gated, cast_weights=cast_weights),
        grid=(t // tm,),
        in_specs=in_specs + [_const_spec(w.shape) for w in weights],
        out_specs=out_specs,
        out_shape=out_shape,
        scratch_shapes=scratch,
        compiler_params=_params(1),
        name="merge_cast" if cast_weights else "merge",
    )(*args, *weights)
    return res if cast_weights else res[0]


def _pad_lanes(a):
    return jnp.pad(a.reshape(1, -1), ((0, 0), (0, LANES - a.shape[-1])))


def kernel(x_prompt, x_sample, state_ssm, state_conv, state_pool, p_prompt, p_sample, norm_ffn1, w_ffn1_gu, w_ffn1_down, norm_mix, w_in, conv_w, conv_b, dt_bias, a_log, d_skip, norm_ssd, w_ssd_out, w_pool_group, pool_scale, w_pool_out, w_o, norm_ffn2, w_ffn2_gu, w_ffn2_down, norm_ple, w_ple_gate, w_ple, norm_final):
    depth = norm_ffn1.shape[0]
    assert depth == 1, "the final norm is fused into the layer's last stage: one layer only"
    batch, seq, d_model = x_prompt.shape
    dec_batch, dec_seq, _ = x_sample.shape
    n_heads = dt_bias.shape[1]
    d_inner = n_heads * HEAD_DIM
    conv_dim = conv_w.shape[2]
    pool_dim = pool_scale.shape[1]
    assert seq % CHUNK == 0 and CHUNK % dec_seq == 0 and dec_batch % (CHUNK // dec_seq) == 0

    row = lambda a: a[0].reshape(1, -1)
    layout, proj_cols = _proj_layout(d_inner, conv_dim, pool_dim, d_model)
    src = {"z": (0, d_inner), "xbc": (d_inner, conv_dim), "dt": (d_inner + conv_dim, n_heads),
           "v": (d_inner + conv_dim + n_heads, pool_dim),
           "gates": (d_inner + conv_dim + n_heads + pool_dim, N_BRANCH * d_model)}
    unit_rows, unit_valid = [], []
    for name in sorted(layout, key=lambda k: layout[k][0]):
        start, width = src[name]
        assert layout[name][0] == len(unit_rows) * PROJ_UNIT
        for off in range(0, width, PROJ_UNIT):
            unit_rows.append(start + off)
            unit_valid.append(min(PROJ_UNIT, width - off))
    assert len(unit_rows) * PROJ_UNIT == proj_cols
    dtb, alog = _pad_lanes(dt_bias[0]), _pad_lanes(a_log[0])
    dskip = jnp.repeat(d_skip[0], HEAD_DIM).reshape(1, d_inner)
    cb = row(conv_b)
    ple_consts = (row(norm_ple), w_ple_gate[0], w_ple[0], norm_final.reshape(1, -1))

    h1, wg1, wu1, wd1 = _ffn_stream(x_sample.reshape(dec_batch * dec_seq, d_model), row(norm_ffn1),
                                    w_ffn1_gu[0], w_ffn1_down[0])
    proj, w_cat = _inproj_stream(h1, row(norm_mix), w_in[0].T, tuple(unit_rows), tuple(unit_valid))
    y, pooled, s2, c2, q2 = _mixer_sample(
        proj, layout, jnp.transpose(state_conv[0], (1, 0, 2)), jnp.transpose(state_pool[0], (1, 0, 2)),
        state_ssm[0].reshape(dec_batch, d_inner, D_STATE),
        conv_w[0], cb, dtb, alog, dskip,
        batch=dec_batch, seq=dec_seq, d_inner=d_inner, pos0=PAST_LEN)
    block_of = lambda k: layout[k][0] // layout[k][1]
    h2, wssd, wpg, wpo, wo = _merge(
        h1, (y, (proj, block_of("z")), row(norm_ssd)), pooled, (proj, block_of("gates")),
        row(pool_scale), (w_ssd_out[0], w_pool_group[0], w_pool_out[0], w_o[0]),
        tm=128, cast_weights=True)
    y_sample, wg2, wu2, wd2, wpleg, wple = _ffn_stream(
        h2, row(norm_ffn2), w_ffn2_gu[0], w_ffn2_down[0],
        (p_sample[0].reshape(dec_batch * dec_seq, -1),) + ple_consts)

    h1 = _ffn(x_prompt.reshape(batch * seq, d_model), row(norm_ffn1), wg1, wu1, wd1, tm=1024)
    yn, pooled, gates, c1, q1, s1 = _inproj_mixer(h1, row(norm_mix), w_cat, layout, conv_w[0], cb,
                                                  dtb, alog, dskip, row(norm_ssd),
                                                  tm=256, seq=seq, d_inner=d_inner)
    h2 = _merge(h1, yn, pooled, (gates, 0), row(pool_scale), (wssd, wpg, wpo, wo),
                tm=512, cast_weights=False)
    y_prompt = _ffn(h2, row(norm_ffn2), wg2, wu2, wd2,
                    (p_prompt[0].reshape(batch * seq, -1), ple_consts[0], wpleg, wple, ple_consts[3]),
                    tm=1024)

    return (y_prompt.reshape(batch, seq, d_model),
            y_sample.reshape(dec_batch, dec_seq, d_model),
            s1.reshape(1, batch, n_heads, HEAD_DIM, D_STATE), c1[None], q1[None],
            s2.reshape(1, dec_batch, n_heads, HEAD_DIM, D_STATE),
            jnp.transpose(c2, (1, 0, 2))[None], jnp.transpose(q2, (1, 0, 2))[None])
```

```python
import functools
import math
from typing import NamedTuple

import jax
import jax.numpy as jnp
from jax import lax
from jax.experimental import pallas as pl
from jax.experimental.pallas import tpu as pltpu

F32 = jnp.float32
BF16 = jnp.bfloat16
EPS = 1e-6
HIGHEST = lax.Precision.HIGHEST
LOG2E = 1.4426950408889634
NEG_LOG2E = -LOG2E

LANES = 128
SUBLANES = 8
VMEM_LIMIT_BYTES = 56 * 1024 * 1024
SAMPLE_MIXER_VMEM_LIMIT_BYTES = 60 * 1024 * 1024

HEAD_DIM = 64
N_GROUPS = 8
D_STATE = 128
CONV_W = 4
CHUNK = 128
POOL_WINDOWS = (2, 4, 8, 16)
POOL_BUF = max(POOL_WINDOWS) - 1
N_BRANCH = 2
PAST_LEN = 16384
HEADS_PER_TILE = LANES // HEAD_DIM
ROW_STRIDE = 4
POOL_PAD = 24
POOL_PRE = 16


def _sigmoid(x):
    return 1.0 / (1.0 + jnp.exp2(x * NEG_LOG2E))


def _silu(x):
    half = 0.5 * x
    return half + half * jnp.tanh(half)


def _softplus(x):
    return jnp.maximum(x, 0.0) + jnp.log(1.0 + jnp.exp(-jnp.abs(x)))


def _rmsnorm(x, g):
    return x * lax.rsqrt(jnp.mean(x * x, axis=-1, keepdims=True) + EPS) * g


def _dot(a, b):
    return jnp.dot(a, b, preferred_element_type=F32)


def _dot_nt(a, b):
    return lax.dot_general(a, b, (((1,), (1,)), ((), ())), preferred_element_type=F32)


def _const_spec(shape):
    nd = len(shape)
    return pl.BlockSpec(shape, lambda *_: (0,) * nd, pipeline_mode=pl.Buffered(1))


def _params(n_axes, vmem_limit_bytes=VMEM_LIMIT_BYTES):
    return pltpu.CompilerParams(dimension_semantics=("arbitrary",) * n_axes,
                                vmem_limit_bytes=vmem_limit_bytes)


FFN_CHUNK = 256


def _ple_epilogue(h, p_ref, nple_ref, wg_ref, wp_ref, nfin_ref):
    gate = _sigmoid(_dot(_rmsnorm(h, nple_ref[...]).astype(BF16), wg_ref[...]))
    h = h + gate * _dot(p_ref[...].astype(BF16), wp_ref[...])
    return _rmsnorm(h, nfin_ref[...])


def _ffn_kernel(x_ref, g_ref, wg_ref, wu_ref, wd_ref, *rest, fc, with_ple):
    if with_ple:
        p_ref, nple_ref, wpg_ref, wp_ref, nfin_ref, o_ref, xn_ref, acc_ref = rest
    else:
        o_ref, xn_ref, acc_ref = rest
    xn_ref[...] = _rmsnorm(x_ref[...], g_ref[...]).astype(BF16)
    for c in range(wd_ref.shape[0] // fc):
        cols = slice(c * fc, (c + 1) * fc)
        xn = xn_ref[...]
        act = (_silu(_dot(xn, wg_ref[:, cols])) * _dot(xn, wu_ref[:, cols])).astype(BF16)
        contrib = _dot(act, wd_ref[cols, :])
        if c == 0:
            acc_ref[...] = contrib
        else:
            acc_ref[...] += contrib
    h = x_ref[...] + 0.5 * acc_ref[...]
    if with_ple:
        h = _ple_epilogue(h, p_ref, nple_ref, wpg_ref, wp_ref, nfin_ref)
    o_ref[...] = h


def _ffn(x, g, wg, wu, wd, ple=None, *, tm):
    t, d = x.shape
    tm = min(tm, t)
    row = lambda w: pl.BlockSpec((tm, w), lambda i: (i, 0))
    consts = [g, wg, wu, wd]
    in_specs = [row(d)] + [_const_spec(a.shape) for a in consts]
    args = [x] + consts
    if ple is not None:
        in_specs += [row(ple[0].shape[1])] + [_const_spec(a.shape) for a in ple[1:]]
        args += list(ple)
    return pl.pallas_call(
        functools.partial(_ffn_kernel, fc=FFN_CHUNK, with_ple=ple is not None),
        grid=(t // tm,),
        in_specs=in_specs,
        out_specs=row(d),
        out_shape=jax.ShapeDtypeStruct((t, d), F32),
        scratch_shapes=[pltpu.VMEM((tm, d), BF16), pltpu.VMEM((tm, d), F32)],
        compiler_params=_params(1),
        name="ffn_ple" if ple is not None else "ffn",
    )(*args)


def _ffn_stream_kernel(x_ref, g_ref, wg32_ref, wu32_ref, wd32_ref, *rest, with_ple):
    if with_ple:
        (p_ref, nple_ref, wpg32_ref, wp32_ref, nfin_ref,
         o_ref, wg_ref, wu_ref, wd_ref, wpg_ref, wp_ref, xn_ref, acc_ref) = rest
    else:
        o_ref, wg_ref, wu_ref, wd_ref, xn_ref, acc_ref = rest
    c = pl.program_id(0)

    @pl.when(c == 0)
    def _():
        xn_ref[...] = _rmsnorm(x_ref[...], g_ref[...]).astype(BF16)
        acc_ref[...] = jnp.zeros(acc_ref.shape, F32)
        if with_ple:
            wpg_ref[...] = wpg32_ref[...].astype(BF16)
            wp_ref[...] = wp32_ref[...].astype(BF16)

    wg_ref[...] = wg32_ref[...].astype(BF16)
    wu_ref[...] = wu32_ref[...].astype(BF16)
    wd_ref[...] = wd32_ref[...].astype(BF16)
    xn = xn_ref[...]
    act = (_silu(_dot(xn, wg_ref[...])) * _dot(xn, wu_ref[...])).astype(BF16)
    acc_ref[...] += _dot(act, wd_ref[...])

    @pl.when(c == pl.num_programs(0) - 1)
    def _():
        h = x_ref[...] + 0.5 * acc_ref[...]
        if with_ple:
            h = _ple_epilogue(h, p_ref, nple_ref, wpg_ref, wp_ref, nfin_ref)
        o_ref[...] = h


def _ffn_stream(x, g, w_gu, w_down, ple=None):
    t, d = x.shape
    d_ff = w_down.shape[0]
    fc = FFN_CHUNK
    n_chunks = d_ff // fc
    whole = lambda a: pl.BlockSpec(a.shape, lambda c: (0,) * a.ndim)
    in_specs = [whole(x), whole(g),
                pl.BlockSpec((d, fc), lambda c: (0, c)),
                pl.BlockSpec((d, fc), lambda c: (0, n_chunks + c)),
                pl.BlockSpec((fc, d), lambda c: (c, 0))]
    args = [x, g, w_gu, w_gu, w_down]
    out_specs = [whole(x),
                 pl.BlockSpec((d, fc), lambda c: (0, c)),
                 pl.BlockSpec((d, fc), lambda c: (0, c)),
                 pl.BlockSpec((fc, d), lambda c: (c, 0))]
    out_shape = [jax.ShapeDtypeStruct((t, d), F32),
                 jax.ShapeDtypeStruct((d, d_ff), BF16),
                 jax.ShapeDtypeStruct((d, d_ff), BF16),
                 jax.ShapeDtypeStruct((d_ff, d), BF16)]
    if ple is not None:
        in_specs += [whole(a) for a in ple]
        args += list(ple)
        out_specs += [whole(ple[2]), whole(ple[3])]
        out_shape += [jax.ShapeDtypeStruct(ple[2].shape, BF16), jax.ShapeDtypeStruct(ple[3].shape, BF16)]
    return pl.pallas_call(
        functools.partial(_ffn_stream_kernel, with_ple=ple is not None),
        grid=(n_chunks,),
        in_specs=in_specs,
        out_specs=out_specs,
        out_shape=out_shape,
        scratch_shapes=[pltpu.VMEM((t, d), BF16), pltpu.VMEM((t, d), F32)],
        compiler_params=_params(1),
        name="ffn_stream_ple" if ple is not None else "ffn_stream",
    )(*args)


PROJ_UNIT = 512


def _proj_layout(d_inner, conv_dim, pool_dim, d_model):
    xbc, z, gates, v = conv_dim, d_inner, N_BRANCH * d_model, pool_dim
    lay = {"xbc": (0, xbc), "z": (xbc, z), "gates": (xbc + z, gates), "v": (xbc + z + gates, v),
           "dt": (xbc + z + gates + v, LANES)}
    for start, width in lay.values():
        assert start % width == 0
    total = -(-(lay["dt"][0] + LANES) // PROJ_UNIT) * PROJ_UNIT
    return lay, total


def _inproj_stream_kernel(rows_ref, valid_ref, h_ref, g_ref, wt_ref, proj_ref, wcat_ref, u_ref):
    c = pl.program_id(0)

    @pl.when(c == 0)
    def _():
        u_ref[...] = _rmsnorm(h_ref[...], g_ref[...]).astype(BF16)

    wt = wt_ref[...]
    row = lax.broadcasted_iota(jnp.int32, wt.shape, 0)
    wcat_ref[...] = jnp.where(row < valid_ref[c], wt, 0.0).T.astype(BF16)
    proj_ref[...] = _dot(u_ref[...], wcat_ref[...])


def _inproj_stream(h, g, w_t, unit_rows, unit_valid):
    t, d = h.shape
    n_units = len(unit_rows)
    total = n_units * PROJ_UNIT
    assert all(r % SUBLANES == 0 and r + PROJ_UNIT <= w_t.shape[0] for r in unit_rows)
    whole = lambda a: pl.BlockSpec(a.shape, lambda c, rows, valid: (0,) * a.ndim)
    return pl.pallas_call(
        _inproj_stream_kernel,
        grid_spec=pltpu.PrefetchScalarGridSpec(
            num_scalar_prefetch=2,
            grid=(n_units,),
            in_specs=[whole(h), whole(g),
                      pl.BlockSpec((pl.Element(PROJ_UNIT), pl.Element(d)),
                                   lambda c, rows, valid: (rows[c] * SUBLANES, 0))],
            out_specs=[pl.BlockSpec((t, PROJ_UNIT), lambda c, rows, valid: (0, c)),
                       pl.BlockSpec((d, PROJ_UNIT), lambda c, rows, valid: (0, c))],
            scratch_shapes=[pltpu.VMEM((t, d), BF16)]),
        out_shape=[jax.ShapeDtypeStruct((t, total), F32),
                   jax.ShapeDtypeStruct((d, total), BF16)],
        compiler_params=_params(1),
        name="inproj_stream",
    )(jnp.asarray([r // SUBLANES for r in unit_rows], jnp.int32),
      jnp.asarray(unit_valid, jnp.int32), h, g, w_t)


def _inproj_mixer_kernel(h_ref, g_ref, w_ref, cw_ref, cb_ref, dtb_ref, alog_ref, dskip_ref, nssd_ref,
                         yn_ref, pooled_ref, gates_ref, conv_ref, pool_ref, ssm_ref,
                         u_ref, raw_ref, vraw_ref, xc_ref, state_ref, y_ref, z_ref, e_ref,
                         *, widths, starts, nc, tiles_per_seq, d_inner):
    i = pl.program_id(0)
    tm = h_ref.shape[0]
    n_slabs = raw_ref.shape[0]

    n_vslabs = vraw_ref.shape[0]
    tile_in_seq = i % tiles_per_seq
    span = SUBLANES * ROW_STRIDE

    @pl.when(tile_in_seq == 0)
    def _():
        raw_ref[:, 0:SUBLANES, :] = jnp.zeros((n_slabs, SUBLANES, LANES), F32)
        vraw_ref[:, 0:POOL_PRE, :] = jnp.zeros((n_vslabs, POOL_PRE, LANES), F32)
        state_ref[...] = jnp.zeros(state_ref.shape, F32)

    u_ref[...] = _rmsnorm(h_ref[...], g_ref[...]).astype(BF16)

    def pool_slab(s):
        cols = slice(s * LANES, (s + 1) * LANES)
        win = POOL_WINDOWS[s * len(POOL_WINDOWS) // n_vslabs]
        for base in range(0, tm, span):
            rows = {d: vraw_ref[s, pl.ds(POOL_PRE + base + d, SUBLANES, stride=ROW_STRIDE), :]
                    for d in range(1 - win, ROW_STRIDE)}
            for j in range(ROW_STRIDE):
                total = rows[j]
                for k in range(1, win):
                    total = total + rows[j - k]
                if base >= POOL_BUF:
                    mean = total * (1.0 / win)
                else:
                    pos = (tile_in_seq * tm + base + j
                           + ROW_STRIDE * lax.broadcasted_iota(jnp.int32, (SUBLANES, LANES), 0))
                    mean = total / jnp.minimum(pos + 1, win).astype(F32)
                pooled_ref[s, pl.ds(base + j, SUBLANES, stride=ROW_STRIDE), :] = mean - rows[j]
        pool_ref[0, :, cols] = vraw_ref[s, POOL_PRE + tm - POOL_BUF:POOL_PRE + tm, :]
        vraw_ref[s, 0:POOL_PRE, :] = vraw_ref[s, tm:tm + POOL_PRE, :]

    def conv_slab(s):
        cols = slice(s * LANES, (s + 1) * LANES)
        taps = [jnp.broadcast_to(cw_ref[k:k + 1, cols], (SUBLANES, LANES)) for k in range(CONV_W)]
        bias = jnp.broadcast_to(cb_ref[:, cols], (SUBLANES, LANES))
        for base in range(0, tm, span):
            shifted = [raw_ref[s, pl.ds(SUBLANES + base - (CONV_W - 1) + m, SUBLANES,
                                        stride=ROW_STRIDE), :]
                       for m in range(ROW_STRIDE + CONV_W - 1)]
            for j in range(ROW_STRIDE):
                acc = bias
                for k in range(CONV_W):
                    acc = acc + shifted[j + k] * taps[k]
                xc_ref[s, pl.ds(base + j, SUBLANES, stride=ROW_STRIDE), :] = _silu(acc)
        conv_ref[0, :, cols] = raw_ref[s, SUBLANES + tm - (CONV_W - 1):SUBLANES + tm, :]
        raw_ref[s, 0:SUBLANES, :] = raw_ref[s, tm:tm + SUBLANES, :]

    def project(k, a, width=nc):
        b = min(a + width, widths[k])
        return _dot(u_ref[...], w_ref[:, starts[k] + a:starts[k] + b])

    n_xbc_units = widths[1] // nc
    slabs_per_unit = nc // LANES
    done = 0
    for n, (k, a) in enumerate([(k, a) for k in (1, 2) for a in range(0, widths[k], nc)]):
        res = project(k, a)
        for s in range(res.shape[1] // LANES):
            piece = res[:, s * LANES:(s + 1) * LANES]
            if k == 1:
                raw_ref[a // LANES + s, SUBLANES:SUBLANES + tm, :] = piece
            else:
                vraw_ref[a // LANES + s, POOL_PRE:POOL_PRE + tm, :] = piece
                pool_slab(a // LANES + s)
        while done < min(n_slabs, n * slabs_per_unit, min(n + 1, n_xbc_units) * slabs_per_unit):
            conv_slab(done)
            done += 1
    dt_raw = project(4, 0)
    while done < n_slabs:
        conv_slab(done)
        done += 1

    q = CHUNK
    n_state = D_STATE
    assert q == LANES and n_state == LANES
    late_nc = nc // 2
    late_units = [(k, a) for k in (0, 3) for a in range(0, widths[k], late_nc)]
    late_outs = {0: z_ref, 3: gates_ref}
    n_blocks = (tm // q) * N_GROUPS
    emitted = 0
    scalars = []
    for ci in range(tm // q):
        terms = _chunk_scalars(dt_raw[ci * q:(ci + 1) * q, :], dtb_ref[...], alog_ref[...], q)
        scalars.append((terms, jnp.exp2(terms.seg2).T))
    tiles_per_group = d_inner // N_GROUPS // LANES

    def emit_late_units(block, of):
        nonlocal emitted
        while emitted < len(late_units) and emitted * of < block * len(late_units):
            k, a = late_units[emitted]
            res = project(k, a, late_nc)
            late_outs[k][:, a:a + res.shape[1]] = res
            emitted += 1

    for ci in range(tm // q):
        rows = slice(ci * q, (ci + 1) * q)
        terms, _ = scalars[ci]
        for g in range(N_GROUPS):
            b_g = xc_ref[d_inner // LANES + g, rows, :].astype(BF16)
            c_g = xc_ref[d_inner // LANES + N_GROUPS + g, rows, :].astype(BF16)
            cb = _dot_nt(c_g, b_g)
            for p in range(tiles_per_group):
                tile = g * tiles_per_group + p
                cols = slice(tile * LANES, (tile + 1) * LANES)
                xpair = xc_ref[tile, rows, :]
                y, e_pair = _diag_pair(xpair, cb, terms, tile * HEADS_PER_TILE)
                y_ref[rows, cols] = y + dskip_ref[:, cols] * xpair
                e_ref[rows, cols] = e_pair
            emit_late_units(ci * N_GROUPS + g + 1, 2 * n_blocks)
    for ci in range(tm // q):
        rows = slice(ci * q, (ci + 1) * q)
        terms, chunk_decay_t = scalars[ci]
        for g in range(N_GROUPS):
            b_g = xc_ref[d_inner // LANES + g, rows, :].astype(BF16)
            c_g = xc_ref[d_inner // LANES + N_GROUPS + g, rows, :].astype(BF16)
            tiles = range(g * tiles_per_group, (g + 1) * tiles_per_group)
            gcols = slice(tiles[0] * LANES, (tiles[-1] + 1) * LANES)
            stack = lambda f: jnp.concatenate([f(t) for t in tiles], axis=0)
            h_prev = state_ref[gcols, :]
            y_ref[rows, gcols] += _dot_nt(c_g, h_prev.astype(BF16)) * e_ref[rows, gcols]
            xw_t = (stack(lambda t: xc_ref[t, rows, :].T)
                    * stack(lambda t: _pair_rows(terms.w_state_t, t * HEADS_PER_TILE, q))).astype(BF16)
            decay_rows = stack(lambda t: _pair_rows(chunk_decay_t, t * HEADS_PER_TILE, n_state))
            state_ref[gcols, :] = decay_rows * h_prev + _dot(xw_t, b_g)
            emit_late_units(n_blocks + ci * N_GROUPS + g + 1, 2 * n_blocks)
    assert emitted == len(late_units)

    _gate_and_norm(y_ref, z_ref, nssd_ref, yn_ref)

    @pl.when(tile_in_seq == tiles_per_seq - 1)
    def _():
        ssm_ref[0] = state_ref[...]


def _inproj_mixer(h, g, w_cat, layout, conv_w, conv_b, dt_bias, a_log, d_skip, norm_ssd,
                  *, tm, seq, d_inner):
    order = ("z", "xbc", "v", "gates", "dt")
    starts = tuple(layout[k][0] for k in order)
    widths = tuple(layout[k][1] for k in order)
    t, d = h.shape
    tm = min(tm, seq)
    tiles_per_seq = seq // tm
    n_all = w_cat.shape[1]
    conv_dim, pool_dim = widths[1], widths[2]
    n_slabs = conv_dim // LANES
    assert tm % (SUBLANES * ROW_STRIDE) == 0 and tm % CHUNK == 0
    row = lambda w: pl.BlockSpec((tm, w), lambda i: (i, 0))
    per_seq = lambda i: (i // tiles_per_seq, 0, 0)
    return pl.pallas_call(
        functools.partial(_inproj_mixer_kernel, widths=widths, starts=starts, nc=PROJ_UNIT,
                          tiles_per_seq=tiles_per_seq, d_inner=d_inner),
        grid=(t // tm,),
        in_specs=[row(d),
                  _const_spec((1, d)),
                  _const_spec((d, n_all)),
                  _const_spec((CONV_W, conv_dim)),
                  _const_spec((1, conv_dim)),
                  _const_spec((1, LANES)),
                  _const_spec((1, LANES)),
                  _const_spec((1, d_inner)),
                  _const_spec((1, d_inner))],
        out_specs=[row(d_inner),
                   pl.BlockSpec((pool_dim // LANES, tm, LANES), lambda i: (0, i, 0)),
                   row(widths[3]),
                   pl.BlockSpec((1, CONV_W - 1, conv_dim), per_seq),
                   pl.BlockSpec((1, POOL_BUF, pool_dim), per_seq),
                   pl.BlockSpec((1, d_inner, D_STATE), per_seq)],
        out_shape=[jax.ShapeDtypeStruct((t, d_inner), BF16),
                   jax.ShapeDtypeStruct((pool_dim // LANES, t, LANES), F32),
                   jax.ShapeDtypeStruct((t, widths[3]), F32),
                   jax.ShapeDtypeStruct((t // seq, CONV_W - 1, conv_dim), F32),
                   jax.ShapeDtypeStruct((t // seq, POOL_BUF, pool_dim), F32),
                   jax.ShapeDtypeStruct((t // seq, d_inner, D_STATE), F32)],
        scratch_shapes=[pltpu.VMEM((tm, d), BF16),
                        pltpu.VMEM((n_slabs, SUBLANES + tm, LANES), F32),
                        pltpu.VMEM((pool_dim // LANES, POOL_PRE + tm, LANES), F32),
                        pltpu.VMEM((n_slabs, tm, LANES), F32),
                        pltpu.VMEM((d_inner, D_STATE), F32),
                        pltpu.VMEM((tm, d_inner), F32),
                        pltpu.VMEM((tm, widths[0]), F32),
                        pltpu.VMEM((tm, d_inner), F32)],
        compiler_params=_params(1),
        name="inproj_mixer",
    )(h, g, w_cat, conv_w, conv_b, dt_bias, a_log, d_skip, norm_ssd)


def _chunk_scalars(dt_raw, dt_bias, a_log, seg_len):
    q = dt_raw.shape[0]
    dt = _softplus(dt_raw + dt_bias)
    da = dt * (-jnp.exp(a_log))
    shift = int(math.log2(seg_len))
    qi = lax.broadcasted_iota(jnp.int32, (q, q), 0)
    si = lax.broadcasted_iota(jnp.int32, (q, q), 1)
    same = (qi >> shift) == (si >> shift)
    causal = jnp.logical_and(same, si <= qi)
    a_cs = jnp.dot(jnp.where(causal, 1.0, 0.0), da, precision=HIGHEST, preferred_element_type=F32)
    seg_tot = jnp.dot(jnp.where(same, 1.0, 0.0), da, precision=HIGHEST, preferred_element_type=F32)
    a2 = a_cs * LOG2E
    seg2 = seg_tot * LOG2E
    return _ChunkTerms(
        a2=a2,
        decay_in_t=(a2 - jnp.log2(dt)).T,
        w_state_t=(jnp.exp2(seg2 - a2) * dt).T,
        seg2=seg2,
        causal=causal)


class _ChunkTerms(NamedTuple):
    a2: jax.Array
    decay_in_t: jax.Array
    w_state_t: jax.Array
    seg2: jax.Array
    causal: jax.Array


def _pair_rows(vals_t, h0, cols):
    return jnp.concatenate([jnp.broadcast_to(vals_t[h0 + k:h0 + k + 1, :], (HEAD_DIM, cols))
                            for k in range(HEADS_PER_TILE)], axis=0)


def _diag_pair(xpair, cb, terms, h0):
    q = xpair.shape[0]
    lane = lax.broadcasted_iota(jnp.int32, xpair.shape, 1)
    spread, ws, xs = [], [], []
    for k in range(HEADS_PER_TILE):
        h = h0 + k
        a_col = jnp.broadcast_to(terms.a2[:, h:h + 1], (q, LANES))
        spread.append(a_col)
        decay_dt = jnp.exp2(jnp.where(terms.causal, a_col - terms.decay_in_t[h:h + 1, :], -jnp.inf))
        ws.append((cb * decay_dt).astype(BF16))
        mine = (lane < HEAD_DIM) if k == 0 else (lane >= HEAD_DIM)
        xs.append(jnp.where(mine, xpair, 0.0).astype(BF16))
    out = _dot(jnp.concatenate(ws, axis=1), jnp.concatenate(xs, axis=0))
    e_pair = jnp.exp2(jnp.where(lane < HEAD_DIM, spread[0], spread[1]))
    return out, e_pair


def _conv_tile_rows(rows_of_tap, w, bias):
    acc = rows_of_tap(0) * w[0:1, :]
    for k in range(1, CONV_W):
        acc = acc + rows_of_tap(k) * w[k:k + 1, :]
    return _silu(bias + acc)


def _mixer_sample_kernel(xbc_ref, dt_ref, v_ref, conv0_ref, pool0_ref, ssm0_ref,
                         cw_ref, cb_ref, dtb_ref, alog_ref, dskip_ref,
                         y_ref, pooled_ref, ssm_ref, conv_ref, pool_ref,
                         head_ref, xc_ref, xwt_ref, e_ref, seg_ref, pe_ref,
                         pooled_sc_ref, conv_sc_ref, pool_sc_ref,
                         *, d_inner, seq, pos0, seqs_per_step, n_chunks):
    c = pl.program_id(0)
    j = pl.program_id(1)
    q = xbc_ref.shape[0]
    conv_dim = xbc_ref.shape[1]
    pool_dim = v_ref.shape[1]
    n_state = D_STATE
    tiles_per_group = d_inner // N_GROUPS // LANES
    gc = pool_dim // len(POOL_WINDOWS)

    @pl.when(c < n_chunks)
    def _():
        slot = c % 2
        lo = SUBLANES - (CONV_W - 1)

        def per_seq(b, carry):
            r0 = pl.multiple_of(b * seq, seq)
            for k in range(CONV_W - 1):
                head_ref[lo + k:lo + k + 1, :] = conv0_ref[k, pl.ds(b, 1), :]
            ct = 1024
            for jc in range(conv_dim // ct):
                cols = slice(jc * ct, (jc + 1) * ct)
                head_ref[SUBLANES:SUBLANES + seq, cols] = xbc_ref[pl.ds(r0, seq), cols]
                xc_ref[slot, pl.ds(r0, seq), cols] = _conv_tile_rows(
                    lambda k: head_ref[lo + k:lo + k + seq, cols], cw_ref[:, cols], cb_ref[:, cols])
            for k in range(CONV_W - 1):
                r = SUBLANES + seq - (CONV_W - 1) + k
                conv_sc_ref[slot, k, pl.ds(b, 1), :] = head_ref[r:r + 1, :]

            for k in range(POOL_BUF):
                r = POOL_PAD - POOL_BUF + k
                pe_ref[r:r + 1, :] = pool0_ref[k, pl.ds(b, 1), :]
            vb = v_ref[pl.ds(r0, seq), :]
            pe_ref[POOL_PAD:POOL_PAD + seq, :] = vb
            t_idx = lax.broadcasted_iota(jnp.int32, (seq, 1), 0) + pos0
            for gi, win in enumerate(POOL_WINDOWS):
                cols = slice(gi * gc, (gi + 1) * gc)
                total = pe_ref[POOL_PAD:POOL_PAD + seq, cols]
                for k in range(1, win):
                    total = total + pe_ref[POOL_PAD - k:POOL_PAD - k + seq, cols]
                cnt = jnp.minimum(t_idx + 1, win).astype(F32)
                pooled = total / cnt - vb[:, cols]
                for s in range(gc // LANES):
                    pooled_sc_ref[slot, gi * (gc // LANES) + s, pl.ds(r0, seq), :] = (
                        pooled[:, s * LANES:(s + 1) * LANES])
            for k in range(POOL_BUF):
                r = POOL_PAD + seq - POOL_BUF + k
                pool_sc_ref[slot, k, pl.ds(b, 1), :] = pe_ref[r:r + 1, :]
            return carry

        lax.fori_loop(j * seqs_per_step, (j + 1) * seqs_per_step, per_seq, 0)

    @pl.when(c >= 1)
    def _():
        slot = (c + 1) % 2
        xc = xc_ref.at[slot]

        @pl.when(j == 0)
        def _():
            pooled_ref[...] = pooled_sc_ref[slot]
            conv_ref[...] = conv_sc_ref[slot]
            pool_ref[...] = pool_sc_ref[slot]
            terms = _chunk_scalars(dt_ref[...], dtb_ref[...], alog_ref[...], seq)
            seg_ref[...] = terms.seg2
            for g in range(N_GROUPS):
                b_g = xc[:, d_inner + g * n_state:d_inner + (g + 1) * n_state].astype(BF16)
                c_g = xc[:, d_inner + (N_GROUPS + g) * n_state:
                         d_inner + (N_GROUPS + g + 1) * n_state].astype(BF16)
                cb = _dot_nt(c_g, b_g)
                for p in range(tiles_per_group):
                    tile = g * tiles_per_group + p
                    h0 = tile * HEADS_PER_TILE
                    cols = slice(tile * LANES, (tile + 1) * LANES)
                    xpair = xc[:, cols]
                    y, e_pair = _diag_pair(xpair, cb, terms, h0)
                    y_ref[:, cols] = y + dskip_ref[:, cols] * xpair
                    e_ref[:, cols] = e_pair
                    xwt_ref[cols, :] = (xpair.T * _pair_rows(terms.w_state_t, h0, q)).astype(BF16)

        row_q = lax.broadcasted_iota(jnp.int32, (q, n_state), 0)
        row128 = lax.broadcasted_iota(jnp.int32, (LANES, n_state), 0)
        for sj in range(seqs_per_step):
            r0 = pl.multiple_of((j * seqs_per_step + sj) * seq, seq)
            chunk_decay = jnp.exp2(seg_ref[pl.ds(r0, 1), :])
            mine = jnp.logical_and(row_q >= r0, row_q < r0 + seq)
            for g in range(N_GROUPS):
                b_cols = slice(d_inner + g * n_state, d_inner + (g + 1) * n_state)
                c_cols = slice(d_inner + (N_GROUPS + g) * n_state,
                               d_inner + (N_GROUPS + g + 1) * n_state)
                b_mine = jnp.where(mine, xc[:, b_cols], 0.0).astype(BF16)
                c_rows = xc[pl.ds(r0, seq), c_cols].astype(BF16)
                for p in range(tiles_per_group):
                    tile = g * tiles_per_group + p
                    h0 = tile * HEADS_PER_TILE
                    cols = slice(tile * LANES, (tile + 1) * LANES)
                    h_prev = ssm0_ref[sj, cols, :]
                    y_off = _dot_nt(c_rows, h_prev.astype(BF16)) * e_ref[pl.ds(r0, seq), cols]
                    y_ref[pl.ds(r0, seq), cols] += y_off
                    decay_rows = jnp.where(row128 < HEAD_DIM, chunk_decay[:, h0:h0 + 1],
                                           chunk_decay[:, h0 + 1:h0 + 2])
                    ssm_ref[sj, cols, :] = decay_rows * h_prev + _dot(xwt_ref[cols, :], b_mine)


def _mixer_sample(proj, layout, conv0, pool0, ssm0, conv_w, conv_b, dt_bias, a_log, d_skip,
                  *, batch, seq, d_inner, pos0):
    q = CHUNK
    n_seq = q // seq
    n_chunks = batch // n_seq
    conv_dim = layout["xbc"][1]
    pool_dim = layout["v"][1]
    sps = math.gcd(n_seq, 8)
    steps = n_seq // sps
    prep = lambda c: jnp.minimum(c, n_chunks - 1)
    scan = lambda c: jnp.maximum(c - 1, 0)
    window = lambda k, chunk: pl.BlockSpec(
        (q, layout[k][1]), lambda c, j: (chunk(c), layout[k][0] // layout[k][1]))
    tok = lambda c, j: (scan(c), 0)
    per_chunk_in = lambda c, j: (0, prep(c), 0)
    per_chunk = lambda c, j: (0, scan(c), 0)
    per_seq = lambda c, j: (jnp.where(c == 0, 0, (c - 1) * steps + j), 0, 0)
    return pl.pallas_call(
        functools.partial(_mixer_sample_kernel, d_inner=d_inner, seq=seq, pos0=pos0,
                          seqs_per_step=sps, n_chunks=n_chunks),
        grid=(n_chunks + 1, steps),
        in_specs=[window("xbc", prep), window("dt", scan), window("v", prep),
                  pl.BlockSpec((CONV_W - 1, n_seq, conv_dim), per_chunk_in),
                  pl.BlockSpec((POOL_BUF, n_seq, pool_dim), per_chunk_in),
                  pl.BlockSpec((sps, d_inner, D_STATE), per_seq),
                  _const_spec((CONV_W, conv_dim)),
                  _const_spec((1, conv_dim)),
                  _const_spec((1, LANES)),
                  _const_spec((1, LANES)),
                  _const_spec((1, d_inner))],
        out_specs=[pl.BlockSpec((q, d_inner), tok),
                   pl.BlockSpec((pool_dim // LANES, q, LANES), per_chunk),
                   pl.BlockSpec((sps, d_inner, D_STATE), per_seq),
                   pl.BlockSpec((CONV_W - 1, n_seq, conv_dim), per_chunk),
                   pl.BlockSpec((POOL_BUF, n_seq, pool_dim), per_chunk)],
        out_shape=[jax.ShapeDtypeStruct((batch * seq, d_inner), F32),
                   jax.ShapeDtypeStruct((pool_dim // LANES, batch * seq, LANES), F32),
                   jax.ShapeDtypeStruct((batch, d_inner, D_STATE), F32),
                   jax.ShapeDtypeStruct((CONV_W - 1, batch, conv_dim), F32),
                   jax.ShapeDtypeStruct((POOL_BUF, batch, pool_dim), F32)],
        scratch_shapes=[pltpu.VMEM((2 * SUBLANES, conv_dim), F32),
                        pltpu.VMEM((2, q, conv_dim), F32),
                        pltpu.VMEM((d_inner, q), BF16),
                        pltpu.VMEM((q, d_inner), F32),
                        pltpu.VMEM((q, LANES), F32),
                        pltpu.VMEM((POOL_PAD + seq, pool_dim), F32),
                        pltpu.VMEM((2, pool_dim // LANES, q, LANES), F32),
                        pltpu.VMEM((2, CONV_W - 1, n_seq, conv_dim), F32),
                        pltpu.VMEM((2, POOL_BUF, n_seq, pool_dim), F32)],
        compiler_params=_params(2, SAMPLE_MIXER_VMEM_LIMIT_BYTES),
        name="mixer_sample",
    )(proj, proj, proj, conv0, pool0, ssm0, conv_w, conv_b, dt_bias, a_log, d_skip)


def _gate_and_norm(y_ref, z_ref, nssd_ref, yn_ref):
    gw = y_ref.shape[1] // N_GROUPS
    for g in range(N_GROUPS):
        cols = slice(g * gw, (g + 1) * gw)
        yz = y_ref[:, cols] * _silu(z_ref[:, cols])
        yn_ref[:, cols] = _rmsnorm(yz, nssd_ref[:, cols]).astype(BF16)


def _merge_kernel(h_ref, *rest, gated, cast_weights):
    if gated:
        yn_ref, pooled_ref, gates_ref, pscale_ref, *rest = rest
    else:
        y_ref, z_ref, pooled_ref, gates_ref, nssd_ref, pscale_ref, *rest = rest
    if cast_weights:
        w32_refs, (o_ref, *w_refs), scratch = rest[:4], rest[4:9], rest[9:]

        @pl.when(pl.program_id(0) == 0)
        def _():
            for dst, src in zip(w_refs, w32_refs):
                dst[...] = src[...].astype(BF16)
    else:
        w_refs, o_ref, scratch = rest[:4], rest[4], rest[5:]
    wssd_ref, wpg_ref, wpo_ref, wo_ref = w_refs
    d_model = h_ref.shape[1]
    if gated:
        (pm_ref,) = scratch
    else:
        yn_ref, pm_ref = scratch
        _gate_and_norm(y_ref, z_ref, nssd_ref, yn_ref)
    a_branch = _dot(yn_ref[...], wssd_ref[...])
    slabs_per_group = pooled_ref.shape[0] // len(POOL_WINDOWS)
    gc = slabs_per_group * LANES
    for gi in range(len(POOL_WINDOWS)):
        cols = slice(gi * gc, (gi + 1) * gc)
        pooled = jnp.concatenate([pooled_ref[gi * slabs_per_group + s]
                                  for s in range(slabs_per_group)], axis=1)
        mixed = _dot(pooled.astype(BF16), wpg_ref[gi])
        pm_ref[:, cols] = (mixed * pscale_ref[:, cols]).astype(BF16)
    b_branch = _dot(pm_ref[...], wpo_ref[...])
    merged = (_sigmoid(gates_ref[:, 0:d_model]) * a_branch
              + _sigmoid(gates_ref[:, d_model:2 * d_model]) * b_branch)
    o_ref[...] = h_ref[...] + _dot(merged.astype(BF16), wo_ref[...])


def _merge(h, ssd, pooled, gates_src, pool_scale, weights, *, tm, cast_weights):
    t, d = h.shape
    tm = min(tm, t)
    gated = not isinstance(ssd, tuple)
    pool_slabs = pooled.shape[0]
    pool_dim = pool_slabs * LANES
    row = lambda w, blk=0: pl.BlockSpec((tm, w), lambda i: (i, blk))
    pooled_spec = pl.BlockSpec((pool_slabs, tm, LANES), lambda i: (0, i, 0))
    gates_spec = row(N_BRANCH * d, gates_src[1])
    scratch = [pltpu.VMEM((tm, pool_dim), BF16)]
    if gated:
        d_inner = ssd.shape[1]
        in_specs = [row(d), row(d_inner), pooled_spec, gates_spec, _const_spec((1, pool_dim))]
        args = [h, ssd, pooled, gates_src[0], pool_scale]
    else:
        y, z_src, norm_ssd = ssd
        d_inner = y.shape[1]
        in_specs = [row(d), row(d_inner), row(d_inner, z_src[1]), pooled_spec, gates_spec,
                    _const_spec((1, d_inner)), _const_spec((1, pool_dim))]
        args = [h, y, z_src[0], pooled, gates_src[0], norm_ssd, pool_scale]
        scratch = [pltpu.VMEM((tm, d_inner), BF16)] + scratch
    out_specs = [row(d)]
    out_shape = [jax.ShapeDtypeStruct((t, d), F32)]
    if cast_weights:
        out_specs += [pl.BlockSpec(w.shape, lambda i, nd=w.ndim: (0,) * nd) for w in weights]
        out_shape += [jax.ShapeDtypeStruct(w.shape, BF16) for w in weights]
    res = pl.pallas_call(
        functools.partial(_merge_kernel, gated=gated, cast_weights=cast_weights),
        grid=(t // tm,),
        in_specs=in_specs + [_const_spec(w.shape) for w in weights],
        out_specs=out_specs,
        out_shape=out_shape,
        scratch_shapes=scratch,
        compiler_params=_params(1),
        name="merge_cast" if cast_weights else "merge",
    )(*args, *weights)
    return res if cast_weights else res[0]


def _pad_lanes(a):
    return jnp.pad(a.reshape(1, -1), ((0, 0), (0, LANES - a.shape[-1])))


def kernel(x_prompt, x_sample, state_ssm, state_conv, state_pool, p_prompt, p_sample, norm_ffn1, w_ffn1_gu, w_ffn1_down, norm_mix, w_in, conv_w, conv_b, dt_bias, a_log, d_skip, norm_ssd, w_ssd_out, w_pool_group, pool_scale, w_pool_out, w_o, norm_ffn2, w_ffn2_gu, w_ffn2_down, norm_ple, w_ple_gate, w_ple, norm_final):
    depth = norm_ffn1.shape[0]
    assert depth == 1, "the final norm is fused into the layer's last stage: one layer only"
    batch, seq, d_model = x_prompt.shape
    dec_batch, dec_seq, _ = x_sample.shape
    n_heads = dt_bias.shape[1]
    d_inner = n_heads * HEAD_DIM
    conv_dim = conv_w.shape[2]
    pool_dim = pool_scale.shape[1]
    assert seq % CHUNK == 0 and CHUNK % dec_seq == 0 and dec_batch % (CHUNK // dec_seq) == 0

    row = lambda a: a[0].reshape(1, -1)
    layout, proj_cols = _proj_layout(d_inner, conv_dim, pool_dim, d_model)
    src = {"z": (0, d_inner), "xbc": (d_inner, conv_dim), "dt": (d_inner + conv_dim, n_heads),
           "v": (d_inner + conv_dim + n_heads, pool_dim),
           "gates": (d_inner + conv_dim + n_heads + pool_dim, N_BRANCH * d_model)}
    unit_rows, unit_valid = [], []
    for name in sorted(layout, key=lambda k: layout[k][0]):
        start, width = src[name]
        assert layout[name][0] == len(unit_rows) * PROJ_UNIT
        for off in range(0, width, PROJ_UNIT):
            unit_rows.append(start + off)
            unit_valid.append(min(PROJ_UNIT, width - off))
    assert len(unit_rows) * PROJ_UNIT == proj_cols
    dtb, alog = _pad_lanes(dt_bias[0]), _pad_lanes(a_log[0])
    dskip = jnp.repeat(d_skip[0], HEAD_DIM).reshape(1, d_inner)
    cb = row(conv_b)
    ple_consts = (row(norm_ple), w_ple_gate[0], w_ple[0], norm_final.reshape(1, -1))

    h1, wg1, wu1, wd1 = _ffn_stream(x_sample.reshape(dec_batch * dec_seq, d_model), row(norm_ffn1),
                                    w_ffn1_gu[0], w_ffn1_down[0])
    proj, w_cat = _inproj_stream(h1, row(norm_mix), w_in[0].T, tuple(unit_rows), tuple(unit_valid))
    y, pooled, s2, c2, q2 = _mixer_sample(
        proj, layout, jnp.transpose(state_conv[0], (1, 0, 2)), jnp.transpose(state_pool[0], (1, 0, 2)),
        state_ssm[0].reshape(dec_batch, d_inner, D_STATE),
        conv_w[0], cb, dtb, alog, dskip,
        batch=dec_batch, seq=dec_seq, d_inner=d_inner, pos0=PAST_LEN)
    block_of = lambda k: layout[k][0] // layout[k][1]
    h2, wssd, wpg, wpo, wo = _merge(
        h1, (y, (proj, block_of("z")), row(norm_ssd)), pooled, (proj, block_of("gates")),
        row(pool_scale), (w_ssd_out[0], w_pool_group[0], w_pool_out[0], w_o[0]),
        tm=128, cast_weights=True)
    y_sample, wg2, wu2, wd2, wpleg, wple = _ffn_stream(
        h2, row(norm_ffn2), w_ffn2_gu[0], w_ffn2_down[0],
        (p_sample[0].reshape(dec_batch * dec_seq, -1),) + ple_consts)

    h1 = _ffn(x_prompt.reshape(batch * seq, d_model), row(norm_ffn1), wg1, wu1, wd1, tm=1024)
    yn, pooled, gates, c1, q1, s1 = _inproj_mixer(h1, row(norm_mix), w_cat, layout, conv_w[0], cb,
                                                  dtb, alog, dskip, row(norm_ssd),
                                                  tm=256, seq=seq, d_inner=d_inner)
    h2 = _merge(h1, yn, pooled, (gates, 0), row(pool_scale), (wssd, wpg, wpo, wo),
                tm=512, cast_weights=False)
    y_prompt = _ffn(h2, row(norm_ffn2), wg2, wu2, wd2,
                    (p_prompt[0].reshape(batch * seq, -1), ple_consts[0], wpleg, wple, ple_consts[3]),
                    tm=1024)

    return (y_prompt.reshape(batch, seq, d_model),
            y_sample.reshape(dec_batch, dec_seq, d_model),
            s1.reshape(1, batch, n_heads, HEAD_DIM, D_STATE), c1[None], q1[None],
            s2.reshape(1, dec_batch, n_heads, HEAD_DIM, D_STATE),
            jnp.transpose(c2, (1, 0, 2))[None], jnp.transpose(q2, (1, 0, 2))[None])
```
